```python
import jax, jax.numpy as jnp
from jax import lax
import numpy as np

D_MODEL = 1024
BATCH = 8
SEQ = 2048
DEPTH = 1
DEC_BATCH = 128
DEC_SEQ = 1
PAST_LEN = 16384
PAGE_SIZE = 128

D_MIX = D_MODEL
D_S5 = D_MIX // 2
S5_GROUP = 16
S5_N_GROUPS = D_S5 // S5_GROUP
S5_STATE = 64
D_GDN = D_MIX - D_S5
GDN_HEAD_DIM = 128
GDN_HEADS = D_GDN // GDN_HEAD_DIM
GDN_CONV = 4
GDN_CHUNK = 64
D_FF = 2816
FFN_CONV = 3
NORM_EPS = 1e-6
D_IN = D_S5 + 4 * D_GDN + 2 * GDN_HEADS

kernel_name = 'hymba_s5_gdn_convffn_step'


def rmsnorm(x, g):
    xf = x.astype(jnp.float32)
    xf = xf * lax.rsqrt(jnp.mean(xf * xf, axis=-1, keepdims=True) + NORM_EPS)
    return (xf * g.astype(jnp.float32)).astype(x.dtype)


def l2norm(x):
    xf = x.astype(jnp.float32)
    return xf * lax.rsqrt(jnp.sum(xf * xf, axis=-1, keepdims=True) + NORM_EPS)


def causal_depthwise_conv(x, buf, w):
    width = w.shape[0]
    seq = x.shape[1]
    xp = jnp.concatenate([buf.astype(x.dtype), x], axis=1)
    y = xp[:, 0:seq] * w[0]
    for j in range(1, width):
        y = y + xp[:, j:j + seq] * w[j]
    return y, xp[:, seq:]


def s5_mixer(u, h0_re, h0_im, a_re, a_im, log_dt, b_re, b_im, c_re, c_im, d, w_glu):
    bsz, seq, _ = u.shape
    uf = u.astype(jnp.float32)
    ug = uf.reshape(bsz, seq, S5_N_GROUPS, S5_GROUP)
    ar = a_re.astype(jnp.float32)
    ai = a_im.astype(jnp.float32)
    dt = jnp.exp(log_dt.astype(jnp.float32))[:, None]
    mag = jnp.exp(ar * dt)
    ab_re = mag * jnp.cos(ai * dt)
    ab_im = mag * jnp.sin(ai * dt)
    den = ar * ar + ai * ai
    p = ab_re - 1.0
    f_re = (p * ar + ab_im * ai) / den
    f_im = (ab_im * ar - p * ai) / den
    bu_re = jnp.einsum('bsgc,gnc->bsgn', ug, b_re.astype(jnp.float32))
    bu_im = jnp.einsum('bsgc,gnc->bsgn', ug, b_im.astype(jnp.float32))
    x_re = f_re * bu_re - f_im * bu_im
    x_im = f_re * bu_im + f_im * bu_re
    h0r = h0_re.astype(jnp.float32)
    h0i = h0_im.astype(jnp.float32)
    x_re = x_re.at[:, 0].add(ab_re * h0r - ab_im * h0i)
    x_im = x_im.at[:, 0].add(ab_re * h0i + ab_im * h0r)
    a_full_re = jnp.broadcast_to(ab_re, x_re.shape)
    a_full_im = jnp.broadcast_to(ab_im, x_im.shape)

    def combine(e1, e2):
        a1r, a1i, b1r, b1i = e1
        a2r, a2i, b2r, b2i = e2
        return (a2r * a1r - a2i * a1i, a2r * a1i + a2i * a1r,
                a2r * b1r - a2i * b1i + b2r, a2r * b1i + a2i * b1r + b2i)

    _, _, h_re, h_im = lax.associative_scan(combine, (a_full_re, a_full_im, x_re, x_im), axis=1)
    y = (jnp.einsum('bsgn,gcn->bsgc', h_re, c_re.astype(jnp.float32))
         - jnp.einsum('bsgn,gcn->bsgc', h_im, c_im.astype(jnp.float32)))
    y = y.reshape(bsz, seq, D_S5) + d.astype(jnp.float32) * uf
    y = jax.nn.gelu(y)
    gl = y @ w_glu.astype(jnp.float32)
    out = gl[..., :D_S5] * jax.nn.sigmoid(gl[..., D_S5:])
    return out.astype(u.dtype), h_re[:, -1], h_im[:, -1]


def gdn_chunked(q, k, v, g, beta, s0):
    bsz, seq = q.shape[0], q.shape[1]
    nc = -(-seq // GDN_CHUNK)
    pad = nc * GDN_CHUNK - seq

    def prep(t):
        t = jnp.pad(t.astype(jnp.float32), [(0, 0), (0, pad)] + [(0, 0)] * (t.ndim - 2))
        t = t.reshape((bsz, nc, GDN_CHUNK) + t.shape[2:])
        return jnp.moveaxis(t, 3, 1)

    q, k, v, g, beta = prep(q), prep(k), prep(v), prep(g), prep(beta)
    dv = v.shape[-1]
    gc = jnp.cumsum(g, axis=-1)
    idx = jnp.arange(GDN_CHUNK)
    causal = idx[:, None] >= idx[None, :]
    strict = idx[:, None] > idx[None, :]
    decay = jnp.exp(jnp.where(causal, gc[..., :, None] - gc[..., None, :], -jnp.inf))
    kb = k * beta[..., None]
    vb = v * beta[..., None]
    lmat = jnp.where(strict, jnp.einsum('bhnid,bhnjd->bhnij', kb, k) * decay, 0.0)
    eye = jnp.eye(GDN_CHUNK, dtype=jnp.float32)
    rhs = jnp.concatenate([vb, kb * jnp.exp(gc)[..., None]], axis=-1)
    sol = lax.linalg.triangular_solve(eye + lmat, rhs, left_side=True, lower=True)
    u_c, w_c = sol[..., :dv], sol[..., dv:]
    attn = jnp.einsum('bhnid,bhnjd->bhnij', q, k) * decay
    qg = q * jnp.exp(gc)[..., None]
    kg = k * jnp.exp(gc[..., -1:] - gc)[..., None]
    glast = jnp.exp(gc[..., -1])
    xs = tuple(jnp.moveaxis(t, 2, 0) for t in (u_c, w_c, attn, qg, kg, glast))

    def step(s, inp):
        uu, ww, aa, qq, kk, gl = inp
        v_new = uu - jnp.einsum('bhcd,bhde->bhce', ww, s)
        o = jnp.einsum('bhcd,bhde->bhce', qq, s) + jnp.einsum('bhij,bhje->bhie', aa, v_new)
        s = s * gl[..., None, None] + jnp.einsum('bhcd,bhce->bhde', kk, v_new)
        return s, o

    s_fin, o = lax.scan(step, s0.astype(jnp.float32), xs)
    o = o.transpose(1, 0, 3, 2, 4).reshape(bsz, nc * GDN_CHUNK, GDN_HEADS, dv)[:, :seq]
    return o, s_fin


def gdn_mixer(qkv, z, beta_raw, a_raw, s0, conv_buf, conv_w, a_log, dt_bias, onorm_g):
    bsz, seq, _ = qkv.shape
    qkv_c, new_buf = causal_depthwise_conv(qkv, conv_buf, conv_w)
    qkv_c = jax.nn.silu(qkv_c)
    q = qkv_c[..., :D_GDN].reshape(bsz, seq, GDN_HEADS, GDN_HEAD_DIM)
    k = qkv_c[..., D_GDN:2 * D_GDN].reshape(bsz, seq, GDN_HEADS, GDN_HEAD_DIM)
    v = qkv_c[..., 2 * D_GDN:].reshape(bsz, seq, GDN_HEADS, GDN_HEAD_DIM)
    q = l2norm(q) * (GDN_HEAD_DIM ** -0.5)
    k = l2norm(k)
    beta = jax.nn.sigmoid(beta_raw.astype(jnp.float32))
    g = -jnp.exp(a_log.astype(jnp.float32)) * jax.nn.softplus(a_raw.astype(jnp.float32) + dt_bias.astype(jnp.float32))
    o, s_fin = gdn_chunked(q, k, v, g, beta, s0)
    zf = z.reshape(bsz, seq, GDN_HEADS, GDN_HEAD_DIM).astype(jnp.float32)
    o = rmsnorm(o, onorm_g) * jax.nn.silu(zf)
    return o.reshape(bsz, seq, D_GDN).astype(qkv.dtype), s_fin, new_buf


def conv_ffn(x, buf, w_up, conv_w, w_down):
    h = x @ w_up
    h, new_buf = causal_depthwise_conv(h, buf, conv_w)
    gate, up = h[..., :D_FF], h[..., D_FF:]
    return (jax.nn.silu(gate) * up) @ w_down, new_buf


def layer(x, s5_re, s5_im, gdn_s, gdn_buf, ffn_buf, norm1_g, w_in, s5_a_re, s5_a_im, s5_log_dt,
          s5_b_re, s5_b_im, s5_c_re, s5_c_im, s5_d, s5_w_glu, gdn_conv_w, gdn_a_log, gdn_dt_bias,
          gdn_onorm_g, w_out, norm2_g, ffn_w_up, ffn_conv_w, ffn_w_down):
    n = rmsnorm(x, norm1_g)
    proj = n @ w_in
    o0 = D_S5
    u_s5 = proj[..., :o0]
    qkv = proj[..., o0:o0 + 3 * D_GDN]
    z = proj[..., o0 + 3 * D_GDN:o0 + 4 * D_GDN]
    beta_raw = proj[..., o0 + 4 * D_GDN:o0 + 4 * D_GDN + GDN_HEADS]
    a_raw = proj[..., o0 + 4 * D_GDN + GDN_HEADS:]
    y_s5, s5_re, s5_im = s5_mixer(u_s5, s5_re, s5_im, s5_a_re, s5_a_im, s5_log_dt,
                                  s5_b_re, s5_b_im, s5_c_re, s5_c_im, s5_d, s5_w_glu)
    y_gdn, gdn_s, gdn_buf = gdn_mixer(qkv, z, beta_raw, a_raw, gdn_s, gdn_buf, gdn_conv_w,
                                      gdn_a_log, gdn_dt_bias, gdn_onorm_g)
    x = x + jnp.concatenate([y_s5, y_gdn], axis=-1) @ w_out
    y_ffn, ffn_buf = conv_ffn(rmsnorm(x, norm2_g), ffn_buf, ffn_w_up, ffn_conv_w, ffn_w_down)
    x = x + y_ffn
    return x, s5_re, s5_im, gdn_s, gdn_buf, ffn_buf


def setup_inputs(seed: int = 0) -> dict:
    key = jax.random.key(seed)
    ks = jax.random.split(key, 32)
    f32 = jnp.float32
    nrm = lambda k, shape, s: jax.random.normal(k, shape, f32) * s
    G, N = S5_N_GROUPS, S5_STATE
    a_im_base = jnp.pi * jnp.arange(N, dtype=f32)
    dt_g = jnp.exp(jax.random.uniform(ks[20], (DEPTH, GDN_HEADS), f32, np.log(1e-3), np.log(1e-1)))
    return {
        'x_prompt': nrm(ks[0], (BATCH, SEQ, D_MODEL), 1.0),
        'x_sample': nrm(ks[1], (DEC_BATCH, DEC_SEQ, D_MODEL), 1.0),
        'state_s5_re': nrm(ks[2], (DEPTH, DEC_BATCH, G, N), 0.5),
        'state_s5_im': nrm(ks[3], (DEPTH, DEC_BATCH, G, N), 0.5),
        'state_gdn': nrm(ks[4], (DEPTH, DEC_BATCH, GDN_HEADS, GDN_HEAD_DIM, GDN_HEAD_DIM), 0.3),
        'state_gdn_conv': nrm(ks[5], (DEPTH, DEC_BATCH, GDN_CONV - 1, 3 * D_GDN), 1.0),
        'state_ffn_conv': nrm(ks[6], (DEPTH, DEC_BATCH, FFN_CONV - 1, 2 * D_FF), 1.0),
        'norm1_g': 1.0 + nrm(ks[7], (DEPTH, D_MODEL), 0.02),
        'w_in': nrm(ks[8], (DEPTH, D_MODEL, D_IN), D_MODEL ** -0.5),
        's5_a_re': -0.5 + nrm(ks[9], (DEPTH, G, N), 0.01),
        's5_a_im': a_im_base + nrm(ks[10], (DEPTH, G, N), 0.01),
        's5_log_dt': jax.random.uniform(ks[11], (DEPTH, G), f32, np.log(1e-3), np.log(1e-1)),
        's5_b_re': nrm(ks[12], (DEPTH, G, N, S5_GROUP), (2 * S5_GROUP) ** -0.5),
        's5_b_im': nrm(ks[13], (DEPTH, G, N, S5_GROUP), (2 * S5_GROUP) ** -0.5),
        's5_c_re': nrm(ks[14], (DEPTH, G, S5_GROUP, N), N ** -0.5),
        's5_c_im': nrm(ks[15], (DEPTH, G, S5_GROUP, N), N ** -0.5),
        's5_d': nrm(ks[16], (DEPTH, D_S5), 0.5),
        's5_w_glu': nrm(ks[17], (DEPTH, D_S5, 2 * D_S5), D_S5 ** -0.5),
        'gdn_conv_w': nrm(ks[18], (DEPTH, GDN_CONV, 3 * D_GDN), GDN_CONV ** -0.5),
        'gdn_a_log': jnp.log(jax.random.uniform(ks[19], (DEPTH, GDN_HEADS), f32, 1.0, 16.0)),
        'gdn_dt_bias': dt_g + jnp.log(-jnp.expm1(-dt_g)),
        'gdn_onorm_g': 1.0 + nrm(ks[21], (DEPTH, GDN_HEAD_DIM), 0.02),
        'w_out': nrm(ks[22], (DEPTH, D_MIX, D_MODEL), D_MIX ** -0.5),
        'norm2_g': 1.0 + nrm(ks[23], (DEPTH, D_MODEL), 0.02),
        'ffn_w_up': nrm(ks[24], (DEPTH, D_MODEL, 2 * D_FF), D_MODEL ** -0.5),
        'ffn_conv_w': nrm(ks[25], (DEPTH, FFN_CONV, 2 * D_FF), FFN_CONV ** -0.5),
        'ffn_w_down': nrm(ks[26], (DEPTH, D_FF, D_MODEL), D_FF ** -0.5),
        'normf_g': 1.0 + nrm(ks[27], (D_MODEL,), 0.02),
    }


def reference(x_prompt, x_sample, state_s5_re, state_s5_im, state_gdn, state_gdn_conv, state_ffn_conv,
              norm1_g, w_in, s5_a_re, s5_a_im, s5_log_dt, s5_b_re, s5_b_im, s5_c_re, s5_c_im, s5_d,
              s5_w_glu, gdn_conv_w, gdn_a_log, gdn_dt_bias, gdn_onorm_g, w_out, norm2_g, ffn_w_up,
              ffn_conv_w, ffn_w_down, normf_g):
    f32 = jnp.float32
    xp, xs = x_prompt, x_sample
    p_new = ([], [], [], [], [])
    s_new = ([], [], [], [], [])
    for l in range(DEPTH):
        lw = (norm1_g[l], w_in[l], s5_a_re[l], s5_a_im[l], s5_log_dt[l], s5_b_re[l], s5_b_im[l],
              s5_c_re[l], s5_c_im[l], s5_d[l], s5_w_glu[l], gdn_conv_w[l], gdn_a_log[l], gdn_dt_bias[l],
              gdn_onorm_g[l], w_out[l], norm2_g[l], ffn_w_up[l], ffn_conv_w[l], ffn_w_down[l])
        outp = layer(xp,
                     jnp.zeros((BATCH, S5_N_GROUPS, S5_STATE), f32),
                     jnp.zeros((BATCH, S5_N_GROUPS, S5_STATE), f32),
                     jnp.zeros((BATCH, GDN_HEADS, GDN_HEAD_DIM, GDN_HEAD_DIM), f32),
                     jnp.zeros((BATCH, GDN_CONV - 1, 3 * D_GDN), xp.dtype),
                     jnp.zeros((BATCH, FFN_CONV - 1, 2 * D_FF), xp.dtype),
                     *lw)
        outs = layer(xs, state_s5_re[l], state_s5_im[l], state_gdn[l], state_gdn_conv[l],
                     state_ffn_conv[l], *lw)
        xp, xs = outp[0], outs[0]
        for i in range(5):
            p_new[i].append(outp[i + 1])
            s_new[i].append(outs[i + 1])
    y_prompt = rmsnorm(xp, normf_g)
    y_sample = rmsnorm(xs, normf_g)
    p_s5_re, p_s5_im, p_gdn, p_gdn_conv, p_ffn_conv = [jnp.stack(t, axis=0) for t in p_new]
    s_s5_re, s_s5_im, s_gdn, s_gdn_conv, s_ffn_conv = [jnp.stack(t, axis=0) for t in s_new]
    return (y_prompt, y_sample, p_s5_re, p_s5_im, p_gdn, p_gdn_conv, p_ffn_conv,
            s_s5_re, s_s5_im, s_gdn, s_gdn_conv, s_ffn_conv)
```

```python
import functools

import jax
import jax.numpy as jnp
from jax import lax
from jax.experimental import pallas as pl
from jax.experimental.pallas import tpu as pltpu

F32 = jnp.float32
BF16 = jnp.bfloat16
NORM_EPS = 1e-6

D_MODEL = 1024
D_S5 = 512
S5_GROUP = 16
S5_N_GROUPS = 32
S5_STATE = 64
S5_CH = S5_N_GROUPS * S5_STATE
D_GDN = 512
GDN_HEADS = 4
GDN_HEAD_DIM = 128
GDN_CONV = 4
D_FF = 2816
FFN_CONV = 3
D_IN = D_S5 + 4 * D_GDN + 2 * GDN_HEADS
GATE_PAD = 128
D_IN_PAD = D_S5 + 4 * D_GDN + GATE_PAD

CHUNK = 64
FFN_COLS = 256
VMEM_LIMIT = 56 * 1024 * 1024


def _dot(a, b):
    return jnp.dot(a, b, preferred_element_type=F32)


def _dot_hi(a, b):
    return jnp.dot(a, b, preferred_element_type=F32, precision=lax.Precision.HIGHEST)


def _dot_nt(a, b):
    return lax.dot_general(a, b, (((1,), (1,)), ((), ())), preferred_element_type=F32)


def _rms(x, g):
    ms = jnp.mean(x * x, axis=-1, keepdims=True)
    return x * lax.rsqrt(ms + NORM_EPS) * g


def _silu(x):
    return x * jax.nn.sigmoid(x)


def _softplus(x):
    return jnp.maximum(x, 0.0) + jnp.log1p(jnp.exp(-jnp.abs(x)))


def _proj_kernel(x_ref, g_ref, w_ref, u_ref, qkv_ref, z_ref, ba_ref):
    n = _rms(x_ref[...], g_ref[...]).astype(BF16)
    o0 = D_S5
    o1 = o0 + 3 * D_GDN
    o2 = o1 + D_GDN
    u_ref[...] = _dot(n, w_ref[:, :o0])
    qkv_ref[...] = _dot(n, w_ref[:, o0:o1])
    z_ref[...] = _dot(n, w_ref[:, o1:o2])
    ba_ref[...] = _dot(n, w_ref[:, o2:])


def _proj(x, g1, w_in_bf, tm):
    n = x.shape[0]
    row = lambda i: (i, 0)
    const = lambda i: (0, 0)
    return pl.pallas_call(
        _proj_kernel,
        grid=(n // tm,),
        in_specs=[pl.BlockSpec((tm, D_MODEL), row),
                  pl.BlockSpec((1, D_MODEL), const),
                  pl.BlockSpec((D_MODEL, D_IN_PAD), const)],
        out_specs=[pl.BlockSpec((tm, D_S5), row),
                   pl.BlockSpec((tm, 3 * D_GDN), row),
                   pl.BlockSpec((tm, D_GDN), row),
                   pl.BlockSpec((tm, GATE_PAD), row)],
        out_shape=[jax.ShapeDtypeStruct((n, D_S5), F32),
                   jax.ShapeDtypeStruct((n, 3 * D_GDN), F32),
                   jax.ShapeDtypeStruct((n, D_GDN), F32),
                   jax.ShapeDtypeStruct((n, GATE_PAD), F32)],
        compiler_params=pltpu.CompilerParams(dimension_semantics=("arbitrary",),
                                             vmem_limit_bytes=VMEM_LIMIT),
        name="proj",
    )(x, g1, w_in_bf)


S5_HALF = S5_CH // 2
S5_COLS = 512


def _s5_kernel(u_ref, h0r_ref, h0i_ref, are_ref, aim_ref, ldt_ref, bre_ref, bim_ref, cre_ref, cim_ref,
               d_ref, wglu_ref, y_ref, hr_out, hi_out,
               abr_s, abi_s, wbr_s, wbi_s, hr_s, hi_s, xr_s, xi_s, *, nb, tt):
    @pl.when(pl.program_id(0) == 0)
    def _init():
        ar = are_ref[...]
        ai = aim_ref[...]
        dt = jnp.exp(ldt_ref[...])
        mag = jnp.exp(ar * dt)
        abr = mag * jnp.cos(ai * dt)
        abi = mag * jnp.sin(ai * dt)
        den = ar * ar + ai * ai
        p = abr - 1.0
        fr = (p * ar + abi * ai) / den
        fi = (abi * ar - p * ai) / den
        abr_s[...] = abr
        abi_s[...] = abi
        for m in range(2):
            frm = fr[:, m * S5_HALF:(m + 1) * S5_HALF]
            fim = fi[:, m * S5_HALF:(m + 1) * S5_HALF]
            wr = bre_ref[m]
            wi = bim_ref[m]
            wbr_s[m] = (wr * frm - wi * fim).astype(BF16)
            wbi_s[m] = (wr * fim + wi * frm).astype(BF16)
        hr_s[...] = h0r_ref[...]
        hi_s[...] = h0i_ref[...]

    u = u_ref[...]
    ub = u.astype(BF16)
    half = D_S5 // 2
    for m in range(2):
        um = ub[:, m * half:(m + 1) * half]
        xr_s[:, m * S5_HALF:(m + 1) * S5_HALF] = _dot(um, wbr_s[m])
        xi_s[:, m * S5_HALF:(m + 1) * S5_HALF] = _dot(um, wbi_s[m])

    for cb in range(S5_CH // S5_COLS):
        cols = slice(cb * S5_COLS, (cb + 1) * S5_COLS)
        a_r = jnp.broadcast_to(abr_s[:, cols], (nb, S5_COLS))
        a_i = jnp.broadcast_to(abi_s[:, cols], (nb, S5_COLS))

        def step(t, carry, cols=cols, a_r=a_r, a_i=a_i):
            hr, hi = carry
            r0 = pl.multiple_of(t * nb, nb)
            nhr = a_r * hr - a_i * hi + xr_s[pl.ds(r0, nb), cols]
            nhi = a_r * hi + a_i * hr + xi_s[pl.ds(r0, nb), cols]
            xr_s[pl.ds(r0, nb), cols] = nhr
            xi_s[pl.ds(r0, nb), cols] = nhi
            return nhr, nhi

        hr, hi = lax.fori_loop(0, tt, step, (hr_s[:, cols], hi_s[:, cols]), unroll=min(tt, 8))
        hr_s[:, cols] = hr
        hi_s[:, cols] = hi

    hr_out[...] = hr_s[...]
    hi_out[...] = hi_s[...]

    ys = []
    for m in range(2):
        hre = xr_s[:, m * S5_HALF:(m + 1) * S5_HALF].astype(BF16)
        him = xi_s[:, m * S5_HALF:(m + 1) * S5_HALF].astype(BF16)
        ych = _dot(hre, cre_ref[m]) - _dot(him, cim_ref[m])
        ys.append(ych + d_ref[:, m * half:(m + 1) * half] * u[:, m * half:(m + 1) * half])
    y = jax.nn.gelu(jnp.concatenate(ys, axis=-1)).astype(BF16)
    gl = _dot(y, wglu_ref[...])
    y_ref[...] = gl[:, :D_S5] * jax.nn.sigmoid(gl[:, D_S5:])


def _s5(u_tb, h0r, h0i, sw, nb, tt):
    n = u_tb.shape[0]
    rows = nb * tt
    const2 = lambda i: (0, 0)
    const3 = lambda i: (0, 0, 0)
    full2 = lambda shape: pl.BlockSpec(shape, const2)
    full3 = lambda shape: pl.BlockSpec(shape, const3)
    return pl.pallas_call(
        functools.partial(_s5_kernel, nb=nb, tt=tt),
        grid=(n // rows,),
        in_specs=[pl.BlockSpec((rows, D_S5), lambda i: (i, 0)),
                  full2((nb, S5_CH)), full2((nb, S5_CH)),
                  full2((1, S5_CH)), full2((1, S5_CH)), full2((1, S5_CH)),
                  full3((2, D_S5 // 2, S5_HALF)), full3((2, D_S5 // 2, S5_HALF)),
                  full3((2, S5_HALF, D_S5 // 2)), full3((2, S5_HALF, D_S5 // 2)),
                  full2((1, D_S5)), full2((D_S5, 2 * D_S5))],
        out_specs=[pl.BlockSpec((rows, D_S5), lambda i: (i, 0)),
                   full2((nb, S5_CH)), full2((nb, S5_CH))],
        out_shape=[jax.ShapeDtypeStruct((n, D_S5), F32),
                   jax.ShapeDtypeStruct((nb, S5_CH), F32),
                   jax.ShapeDtypeStruct((nb, S5_CH), F32)],
        scratch_shapes=[pltpu.VMEM((1, S5_CH), F32), pltpu.VMEM((1, S5_CH), F32),
                        pltpu.VMEM((2, D_S5 // 2, S5_HALF), BF16), pltpu.VMEM((2, D_S5 // 2, S5_HALF), BF16),
                        pltpu.VMEM((nb, S5_CH), F32), pltpu.VMEM((nb, S5_CH), F32),
                        pltpu.VMEM((rows, S5_CH), F32), pltpu.VMEM((rows, S5_CH), F32)],
        compiler_params=pltpu.CompilerParams(dimension_semantics=("arbitrary",),
                                             vmem_limit_bytes=VMEM_LIMIT),
        name="s5",
    )(u_tb, h0r, h0i, sw["a_re"], sw["a_im"], sw["log_dt"], sw["b_re"], sw["b_im"],
      sw["c_re"], sw["c_im"], sw["d"], sw["w_glu"])


def _s5_weights(a_re, a_im, log_dt, b_re, b_im, c_re, c_im, d, w_glu):
    g, n, c = S5_N_GROUPS, S5_STATE, S5_GROUP
    eye = jnp.eye(g // 2, dtype=F32)

    def b_blockdiag(b):
        bb = b.reshape(2, g // 2, n, c)
        return jnp.einsum("mgnc,gh->mgchn", bb, eye).reshape(2, (g // 2) * c, (g // 2) * n)

    def c_blockdiag(cm):
        cc = cm.reshape(2, g // 2, c, n)
        return jnp.einsum("mgcn,gh->mgnhc", cc, eye).reshape(2, (g // 2) * n, (g // 2) * c)

    return {
        "a_re": a_re.reshape(1, S5_CH), "a_im": a_im.reshape(1, S5_CH),
        "log_dt": jnp.repeat(log_dt, n).reshape(1, S5_CH),
        "b_re": b_blockdiag(b_re), "b_im": b_blockdiag(b_im),
        "c_re": c_blockdiag(c_re).astype(BF16), "c_im": c_blockdiag(c_im).astype(BF16),
        "d": d.reshape(1, D_S5), "w_glu": w_glu.astype(BF16),
    }


def _l2norm(x):
    return x * lax.rsqrt(jnp.sum(x * x, axis=-1, keepdims=True) + NORM_EPS)


def _gates(ba, alog_row, dtb_row):
    beta = jax.nn.sigmoid(ba)
    g = -jnp.exp(alog_row) * _softplus(ba + dtb_row)
    return beta, g


def _out_gate(o, og_row, z):
    return _rms(o, og_row) * _silu(z)


def _tri_inverse(lmat, ii, jj):
    eye = (ii == jj).astype(F32)
    t = eye - jnp.where((ii >> 1) == (jj >> 1), lmat, 0.0)
    shift = 1
    while (1 << shift) < CHUNK:
        same_big = (ii >> (shift + 1)) == (jj >> (shift + 1))
        same_small = (ii >> shift) == (jj >> shift)
        c = jnp.where(same_big & jnp.logical_not(same_small), lmat, 0.0)
        t = t - _dot_hi(_dot_hi(t, c), t)
        shift += 1
    return t


def _gdn_kernel(qkv_ref, z_ref, ba_ref, convw_ref, alog_ref, dtb_ref, og_ref,
                y_ref, sfin_ref, convout_ref, s_s, ext_s, *, tg):
    ti = pl.program_id(1)
    nt = pl.num_programs(1)

    @pl.when(ti == 0)
    def _reset():
        s_s[...] = jnp.zeros_like(s_s)
        ext_s[0:8, :] = jnp.zeros((8, 3 * D_GDN), F32)

    ext_s[8:, :] = qkv_ref[...]

    ba = ba_ref[...]
    beta_all, g_all = _gates(ba, alog_ref[...], dtb_ref[...])

    ri = lax.broadcasted_iota(jnp.int32, (tg, tg), 0)
    ci = lax.broadcasted_iota(jnp.int32, (tg, tg), 1)
    lbd = (((ri >> 6) == (ci >> 6)) & (ci <= ri)).astype(F32)
    gc_col = _dot_hi(lbd, g_all)
    gc_row = gc_col.T

    ii = lax.broadcasted_iota(jnp.int32, (CHUNK, CHUNK), 0)
    jj = lax.broadcasted_iota(jnp.int32, (CHUNK, CHUNK), 1)
    causal = jj <= ii
    strict = jj < ii
    scale = GDN_HEAD_DIM ** -0.5
    og = og_ref[...]

    for h in range(GDN_HEADS):
        def conv_cols(part, h=h):
            c0 = part * D_GDN + h * GDN_HEAD_DIM
            acc = ext_s[pl.ds(5, tg), c0:c0 + GDN_HEAD_DIM] * convw_ref[0:1, c0:c0 + GDN_HEAD_DIM]
            for j in range(1, GDN_CONV):
                acc = acc + ext_s[pl.ds(5 + j, tg), c0:c0 + GDN_HEAD_DIM] * convw_ref[j:j + 1, c0:c0 + GDN_HEAD_DIM]
            return _silu(acc)

        q_all = _l2norm(conv_cols(0)) * scale
        k_all = _l2norm(conv_cols(1))
        v_all = conv_cols(2)
        z_all = z_ref[:, h * GDN_HEAD_DIM:(h + 1) * GDN_HEAD_DIM]
        state = s_s[h]
        for c in range(tg // CHUNK):
            rows = slice(c * CHUNK, (c + 1) * CHUNK)
            q = q_all[rows]
            k = k_all[rows]
            v = v_all[rows]
            beta = beta_all[rows, h:h + 1]
            gcc = gc_col[rows, GDN_HEADS + h:GDN_HEADS + h + 1]
            gcr = gc_row[GDN_HEADS + h:GDN_HEADS + h + 1, rows]
            decay = jnp.exp(jnp.where(causal, gcc - gcr, -jnp.inf))
            kb = k * beta
            vb = v * beta
            kbf = k.astype(BF16)
            lmat = jnp.where(strict, _dot_nt(kb.astype(BF16), kbf) * decay, 0.0)
            tinv = _tri_inverse(lmat, ii, jj)
            egc = jnp.exp(gcc)
            u_c = _dot_hi(tinv, vb)
            w_c = _dot_hi(tinv, kb * egc)
            attn = jnp.where(causal, _dot_nt(q.astype(BF16), kbf) * decay, 0.0)
            g_last = gcc[CHUNK - 1:CHUNK, :]
            qg = q * egc
            kg = k * jnp.exp(g_last - gcc)
            sb = state.astype(BF16)
            v_new = u_c - _dot(w_c.astype(BF16), sb)
            vnb = v_new.astype(BF16)
            o = _dot(qg.astype(BF16), sb) + _dot(attn.astype(BF16), vnb)
            state = state * jnp.exp(g_last) + _dot(kg.T.astype(BF16), vnb)
            y_ref[rows, h * GDN_HEAD_DIM:(h + 1) * GDN_HEAD_DIM] = _out_gate(o, og, z_all[rows])
        s_s[h] = state

    ext_s[0:8, :] = ext_s[pl.ds(tg, 8), :]

    @pl.when(ti == nt - 1)
    def _fin():
        sfin_ref[0] = s_s[...]
        convout_ref[0] = ext_s[pl.ds(8 - (GDN_CONV - 1), GDN_CONV - 1), :]


def _gdn_prompt(qkv, z, ba, gw, bsz, seq, tg):
    nt = seq // tg
    row = lambda b, i: (b * nt + i, 0)
    const = lambda b, i: (0, 0)
    return pl.pallas_call(
        functools.partial(_gdn_kernel, tg=tg),
        grid=(bsz, nt),
        in_specs=[pl.BlockSpec((tg, 3 * D_GDN), row),
                  pl.BlockSpec((tg, D_GDN), row),
                  pl.BlockSpec((tg, GATE_PAD), row),
                  pl.BlockSpec((GDN_CONV, 3 * D_GDN), const),
                  pl.BlockSpec((1, GATE_PAD), const),
                  pl.BlockSpec((1, GATE_PAD), const),
                  pl.BlockSpec((1, GDN_HEAD_DIM), const)],
        out_specs=[pl.BlockSpec((tg, D_GDN), row),
                   pl.BlockSpec((1, GDN_HEADS, GDN_HEAD_DIM, GDN_HEAD_DIM), lambda b, i: (b, 0, 0, 0)),
                   pl.BlockSpec((1, GDN_CONV - 1, 3 * D_GDN), lambda b, i: (b, 0, 0))],
        out_shape=[jax.ShapeDtypeStruct((bsz * seq, D_GDN), F32),
                   jax.ShapeDtypeStruct((bsz, GDN_HEADS, GDN_HEAD_DIM, GDN_HEAD_DIM), F32),
                   jax.ShapeDtypeStruct((bsz, GDN_CONV - 1, 3 * D_GDN), F32)],
        scratch_shapes=[pltpu.VMEM((GDN_HEADS, GDN_HEAD_DIM, GDN_HEAD_DIM), F32),
                        pltpu.VMEM((tg + 8, 3 * D_GDN), F32)],
        compiler_params=pltpu.CompilerParams(dimension_semantics=("arbitrary", "arbitrary"),
                                             vmem_limit_bytes=VMEM_LIMIT),
        name="gdn_prompt",
    )(qkv, z, ba, gw["conv_w"], gw["a_log"], gw["dt_bias"], gw["onorm_g"])


def _gdn_step_kernel(qkv_ref, st_ref, z_ref, ba_ref, s0_ref, convw_ref, alog_ref, dtb_ref, og_ref,
                     y_ref, s1_ref, *, bb):
    beta_all, g_all = _gates(ba_ref[...], alog_ref[...], dtb_ref[...])
    alpha_all = jnp.exp(g_all)
    scale = GDN_HEAD_DIM ** -0.5
    og = og_ref[...]

    for h in range(GDN_HEADS):
        def conv_cols(part, h=h):
            c0 = part * D_GDN + h * GDN_HEAD_DIM
            cs = slice(c0, c0 + GDN_HEAD_DIM)
            acc = qkv_ref[:, cs] * convw_ref[GDN_CONV - 1:GDN_CONV, cs]
            for j in range(GDN_CONV - 1):
                acc = acc + st_ref[j, :, cs] * convw_ref[j:j + 1, cs]
            return _silu(acc)

        q = _l2norm(conv_cols(0)) * scale
        k = _l2norm(conv_cols(1))
        v = conv_cols(2)
        beta = beta_all[:, h:h + 1]
        alpha = alpha_all[:, GDN_HEADS + h:GDN_HEADS + h + 1]
        qk = jnp.sum(q * k, axis=-1, keepdims=True)
        kt = k.T
        qt = q.T
        o_rows = []
        for b in range(bb):
            s0 = s0_ref[b, h]
            kcol = kt[:, b:b + 1]
            qcol = qt[:, b:b + 1]
            ks = jnp.sum(kcol * s0, axis=0, keepdims=True)
            qs = jnp.sum(qcol * s0, axis=0, keepdims=True)
            al = alpha[b:b + 1, :]
            v_new = beta[b:b + 1, :] * (v[b:b + 1, :] - al * ks)
            o_rows.append(al * qs + qk[b:b + 1, :] * v_new)
            s1_ref[b, h] = al * s0 + kcol * v_new
        o = jnp.concatenate(o_rows, axis=0)
        cs = slice(h * GDN_HEAD_DIM, (h + 1) * GDN_HEAD_DIM)
        y_ref[:, cs] = _out_gate(o, og, z_ref[:, cs])


def _gdn_sample(qkv, st_t, z, ba, s0, gw, bb):
    n = qkv.shape[0]
    row = lambda i: (i, 0)
    const = lambda i: (0, 0)
    state = lambda i: (i, 0, 0, 0)
    return pl.pallas_call(
        functools.partial(_gdn_step_kernel, bb=bb),
        grid=(n // bb,),
        in_specs=[pl.BlockSpec((bb, 3 * D_GDN), row),
                  pl.BlockSpec((GDN_CONV - 1, bb, 3 * D_GDN), lambda i: (0, i, 0)),
                  pl.BlockSpec((bb, D_GDN), row),
                  pl.BlockSpec((bb, GATE_PAD), row),
                  pl.BlockSpec((bb, GDN_HEADS, GDN_HEAD_DIM, GDN_HEAD_DIM), state),
                  pl.BlockSpec((GDN_CONV, 3 * D_GDN), const),
                  pl.BlockSpec((1, GATE_PAD), const),
                  pl.BlockSpec((1, GATE_PAD), const),
                  pl.BlockSpec((1, GDN_HEAD_DIM), const)],
        out_specs=[pl.BlockSpec((bb, D_GDN), row),
                   pl.BlockSpec((bb, GDN_HEADS, GDN_HEAD_DIM, GDN_HEAD_DIM), state)],
        out_shape=[jax.ShapeDtypeStruct((n, D_GDN), F32),
                   jax.ShapeDtypeStruct(s0.shape, F32)],
        compiler_params=pltpu.CompilerParams(dimension_semantics=("arbitrary",),
                                             vmem_limit_bytes=VMEM_LIMIT),
        name="gdn_sample",
    )(qkv, st_t, z, ba, s0, gw["conv_w"], gw["a_log"], gw["dt_bias"], gw["onorm_g"])


def _residual_in(x_ref, ys5_ref, ygdn_ref, wout_ref, g2_ref, x1_s, n2_s):
    x1 = (x_ref[...] + _dot(ys5_ref[...].astype(BF16), wout_ref[:D_S5, :])
          + _dot(ygdn_ref[...].astype(BF16), wout_ref[D_S5:, :]))
    x1_s[...] = x1
    n2_s[...] = _rms(x1, g2_ref[...]).astype(BF16)


def _ffn_prompt_kernel(x_ref, ys5_ref, ygdn_ref, wout_ref, g2_ref, wup_ref, cw_ref, wdn_ref, gf_ref,
                       y_ref, hlast_ref, x1_s, n2_s, acc_s, ext_s, carry_s, *, tm):
    ti = pl.program_id(1)
    nt = pl.num_programs(1)

    @pl.when(ti == 0)
    def _reset():
        carry_s[...] = jnp.zeros_like(carry_s)

    _residual_in(x_ref, ys5_ref, ygdn_ref, wout_ref, g2_ref, x1_s, n2_s)
    n2 = n2_s[...]
    for f in range(D_FF // FFN_COLS):
        halves = []
        for part in range(2):
            c0 = part * D_FF + f * FFN_COLS
            cs = slice(c0, c0 + FFN_COLS)
            hcur = _dot(n2, wup_ref[:, cs])
            es = slice(part * FFN_COLS, (part + 1) * FFN_COLS)
            ext_s[0:8, es] = carry_s[:, cs]
            ext_s[8:, es] = hcur
            carry_s[:, cs] = hcur[tm - 8:, :]
            conv = (ext_s[pl.ds(6, tm), es] * cw_ref[0:1, cs]
                    + ext_s[pl.ds(7, tm), es] * cw_ref[1:2, cs]
                    + hcur * cw_ref[2:3, cs])
            halves.append(conv)
        act = (_silu(halves[0]) * halves[1]).astype(BF16)
        contrib = _dot(act, wdn_ref[f * FFN_COLS:(f + 1) * FFN_COLS, :])
        if f == 0:
            acc_s[...] = contrib
        else:
            acc_s[...] += contrib
    y_ref[...] = _rms(x1_s[...] + acc_s[...], gf_ref[...])

    @pl.when(ti == nt - 1)
    def _fin():
        hlast_ref[0] = carry_s[8 - (FFN_CONV - 1):, :]


def _ffn_prompt(x, ys5, ygdn, fw, bsz, seq, tm):
    nt = seq // tm
    row = lambda b, i: (b * nt + i, 0)
    const = lambda b, i: (0, 0)
    return pl.pallas_call(
        functools.partial(_ffn_prompt_kernel, tm=tm),
        grid=(bsz, nt),
        in_specs=[pl.BlockSpec((tm, D_MODEL), row),
                  pl.BlockSpec((tm, D_S5), row),
                  pl.BlockSpec((tm, D_GDN), row),
                  pl.BlockSpec((D_MODEL, D_MODEL), const),
                  pl.BlockSpec((1, D_MODEL), const),
                  pl.BlockSpec((D_MODEL, 2 * D_FF), const),
                  pl.BlockSpec((FFN_CONV, 2 * D_FF), const),
                  pl.BlockSpec((D_FF, D_MODEL), const),
                  pl.BlockSpec((1, D_MODEL), const)],
        out_specs=[pl.BlockSpec((tm, D_MODEL), row),
                   pl.BlockSpec((1, FFN_CONV - 1, 2 * D_FF), lambda b, i: (b, 0, 0))],
        out_shape=[jax.ShapeDtypeStruct((bsz * seq, D_MODEL), F32),
                   jax.ShapeDtypeStruct((bsz, FFN_CONV - 1, 2 * D_FF), F32)],
        scratch_shapes=[pltpu.VMEM((tm, D_MODEL), F32),
                        pltpu.VMEM((tm, D_MODEL), BF16),
                        pltpu.VMEM((tm, D_MODEL), F32),
                        pltpu.VMEM((tm + 8, 2 * FFN_COLS), F32),
                        pltpu.VMEM((8, 2 * D_FF), F32)],
        compiler_params=pltpu.CompilerParams(dimension_semantics=("arbitrary", "arbitrary"),
                                             vmem_limit_bytes=VMEM_LIMIT),
        name="ffn_prompt",
    )(x, ys5, ygdn, fw["w_out"], fw["norm2_g"], fw["w_up"], fw["conv_w"], fw["w_down"], fw["normf_g"])


def _ffn_sample_kernel(x_ref, ys5_ref, ygdn_ref, st_ref, wout_ref, g2_ref, wup_ref, cw_ref, wdn_ref, gf_ref,
                       y_ref, h_ref, x1_s, n2_s, acc_s):
    _residual_in(x_ref, ys5_ref, ygdn_ref, wout_ref, g2_ref, x1_s, n2_s)
    n2 = n2_s[...]
    for f in range(D_FF // FFN_COLS):
        halves = []
        for part in range(2):
            c0 = part * D_FF + f * FFN_COLS
            cs = slice(c0, c0 + FFN_COLS)
            hcur = _dot(n2, wup_ref[:, cs])
            h_ref[:, cs] = hcur
            halves.append(st_ref[0, :, cs] * cw_ref[0:1, cs] + st_ref[1, :, cs] * cw_ref[1:2, cs]
                          + hcur * cw_ref[2:3, cs])
        act = (_silu(halves[0]) * halves[1]).astype(BF16)
        contrib = _dot(act, wdn_ref[f * FFN_COLS:(f + 1) * FFN_COLS, :])
        if f == 0:
            acc_s[...] = contrib
        else:
            acc_s[...] += contrib
    y_ref[...] = _rms(x1_s[...] + acc_s[...], gf_ref[...])


def _ffn_sample(x, ys5, ygdn, st_t, fw):
    n = x.shape[0]
    c2 = lambda i: (0, 0)
    c3 = lambda i: (0, 0, 0)
    return pl.pallas_call(
        _ffn_sample_kernel,
        grid=(1,),
        in_specs=[pl.BlockSpec((n, D_MODEL), c2),
                  pl.BlockSpec((n, D_S5), c2),
                  pl.BlockSpec((n, D_GDN), c2),
                  pl.BlockSpec((FFN_CONV - 1, n, 2 * D_FF), c3),
                  pl.BlockSpec((D_MODEL, D_MODEL), c2),
                  pl.BlockSpec((1, D_MODEL), c2),
                  pl.BlockSpec((D_MODEL, 2 * D_FF), c2),
                  pl.BlockSpec((FFN_CONV, 2 * D_FF), c2),
                  pl.BlockSpec((D_FF, D_MODEL), c2),
                  pl.BlockSpec((1, D_MODEL), c2)],
        out_specs=[pl.BlockSpec((n, D_MODEL), c2),
                   pl.BlockSpec((n, 2 * D_FF), c2)],
        out_shape=[jax.ShapeDtypeStruct((n, D_MODEL), F32),
                   jax.ShapeDtypeStruct((n, 2 * D_FF), F32)],
        scratch_shapes=[pltpu.VMEM((n, D_MODEL), F32),
                        pltpu.VMEM((n, D_MODEL), BF16),
                        pltpu.VMEM((n, D_MODEL), F32)],
        compiler_params=pltpu.CompilerParams(dimension_semantics=("arbitrary",),
                                             vmem_limit_bytes=VMEM_LIMIT),
        name="ffn_sample",
    )(x, ys5, ygdn, st_t, fw["w_out"], fw["norm2_g"], fw["w_up"], fw["conv_w"], fw["w_down"], fw["normf_g"])


PROJ_ROWS = 512
S5_STEPS = 64
GDN_ROWS = 256
FFN_ROWS = 256
GDN_SAMPLE_ROWS = 8


def kernel(x_prompt, x_sample, state_s5_re, state_s5_im, state_gdn, state_gdn_conv, state_ffn_conv, norm1_g, w_in, s5_a_re, s5_a_im, s5_log_dt, s5_b_re, s5_b_im, s5_c_re, s5_c_im, s5_d, s5_w_glu, gdn_conv_w, gdn_a_log, gdn_dt_bias, gdn_onorm_g, w_out, norm2_g, ffn_w_up, ffn_conv_w, ffn_w_down, normf_g):
    depth = w_in.shape[0]
    assert depth == 1, "the final rmsnorm is fused into the last layer's ffn kernel"
    bsz, seq, _ = x_prompt.shape
    nsmp = x_sample.shape[0]
    assert x_sample.shape[1] == 1
    l = 0

    xp = x_prompt.reshape(bsz * seq, D_MODEL)
    xs = x_sample.reshape(nsmp, D_MODEL)

    g1 = norm1_g[l].reshape(1, D_MODEL)
    w_in_bf = jnp.pad(w_in[l], ((0, 0), (0, D_IN_PAD - D_IN))).astype(BF16)
    sw = _s5_weights(s5_a_re[l], s5_a_im[l], s5_log_dt[l], s5_b_re[l], s5_b_im[l], s5_c_re[l], s5_c_im[l],
                     s5_d[l], s5_w_glu[l])
    gate_row = lambda v: jnp.pad(v, (GDN_HEADS, GATE_PAD - 2 * GDN_HEADS)).reshape(1, GATE_PAD)
    gw = {"conv_w": gdn_conv_w[l], "a_log": gate_row(gdn_a_log[l]), "dt_bias": gate_row(gdn_dt_bias[l]),
          "onorm_g": gdn_onorm_g[l].reshape(1, GDN_HEAD_DIM)}
    fw = {"w_out": w_out[l].astype(BF16), "norm2_g": norm2_g[l].reshape(1, D_MODEL),
          "w_up": ffn_w_up[l].astype(BF16), "conv_w": ffn_conv_w[l], "w_down": ffn_w_down[l].astype(BF16),
          "normf_g": normf_g.reshape(1, D_MODEL)}

    u_p, qkv_p, z_p, ba_p = _proj(xp, g1, w_in_bf, PROJ_ROWS)
    u_tb = u_p.reshape(bsz, seq, D_S5).transpose(1, 0, 2).reshape(seq * bsz, D_S5)
    zeros_h = jnp.zeros((bsz, S5_CH), F32)
    ys5_tb, p_hr, p_hi = _s5(u_tb, zeros_h, zeros_h, sw, bsz, S5_STEPS)
    ys5_p = ys5_tb.reshape(seq, bsz, D_S5).transpose(1, 0, 2).reshape(bsz * seq, D_S5)
    ygdn_p, p_gdn, p_gdn_conv = _gdn_prompt(qkv_p, z_p, ba_p, gw, bsz, seq, GDN_ROWS)
    y_p, p_ffn_conv = _ffn_prompt(xp, ys5_p, ygdn_p, fw, bsz, seq, FFN_ROWS)

    u_s, qkv_s, z_s, ba_s = _proj(xs, g1, w_in_bf, nsmp)
    ys5_s, s_hr, s_hi = _s5(u_s, state_s5_re[l].reshape(nsmp, S5_CH), state_s5_im[l].reshape(nsmp, S5_CH),
                            sw, nsmp, 1)
    gconv_t = state_gdn_conv[l].transpose(1, 0, 2)
    ygdn_s, s_gdn = _gdn_sample(qkv_s, gconv_t, z_s, ba_s, state_gdn[l], gw, GDN_SAMPLE_ROWS)
    fconv_t = state_ffn_conv[l].transpose(1, 0, 2)
    y_s, h_s = _ffn_sample(xs, ys5_s, ygdn_s, fconv_t, fw)
    s_gdn_conv = jnp.concatenate([state_gdn_conv[l][:, 1:], qkv_s[:, None, :]], axis=1)
    s_ffn_conv = jnp.concatenate([state_ffn_conv[l][:, 1:], h_s[:, None, :]], axis=1)

    st = lambda a: a[None]
    s5_shape = lambda a, n: a.reshape(1, n, S5_N_GROUPS, S5_STATE)
    return (y_p.reshape(bsz, seq, D_MODEL), y_s.reshape(nsmp, 1, D_MODEL),
            s5_shape(p_hr, bsz), s5_shape(p_hi, bsz), st(p_gdn), st(p_gdn_conv), st(p_ffn_conv),
            s5_shape(s_hr, nsmp), s5_shape(s_hi, nsmp), st(s_gdn), st(s_gdn_conv), st(s_ffn_conv))
```

```python
import functools

import jax
import jax.numpy as jnp
from jax import lax
from jax.experimental import pallas as pl
from jax.experimental.pallas import tpu as pltpu

F32 = jnp.float32
BF16 = jnp.bfloat16
NORM_EPS = 1e-6

D_MODEL = 1024
D_S5 = 512
S5_GROUP = 16
S5_N_GROUPS = 32
S5_STATE = 64
S5_CH = S5_N_GROUPS * S5_STATE
D_GDN = 512
GDN_HEADS = 4
GDN_HEAD_DIM = 128
GDN_CONV = 4
D_FF = 2816
FFN_CONV = 3
D_IN = D_S5 + 4 * D_GDN + 2 * GDN_HEADS
GATE_PAD = 128
D_IN_PAD = D_S5 + 4 * D_GDN + GATE_PAD

CHUNK = 64
FFN_COLS = 256
VMEM_LIMIT = 56 * 1024 * 1024


def _dot(a, b):
    return jnp.dot(a, b, preferred_element_type=F32)


def _dot_hi(a, b):
    return jnp.dot(a, b, preferred_element_type=F32, precision=lax.Precision.HIGHEST)


def _dot_nt(a, b):
    return lax.dot_general(a, b, (((1,), (1,)), ((), ())), preferred_element_type=F32)


def _rms(x, g):
    ms = jnp.mean(x * x, axis=-1, keepdims=True)
    return x * lax.rsqrt(ms + NORM_EPS) * g


def _silu(x):
    return x * jax.nn.sigmoid(x)


def _softplus(x):
    return jnp.maximum(x, 0.0) + jnp.log1p(jnp.exp(-jnp.abs(x)))


def _proj_kernel(x_ref, g_ref, w_ref, u_ref, qkv_ref, z_ref, ba_ref):
    n = _rms(x_ref[...], g_ref[...]).astype(BF16)
    o0 = D_S5
    o1 = o0 + 3 * D_GDN
    o2 = o1 + D_GDN
    u_ref[...] = _dot(n, w_ref[:, :o0])
    qkv_ref[...] = _dot(n, w_ref[:, o0:o1])
    z_ref[...] = _dot(n, w_ref[:, o1:o2])
    ba_ref[...] = _dot(n, w_ref[:, o2:])


def _proj(x, g1, w_in_bf, tm):
    n = x.shape[0]
    row = lambda i: (i, 0)
    const = lambda i: (0, 0)
    return pl.pallas_call(
        _proj_kernel,
        grid=(n // tm,),
        in_specs=[pl.BlockSpec((tm, D_MODEL), row),
                  pl.BlockSpec((1, D_MODEL), const),
                  pl.BlockSpec((D_MODEL, D_IN_PAD), const)],
        out_specs=[pl.BlockSpec((tm, D_S5), row),
                   pl.BlockSpec((tm, 3 * D_GDN), row),
                   pl.BlockSpec((tm, D_GDN), row),
                   pl.BlockSpec((tm, GATE_PAD), row)],
        out_shape=[jax.ShapeDtypeStruct((n, D_S5), F32),
                   jax.ShapeDtypeStruct((n, 3 * D_GDN), F32),
                   jax.ShapeDtypeStruct((n, D_GDN), F32),
                   jax.ShapeDtypeStruct((n, GATE_PAD), F32)],
        compiler_params=pltpu.CompilerParams(dimension_semantics=("arbitrary",),
                                             vmem_limit_bytes=VMEM_LIMIT),
        name="proj",
    )(x, g1, w_in_bf)


S5_HALF = S5_CH // 2
S5_COLS = 512


def _s5_kernel(u_ref, h0r_ref, h0i_ref, are_ref, aim_ref, ldt_ref, bre_ref, bim_ref, cre_ref, cim_ref,
               d_ref, wglu_ref, y_ref, hr_out, hi_out,
               abr_s, abi_s, wbr_s, wbi_s, hr_s, hi_s, xr_s, xi_s, *, nb, tt):
    @pl.when(pl.program_id(0) == 0)
    def _init():
        ar = are_ref[...]
        ai = aim_ref[...]
        dt = jnp.exp(ldt_ref[...])
        mag = jnp.exp(ar * dt)
        abr = mag * jnp.cos(ai * dt)
        abi = mag * jnp.sin(ai * dt)
        den = ar * ar + ai * ai
        p = abr - 1.0
        fr = (p * ar + abi * ai) / den
        fi = (abi * ar - p * ai) / den
        abr_s[...] = abr
        abi_s[...] = abi
        for m in range(2):
            frm = fr[:, m * S5_HALF:(m + 1) * S5_HALF]
            fim = fi[:, m * S5_HALF:(m + 1) * S5_HALF]
            wr = bre_ref[m]
            wi = bim_ref[m]
            wbr_s[m] = (wr * frm - wi * fim).astype(BF16)
            wbi_s[m] = (wr * fim + wi * frm).astype(BF16)
        hr_s[...] = h0r_ref[...]
        hi_s[...] = h0i_ref[...]

    u = u_ref[...]
    ub = u.astype(BF16)
    half = D_S5 // 2
    for m in range(2):
        um = ub[:, m * half:(m + 1) * half]
        xr_s[:, m * S5_HALF:(m + 1) * S5_HALF] = _dot(um, wbr_s[m])
        xi_s[:, m * S5_HALF:(m + 1) * S5_HALF] = _dot(um, wbi_s[m])

    for cb in range(S5_CH // S5_COLS):
        cols = slice(cb * S5_COLS, (cb + 1) * S5_COLS)
        a_r = jnp.broadcast_to(abr_s[:, cols], (nb, S5_COLS))
        a_i = jnp.broadcast_to(abi_s[:, cols], (nb, S5_COLS))

        def step(t, carry, cols=cols, a_r=a_r, a_i=a_i):
            hr, hi = carry
            r0 = pl.multiple_of(t * nb, nb)
            nhr = a_r * hr - a_i * hi + xr_s[pl.ds(r0, nb), cols]
            nhi = a_r * hi + a_i * hr + xi_s[pl.ds(r0, nb), cols]
            xr_s[pl.ds(r0, nb), cols] = nhr
            xi_s[pl.ds(r0, nb), cols] = nhi
            return nhr, nhi

        hr, hi = lax.fori_loop(0, tt, step, (hr_s[:, cols], hi_s[:, cols]), unroll=min(tt, 8))
        hr_s[:, cols] = hr
        hi_s[:, cols] = hi

    hr_out[...] = hr_s[...]
    hi_out[...] = hi_s[...]

    ys = []
    for m in range(2):
        hre = xr_s[:, m * S5_HALF:(m + 1) * S5_HALF].astype(BF16)
        him = xi_s[:, m * S5_HALF:(m + 1) * S5_HALF].astype(BF16)
        ych = _dot(hre, cre_ref[m]) - _dot(him, cim_ref[m])
        ys.append(ych + d_ref[:, m * half:(m + 1) * half] * u[:, m * half:(m + 1) * half])
    y = jax.nn.gelu(jnp.concatenate(ys, axis=-1)).astype(BF16)
    gl = _dot(y, wglu_ref[...])
    y_ref[...] = gl[:, :D_S5] * jax.nn.sigmoid(gl[:, D_S5:])


def _s5(u_tb, h0r, h0i, sw, nb, tt):
    n = u_tb.shape[0]
    rows = nb * tt
    const2 = lambda i: (0, 0)
    const3 = lambda i: (0, 0, 0)
    full2 = lambda shape: pl.BlockSpec(shape, const2)
    full3 = lambda shape: pl.BlockSpec(shape, const3)
    return pl.pallas_call(
        functools.partial(_s5_kernel, nb=nb, tt=tt),
        grid=(n // rows,),
        in_specs=[pl.BlockSpec((rows, D_S5), lambda i: (i, 0)),
                  full2((nb, S5_CH)), full2((nb, S5_CH)),
                  full2((1, S5_CH)), full2((1, S5_CH)), full2((1, S5_CH)),
                  full3((2, D_S5 // 2, S5_HALF)), full3((2, D_S5 // 2, S5_HALF)),
                  full3((2, S5_HALF, D_S5 // 2)), full3((2, S5_HALF, D_S5 // 2)),
                  full2((1, D_S5)), full2((D_S5, 2 * D_S5))],
        out_specs=[pl.BlockSpec((rows, D_S5), lambda i: (i, 0)),
                   full2((nb, S5_CH)), full2((nb, S5_CH))],
        out_shape=[jax.ShapeDtypeStruct((n, D_S5), F32),
                   jax.ShapeDtypeStruct((nb, S5_CH), F32),
                   jax.ShapeDtypeStruct((nb, S5_CH), F32)],
        scratch_shapes=[pltpu.VMEM((1, S5_CH), F32), pltpu.VMEM((1, S5_CH), F32),
                        pltpu.VMEM((2, D_S5 // 2, S5_HALF), BF16), pltpu.VMEM((2, D_S5 // 2, S5_HALF), BF16),
                        pltpu.VMEM((nb, S5_CH), F32), pltpu.VMEM((nb, S5_CH), F32),
                        pltpu.VMEM((rows, S5_CH), F32), pltpu.VMEM((rows, S5_CH), F32)],
        compiler_params=pltpu.CompilerParams(dimension_semantics=("arbitrary",),
                                             vmem_limit_bytes=VMEM_LIMIT),
        name="s5",
    )(u_tb, h0r, h0i, sw["a_re"], sw["a_im"], sw["log_dt"], sw["b_re"], sw["b_im"],
      sw["c_re"], sw["c_im"], sw["d"], sw["w_glu"])


def _s5_weights(a_re, a_im, log_dt, b_re, b_im, c_re, c_im, d, w_glu):
    g, n, c = S5_N_GROUPS, S5_STATE, S5_GROUP
    eye = jnp.eye(g // 2, dtype=F32)

    def b_blockdiag(b):
        bb = b.reshape(2, g // 2, n, c)
        return jnp.einsum("mgnc,gh->mgchn", bb, eye).reshape(2, (g // 2) * c, (g // 2) * n)

    def c_blockdiag(cm):
        cc = cm.reshape(2, g // 2, c, n)
        return jnp.einsum("mgcn,gh->mgnhc", cc, eye).reshape(2, (g // 2) * n, (g // 2) * c)

    return {
        "a_re": a_re.reshape(1, S5_CH), "a_im": a_im.reshape(1, S5_CH),
        "log_dt": jnp.repeat(log_dt, n).reshape(1, S5_CH),
        "b_re": b_blockdiag(b_re), "b_im": b_blockdiag(b_im),
        "c_re": c_blockdiag(c_re).astype(BF16), "c_im": c_blockdiag(c_im).astype(BF16),
        "d": d.reshape(1, D_S5), "w_glu": w_glu.astype(BF16),
    }


def _l2norm(x):
    return x * lax.rsqrt(jnp.sum(x * x, axis=-1, keepdims=True) + NORM_EPS)


def _gates(ba, alog_row, dtb_row):
    beta = jax.nn.sigmoid(ba)
    g = -jnp.exp(alog_row) * _softplus(ba + dtb_row)
    return beta, g


def _out_gate(o, og_row, z):
    return _rms(o, og_row) * _silu(z)


def _split(a):
    hi = a.astype(BF16)
    lo = (a - hi.astype(F32)).astype(BF16)
    return hi, lo


def _bdot3(a, b):
    ah, al = a
    bh, bl = b
    e = lambda x, y: jnp.einsum("bij,bjk->bik", x, y, preferred_element_type=F32)
    return e(ah, bh) + e(ah, bl) + e(al, bh)


def _tri_inverse(lmat, ii, jj):
    eye = (ii == jj).astype(F32)
    t = eye[None] - jnp.where(((ii >> 1) == (jj >> 1))[None], lmat, 0.0)
    shift = 1
    while (1 << shift) < CHUNK:
        same_big = (ii >> (shift + 1)) == (jj >> (shift + 1))
        same_small = (ii >> shift) == (jj >> shift)
        m = (same_big & jnp.logical_not(same_small))[None]
        c = _split(jnp.where(m, lmat, 0.0))
        ts = _split(t)
        t = t - _bdot3(_split(_bdot3(ts, c)), ts)
        shift += 1
    return t


def _gdn_kernel(qkv_ref, z_ref, ba_ref, convw_ref, alog_ref, dtb_ref, og_ref,
                y_ref, sfin_ref, convout_ref, s_s, ext_s, u_s, wq_s, attn_s, kgt_s, *, tg):
    ti = pl.program_id(1)
    nt = pl.num_programs(1)

    @pl.when(ti == 0)
    def _reset():
        s_s[...] = jnp.zeros_like(s_s)
        ext_s[0:8, :] = jnp.zeros((8, 3 * D_GDN), F32)

    ext_s[8:, :] = qkv_ref[...]

    ba = ba_ref[...]
    beta_all, g_all = _gates(ba, alog_ref[...], dtb_ref[...])

    ri = lax.broadcasted_iota(jnp.int32, (tg, tg), 0)
    ci = lax.broadcasted_iota(jnp.int32, (tg, tg), 1)
    lbd = (((ri >> 6) == (ci >> 6)) & (ci <= ri)).astype(F32)
    gc_col = _dot_hi(lbd, g_all)
    gc_row = gc_col.T

    ii = lax.broadcasted_iota(jnp.int32, (CHUNK, CHUNK), 0)
    jj = lax.broadcasted_iota(jnp.int32, (CHUNK, CHUNK), 1)
    causal = jj <= ii
    strict = jj < ii
    scale = GDN_HEAD_DIM ** -0.5
    og = og_ref[...]

    nc = tg // CHUNK
    hcols = lambda h: slice(h * GDN_HEAD_DIM, (h + 1) * GDN_HEAD_DIM)

    lmats = [None] * (nc * GDN_HEADS)
    rhss = [None] * (nc * GDN_HEADS)
    g_lasts = [None] * (nc * GDN_HEADS)
    for h in range(GDN_HEADS):
        def conv_cols(part, h=h):
            c0 = part * D_GDN + h * GDN_HEAD_DIM
            acc = ext_s[pl.ds(5, tg), c0:c0 + GDN_HEAD_DIM] * convw_ref[0:1, c0:c0 + GDN_HEAD_DIM]
            for j in range(1, GDN_CONV):
                acc = acc + ext_s[pl.ds(5 + j, tg), c0:c0 + GDN_HEAD_DIM] * convw_ref[j:j + 1, c0:c0 + GDN_HEAD_DIM]
            return _silu(acc)

        q_all = _l2norm(conv_cols(0)) * scale
        k_all = _l2norm(conv_cols(1))
        v_all = conv_cols(2)
        for c in range(nc):
            sys = c * GDN_HEADS + h
            rows = slice(c * CHUNK, (c + 1) * CHUNK)
            q = q_all[rows]
            k = k_all[rows]
            v = v_all[rows]
            beta = beta_all[rows, h:h + 1]
            gcc = gc_col[rows, GDN_HEADS + h:GDN_HEADS + h + 1]
            gcr = gc_row[GDN_HEADS + h:GDN_HEADS + h + 1, rows]
            decay = jnp.exp(jnp.where(causal, gcc - gcr, -jnp.inf))
            kb = k * beta
            egc = jnp.exp(gcc)
            g_last = gcc[CHUNK - 1:CHUNK, :]
            kq = _dot_nt(jnp.concatenate([kb, q], axis=0).astype(BF16), k.astype(BF16))
            lmats[sys] = jnp.where(strict, kq[:CHUNK] * decay, 0.0)
            attn_s[sys] = jnp.where(causal, kq[CHUNK:] * decay, 0.0).astype(BF16)
            rhss[sys] = jnp.concatenate([v * beta, kb * egc], axis=1)
            wq_s[sys, CHUNK:, :] = (q * egc).astype(BF16)
            kgt_s[sys] = (k * jnp.exp(g_last - gcc)).T.astype(BF16)
            g_lasts[sys] = jnp.exp(g_last)

    tinv = _tri_inverse(jnp.stack(lmats), ii, jj)
    sol = _bdot3(_split(tinv), _split(jnp.stack(rhss)))
    for sys in range(nc * GDN_HEADS):
        c, h = divmod(sys, GDN_HEADS)
        u_s[c * CHUNK:(c + 1) * CHUNK, hcols(h)] = sol[sys, :, :GDN_HEAD_DIM]
        wq_s[sys, :CHUNK, :] = sol[sys, :, GDN_HEAD_DIM:].astype(BF16)

    states = [s_s[h] for h in range(GDN_HEADS)]
    for c in range(nc):
        rows = slice(c * CHUNK, (c + 1) * CHUNK)
        for h in range(GDN_HEADS):
            sys = c * GDN_HEADS + h
            r = _dot(wq_s[sys], states[h].astype(BF16))
            v_new = (u_s[rows, hcols(h)] - r[:CHUNK]).astype(BF16)
            o = r[CHUNK:] + _dot(attn_s[sys], v_new)
            states[h] = states[h] * g_lasts[sys] + _dot(kgt_s[sys], v_new)
            y_ref[rows, hcols(h)] = _out_gate(o, og, z_ref[rows, hcols(h)])
    for h in range(GDN_HEADS):
        s_s[h] = states[h]

    ext_s[0:8, :] = ext_s[pl.ds(tg, 8), :]

    @pl.when(ti == nt - 1)
    def _fin():
        sfin_ref[0] = s_s[...]
        convout_ref[0] = ext_s[pl.ds(8 - (GDN_CONV - 1), GDN_CONV - 1), :]


def _gdn_prompt(qkv, z, ba, gw, bsz, seq, tg):
    nt = seq // tg
    nsys = (tg // CHUNK) * GDN_HEADS
    row = lambda b, i: (b * nt + i, 0)
    const = lambda b, i: (0, 0)
    return pl.pallas_call(
        functools.partial(_gdn_kernel, tg=tg),
        grid=(bsz, nt),
        in_specs=[pl.BlockSpec((tg, 3 * D_GDN), row),
                  pl.BlockSpec((tg, D_GDN), row),
                  pl.BlockSpec((tg, GATE_PAD), row),
                  pl.BlockSpec((GDN_CONV, 3 * D_GDN), const),
                  pl.BlockSpec((1, GATE_PAD), const),
                  pl.BlockSpec((1, GATE_PAD), const),
                  pl.BlockSpec((1, GDN_HEAD_DIM), const)],
        out_specs=[pl.BlockSpec((tg, D_GDN), row),
                   pl.BlockSpec((1, GDN_HEADS, GDN_HEAD_DIM, GDN_HEAD_DIM), lambda b, i: (b, 0, 0, 0)),
                   pl.BlockSpec((1, GDN_CONV - 1, 3 * D_GDN), lambda b, i: (b, 0, 0))],
        out_shape=[jax.ShapeDtypeStruct((bsz * seq, D_GDN), F32),
                   jax.ShapeDtypeStruct((bsz, GDN_HEADS, GDN_HEAD_DIM, GDN_HEAD_DIM), F32),
                   jax.ShapeDtypeStruct((bsz, GDN_CONV - 1, 3 * D_GDN), F32)],
        scratch_shapes=[pltpu.VMEM((GDN_HEADS, GDN_HEAD_DIM, GDN_HEAD_DIM), F32),
                        pltpu.VMEM((tg + 8, 3 * D_GDN), F32),
                        pltpu.VMEM((tg, D_GDN), F32),
                        pltpu.VMEM((nsys, 2 * CHUNK, GDN_HEAD_DIM), BF16),
                        pltpu.VMEM((nsys, CHUNK, CHUNK), BF16),
                        pltpu.VMEM((nsys, GDN_HEAD_DIM, CHUNK), BF16)],
        compiler_params=pltpu.CompilerParams(dimension_semantics=("arbitrary", "arbitrary"),
                                             vmem_limit_bytes=VMEM_LIMIT),
        name="gdn_prompt",
    )(qkv, z, ba, gw["conv_w"], gw["a_log"], gw["dt_bias"], gw["onorm_g"])


def _gdn_step_kernel(qkv_ref, st_ref, z_ref, ba_ref, s0_ref, convw_ref, alog_ref, dtb_ref, og_ref,
                     y_ref, s1_ref, *, bb):
    beta_all, g_all = _gates(ba_ref[...], alog_ref[...], dtb_ref[...])
    alpha_all = jnp.exp(g_all)
    scale = GDN_HEAD_DIM ** -0.5
    og = og_ref[...]

    for h in range(GDN_HEADS):
        def conv_cols(part, h=h):
            c0 = part * D_GDN + h * GDN_HEAD_DIM
            cs = slice(c0, c0 + GDN_HEAD_DIM)
            acc = qkv_ref[:, cs] * convw_ref[GDN_CONV - 1:GDN_CONV, cs]
            for j in range(GDN_CONV - 1):
                acc = acc + st_ref[j, :, cs] * convw_ref[j:j + 1, cs]
            return _silu(acc)

        q = _l2norm(conv_cols(0)) * scale
        k = _l2norm(conv_cols(1))
        v = conv_cols(2)
        beta = beta_all[:, h:h + 1]
        alpha = alpha_all[:, GDN_HEADS + h:GDN_HEADS + h + 1]
        qk = jnp.sum(q * k, axis=-1, keepdims=True)
        kt = k.T
        qt = q.T
        o_rows = []
        for b in range(bb):
            s0 = s0_ref[b, h]
            kcol = kt[:, b:b + 1]
            qcol = qt[:, b:b + 1]
            ks = jnp.sum(kcol * s0, axis=0, keepdims=True)
            qs = jnp.sum(qcol * s0, axis=0, keepdims=True)
            al = alpha[b:b + 1, :]
            v_new = beta[b:b + 1, :] * (v[b:b + 1, :] - al * ks)
            o_rows.append(al * qs + qk[b:b + 1, :] * v_new)
            s1_ref[b, h] = al * s0 + kcol * v_new
        o = jnp.concatenate(o_rows, axis=0)
        cs = slice(h * GDN_HEAD_DIM, (h + 1) * GDN_HEAD_DIM)
        y_ref[:, cs] = _out_gate(o, og, z_ref[:, cs])


def _gdn_sample(qkv, st_t, z, ba, s0, gw, bb):
    n = qkv.shape[0]
    row = lambda i: (i, 0)
    const = lambda i: (0, 0)
    state = lambda i: (i, 0, 0, 0)
    return pl.pallas_call(
        functools.partial(_gdn_step_kernel, bb=bb),
        grid=(n // bb,),
        in_specs=[pl.BlockSpec((bb, 3 * D_GDN), row),
                  pl.BlockSpec((GDN_CONV - 1, bb, 3 * D_GDN), lambda i: (0, i, 0)),
                  pl.BlockSpec((bb, D_GDN), row),
                  pl.BlockSpec((bb, GATE_PAD), row),
                  pl.BlockSpec((bb, GDN_HEADS, GDN_HEAD_DIM, GDN_HEAD_DIM), state),
                  pl.BlockSpec((GDN_CONV, 3 * D_GDN), const),
                  pl.BlockSpec((1, GATE_PAD), const),
                  pl.BlockSpec((1, GATE_PAD), const),
                  pl.BlockSpec((1, GDN_HEAD_DIM), const)],
        out_specs=[pl.BlockSpec((bb, D_GDN), row),
                   pl.BlockSpec((bb, GDN_HEADS, GDN_HEAD_DIM, GDN_HEAD_DIM), state)],
        out_shape=[jax.ShapeDtypeStruct((n, D_GDN), F32),
                   jax.ShapeDtypeStruct(s0.shape, F32)],
        compiler_params=pltpu.CompilerParams(dimension_semantics=("arbitrary",),
                                             vmem_limit_bytes=VMEM_LIMIT),
        name="gdn_sample",
    )(qkv, st_t, z, ba, s0, gw["conv_w"], gw["a_log"], gw["dt_bias"], gw["onorm_g"])


def _residual_in(x_ref, ys5_ref, ygdn_ref, wout_ref, g2_ref, x1_s, n2_s):
    x1 = (x_ref[...] + _dot(ys5_ref[...].astype(BF16), wout_ref[:D_S5, :])
          + _dot(ygdn_ref[...].astype(BF16), wout_ref[D_S5:, :]))
    x1_s[...] = x1
    n2_s[...] = _rms(x1, g2_ref[...]).astype(BF16)


def _ffn_prompt_kernel(x_ref, ys5_ref, ygdn_ref, wout_ref, g2_ref, wup_ref, cw_ref, wdn_ref, gf_ref,
                       y_ref, hlast_ref, x1_s, n2_s, acc_s, ext_s, carry_s, *, tm):
    ti = pl.program_id(1)
    nt = pl.num_programs(1)

    @pl.when(ti == 0)
    def _reset():
        carry_s[...] = jnp.zeros_like(carry_s)

    _residual_in(x_ref, ys5_ref, ygdn_ref, wout_ref, g2_ref, x1_s, n2_s)
    n2 = n2_s[...]
    for f in range(D_FF // FFN_COLS):
        halves = []
        for part in range(2):
            c0 = part * D_FF + f * FFN_COLS
            cs = slice(c0, c0 + FFN_COLS)
            hcur = _dot(n2, wup_ref[:, cs])
            es = slice(part * FFN_COLS, (part + 1) * FFN_COLS)
            ext_s[0:8, es] = carry_s[:, cs]
            ext_s[8:, es] = hcur
            carry_s[:, cs] = hcur[tm - 8:, :]
            conv = (ext_s[pl.ds(6, tm), es] * cw_ref[0:1, cs]
                    + ext_s[pl.ds(7, tm), es] * cw_ref[1:2, cs]
                    + hcur * cw_ref[2:3, cs])
            halves.append(conv)
        act = (_silu(halves[0]) * halves[1]).astype(BF16)
        contrib = _dot(act, wdn_ref[f * FFN_COLS:(f + 1) * FFN_COLS, :])
        if f == 0:
            acc_s[...] = contrib
        else:
            acc_s[...] += contrib
    y_ref[...] = _rms(x1_s[...] + acc_s[...], gf_ref[...])

    @pl.when(ti == nt - 1)
    def _fin():
        hlast_ref[0] = carry_s[8 - (FFN_CONV - 1):, :]


def _ffn_prompt(x, ys5, ygdn, fw, bsz, seq, tm):
    nt = seq // tm
    row = lambda b, i: (b * nt + i, 0)
    const = lambda b, i: (0, 0)
    return pl.pallas_call(
        functools.partial(_ffn_prompt_kernel, tm=tm),
        grid=(bsz, nt),
        in_specs=[pl.BlockSpec((tm, D_MODEL), row),
                  pl.BlockSpec((tm, D_S5), row),
                  pl.BlockSpec((tm, D_GDN), row),
                  pl.BlockSpec((D_MODEL, D_MODEL), const),
                  pl.BlockSpec((1, D_MODEL), const),
                  pl.BlockSpec((D_MODEL, 2 * D_FF), const),
                  pl.BlockSpec((FFN_CONV, 2 * D_FF), const),
                  pl.BlockSpec((D_FF, D_MODEL), const),
                  pl.BlockSpec((1, D_MODEL), const)],
        out_specs=[pl.BlockSpec((tm, D_MODEL), row),
                   pl.BlockSpec((1, FFN_CONV - 1, 2 * D_FF), lambda b, i: (b, 0, 0))],
        out_shape=[jax.ShapeDtypeStruct((bsz * seq, D_MODEL), F32),
                   jax.ShapeDtypeStruct((bsz, FFN_CONV - 1, 2 * D_FF), F32)],
        scratch_shapes=[pltpu.VMEM((tm, D_MODEL), F32),
                        pltpu.VMEM((tm, D_MODEL), BF16),
                        pltpu.VMEM((tm, D_MODEL), F32),
                        pltpu.VMEM((tm + 8, 2 * FFN_COLS), F32),
                        pltpu.VMEM((8, 2 * D_FF), F32)],
        compiler_params=pltpu.CompilerParams(dimension_semantics=("arbitrary", "arbitrary"),
                                             vmem_limit_bytes=VMEM_LIMIT),
        name="ffn_prompt",
    )(x, ys5, ygdn, fw["w_out"], fw["norm2_g"], fw["w_up"], fw["conv_w"], fw["w_down"], fw["normf_g"])


def _ffn_sample_kernel(x_ref, ys5_ref, ygdn_ref, st_ref, wout_ref, g2_ref, wup_ref, cw_ref, wdn_ref, gf_ref,
                       y_ref, h_ref, x1_s, n2_s, acc_s):
    _residual_in(x_ref, ys5_ref, ygdn_ref, wout_ref, g2_ref, x1_s, n2_s)
    n2 = n2_s[...]
    for f in range(D_FF // FFN_COLS):
        halves = []
        for part in range(2):
            c0 = part * D_FF + f * FFN_COLS
            cs = slice(c0, c0 + FFN_COLS)
            hcur = _dot(n2, wup_ref[:, cs])
            h_ref[:, cs] = hcur
            halves.append(st_ref[0, :, cs] * cw_ref[0:1, cs] + st_ref[1, :, cs] * cw_ref[1:2, cs]
                          + hcur * cw_ref[2:3, cs])
        act = (_silu(halves[0]) * halves[1]).astype(BF16)
        contrib = _dot(act, wdn_ref[f * FFN_COLS:(f + 1) * FFN_COLS, :])
        if f == 0:
            acc_s[...] = contrib
        else:
            acc_s[...] += contrib
    y_ref[...] = _rms(x1_s[...] + acc_s[...], gf_ref[...])


def _ffn_sample(x, ys5, ygdn, st_t, fw):
    n = x.shape[0]
    c2 = lambda i: (0, 0)
    c3 = lambda i: (0, 0, 0)
    return pl.pallas_call(
        _ffn_sample_kernel,
        grid=(1,),
        in_specs=[pl.BlockSpec((n, D_MODEL), c2),
                  pl.BlockSpec((n, D_S5), c2),
                  pl.BlockSpec((n, D_GDN), c2),
                  pl.BlockSpec((FFN_CONV - 1, n, 2 * D_FF), c3),
                  pl.BlockSpec((D_MODEL, D_MODEL), c2),
                  pl.BlockSpec((1, D_MODEL), c2),
                  pl.BlockSpec((D_MODEL, 2 * D_FF), c2),
                  pl.BlockSpec((FFN_CONV, 2 * D_FF), c2),
                  pl.BlockSpec((D_FF, D_MODEL), c2),
                  pl.BlockSpec((1, D_MODEL), c2)],
        out_specs=[pl.BlockSpec((n, D_MODEL), c2),
                   pl.BlockSpec((n, 2 * D_FF), c2)],
        out_shape=[jax.ShapeDtypeStruct((n, D_MODEL), F32),
                   jax.ShapeDtypeStruct((n, 2 * D_FF), F32)],
        scratch_shapes=[pltpu.VMEM((n, D_MODEL), F32),
                        pltpu.VMEM((n, D_MODEL), BF16),
                        pltpu.VMEM((n, D_MODEL), F32)],
        compiler_params=pltpu.CompilerParams(dimension_semantics=("arbitrary",),
                                             vmem_limit_bytes=VMEM_LIMIT),
        name="ffn_sample",
    )(x, ys5, ygdn, st_t, fw["w_out"], fw["norm2_g"], fw["w_up"], fw["conv_w"], fw["w_down"], fw["normf_g"])


PROJ_ROWS = 512
S5_STEPS = 64
GDN_ROWS = 256
FFN_ROWS = 256
GDN_SAMPLE_ROWS = 8


def kernel(x_prompt, x_sample, state_s5_re, state_s5_im, state_gdn, state_gdn_conv, state_ffn_conv, norm1_g, w_in, s5_a_re, s5_a_im, s5_log_dt, s5_b_re, s5_b_im, s5_c_re, s5_c_im, s5_d, s5_w_glu, gdn_conv_w, gdn_a_log, gdn_dt_bias, gdn_onorm_g, w_out, norm2_g, ffn_w_up, ffn_conv_w, ffn_w_down, normf_g):
    depth = w_in.shape[0]
    assert depth == 1, "the final rmsnorm is fused into the last layer's ffn kernel"
    bsz, seq, _ = x_prompt.shape
    nsmp = x_sample.shape[0]
    assert x_sample.shape[1] == 1
    l = 0

    xp = x_prompt.reshape(bsz * seq, D_MODEL)
    xs = x_sample.reshape(nsmp, D_MODEL)

    g1 = norm1_g[l].reshape(1, D_MODEL)
    w_in_bf = jnp.pad(w_in[l], ((0, 0), (0, D_IN_PAD - D_IN))).astype(BF16)
    sw = _s5_weights(s5_a_re[l], s5_a_im[l], s5_log_dt[l], s5_b_re[l], s5_b_im[l], s5_c_re[l], s5_c_im[l],
                     s5_d[l], s5_w_glu[l])
    gate_row = lambda v: jnp.pad(v, (GDN_HEADS, GATE_PAD - 2 * GDN_HEADS)).reshape(1, GATE_PAD)
    gw = {"conv_w": gdn_conv_w[l], "a_log": gate_row(gdn_a_log[l]), "dt_bias": gate_row(gdn_dt_bias[l]),
          "onorm_g": gdn_onorm_g[l].reshape(1, GDN_HEAD_DIM)}
    fw = {"w_out": w_out[l].astype(BF16), "norm2_g": norm2_g[l].reshape(1, D_MODEL),
          "w_up": ffn_w_up[l].astype(BF16), "conv_w": ffn_conv_w[l], "w_down": ffn_w_down[l].astype(BF16),
          "normf_g": normf_g.reshape(1, D_MODEL)}

    u_p, qkv_p, z_p, ba_p = _proj(xp, g1, w_in_bf, PROJ_ROWS)
    u_tb = u_p.reshape(bsz, seq, D_S5).transpose(1, 0, 2).reshape(seq * bsz, D_S5)
    zeros_h = jnp.zeros((bsz, S5_CH), F32)
    ys5_tb, p_hr, p_hi = _s5(u_tb, zeros_h, zeros_h, sw, bsz, S5_STEPS)
    ys5_p = ys5_tb.reshape(seq, bsz, D_S5).transpose(1, 0, 2).reshape(bsz * seq, D_S5)
    ygdn_p, p_gdn, p_gdn_conv = _gdn_prompt(qkv_p, z_p, ba_p, gw, bsz, seq, GDN_ROWS)
    y_p, p_ffn_conv = _ffn_prompt(xp, ys5_p, ygdn_p, fw, bsz, seq, FFN_ROWS)

    u_s, qkv_s, z_s, ba_s = _proj(xs, g1, w_in_bf, nsmp)
    ys5_s, s_hr, s_hi = _s5(u_s, state_s5_re[l].reshape(nsmp, S5_CH), state_s5_im[l].reshape(nsmp, S5_CH),
                            sw, nsmp, 1)
    gconv_t = state_gdn_conv[l].transpose(1, 0, 2)
    ygdn_s, s_gdn = _gdn_sample(qkv_s, gconv_t, z_s, ba_s, state_gdn[l], gw, GDN_SAMPLE_ROWS)
    fconv_t = state_ffn_conv[l].transpose(1, 0, 2)
    y_s, h_s = _ffn_sample(xs, ys5_s, ygdn_s, fconv_t, fw)
    s_gdn_conv = jnp.concatenate([state_gdn_conv[l][:, 1:], qkv_s[:, None, :]], axis=1)
    s_ffn_conv = jnp.concatenate([state_ffn_conv[l][:, 1:], h_s[:, None, :]], axis=1)

    st = lambda a: a[None]
    s5_shape = lambda a, n: a.reshape(1, n, S5_N_GROUPS, S5_STATE)
    return (y_p.reshape(bsz, seq, D_MODEL), y_s.reshape(nsmp, 1, D_MODEL),
            s5_shape(p_hr, bsz), s5_shape(p_hi, bsz), st(p_gdn), st(p_gdn_conv), st(p_ffn_conv),
            s5_shape(s_hr, nsmp), s5_shape(s_hi, nsmp), st(s_gdn), st(s_gdn_conv), st(s_ffn_conv))
```

```python
import functools

import jax
import jax.numpy as jnp
from jax import lax
from jax.experimental import pallas as pl
from jax.experimental.pallas import tpu as pltpu

F32 = jnp.float32
BF16 = jnp.bfloat16
NORM_EPS = 1e-6

D_MODEL = 1024
D_S5 = 512
S5_GROUP = 16
S5_N_GROUPS = 32
S5_STATE = 64
S5_CH = S5_N_GROUPS * S5_STATE
D_GDN = 512
GDN_HEADS = 4
GDN_HEAD_DIM = 128
GDN_CONV = 4
D_FF = 2816
FFN_CONV = 3
D_IN = D_S5 + 4 * D_GDN + 2 * GDN_HEADS
GATE_PAD = 128
D_IN_PAD = D_S5 + 4 * D_GDN + GATE_PAD

CHUNK = 64
FFN_COLS = 256
VMEM_LIMIT = 56 * 1024 * 1024


def _dot(a, b):
    return jnp.dot(a, b, preferred_element_type=F32)


def _dot_hi(a, b):
    return jnp.dot(a, b, preferred_element_type=F32, precision=lax.Precision.HIGHEST)


def _dot_nt(a, b):
    return lax.dot_general(a, b, (((1,), (1,)), ((), ())), preferred_element_type=F32)


def _rms(x, g):
    ms = jnp.mean(x * x, axis=-1, keepdims=True)
    return x * lax.rsqrt(ms + NORM_EPS) * g


def _silu(x):
    return x * jax.nn.sigmoid(x)


def _softplus(x):
    return jnp.maximum(x, 0.0) + jnp.log1p(jnp.exp(-jnp.abs(x)))


def _proj_kernel(x_ref, g_ref, w_ref, u_ref, qkv_ref, z_ref, ba_ref):
    n = _rms(x_ref[...], g_ref[...]).astype(BF16)
    o0 = D_S5
    o1 = o0 + 3 * D_GDN
    o2 = o1 + D_GDN
    u_ref[...] = _dot(n, w_ref[:, :o0])
    qkv_ref[...] = _dot(n, w_ref[:, o0:o1])
    z_ref[...] = _dot(n, w_ref[:, o1:o2])
    ba_ref[...] = _dot(n, w_ref[:, o2:])


def _proj(x, g1, w_in_bf, tm):
    n = x.shape[0]
    row = lambda i: (i, 0)
    const = lambda i: (0, 0)
    return pl.pallas_call(
        _proj_kernel,
        grid=(n // tm,),
        in_specs=[pl.BlockSpec((tm, D_MODEL), row),
                  pl.BlockSpec((1, D_MODEL), const),
                  pl.BlockSpec((D_MODEL, D_IN_PAD), const)],
        out_specs=[pl.BlockSpec((tm, D_S5), row),
                   pl.BlockSpec((tm, 3 * D_GDN), row),
                   pl.BlockSpec((tm, D_GDN), row),
                   pl.BlockSpec((tm, GATE_PAD), row)],
        out_shape=[jax.ShapeDtypeStruct((n, D_S5), F32),
                   jax.ShapeDtypeStruct((n, 3 * D_GDN), F32),
                   jax.ShapeDtypeStruct((n, D_GDN), F32),
                   jax.ShapeDtypeStruct((n, GATE_PAD), F32)],
        compiler_params=pltpu.CompilerParams(dimension_semantics=("arbitrary",),
                                             vmem_limit_bytes=VMEM_LIMIT),
        name="proj",
    )(x, g1, w_in_bf)


S5_HALF = S5_CH // 2
S5_COLS = 512


def _s5_kernel(u_ref, h0r_ref, h0i_ref, are_ref, aim_ref, ldt_ref, bre_ref, bim_ref, cre_ref, cim_ref,
               d_ref, wglu_ref, y_ref, hr_out, hi_out,
               abr_s, abi_s, wbr_s, wbi_s, hr_s, hi_s, xr_s, xi_s, *, nb, tt):
    @pl.when(pl.program_id(0) == 0)
    def _init():
        ar = are_ref[...]
        ai = aim_ref[...]
        dt = jnp.exp(ldt_ref[...])
        mag = jnp.exp(ar * dt)
        abr = mag * jnp.cos(ai * dt)
        abi = mag * jnp.sin(ai * dt)
        den = ar * ar + ai * ai
        p = abr - 1.0
        fr = (p * ar + abi * ai) / den
        fi = (abi * ar - p * ai) / den
        abr_s[...] = abr
        abi_s[...] = abi
        for m in range(2):
            frm = fr[:, m * S5_HALF:(m + 1) * S5_HALF]
            fim = fi[:, m * S5_HALF:(m + 1) * S5_HALF]
            wr = bre_ref[m]
            wi = bim_ref[m]
            wbr_s[m] = (wr * frm - wi * fim).astype(BF16)
            wbi_s[m] = (wr * fim + wi * frm).astype(BF16)
        hr_s[...] = h0r_ref[...]
        hi_s[...] = h0i_ref[...]

    u = u_ref[...]
    ub = u.astype(BF16)
    half = D_S5 // 2
    for m in range(2):
        um = ub[:, m * half:(m + 1) * half]
        xr_s[:, m * S5_HALF:(m + 1) * S5_HALF] = _dot(um, wbr_s[m])
        xi_s[:, m * S5_HALF:(m + 1) * S5_HALF] = _dot(um, wbi_s[m])

    for cb in range(S5_CH // S5_COLS):
        cols = slice(cb * S5_COLS, (cb + 1) * S5_COLS)
        a_r = jnp.broadcast_to(abr_s[:, cols], (nb, S5_COLS))
        a_i = jnp.broadcast_to(abi_s[:, cols], (nb, S5_COLS))

        def step(t, carry, cols=cols, a_r=a_r, a_i=a_i):
            hr, hi = carry
            r0 = pl.multiple_of(t * nb, nb)
            nhr = a_r * hr - a_i * hi + xr_s[pl.ds(r0, nb), cols]
            nhi = a_r * hi + a_i * hr + xi_s[pl.ds(r0, nb), cols]
            xr_s[pl.ds(r0, nb), cols] = nhr
            xi_s[pl.ds(r0, nb), cols] = nhi
            return nhr, nhi

        hr, hi = lax.fori_loop(0, tt, step, (hr_s[:, cols], hi_s[:, cols]), unroll=min(tt, 8))
        hr_s[:, cols] = hr
        hi_s[:, cols] = hi

    hr_out[...] = hr_s[...]
    hi_out[...] = hi_s[...]

    ys = []
    for m in range(2):
        hre = xr_s[:, m * S5_HALF:(m + 1) * S5_HALF].astype(BF16)
        him = xi_s[:, m * S5_HALF:(m + 1) * S5_HALF].astype(BF16)
        ych = _dot(hre, cre_ref[m]) - _dot(him, cim_ref[m])
        ys.append(ych + d_ref[:, m * half:(m + 1) * half] * u[:, m * half:(m + 1) * half])
    y = jax.nn.gelu(jnp.concatenate(ys, axis=-1)).astype(BF16)
    gl = _dot(y, wglu_ref[...])
    y_ref[...] = gl[:, :D_S5] * jax.nn.sigmoid(gl[:, D_S5:])


def _s5(u_tb, h0r, h0i, sw, nb, tt):
    n = u_tb.shape[0]
    rows = nb * tt
    const2 = lambda i: (0, 0)
    const3 = lambda i: (0, 0, 0)
    full2 = lambda shape: pl.BlockSpec(shape, const2)
    full3 = lambda shape: pl.BlockSpec(shape, const3)
    return pl.pallas_call(
        functools.partial(_s5_kernel, nb=nb, tt=tt),
        grid=(n // rows,),
        in_specs=[pl.BlockSpec((rows, D_S5), lambda i: (i, 0)),
                  full2((nb, S5_CH)), full2((nb, S5_CH)),
                  full2((1, S5_CH)), full2((1, S5_CH)), full2((1, S5_CH)),
                  full3((2, D_S5 // 2, S5_HALF)), full3((2, D_S5 // 2, S5_HALF)),
                  full3((2, S5_HALF, D_S5 // 2)), full3((2, S5_HALF, D_S5 // 2)),
                  full2((1, D_S5)), full2((D_S5, 2 * D_S5))],
        out_specs=[pl.BlockSpec((rows, D_S5), lambda i: (i, 0)),
                   full2((nb, S5_CH)), full2((nb, S5_CH))],
        out_shape=[jax.ShapeDtypeStruct((n, D_S5), F32),
                   jax.ShapeDtypeStruct((nb, S5_CH), F32),
                   jax.ShapeDtypeStruct((nb, S5_CH), F32)],
        scratch_shapes=[pltpu.VMEM((1, S5_CH), F32), pltpu.VMEM((1, S5_CH), F32),
                        pltpu.VMEM((2, D_S5 // 2, S5_HALF), BF16), pltpu.VMEM((2, D_S5 // 2, S5_HALF), BF16),
                        pltpu.VMEM((nb, S5_CH), F32), pltpu.VMEM((nb, S5_CH), F32),
                        pltpu.VMEM((rows, S5_CH), F32), pltpu.VMEM((rows, S5_CH), F32)],
        compiler_params=pltpu.CompilerParams(dimension_semantics=("arbitrary",),
                                             vmem_limit_bytes=VMEM_LIMIT),
        name="s5",
    )(u_tb, h0r, h0i, sw["a_re"], sw["a_im"], sw["log_dt"], sw["b_re"], sw["b_im"],
      sw["c_re"], sw["c_im"], sw["d"], sw["w_glu"])


def _s5_weights(a_re, a_im, log_dt, b_re, b_im, c_re, c_im, d, w_glu):
    g, n, c = S5_N_GROUPS, S5_STATE, S5_GROUP
    eye = jnp.eye(g // 2, dtype=F32)

    def b_blockdiag(b):
        bb = b.reshape(2, g // 2, n, c)
        return jnp.einsum("mgnc,gh->mgchn", bb, eye).reshape(2, (g // 2) * c, (g // 2) * n)

    def c_blockdiag(cm):
        cc = cm.reshape(2, g // 2, c, n)
        return jnp.einsum("mgcn,gh->mgnhc", cc, eye).reshape(2, (g // 2) * n, (g // 2) * c)

    return {
        "a_re": a_re.reshape(1, S5_CH), "a_im": a_im.reshape(1, S5_CH),
        "log_dt": jnp.repeat(log_dt, n).reshape(1, S5_CH),
        "b_re": b_blockdiag(b_re), "b_im": b_blockdiag(b_im),
        "c_re": c_blockdiag(c_re).astype(BF16), "c_im": c_blockdiag(c_im).astype(BF16),
        "d": d.reshape(1, D_S5), "w_glu": w_glu.astype(BF16),
    }


def _l2norm(x):
    return x * lax.rsqrt(jnp.sum(x * x, axis=-1, keepdims=True) + NORM_EPS)


def _gates(ba, alog_row, dtb_row):
    beta = jax.nn.sigmoid(ba)
    g = -jnp.exp(alog_row) * _softplus(ba + dtb_row)
    return beta, g


def _out_gate(o, og_row, z):
    return _rms(o, og_row) * _silu(z)


def _bdot(a, b):
    return jnp.einsum("bij,bjk->bik", a.astype(BF16), b.astype(BF16), preferred_element_type=F32)


def _tri_inverse_offdiag(lmat, ii, jj):
    n = -jnp.where(((ii >> 1) == (jj >> 1))[None], lmat, 0.0)
    shift = 1
    while (1 << shift) < CHUNK:
        same_big = (ii >> (shift + 1)) == (jj >> (shift + 1))
        same_small = (ii >> shift) == (jj >> shift)
        c = jnp.where((same_big & jnp.logical_not(same_small))[None], lmat, 0.0)
        w = c + _bdot(n, c)
        n = n - (w + _bdot(w, n))
        shift += 1
    return n


def _gdn_kernel(qkv_ref, z_ref, ba_ref, convw_ref, alog_ref, dtb_ref, og_ref,
                y_ref, sfin_ref, convout_ref, s_s, ext_s, u_s, wq_s, attn_s, kgt_s, *, tg):
    ti = pl.program_id(1)
    nt = pl.num_programs(1)

    @pl.when(ti == 0)
    def _reset():
        s_s[...] = jnp.zeros_like(s_s)
        ext_s[0:8, :] = jnp.zeros((8, 3 * D_GDN), F32)

    ext_s[8:, :] = qkv_ref[...]

    ba = ba_ref[...]
    beta_all, g_all = _gates(ba, alog_ref[...], dtb_ref[...])

    ri = lax.broadcasted_iota(jnp.int32, (tg, tg), 0)
    ci = lax.broadcasted_iota(jnp.int32, (tg, tg), 1)
    lbd = (((ri >> 6) == (ci >> 6)) & (ci <= ri)).astype(F32)
    gc_col = _dot_hi(lbd, g_all)
    gc_row = gc_col.T

    ii = lax.broadcasted_iota(jnp.int32, (CHUNK, CHUNK), 0)
    jj = lax.broadcasted_iota(jnp.int32, (CHUNK, CHUNK), 1)
    causal = jj <= ii
    strict = jj < ii
    scale = GDN_HEAD_DIM ** -0.5
    og = og_ref[...]

    nc = tg // CHUNK
    hcols = lambda h: slice(h * GDN_HEAD_DIM, (h + 1) * GDN_HEAD_DIM)

    lmats = [None] * (nc * GDN_HEADS)
    rhss = [None] * (nc * GDN_HEADS)
    g_lasts = [None] * (nc * GDN_HEADS)
    for h in range(GDN_HEADS):
        def conv_cols(part, h=h):
            c0 = part * D_GDN + h * GDN_HEAD_DIM
            acc = ext_s[pl.ds(5, tg), c0:c0 + GDN_HEAD_DIM] * convw_ref[0:1, c0:c0 + GDN_HEAD_DIM]
            for j in range(1, GDN_CONV):
                acc = acc + ext_s[pl.ds(5 + j, tg), c0:c0 + GDN_HEAD_DIM] * convw_ref[j:j + 1, c0:c0 + GDN_HEAD_DIM]
            return _silu(acc)

        q_all = _l2norm(conv_cols(0)) * scale
        k_all = _l2norm(conv_cols(1))
        v_all = conv_cols(2)
        for c in range(nc):
            sys = c * GDN_HEADS + h
            rows = slice(c * CHUNK, (c + 1) * CHUNK)
            q = q_all[rows]
            k = k_all[rows]
            v = v_all[rows]
            beta = beta_all[rows, h:h + 1]
            gcc = gc_col[rows, GDN_HEADS + h:GDN_HEADS + h + 1]
            gcr = gc_row[GDN_HEADS + h:GDN_HEADS + h + 1, rows]
            decay = jnp.exp(jnp.where(causal, gcc - gcr, -jnp.inf))
            kb = k * beta
            egc = jnp.exp(gcc)
            g_last = gcc[CHUNK - 1:CHUNK, :]
            kq = _dot_nt(jnp.concatenate([kb, q], axis=0).astype(BF16), k.astype(BF16))
            lmats[sys] = jnp.where(strict, kq[:CHUNK] * decay, 0.0)
            attn_s[sys] = jnp.where(causal, kq[CHUNK:] * decay, 0.0).astype(BF16)
            rhss[sys] = jnp.concatenate([v * beta, kb * egc], axis=1)
            wq_s[sys, CHUNK:, :] = (q * egc).astype(BF16)
            kgt_s[sys] = (k * jnp.exp(g_last - gcc)).T.astype(BF16)
            g_lasts[sys] = jnp.exp(g_last)

    noff = _tri_inverse_offdiag(jnp.stack(lmats), ii, jj)
    rhs = jnp.stack(rhss)
    sol = rhs + _bdot(noff, rhs)
    for sys in range(nc * GDN_HEADS):
        c, h = divmod(sys, GDN_HEADS)
        u_s[c * CHUNK:(c + 1) * CHUNK, hcols(h)] = sol[sys, :, :GDN_HEAD_DIM]
        wq_s[sys, :CHUNK, :] = sol[sys, :, GDN_HEAD_DIM:].astype(BF16)

    states = [s_s[h] for h in range(GDN_HEADS)]
    for c in range(nc):
        rows = slice(c * CHUNK, (c + 1) * CHUNK)
        for h in range(GDN_HEADS):
            sys = c * GDN_HEADS + h
            r = _dot(wq_s[sys], states[h].astype(BF16))
            v_new = (u_s[rows, hcols(h)] - r[:CHUNK]).astype(BF16)
            o = r[CHUNK:] + _dot(attn_s[sys], v_new)
            states[h] = states[h] * g_lasts[sys] + _dot(kgt_s[sys], v_new)
            y_ref[rows, hcols(h)] = _out_gate(o, og, z_ref[rows, hcols(h)])
    for h in range(GDN_HEADS):
        s_s[h] = states[h]

    ext_s[0:8, :] = ext_s[pl.ds(tg, 8), :]

    @pl.when(ti == nt - 1)
    def _fin():
        sfin_ref[0] = s_s[...]
        convout_ref[0] = ext_s[pl.ds(8 - (GDN_CONV - 1), GDN_CONV - 1), :]


def _gdn_prompt(qkv, z, ba, gw, bsz, seq, tg):
    nt = seq // tg
    nsys = (tg // CHUNK) * GDN_HEADS
    row = lambda b, i: (b * nt + i, 0)
    const = lambda b, i: (0, 0)
    return pl.pallas_call(
        functools.partial(_gdn_kernel, tg=tg),
        grid=(bsz, nt),
        in_specs=[pl.BlockSpec((tg, 3 * D_GDN), row),
                  pl.BlockSpec((tg, D_GDN), row),
                  pl.BlockSpec((tg, GATE_PAD), row),
                  pl.BlockSpec((GDN_CONV, 3 * D_GDN), const),
                  pl.BlockSpec((1, GATE_PAD), const),
                  pl.BlockSpec((1, GATE_PAD), const),
                  pl.BlockSpec((1, GDN_HEAD_DIM), const)],
        out_specs=[pl.BlockSpec((tg, D_GDN), row),
                   pl.BlockSpec((1, GDN_HEADS, GDN_HEAD_DIM, GDN_HEAD_DIM), lambda b, i: (b, 0, 0, 0)),
                   pl.BlockSpec((1, GDN_CONV - 1, 3 * D_GDN), lambda b, i: (b, 0, 0))],
        out_shape=[jax.ShapeDtypeStruct((bsz * seq, D_GDN), F32),
                   jax.ShapeDtypeStruct((bsz, GDN_HEADS, GDN_HEAD_DIM, GDN_HEAD_DIM), F32),
                   jax.ShapeDtypeStruct((bsz, GDN_CONV - 1, 3 * D_GDN), F32)],
        scratch_shapes=[pltpu.VMEM((GDN_HEADS, GDN_HEAD_DIM, GDN_HEAD_DIM), F32),
                        pltpu.VMEM((tg + 8, 3 * D_GDN), F32),
                        pltpu.VMEM((tg, D_GDN), F32),
                        pltpu.VMEM((nsys, 2 * CHUNK, GDN_HEAD_DIM), BF16),
                        pltpu.VMEM((nsys, CHUNK, CHUNK), BF16),
                        pltpu.VMEM((nsys, GDN_HEAD_DIM, CHUNK), BF16)],
        compiler_params=pltpu.CompilerParams(dimension_semantics=("arbitrary", "arbitrary"),
                                             vmem_limit_bytes=VMEM_LIMIT),
        name="gdn_prompt",
    )(qkv, z, ba, gw["conv_w"], gw["a_log"], gw["dt_bias"], gw["onorm_g"])


def _gdn_step_kernel(qkv_ref, st_ref, z_ref, ba_ref, s0_ref, convw_ref, alog_ref, dtb_ref, og_ref,
                     y_ref, s1_ref, *, bb):
    beta_all, g_all = _gates(ba_ref[...], alog_ref[...], dtb_ref[...])
    alpha_all = jnp.exp(g_all)
    scale = GDN_HEAD_DIM ** -0.5
    og = og_ref[...]

    for h in range(GDN_HEADS):
        def conv_cols(part, h=h):
            c0 = part * D_GDN + h * GDN_HEAD_DIM
            cs = slice(c0, c0 + GDN_HEAD_DIM)
            acc = qkv_ref[:, cs] * convw_ref[GDN_CONV - 1:GDN_CONV, cs]
            for j in range(GDN_CONV - 1):
                acc = acc + st_ref[j, :, cs] * convw_ref[j:j + 1, cs]
            return _silu(acc)

        q = _l2norm(conv_cols(0)) * scale
        k = _l2norm(conv_cols(1))
        v = conv_cols(2)
        beta = beta_all[:, h:h + 1]
        alpha = alpha_all[:, GDN_HEADS + h:GDN_HEADS + h + 1]
        qk = jnp.sum(q * k, axis=-1, keepdims=True)
        kt = k.T
        qt = q.T
        o_rows = []
        for b in range(bb):
            s0 = s0_ref[b, h]
            kcol = kt[:, b:b + 1]
            qcol = qt[:, b:b + 1]
            ks = jnp.sum(kcol * s0, axis=0, keepdims=True)
            qs = jnp.sum(qcol * s0, axis=0, keepdims=True)
            al = alpha[b:b + 1, :]
            v_new = beta[b:b + 1, :] * (v[b:b + 1, :] - al * ks)
            o_rows.append(al * qs + qk[b:b + 1, :] * v_new)
            s1_ref[b, h] = al * s0 + kcol * v_new
        o = jnp.concatenate(o_rows, axis=0)
        cs = slice(h * GDN_HEAD_DIM, (h + 1) * GDN_HEAD_DIM)
        y_ref[:, cs] = _out_gate(o, og, z_ref[:, cs])


def _gdn_sample(qkv, st_t, z, ba, s0, gw, bb):
    n = qkv.shape[0]
    row = lambda i: (i, 0)
    const = lambda i: (0, 0)
    state = lambda i: (i, 0, 0, 0)
    return pl.pallas_call(
        functools.partial(_gdn_step_kernel, bb=bb),
        grid=(n // bb,),
        in_specs=[pl.BlockSpec((bb, 3 * D_GDN), row),
                  pl.BlockSpec((GDN_CONV - 1, bb, 3 * D_GDN), lambda i: (0, i, 0)),
                  pl.BlockSpec((bb, D_GDN), row),
                  pl.BlockSpec((bb, GATE_PAD), row),
                  pl.BlockSpec((bb, GDN_HEADS, GDN_HEAD_DIM, GDN_HEAD_DIM), state),
                  pl.BlockSpec((GDN_CONV, 3 * D_GDN), const),
                  pl.BlockSpec((1, GATE_PAD), const),
                  pl.BlockSpec((1, GATE_PAD), const),
                  pl.BlockSpec((1, GDN_HEAD_DIM), const)],
        out_specs=[pl.BlockSpec((bb, D_GDN), row),
                   pl.BlockSpec((bb, GDN_HEADS, GDN_HEAD_DIM, GDN_HEAD_DIM), state)],
        out_shape=[jax.ShapeDtypeStruct((n, D_GDN), F32),
                   jax.ShapeDtypeStruct(s0.shape, F32)],
        compiler_params=pltpu.CompilerParams(dimension_semantics=("arbitrary",),
                                             vmem_limit_bytes=VMEM_LIMIT),
        name="gdn_sample",
    )(qkv, st_t, z, ba, s0, gw["conv_w"], gw["a_log"], gw["dt_bias"], gw["onorm_g"])


def _residual_in(x_ref, ys5_ref, ygdn_ref, wout_ref, g2_ref, x1_s, n2_s):
    x1 = (x_ref[...] + _dot(ys5_ref[...].astype(BF16), wout_ref[:D_S5, :])
          + _dot(ygdn_ref[...].astype(BF16), wout_ref[D_S5:, :]))
    x1_s[...] = x1
    n2_s[...] = _rms(x1, g2_ref[...]).astype(BF16)


def _ffn_prompt_kernel(x_ref, ys5_ref, ygdn_ref, wout_ref, g2_ref, wup_ref, cw_ref, wdn_ref, gf_ref,
                       y_ref, hlast_ref, x1_s, n2_s, act_s, carry_s, *, tm):
    ti = pl.program_id(1)
    nt = pl.num_programs(1)

    @pl.when(ti == 0)
    def _reset():
        carry_s[...] = jnp.zeros_like(carry_s)

    _residual_in(x_ref, ys5_ref, ygdn_ref, wout_ref, g2_ref, x1_s, n2_s)
    n2 = n2_s[...]
    for f in range(D_FF // FFN_COLS):
        halves = []
        for part in range(2):
            c0 = part * D_FF + f * FFN_COLS
            cs = slice(c0, c0 + FFN_COLS)
            hcur = _dot(n2, wup_ref[:, cs])
            ext = jnp.concatenate([carry_s[:, cs], hcur], axis=0)
            carry_s[:, cs] = hcur[tm - 8:, :]
            conv = (ext[6:6 + tm] * cw_ref[0:1, cs]
                    + ext[7:7 + tm] * cw_ref[1:2, cs]
                    + hcur * cw_ref[2:3, cs])
            halves.append(conv)
        act_s[:, f * FFN_COLS:(f + 1) * FFN_COLS] = (_silu(halves[0]) * halves[1]).astype(BF16)
    y_ref[...] = _rms(x1_s[...] + _dot(act_s[...], wdn_ref[...]), gf_ref[...])

    @pl.when(ti == nt - 1)
    def _fin():
        hlast_ref[0] = carry_s[8 - (FFN_CONV - 1):, :]


def _ffn_prompt(x, ys5, ygdn, fw, bsz, seq, tm):
    nt = seq // tm
    row = lambda b, i: (b * nt + i, 0)
    const = lambda b, i: (0, 0)
    resident = lambda shape: pl.BlockSpec(shape, const, pipeline_mode=pl.Buffered(1))
    return pl.pallas_call(
        functools.partial(_ffn_prompt_kernel, tm=tm),
        grid=(bsz, nt),
        in_specs=[pl.BlockSpec((tm, D_MODEL), row),
                  pl.BlockSpec((tm, D_S5), row),
                  pl.BlockSpec((tm, D_GDN), row),
                  resident((D_MODEL, D_MODEL)),
                  resident((1, D_MODEL)),
                  resident((D_MODEL, 2 * D_FF)),
                  resident((FFN_CONV, 2 * D_FF)),
                  resident((D_FF, D_MODEL)),
                  resident((1, D_MODEL))],
        out_specs=[pl.BlockSpec((tm, D_MODEL), row),
                   pl.BlockSpec((1, FFN_CONV - 1, 2 * D_FF), lambda b, i: (b, 0, 0))],
        out_shape=[jax.ShapeDtypeStruct((bsz * seq, D_MODEL), F32),
                   jax.ShapeDtypeStruct((bsz, FFN_CONV - 1, 2 * D_FF), F32)],
        scratch_shapes=[pltpu.VMEM((tm, D_MODEL), F32),
                        pltpu.VMEM((tm, D_MODEL), BF16),
                        pltpu.VMEM((tm, D_FF), BF16),
                        pltpu.VMEM((8, 2 * D_FF), F32)],
        compiler_params=pltpu.CompilerParams(dimension_semantics=("arbitrary", "arbitrary"),
                                             vmem_limit_bytes=VMEM_LIMIT),
        name="ffn_prompt",
    )(x, ys5, ygdn, fw["w_out"], fw["norm2_g"], fw["w_up"], fw["conv_w"], fw["w_down"], fw["normf_g"])


def _ffn_sample_kernel(x_ref, ys5_ref, ygdn_ref, st_ref, wout_ref, g2_ref, wup_ref, cw_ref, wdn_ref, gf_ref,
                       y_ref, h_ref, x1_s, n2_s, acc_s):
    _residual_in(x_ref, ys5_ref, ygdn_ref, wout_ref, g2_ref, x1_s, n2_s)
    n2 = n2_s[...]
    for f in range(D_FF // FFN_COLS):
        halves = []
        for part in range(2):
            c0 = part * D_FF + f * FFN_COLS
            cs = slice(c0, c0 + FFN_COLS)
            hcur = _dot(n2, wup_ref[:, cs])
            h_ref[:, cs] = hcur
            halves.append(st_ref[0, :, cs] * cw_ref[0:1, cs] + st_ref[1, :, cs] * cw_ref[1:2, cs]
                          + hcur * cw_ref[2:3, cs])
        act = (_silu(halves[0]) * halves[1]).astype(BF16)
        contrib = _dot(act, wdn_ref[f * FFN_COLS:(f + 1) * FFN_COLS, :])
        if f == 0:
            acc_s[...] = contrib
        else:
            acc_s[...] += contrib
    y_ref[...] = _rms(x1_s[...] + acc_s[...], gf_ref[...])


def _ffn_sample(x, ys5, ygdn, st_t, fw):
    n = x.shape[0]
    c2 = lambda i: (0, 0)
    c3 = lambda i: (0, 0, 0)
    return pl.pallas_call(
        _ffn_sample_kernel,
        grid=(1,),
        in_specs=[pl.BlockSpec((n, D_MODEL), c2),
                  pl.BlockSpec((n, D_S5), c2),
                  pl.BlockSpec((n, D_GDN), c2),
                  pl.BlockSpec((FFN_CONV - 1, n, 2 * D_FF), c3),
                  pl.BlockSpec((D_MODEL, D_MODEL), c2),
                  pl.BlockSpec((1, D_MODEL), c2),
                  pl.BlockSpec((D_MODEL, 2 * D_FF), c2),
                  pl.BlockSpec((FFN_CONV, 2 * D_FF), c2),
                  pl.BlockSpec((D_FF, D_MODEL), c2),
                  pl.BlockSpec((1, D_MODEL), c2)],
        out_specs=[pl.BlockSpec((n, D_MODEL), c2),
                   pl.BlockSpec((n, 2 * D_FF), c2)],
        out_shape=[jax.ShapeDtypeStruct((n, D_MODEL), F32),
                   jax.ShapeDtypeStruct((n, 2 * D_FF), F32)],
        scratch_shapes=[pltpu.VMEM((n, D_MODEL), F32),
                        pltpu.VMEM((n, D_MODEL), BF16),
                        pltpu.VMEM((n, D_MODEL), F32)],
        compiler_params=pltpu.CompilerParams(dimension_semantics=("arbitrary",),
                                             vmem_limit_bytes=VMEM_LIMIT),
        name="ffn_sample",
    )(x, ys5, ygdn, st_t, fw["w_out"], fw["norm2_g"], fw["w_up"], fw["conv_w"], fw["w_down"], fw["normf_g"])


PROJ_ROWS = 512
S5_STEPS = 64
GDN_ROWS = 256
FFN_ROWS = 512
GDN_SAMPLE_ROWS = 8


def kernel(x_prompt, x_sample, state_s5_re, state_s5_im, state_gdn, state_gdn_conv, state_ffn_conv, norm1_g, w_in, s5_a_re, s5_a_im, s5_log_dt, s5_b_re, s5_b_im, s5_c_re, s5_c_im, s5_d, s5_w_glu, gdn_conv_w, gdn_a_log, gdn_dt_bias, gdn_onorm_g, w_out, norm2_g, ffn_w_up, ffn_conv_w, ffn_w_down, normf_g):
    depth = w_in.shape[0]
    assert depth == 1, "the final rmsnorm is fused into the last layer's ffn kernel"
    bsz, seq, _ = x_prompt.shape
    nsmp = x_sample.shape[0]
    assert x_sample.shape[1] == 1
    l = 0

    xp = x_prompt.reshape(bsz * seq, D_MODEL)
    xs = x_sample.reshape(nsmp, D_MODEL)

    g1 = norm1_g[l].reshape(1, D_MODEL)
    w_in_bf = jnp.pad(w_in[l], ((0, 0), (0, D_IN_PAD - D_IN))).astype(BF16)
    sw = _s5_weights(s5_a_re[l], s5_a_im[l], s5_log_dt[l], s5_b_re[l], s5_b_im[l], s5_c_re[l], s5_c_im[l],
                     s5_d[l], s5_w_glu[l])
    gate_row = lambda v: jnp.pad(v, (GDN_HEADS, GATE_PAD - 2 * GDN_HEADS)).reshape(1, GATE_PAD)
    gw = {"conv_w": gdn_conv_w[l], "a_log": gate_row(gdn_a_log[l]), "dt_bias": gate_row(gdn_dt_bias[l]),
          "onorm_g": gdn_onorm_g[l].reshape(1, GDN_HEAD_DIM)}
    fw = {"w_out": w_out[l].astype(BF16), "norm2_g": norm2_g[l].reshape(1, D_MODEL),
          "w_up": ffn_w_up[l].astype(BF16), "conv_w": ffn_conv_w[l], "w_down": ffn_w_down[l].astype(BF16),
          "normf_g": normf_g.reshape(1, D_MODEL)}

    u_p, qkv_p, z_p, ba_p = _proj(xp, g1, w_in_bf, PROJ_ROWS)
    u_tb = u_p.reshape(bsz, seq, D_S5).transpose(1, 0, 2).reshape(seq * bsz, D_S5)
    zeros_h = jnp.zeros((bsz, S5_CH), F32)
    ys5_tb, p_hr, p_hi = _s5(u_tb, zeros_h, zeros_h, sw, bsz, S5_STEPS)
    ys5_p = ys5_tb.reshape(seq, bsz, D_S5).transpose(1, 0, 2).reshape(bsz * seq, D_S5)
    ygdn_p, p_gdn, p_gdn_conv = _gdn_prompt(qkv_p, z_p, ba_p, gw, bsz, seq, GDN_ROWS)
    y_p, p_ffn_conv = _ffn_prompt(xp, ys5_p, ygdn_p, fw, bsz, seq, FFN_ROWS)

    u_s, qkv_s, z_s, ba_s = _proj(xs, g1, w_in_bf, nsmp)
    ys5_s, s_hr, s_hi = _s5(u_s, state_s5_re[l].reshape(nsmp, S5_CH), state_s5_im[l].reshape(nsmp, S5_CH),
                            sw, nsmp, 1)
    gconv_t = state_gdn_conv[l].transpose(1, 0, 2)
    ygdn_s, s_gdn = _gdn_sample(qkv_s, gconv_t, z_s, ba_s, state_gdn[l], gw, GDN_SAMPLE_ROWS)
    fconv_t = state_ffn_conv[l].transpose(1, 0, 2)
    y_s, h_s = _ffn_sample(xs, ys5_s, ygdn_s, fconv_t, fw)
    s_gdn_conv = jnp.concatenate([state_gdn_conv[l][:, 1:], qkv_s[:, None, :]], axis=1)
    s_ffn_conv = jnp.concatenate([state_ffn_conv[l][:, 1:], h_s[:, None, :]], axis=1)

    st = lambda a: a[None]
    s5_shape = lambda a, n: a.reshape(1, n, S5_N_GROUPS, S5_STATE)
    return (y_p.reshape(bsz, seq, D_MODEL), y_s.reshape(nsmp, 1, D_MODEL),
            s5_shape(p_hr, bsz), s5_shape(p_hi, bsz), st(p_gdn), st(p_gdn_conv), st(p_ffn_conv),
            s5_shape(s_hr, nsmp), s5_shape(s_hi, nsmp), st(s_gdn), st(s_gdn_conv), st(s_ffn_conv))
```

```python
import functools

import jax
import jax.numpy as jnp
from jax import lax
from jax.experimental import pallas as pl
from jax.experimental.pallas import tpu as pltpu

F32 = jnp.float32
BF16 = jnp.bfloat16
NORM_EPS = 1e-6

D_MODEL = 1024
D_S5 = 512
S5_GROUP = 16
S5_N_GROUPS = 32
S5_STATE = 64
S5_CH = S5_N_GROUPS * S5_STATE
D_GDN = 512
GDN_HEADS = 4
GDN_HEAD_DIM = 128
GDN_CONV = 4
D_FF = 2816
FFN_CONV = 3
D_IN = D_S5 + 4 * D_GDN + 2 * GDN_HEADS
GATE_PAD = 128
D_IN_PAD = D_S5 + 4 * D_GDN + GATE_PAD

CHUNK = 64
FFN_COLS = 256
VMEM_LIMIT = 56 * 1024 * 1024


def _dot(a, b):
    return jnp.dot(a, b, preferred_element_type=F32)


def _dot_hi(a, b):
    return jnp.dot(a, b, preferred_element_type=F32, precision=lax.Precision.HIGHEST)


def _dot_nt(a, b):
    return lax.dot_general(a, b, (((1,), (1,)), ((), ())), preferred_element_type=F32)


def _rms(x, g):
    ms = jnp.mean(x * x, axis=-1, keepdims=True)
    return x * lax.rsqrt(ms + NORM_EPS) * g


def _silu(x):
    return x * jax.nn.sigmoid(x)


def _rows_back(ext, back, n):
    return pltpu.roll(ext, back, 0)[8:8 + n]


def _softplus(x):
    return jnp.maximum(x, 0.0) + jnp.log1p(jnp.exp(-jnp.abs(x)))


def _proj_kernel(x_ref, g_ref, w_ref, u_ref, qkv_ref, z_ref, ba_ref):
    n = _rms(x_ref[...], g_ref[...]).astype(BF16)
    o0 = D_S5
    o1 = o0 + 3 * D_GDN
    o2 = o1 + D_GDN
    u_ref[...] = _dot(n, w_ref[:, :o0])
    qkv_ref[...] = _dot(n, w_ref[:, o0:o1])
    z_ref[...] = _dot(n, w_ref[:, o1:o2])
    ba_ref[...] = _dot(n, w_ref[:, o2:])


def _proj(x, g1, w_in_bf, tm):
    n = x.shape[0]
    row = lambda i: (i, 0)
    const = lambda i: (0, 0)
    return pl.pallas_call(
        _proj_kernel,
        grid=(n // tm,),
        in_specs=[pl.BlockSpec((tm, D_MODEL), row),
                  pl.BlockSpec((1, D_MODEL), const),
                  pl.BlockSpec((D_MODEL, D_IN_PAD), const)],
        out_specs=[pl.BlockSpec((tm, D_S5), row),
                   pl.BlockSpec((tm, 3 * D_GDN), row),
                   pl.BlockSpec((tm, D_GDN), row),
                   pl.BlockSpec((tm, GATE_PAD), row)],
        out_shape=[jax.ShapeDtypeStruct((n, D_S5), F32),
                   jax.ShapeDtypeStruct((n, 3 * D_GDN), F32),
                   jax.ShapeDtypeStruct((n, D_GDN), F32),
                   jax.ShapeDtypeStruct((n, GATE_PAD), F32)],
        compiler_params=pltpu.CompilerParams(dimension_semantics=("arbitrary",),
                                             vmem_limit_bytes=VMEM_LIMIT),
        name="proj",
    )(x, g1, w_in_bf)


S5_HALF = S5_CH // 2
S5_COLS = 512


def _s5_kernel(u_ref, h0r_ref, h0i_ref, are_ref, aim_ref, ldt_ref, bre_ref, bim_ref, cre_ref, cim_ref,
               d_ref, wglu_ref, y_ref, hr_out, hi_out,
               abr_s, abi_s, wbr_s, wbi_s, hr_s, hi_s, xr_s, xi_s, *, nb, tt):
    @pl.when(pl.program_id(0) == 0)
    def _init():
        ar = are_ref[...]
        ai = aim_ref[...]
        dt = jnp.exp(ldt_ref[...])
        mag = jnp.exp(ar * dt)
        abr = mag * jnp.cos(ai * dt)
        abi = mag * jnp.sin(ai * dt)
        den = ar * ar + ai * ai
        p = abr - 1.0
        fr = (p * ar + abi * ai) / den
        fi = (abi * ar - p * ai) / den
        abr_s[...] = abr
        abi_s[...] = abi
        for m in range(2):
            frm = fr[:, m * S5_HALF:(m + 1) * S5_HALF]
            fim = fi[:, m * S5_HALF:(m + 1) * S5_HALF]
            wr = bre_ref[m]
            wi = bim_ref[m]
            wbr_s[m] = (wr * frm - wi * fim).astype(BF16)
            wbi_s[m] = (wr * fim + wi * frm).astype(BF16)
        hr_s[...] = h0r_ref[...]
        hi_s[...] = h0i_ref[...]

    u = u_ref[...]
    ub = u.astype(BF16)
    half = D_S5 // 2
    for m in range(2):
        um = ub[:, m * half:(m + 1) * half]
        xr_s[:, m * S5_HALF:(m + 1) * S5_HALF] = _dot(um, wbr_s[m])
        xi_s[:, m * S5_HALF:(m + 1) * S5_HALF] = _dot(um, wbi_s[m])

    for cb in range(S5_CH // S5_COLS):
        cols = slice(cb * S5_COLS, (cb + 1) * S5_COLS)
        a_r = jnp.broadcast_to(abr_s[:, cols], (nb, S5_COLS))
        a_i = jnp.broadcast_to(abi_s[:, cols], (nb, S5_COLS))

        def step(t, carry, cols=cols, a_r=a_r, a_i=a_i):
            hr, hi = carry
            r0 = pl.multiple_of(t * nb, nb)
            nhr = a_r * hr - a_i * hi + xr_s[pl.ds(r0, nb), cols]
            nhi = a_r * hi + a_i * hr + xi_s[pl.ds(r0, nb), cols]
            xr_s[pl.ds(r0, nb), cols] = nhr
            xi_s[pl.ds(r0, nb), cols] = nhi
            return nhr, nhi

        hr, hi = lax.fori_loop(0, tt, step, (hr_s[:, cols], hi_s[:, cols]), unroll=min(tt, 8))
        hr_s[:, cols] = hr
        hi_s[:, cols] = hi

    hr_out[...] = hr_s[...]
    hi_out[...] = hi_s[...]

    ys = []
    for m in range(2):
        hre = xr_s[:, m * S5_HALF:(m + 1) * S5_HALF].astype(BF16)
        him = xi_s[:, m * S5_HALF:(m + 1) * S5_HALF].astype(BF16)
        ych = _dot(hre, cre_ref[m]) - _dot(him, cim_ref[m])
        ys.append(ych + d_ref[:, m * half:(m + 1) * half] * u[:, m * half:(m + 1) * half])
    y = jax.nn.gelu(jnp.concatenate(ys, axis=-1)).astype(BF16)
    gl = _dot(y, wglu_ref[...])
    y_ref[...] = gl[:, :D_S5] * jax.nn.sigmoid(gl[:, D_S5:])


def _s5(u_tb, h0r, h0i, sw, nb, tt):
    n = u_tb.shape[0]
    rows = nb * tt
    const2 = lambda i: (0, 0)
    const3 = lambda i: (0, 0, 0)
    full2 = lambda shape: pl.BlockSpec(shape, const2)
    full3 = lambda shape: pl.BlockSpec(shape, const3)
    return pl.pallas_call(
        functools.partial(_s5_kernel, nb=nb, tt=tt),
        grid=(n // rows,),
        in_specs=[pl.BlockSpec((rows, D_S5), lambda i: (i, 0)),
                  full2((nb, S5_CH)), full2((nb, S5_CH)),
                  full2((1, S5_CH)), full2((1, S5_CH)), full2((1, S5_CH)),
                  full3((2, D_S5 // 2, S5_HALF)), full3((2, D_S5 // 2, S5_HALF)),
                  full3((2, S5_HALF, D_S5 // 2)), full3((2, S5_HALF, D_S5 // 2)),
                  full2((1, D_S5)), full2((D_S5, 2 * D_S5))],
        out_specs=[pl.BlockSpec((rows, D_S5), lambda i: (i, 0)),
                   full2((nb, S5_CH)), full2((nb, S5_CH))],
        out_shape=[jax.ShapeDtypeStruct((n, D_S5), F32),
                   jax.ShapeDtypeStruct((nb, S5_CH), F32),
                   jax.ShapeDtypeStruct((nb, S5_CH), F32)],
        scratch_shapes=[pltpu.VMEM((1, S5_CH), F32), pltpu.VMEM((1, S5_CH), F32),
                        pltpu.VMEM((2, D_S5 // 2, S5_HALF), BF16), pltpu.VMEM((2, D_S5 // 2, S5_HALF), BF16),
                        pltpu.VMEM((nb, S5_CH), F32), pltpu.VMEM((nb, S5_CH), F32),
                        pltpu.VMEM((rows, S5_CH), F32), pltpu.VMEM((rows, S5_CH), F32)],
        compiler_params=pltpu.CompilerParams(dimension_semantics=("arbitrary",),
                                             vmem_limit_bytes=VMEM_LIMIT),
        name="s5",
    )(u_tb, h0r, h0i, sw["a_re"], sw["a_im"], sw["log_dt"], sw["b_re"], sw["b_im"],
      sw["c_re"], sw["c_im"], sw["d"], sw["w_glu"])


def _s5_weights(a_re, a_im, log_dt, b_re, b_im, c_re, c_im, d, w_glu):
    g, n, c = S5_N_GROUPS, S5_STATE, S5_GROUP
    eye = jnp.eye(g // 2, dtype=F32)

    def b_blockdiag(b):
        bb = b.reshape(2, g // 2, n, c)
        return jnp.einsum("mgnc,gh->mgchn", bb, eye).reshape(2, (g // 2) * c, (g // 2) * n)

    def c_blockdiag(cm):
        cc = cm.reshape(2, g // 2, c, n)
        return jnp.einsum("mgcn,gh->mgnhc", cc, eye).reshape(2, (g // 2) * n, (g // 2) * c)

    return {
        "a_re": a_re.reshape(1, S5_CH), "a_im": a_im.reshape(1, S5_CH),
        "log_dt": jnp.repeat(log_dt, n).reshape(1, S5_CH),
        "b_re": b_blockdiag(b_re), "b_im": b_blockdiag(b_im),
        "c_re": c_blockdiag(c_re).astype(BF16), "c_im": c_blockdiag(c_im).astype(BF16),
        "d": d.reshape(1, D_S5), "w_glu": w_glu.astype(BF16),
    }


def _l2norm(x):
    return x * lax.rsqrt(jnp.sum(x * x, axis=-1, keepdims=True) + NORM_EPS)


def _gates(ba, alog_row, dtb_row):
    beta = jax.nn.sigmoid(ba)
    g = -jnp.exp(alog_row) * _softplus(ba + dtb_row)
    return beta, g


def _out_gate(o, og_row, z):
    return _rms(o, og_row) * _silu(z)


def _bdot(a, b):
    return jnp.einsum("bij,bjk->bik", a.astype(BF16), b.astype(BF16), preferred_element_type=F32)


def _tri_inverse_offdiag(lmat, ii, jj):
    n = -jnp.where(((ii >> 1) == (jj >> 1))[None], lmat, 0.0)
    shift = 1
    while (1 << shift) < CHUNK:
        same_big = (ii >> (shift + 1)) == (jj >> (shift + 1))
        same_small = (ii >> shift) == (jj >> shift)
        c = jnp.where((same_big & jnp.logical_not(same_small))[None], lmat, 0.0)
        w = c + _bdot(n, c)
        n = n - (w + _bdot(w, n))
        shift += 1
    return n


def _gdn_kernel(qkv_ref, z_ref, ba_ref, convw_ref, alog_ref, dtb_ref, og_ref,
                y_ref, sfin_ref, convout_ref, s_s, ext_s, set_a, set_b, *, tg, nt):
    i = pl.program_id(0)

    @pl.when(i == 0)
    def _zero_handover():
        for ref in set_b:
            ref[...] = jnp.zeros_like(ref)

    @pl.when(lax.rem(i, nt) == 0)
    def _new_sequence_conv():
        ext_s[0:8, :] = jnp.zeros((8, 3 * D_GDN), F32)

    @pl.when(lax.rem(jnp.maximum(i - 1, 0), nt) == 0)
    def _new_sequence_state():
        s_s[...] = jnp.zeros_like(s_s)

    args = (qkv_ref, z_ref, ba_ref, convw_ref, alog_ref, dtb_ref, og_ref, y_ref, sfin_ref, convout_ref,
            s_s, ext_s)

    @pl.when(lax.rem(i, 2) == 0)
    def _even():
        _gdn_step(*args, rd=set_b, wr=set_a, tg=tg)

    @pl.when(lax.rem(i, 2) == 1)
    def _odd():
        _gdn_step(*args, rd=set_a, wr=set_b, tg=tg)


def _gdn_step(qkv_ref, z_ref, ba_ref, convw_ref, alog_ref, dtb_ref, og_ref, y_ref, sfin_ref, convout_ref,
              s_s, ext_s, *, rd, wr, tg):
    nc = tg // CHUNK
    hcols = lambda h: slice(h * GDN_HEAD_DIM, (h + 1) * GDN_HEAD_DIM)
    og = og_ref[...]

    aq_r, b_r, o_r, gl_r = rd
    states = [s_s[h] for h in range(GDN_HEADS)]
    for c in range(nc):
        rows = slice(c * CHUNK, (c + 1) * CHUNK)
        for h in range(GDN_HEADS):
            sys = c * GDN_HEADS + h
            r = _dot(aq_r[sys], states[h].astype(BF16))
            o = r[GDN_HEAD_DIM:] + o_r[sys]
            states[h] = states[h] * gl_r[sys, 0:1, :] + r[:GDN_HEAD_DIM] + b_r[sys]
            y_ref[rows, hcols(h)] = _out_gate(o, og, z_ref[rows, hcols(h)])
    for h in range(GDN_HEADS):
        s_s[h] = states[h]
        sfin_ref[0, h] = states[h]

    aq_s, b_s, o_s, gl_s = wr
    ext_s[8:, :] = qkv_ref[...]

    ba = ba_ref[...]
    beta_all, g_all = _gates(ba, alog_ref[...], dtb_ref[...])

    ri = lax.broadcasted_iota(jnp.int32, (tg, tg), 0)
    ci = lax.broadcasted_iota(jnp.int32, (tg, tg), 1)
    lbd = (((ri >> 6) == (ci >> 6)) & (ci <= ri)).astype(F32)
    gc_col = _dot_hi(lbd, g_all)
    gc_row = gc_col.T

    ii = lax.broadcasted_iota(jnp.int32, (CHUNK, CHUNK), 0)
    jj = lax.broadcasted_iota(jnp.int32, (CHUNK, CHUNK), 1)
    causal = jj <= ii
    strict = jj < ii
    scale = GDN_HEAD_DIM ** -0.5

    nsys = nc * GDN_HEADS
    lmats = [None] * nsys
    rhss = [None] * nsys
    kgt_attn = [None] * nsys
    qgs = [None] * nsys
    for h in range(GDN_HEADS):
        def conv_cols(part, h=h):
            c0 = part * D_GDN + h * GDN_HEAD_DIM
            cs = slice(c0, c0 + GDN_HEAD_DIM)
            ext = ext_s[:, cs]
            acc = ext[8:] * convw_ref[GDN_CONV - 1:GDN_CONV, cs]
            for j in range(GDN_CONV - 1):
                acc = acc + _rows_back(ext, GDN_CONV - 1 - j, tg) * convw_ref[j:j + 1, cs]
            return _silu(acc)

        q_all = _l2norm(conv_cols(0)) * scale
        k_all = _l2norm(conv_cols(1))
        v_all = conv_cols(2)
        for c in range(nc):
            sys = c * GDN_HEADS + h
            rows = slice(c * CHUNK, (c + 1) * CHUNK)
            q = q_all[rows]
            k = k_all[rows]
            v = v_all[rows]
            beta = beta_all[rows, h:h + 1]
            gcc = gc_col[rows, GDN_HEADS + h:GDN_HEADS + h + 1]
            gcr = gc_row[GDN_HEADS + h:GDN_HEADS + h + 1, rows]
            decay = jnp.exp(jnp.where(causal, gcc - gcr, -jnp.inf))
            kb = k * beta
            egc = jnp.exp(gcc)
            g_last = gcc[CHUNK - 1:CHUNK, :]
            kq = _dot_nt(jnp.concatenate([kb, q], axis=0).astype(BF16), k.astype(BF16))
            lmats[sys] = jnp.where(strict, kq[:CHUNK] * decay, 0.0)
            attn = jnp.where(causal, kq[CHUNK:] * decay, 0.0)
            rhss[sys] = jnp.concatenate([v * beta, kb * egc], axis=1)
            kgt_attn[sys] = jnp.concatenate([(k * jnp.exp(g_last - gcc)).T, attn], axis=0)
            qgs[sys] = q * egc
            gl_s[sys] = jnp.broadcast_to(jnp.exp(g_last), (8, GDN_HEAD_DIM))

    noff = _tri_inverse_offdiag(jnp.stack(lmats), ii, jj)
    rhs = jnp.stack(rhss)
    sol = rhs + _bdot(noff, rhs)
    prod = _bdot(jnp.stack(kgt_attn), sol)
    dk = GDN_HEAD_DIM
    for sys in range(nsys):
        aq_s[sys, :dk, :] = (-prod[sys, :dk, dk:]).astype(BF16)
        aq_s[sys, dk:, :] = (qgs[sys] - prod[sys, dk:, dk:]).astype(BF16)
        b_s[sys] = prod[sys, :dk, :dk]
        o_s[sys] = prod[sys, dk:, :dk]

    ext_s[0:8, :] = ext_s[pl.ds(tg, 8), :]
    convout_ref[0] = ext_s[pl.ds(8 - (GDN_CONV - 1), GDN_CONV - 1), :]


def _gdn_prompt(qkv, z, ba, gw, bsz, seq, tg):
    nt = seq // tg
    ntiles = bsz * nt
    nsys = (tg // CHUNK) * GDN_HEADS
    cur = lambda i: jnp.minimum(i, ntiles - 1)
    prev = lambda i: jnp.maximum(i - 1, 0)
    const = lambda i: (0, 0)
    handover = lambda: [pltpu.VMEM((nsys, GDN_HEAD_DIM + CHUNK, GDN_HEAD_DIM), BF16),
                        pltpu.VMEM((nsys, GDN_HEAD_DIM, GDN_HEAD_DIM), F32),
                        pltpu.VMEM((nsys, CHUNK, GDN_HEAD_DIM), F32),
                        pltpu.VMEM((nsys, 8, GDN_HEAD_DIM), F32)]
    return pl.pallas_call(
        functools.partial(_gdn_kernel, tg=tg, nt=nt),
        grid=(ntiles + 1,),
        in_specs=[pl.BlockSpec((tg, 3 * D_GDN), lambda i: (cur(i), 0)),
                  pl.BlockSpec((tg, D_GDN), lambda i: (prev(i), 0)),
                  pl.BlockSpec((tg, GATE_PAD), lambda i: (cur(i), 0)),
                  pl.BlockSpec((GDN_CONV, 3 * D_GDN), const),
                  pl.BlockSpec((1, GATE_PAD), const),
                  pl.BlockSpec((1, GATE_PAD), const),
                  pl.BlockSpec((1, GDN_HEAD_DIM), const)],
        out_specs=[pl.BlockSpec((tg, D_GDN), lambda i: (prev(i), 0)),
                   pl.BlockSpec((1, GDN_HEADS, GDN_HEAD_DIM, GDN_HEAD_DIM), lambda i: (prev(i) // nt, 0, 0, 0)),
                   pl.BlockSpec((1, GDN_CONV - 1, 3 * D_GDN), lambda i: (cur(i) // nt, 0, 0))],
        out_shape=[jax.ShapeDtypeStruct((bsz * seq, D_GDN), F32),
                   jax.ShapeDtypeStruct((bsz, GDN_HEADS, GDN_HEAD_DIM, GDN_HEAD_DIM), F32),
                   jax.ShapeDtypeStruct((bsz, GDN_CONV - 1, 3 * D_GDN), F32)],
        scratch_shapes=[pltpu.VMEM((GDN_HEADS, GDN_HEAD_DIM, GDN_HEAD_DIM), F32),
                        pltpu.VMEM((tg + 8, 3 * D_GDN), F32),
                        handover(), handover()],
        compiler_params=pltpu.CompilerParams(dimension_semantics=("arbitrary",),
                                             vmem_limit_bytes=VMEM_LIMIT),
        name="gdn_prompt",
    )(qkv, z, ba, gw["conv_w"], gw["a_log"], gw["dt_bias"], gw["onorm_g"])


def _gdn_step_kernel(qkv_ref, st_ref, z_ref, ba_ref, s0_ref, convw_ref, alog_ref, dtb_ref, og_ref,
                     y_ref, s1_ref, *, bb):
    beta_all, g_all = _gates(ba_ref[...], alog_ref[...], dtb_ref[...])
    alpha_all = jnp.exp(g_all)
    scale = GDN_HEAD_DIM ** -0.5
    og = og_ref[...]

    for h in range(GDN_HEADS):
        def conv_cols(part, h=h):
            c0 = part * D_GDN + h * GDN_HEAD_DIM
            cs = slice(c0, c0 + GDN_HEAD_DIM)
            acc = qkv_ref[:, cs] * convw_ref[GDN_CONV - 1:GDN_CONV, cs]
            for j in range(GDN_CONV - 1):
                acc = acc + st_ref[j, :, cs] * convw_ref[j:j + 1, cs]
            return _silu(acc)

        q = _l2norm(conv_cols(0)) * scale
        k = _l2norm(conv_cols(1))
        v = conv_cols(2)
        beta = beta_all[:, h:h + 1]
        alpha = alpha_all[:, GDN_HEADS + h:GDN_HEADS + h + 1]
        qk = jnp.sum(q * k, axis=-1, keepdims=True)
        kt = k.T
        qt = q.T
        o_rows = []
        for b in range(bb):
            s0 = s0_ref[b, h]
            kcol = kt[:, b:b + 1]
            qcol = qt[:, b:b + 1]
            ks = jnp.sum(kcol * s0, axis=0, keepdims=True)
            qs = jnp.sum(qcol * s0, axis=0, keepdims=True)
            al = alpha[b:b + 1, :]
            v_new = beta[b:b + 1, :] * (v[b:b + 1, :] - al * ks)
            o_rows.append(al * qs + qk[b:b + 1, :] * v_new)
            s1_ref[b, h] = al * s0 + kcol * v_new
        o = jnp.concatenate(o_rows, axis=0)
        cs = slice(h * GDN_HEAD_DIM, (h + 1) * GDN_HEAD_DIM)
        y_ref[:, cs] = _out_gate(o, og, z_ref[:, cs])


def _gdn_sample(qkv, st_t, z, ba, s0, gw, bb):
    n = qkv.shape[0]
    row = lambda i: (i, 0)
    const = lambda i: (0, 0)
    state = lambda i: (i, 0, 0, 0)
    return pl.pallas_call(
        functools.partial(_gdn_step_kernel, bb=bb),
        grid=(n // bb,),
        in_specs=[pl.BlockSpec((bb, 3 * D_GDN), row),
                  pl.BlockSpec((GDN_CONV - 1, bb, 3 * D_GDN), lambda i: (0, i, 0)),
                  pl.BlockSpec((bb, D_GDN), row),
                  pl.BlockSpec((bb, GATE_PAD), row),
                  pl.BlockSpec((bb, GDN_HEADS, GDN_HEAD_DIM, GDN_HEAD_DIM), state),
                  pl.BlockSpec((GDN_CONV, 3 * D_GDN), const),
                  pl.BlockSpec((1, GATE_PAD), const),
                  pl.BlockSpec((1, GATE_PAD), const),
                  pl.BlockSpec((1, GDN_HEAD_DIM), const)],
        out_specs=[pl.BlockSpec((bb, D_GDN), row),
                   pl.BlockSpec((bb, GDN_HEADS, GDN_HEAD_DIM, GDN_HEAD_DIM), state)],
        out_shape=[jax.ShapeDtypeStruct((n, D_GDN), F32),
                   jax.ShapeDtypeStruct(s0.shape, F32)],
        compiler_params=pltpu.CompilerParams(dimension_semantics=("arbitrary",),
                                             vmem_limit_bytes=VMEM_LIMIT),
        name="gdn_sample",
    )(qkv, st_t, z, ba, s0, gw["conv_w"], gw["a_log"], gw["dt_bias"], gw["onorm_g"])


def _residual_in(x_ref, ys5_ref, ygdn_ref, wout_ref, g2_ref, x1_s, n2_s):
    x1 = (x_ref[...] + _dot(ys5_ref[...].astype(BF16), wout_ref[:D_S5, :])
          + _dot(ygdn_ref[...].astype(BF16), wout_ref[D_S5:, :]))
    x1_s[...] = x1
    n2_s[...] = _rms(x1, g2_ref[...]).astype(BF16)


def _ffn_prompt_kernel(x_ref, ys5_ref, ygdn_ref, wout_ref, g2_ref, wup_ref, cw_ref, wdn_ref, gf_ref,
                       y_ref, hlast_ref, x1_s, n2_s, act_s, carry_s, *, tm):
    ti = pl.program_id(1)
    nt = pl.num_programs(1)

    @pl.when(ti == 0)
    def _reset():
        carry_s[...] = jnp.zeros_like(carry_s)

    _residual_in(x_ref, ys5_ref, ygdn_ref, wout_ref, g2_ref, x1_s, n2_s)
    n2 = n2_s[...]
    for f in range(D_FF // FFN_COLS):
        halves = []
        for part in range(2):
            c0 = part * D_FF + f * FFN_COLS
            cs = slice(c0, c0 + FFN_COLS)
            hcur = _dot(n2, wup_ref[:, cs])
            ext = jnp.concatenate([carry_s[:, cs], hcur], axis=0)
            carry_s[:, cs] = hcur[tm - 8:, :]
            conv = (_rows_back(ext, 2, tm) * cw_ref[0:1, cs]
                    + _rows_back(ext, 1, tm) * cw_ref[1:2, cs]
                    + hcur * cw_ref[2:3, cs])
            halves.append(conv)
        act_s[:, f * FFN_COLS:(f + 1) * FFN_COLS] = (_silu(halves[0]) * halves[1]).astype(BF16)
    y_ref[...] = _rms(x1_s[...] + _dot(act_s[...], wdn_ref[...]), gf_ref[...])

    @pl.when(ti == nt - 1)
    def _fin():
        hlast_ref[0] = carry_s[8 - (FFN_CONV - 1):, :]


def _ffn_prompt(x, ys5, ygdn, fw, bsz, seq, tm):
    nt = seq // tm
    row = lambda b, i: (b * nt + i, 0)
    const = lambda b, i: (0, 0)
    resident = lambda shape: pl.BlockSpec(shape, const, pipeline_mode=pl.Buffered(1))
    return pl.pallas_call(
        functools.partial(_ffn_prompt_kernel, tm=tm),
        grid=(bsz, nt),
        in_specs=[pl.BlockSpec((tm, D_MODEL), row),
                  pl.BlockSpec((tm, D_S5), row),
                  pl.BlockSpec((tm, D_GDN), row),
                  resident((D_MODEL, D_MODEL)),
                  resident((1, D_MODEL)),
                  resident((D_MODEL, 2 * D_FF)),
                  resident((FFN_CONV, 2 * D_FF)),
                  resident((D_FF, D_MODEL)),
                  resident((1, D_MODEL))],
        out_specs=[pl.BlockSpec((tm, D_MODEL), row),
                   pl.BlockSpec((1, FFN_CONV - 1, 2 * D_FF), lambda b, i: (b, 0, 0))],
        out_shape=[jax.ShapeDtypeStruct((bsz * seq, D_MODEL), F32),
                   jax.ShapeDtypeStruct((bsz, FFN_CONV - 1, 2 * D_FF), F32)],
        scratch_shapes=[pltpu.VMEM((tm, D_MODEL), F32),
                        pltpu.VMEM((tm, D_MODEL), BF16),
                        pltpu.VMEM((tm, D_FF), BF16),
                        pltpu.VMEM((8, 2 * D_FF), F32)],
        compiler_params=pltpu.CompilerParams(dimension_semantics=("arbitrary", "arbitrary"),
                                             vmem_limit_bytes=VMEM_LIMIT),
        name="ffn_prompt",
    )(x, ys5, ygdn, fw["w_out"], fw["norm2_g"], fw["w_up"], fw["conv_w"], fw["w_down"], fw["normf_g"])


def _ffn_sample_kernel(x_ref, ys5_ref, ygdn_ref, st_ref, wout_ref, g2_ref, wup_ref, cw_ref, wdn_ref, gf_ref,
                       y_ref, h_ref, x1_s, n2_s, acc_s):
    _residual_in(x_ref, ys5_ref, ygdn_ref, wout_ref, g2_ref, x1_s, n2_s)
    n2 = n2_s[...]
    for f in range(D_FF // FFN_COLS):
        halves = []
        for part in range(2):
            c0 = part * D_FF + f * FFN_COLS
            cs = slice(c0, c0 + FFN_COLS)
            hcur = _dot(n2, wup_ref[:, cs])
            h_ref[:, cs] = hcur
            halves.append(st_ref[0, :, cs] * cw_ref[0:1, cs] + st_ref[1, :, cs] * cw_ref[1:2, cs]
                          + hcur * cw_ref[2:3, cs])
        act = (_silu(halves[0]) * halves[1]).astype(BF16)
        contrib = _dot(act, wdn_ref[f * FFN_COLS:(f + 1) * FFN_COLS, :])
        if f == 0:
            acc_s[...] = contrib
        else:
            acc_s[...] += contrib
    y_ref[...] = _rms(x1_s[...] + acc_s[...], gf_ref[...])


def _ffn_sample(x, ys5, ygdn, st_t, fw):
    n = x.shape[0]
    c2 = lambda i: (0, 0)
    c3 = lambda i: (0, 0, 0)
    return pl.pallas_call(
        _ffn_sample_kernel,
        grid=(1,),
        in_specs=[pl.BlockSpec((n, D_MODEL), c2),
                  pl.BlockSpec((n, D_S5), c2),
                  pl.BlockSpec((n, D_GDN), c2),
                  pl.BlockSpec((FFN_CONV - 1, n, 2 * D_FF), c3),
                  pl.BlockSpec((D_MODEL, D_MODEL), c2),
                  pl.BlockSpec((1, D_MODEL), c2),
                  pl.BlockSpec((D_MODEL, 2 * D_FF), c2),
                  pl.BlockSpec((FFN_CONV, 2 * D_FF), c2),
                  pl.BlockSpec((D_FF, D_MODEL), c2),
                  pl.BlockSpec((1, D_MODEL), c2)],
        out_specs=[pl.BlockSpec((n, D_MODEL), c2),
                   pl.BlockSpec((n, 2 * D_FF), c2)],
        out_shape=[jax.ShapeDtypeStruct((n, D_MODEL), F32),
                   jax.ShapeDtypeStruct((n, 2 * D_FF), F32)],
        scratch_shapes=[pltpu.VMEM((n, D_MODEL), F32),
                        pltpu.VMEM((n, D_MODEL), BF16),
                        pltpu.VMEM((n, D_MODEL), F32)],
        compiler_params=pltpu.CompilerParams(dimension_semantics=("arbitrary",),
                                             vmem_limit_bytes=VMEM_LIMIT),
        name="ffn_sample",
    )(x, ys5, ygdn, st_t, fw["w_out"], fw["norm2_g"], fw["w_up"], fw["conv_w"], fw["w_down"], fw["normf_g"])


PROJ_ROWS = 512
S5_STEPS = 64
GDN_ROWS = 256
FFN_ROWS = 512
GDN_SAMPLE_ROWS = 8


def kernel(x_prompt, x_sample, state_s5_re, state_s5_im, state_gdn, state_gdn_conv, state_ffn_conv, norm1_g, w_in, s5_a_re, s5_a_im, s5_log_dt, s5_b_re, s5_b_im, s5_c_re, s5_c_im, s5_d, s5_w_glu, gdn_conv_w, gdn_a_log, gdn_dt_bias, gdn_onorm_g, w_out, norm2_g, ffn_w_up, ffn_conv_w, ffn_w_down, normf_g):
    depth = w_in.shape[0]
    assert depth == 1, "the final rmsnorm is fused into the last layer's ffn kernel"
    bsz, seq, _ = x_prompt.shape
    nsmp = x_sample.shape[0]
    assert x_sample.shape[1] == 1
    l = 0

    xp = x_prompt.reshape(bsz * seq, D_MODEL)
    xs = x_sample.reshape(nsmp, D_MODEL)

    g1 = norm1_g[l].reshape(1, D_MODEL)
    w_in_bf = jnp.pad(w_in[l], ((0, 0), (0, D_IN_PAD - D_IN))).astype(BF16)
    sw = _s5_weights(s5_a_re[l], s5_a_im[l], s5_log_dt[l], s5_b_re[l], s5_b_im[l], s5_c_re[l], s5_c_im[l],
                     s5_d[l], s5_w_glu[l])
    gate_row = lambda v: jnp.pad(v, (GDN_HEADS, GATE_PAD - 2 * GDN_HEADS)).reshape(1, GATE_PAD)
    gw = {"conv_w": gdn_conv_w[l], "a_log": gate_row(gdn_a_log[l]), "dt_bias": gate_row(gdn_dt_bias[l]),
          "onorm_g": gdn_onorm_g[l].reshape(1, GDN_HEAD_DIM)}
    fw = {"w_out": w_out[l].astype(BF16), "norm2_g": norm2_g[l].reshape(1, D_MODEL),
          "w_up": ffn_w_up[l].astype(BF16), "conv_w": ffn_conv_w[l], "w_down": ffn_w_down[l].astype(BF16),
          "normf_g": normf_g.reshape(1, D_MODEL)}

    u_p, qkv_p, z_p, ba_p = _proj(xp, g1, w_in_bf, PROJ_ROWS)
    u_tb = u_p.reshape(bsz, seq, D_S5).transpose(1, 0, 2).reshape(seq * bsz, D_S5)
    zeros_h = jnp.zeros((bsz, S5_CH), F32)
    ys5_tb, p_hr, p_hi = _s5(u_tb, zeros_h, zeros_h, sw, bsz, S5_STEPS)
    ys5_p = ys5_tb.reshape(seq, bsz, D_S5).transpose(1, 0, 2).reshape(bsz * seq, D_S5)
    ygdn_p, p_gdn, p_gdn_conv = _gdn_prompt(qkv_p, z_p, ba_p, gw, bsz, seq, GDN_ROWS)
    y_p, p_ffn_conv = _ffn_prompt(xp, ys5_p, ygdn_p, fw, bsz, seq, FFN_ROWS)

    u_s, qkv_s, z_s, ba_s = _proj(xs, g1, w_in_bf, nsmp)
    ys5_s, s_hr, s_hi = _s5(u_s, state_s5_re[l].reshape(nsmp, S5_CH), state_s5_im[l].reshape(nsmp, S5_CH),
                            sw, nsmp, 1)
    gconv_t = state_gdn_conv[l].transpose(1, 0, 2)
    ygdn_s, s_gdn = _gdn_sample(qkv_s, gconv_t, z_s, ba_s, state_gdn[l], gw, GDN_SAMPLE_ROWS)
    fconv_t = state_ffn_conv[l].transpose(1, 0, 2)
    y_s, h_s = _ffn_sample(xs, ys5_s, ygdn_s, fconv_t, fw)
    s_gdn_conv = jnp.concatenate([state_gdn_conv[l][:, 1:], qkv_s[:, None, :]], axis=1)
    s_ffn_conv = jnp.concatenate([state_ffn_conv[l][:, 1:], h_s[:, None, :]], axis=1)

    st = lambda a: a[None]
    s5_shape = lambda a, n: a.reshape(1, n, S5_N_GROUPS, S5_STATE)
    return (y_p.reshape(bsz, seq, D_MODEL), y_s.reshape(nsmp, 1, D_MODEL),
            s5_shape(p_hr, bsz), s5_shape(p_hi, bsz), st(p_gdn), st(p_gdn_conv), st(p_ffn_conv),
            s5_shape(s_hr, nsmp), s5_shape(s_hi, nsmp), st(s_gdn), st(s_gdn_conv), st(s_ffn_conv))
```

```python
import functools

import jax
import jax.numpy as jnp
from jax import lax
from jax.experimental import pallas as pl
from jax.experimental.pallas import tpu as pltpu

F32 = jnp.float32
BF16 = jnp.bfloat16
NORM_EPS = 1e-6

D_MODEL = 1024
D_S5 = 512
S5_GROUP = 16
S5_N_GROUPS = 32
S5_STATE = 64
S5_CH = S5_N_GROUPS * S5_STATE
D_GDN = 512
GDN_HEADS = 4
GDN_HEAD_DIM = 128
GDN_CONV = 4
D_FF = 2816
FFN_CONV = 3
D_IN = D_S5 + 4 * D_GDN + 2 * GDN_HEADS
GATE_PAD = 128
D_IN_PAD = D_S5 + 4 * D_GDN + GATE_PAD

CHUNK = 64
FFN_COLS = 256
VMEM_LIMIT = 56 * 1024 * 1024


def _dot(a, b):
    return jnp.dot(a, b, preferred_element_type=F32)


def _dot_hi(a, b):
    return jnp.dot(a, b, preferred_element_type=F32, precision=lax.Precision.HIGHEST)


def _dot_nt(a, b):
    return lax.dot_general(a, b, (((1,), (1,)), ((), ())), preferred_element_type=F32)


def _rms(x, g):
    ms = jnp.mean(x * x, axis=-1, keepdims=True)
    return x * lax.rsqrt(ms + NORM_EPS) * g


def _silu(x):
    return x * jax.nn.sigmoid(x)


def _rows_back(ext, back, n):
    return pltpu.roll(ext, back, 0)[8:8 + n]


def _softplus(x):
    return jnp.maximum(x, 0.0) + jnp.log1p(jnp.exp(-jnp.abs(x)))


def _proj_kernel(x_ref, g_ref, w_ref, u_ref, qkv_ref, z_ref, ba_ref):
    n = _rms(x_ref[...], g_ref[...]).astype(BF16)
    o0 = D_S5
    o1 = o0 + 3 * D_GDN
    o2 = o1 + D_GDN
    u_ref[...] = _dot(n, w_ref[:, :o0])
    qkv_ref[...] = _dot(n, w_ref[:, o0:o1])
    z_ref[...] = _dot(n, w_ref[:, o1:o2])
    ba_ref[...] = _dot(n, w_ref[:, o2:])


def _proj(x, g1, w_in_bf, tm):
    n = x.shape[0]
    row = lambda i: (i, 0)
    const = lambda i: (0, 0)
    return pl.pallas_call(
        _proj_kernel,
        grid=(n // tm,),
        in_specs=[pl.BlockSpec((tm, D_MODEL), row),
                  pl.BlockSpec((1, D_MODEL), const),
                  pl.BlockSpec((D_MODEL, D_IN_PAD), const)],
        out_specs=[pl.BlockSpec((tm, D_S5), row),
                   pl.BlockSpec((tm, 3 * D_GDN), row),
                   pl.BlockSpec((tm, D_GDN), row),
                   pl.BlockSpec((tm, GATE_PAD), row)],
        out_shape=[jax.ShapeDtypeStruct((n, D_S5), F32),
                   jax.ShapeDtypeStruct((n, 3 * D_GDN), F32),
                   jax.ShapeDtypeStruct((n, D_GDN), F32),
                   jax.ShapeDtypeStruct((n, GATE_PAD), F32)],
        compiler_params=pltpu.CompilerParams(dimension_semantics=("arbitrary",),
                                             vmem_limit_bytes=VMEM_LIMIT),
        name="proj",
    )(x, g1, w_in_bf)


S5_HALF = S5_CH // 2
S5_COLS = 512


def _s5_kernel(u_ref, h0r_ref, h0i_ref, are_ref, aim_ref, ldt_ref, bre_ref, bim_ref, cre_ref, cim_ref,
               d_ref, wglu_ref, y_ref, hr_out, hi_out,
               abr_s, abi_s, wbr_s, wbi_s, hr_s, hi_s, xr_s, xi_s, *, nb, tt):
    @pl.when(pl.program_id(0) == 0)
    def _init():
        ar = are_ref[...]
        ai = aim_ref[...]
        dt = jnp.exp(ldt_ref[...])
        mag = jnp.exp(ar * dt)
        abr = mag * jnp.cos(ai * dt)
        abi = mag * jnp.sin(ai * dt)
        den = ar * ar + ai * ai
        p = abr - 1.0
        fr = (p * ar + abi * ai) / den
        fi = (abi * ar - p * ai) / den
        abr_s[...] = abr
        abi_s[...] = abi
        for m in range(2):
            frm = fr[:, m * S5_HALF:(m + 1) * S5_HALF]
            fim = fi[:, m * S5_HALF:(m + 1) * S5_HALF]
            wr = bre_ref[m]
            wi = bim_ref[m]
            wbr_s[m] = (wr * frm - wi * fim).astype(BF16)
            wbi_s[m] = (wr * fim + wi * frm).astype(BF16)
        hr_s[...] = h0r_ref[...]
        hi_s[...] = h0i_ref[...]

    u = u_ref[...]
    ub = u.astype(BF16)
    half = D_S5 // 2
    for m in range(2):
        um = ub[:, m * half:(m + 1) * half]
        xr_s[:, m * S5_HALF:(m + 1) * S5_HALF] = _dot(um, wbr_s[m])
        xi_s[:, m * S5_HALF:(m + 1) * S5_HALF] = _dot(um, wbi_s[m])

    for cb in range(S5_CH // S5_COLS):
        cols = slice(cb * S5_COLS, (cb + 1) * S5_COLS)
        a_r = jnp.broadcast_to(abr_s[:, cols], (nb, S5_COLS))
        a_i = jnp.broadcast_to(abi_s[:, cols], (nb, S5_COLS))

        def step(t, carry, cols=cols, a_r=a_r, a_i=a_i):
            hr, hi = carry
            r0 = pl.multiple_of(t * nb, nb)
            nhr = a_r * hr - a_i * hi + xr_s[pl.ds(r0, nb), cols]
            nhi = a_r * hi + a_i * hr + xi_s[pl.ds(r0, nb), cols]
            xr_s[pl.ds(r0, nb), cols] = nhr
            xi_s[pl.ds(r0, nb), cols] = nhi
            return nhr, nhi

        hr, hi = lax.fori_loop(0, tt, step, (hr_s[:, cols], hi_s[:, cols]), unroll=min(tt, 8))
        hr_s[:, cols] = hr
        hi_s[:, cols] = hi

    hr_out[...] = hr_s[...]
    hi_out[...] = hi_s[...]

    ys = []
    for m in range(2):
        hre = xr_s[:, m * S5_HALF:(m + 1) * S5_HALF].astype(BF16)
        him = xi_s[:, m * S5_HALF:(m + 1) * S5_HALF].astype(BF16)
        ych = _dot(hre, cre_ref[m]) - _dot(him, cim_ref[m])
        ys.append(ych + d_ref[:, m * half:(m + 1) * half] * u[:, m * half:(m + 1) * half])
    y = jax.nn.gelu(jnp.concatenate(ys, axis=-1)).astype(BF16)
    gl = _dot(y, wglu_ref[...])
    y_ref[...] = gl[:, :D_S5] * jax.nn.sigmoid(gl[:, D_S5:])


def _s5(u_tb, h0r, h0i, sw, nb, tt):
    n = u_tb.shape[0]
    rows = nb * tt
    const2 = lambda i: (0, 0)
    const3 = lambda i: (0, 0, 0)
    full2 = lambda shape: pl.BlockSpec(shape, const2)
    full3 = lambda shape: pl.BlockSpec(shape, const3)
    return pl.pallas_call(
        functools.partial(_s5_kernel, nb=nb, tt=tt),
        grid=(n // rows,),
        in_specs=[pl.BlockSpec((rows, D_S5), lambda i: (i, 0)),
                  full2((nb, S5_CH)), full2((nb, S5_CH)),
                  full2((1, S5_CH)), full2((1, S5_CH)), full2((1, S5_CH)),
                  full3((2, D_S5 // 2, S5_HALF)), full3((2, D_S5 // 2, S5_HALF)),
                  full3((2, S5_HALF, D_S5 // 2)), full3((2, S5_HALF, D_S5 // 2)),
                  full2((1, D_S5)), full2((D_S5, 2 * D_S5))],
        out_specs=[pl.BlockSpec((rows, D_S5), lambda i: (i, 0)),
                   full2((nb, S5_CH)), full2((nb, S5_CH))],
        out_shape=[jax.ShapeDtypeStruct((n, D_S5), F32),
                   jax.ShapeDtypeStruct((nb, S5_CH), F32),
                   jax.ShapeDtypeStruct((nb, S5_CH), F32)],
        scratch_shapes=[pltpu.VMEM((1, S5_CH), F32), pltpu.VMEM((1, S5_CH), F32),
                        pltpu.VMEM((2, D_S5 // 2, S5_HALF), BF16), pltpu.VMEM((2, D_S5 // 2, S5_HALF), BF16),
                        pltpu.VMEM((nb, S5_CH), F32), pltpu.VMEM((nb, S5_CH), F32),
                        pltpu.VMEM((rows, S5_CH), F32), pltpu.VMEM((rows, S5_CH), F32)],
        compiler_params=pltpu.CompilerParams(dimension_semantics=("arbitrary",),
                                             vmem_limit_bytes=VMEM_LIMIT),
        name="s5",
    )(u_tb, h0r, h0i, sw["a_re"], sw["a_im"], sw["log_dt"], sw["b_re"], sw["b_im"],
      sw["c_re"], sw["c_im"], sw["d"], sw["w_glu"])


def _s5_weights(a_re, a_im, log_dt, b_re, b_im, c_re, c_im, d, w_glu):
    g, n, c = S5_N_GROUPS, S5_STATE, S5_GROUP
    eye = jnp.eye(g // 2, dtype=F32)

    def b_blockdiag(b):
        bb = b.reshape(2, g // 2, n, c)
        return jnp.einsum("mgnc,gh->mgchn", bb, eye).reshape(2, (g // 2) * c, (g // 2) * n)

    def c_blockdiag(cm):
        cc = cm.reshape(2, g // 2, c, n)
        return jnp.einsum("mgcn,gh->mgnhc", cc, eye).reshape(2, (g // 2) * n, (g // 2) * c)

    return {
        "a_re": a_re.reshape(1, S5_CH), "a_im": a_im.reshape(1, S5_CH),
        "log_dt": jnp.repeat(log_dt, n).reshape(1, S5_CH),
        "b_re": b_blockdiag(b_re), "b_im": b_blockdiag(b_im),
        "c_re": c_blockdiag(c_re).astype(BF16), "c_im": c_blockdiag(c_im).astype(BF16),
        "d": d.reshape(1, D_S5), "w_glu": w_glu.astype(BF16),
    }


def _l2norm(x):
    return x * lax.rsqrt(jnp.sum(x * x, axis=-1, keepdims=True) + NORM_EPS)


def _gates(ba, alog_row, dtb_row):
    beta = jax.nn.sigmoid(ba)
    g = -jnp.exp(alog_row) * _softplus(ba + dtb_row)
    return beta, g


def _out_gate(o, og_row, z):
    return _rms(o, og_row) * _silu(z)


def _bdot(a, b):
    return jnp.einsum("bij,bjk->bik", a.astype(BF16), b.astype(BF16), preferred_element_type=F32)


def _tri_merge_level(n, lmat, ii, jj, shift):
    same_big = (ii >> (shift + 1)) == (jj >> (shift + 1))
    same_small = (ii >> shift) == (jj >> shift)
    c = jnp.where((same_big & jnp.logical_not(same_small))[None], lmat, 0.0)
    w = c + _bdot(n, c)
    return n - (w + _bdot(w, n))


def _interleave(*stages):
    out = []
    pos = [0] * len(stages)
    while any(p < len(st) for p, st in zip(pos, stages)):
        k = min((i for i, st in enumerate(stages) if pos[i] < len(st)),
                key=lambda i: (pos[i] + 0.5) / len(stages[i]))
        out.append(stages[k][pos[k]])
        pos[k] += 1
    return out


def _gdn_kernel(x_ref, g1_ref, win_ref, convw_ref, alog_ref, dtb_ref, og_ref,
                u_ref, y_ref, sfin_ref, convout_ref, s_s, proj_a, proj_b, set_a, set_b, *, tg, nt):
    s = pl.program_id(0)

    @pl.when(s == 0)
    def _zero_buffers():
        for ref in (*proj_b, *set_a):
            ref[...] = jnp.zeros_like(ref)

    @pl.when(lax.rem(jnp.maximum(s - 1, 0), nt) == 0)
    def _new_sequence_conv():
        for ext in (proj_a[0], proj_b[0]):
            ext[0:8, :] = jnp.zeros((8, 3 * D_GDN), F32)

    @pl.when(lax.rem(jnp.maximum(s - 2, 0), nt) == 0)
    def _new_sequence_state():
        s_s[...] = jnp.zeros_like(s_s)

    args = (x_ref, g1_ref, win_ref, convw_ref, alog_ref, dtb_ref, og_ref, u_ref, y_ref, sfin_ref, convout_ref, s_s)

    @pl.when(lax.rem(s, 2) == 0)
    def _even():
        _gdn_step(*args, proj_wr=proj_a, proj_rd=proj_b, wr=set_b, rd=set_a, tg=tg)

    @pl.when(lax.rem(s, 2) == 1)
    def _odd():
        _gdn_step(*args, proj_wr=proj_b, proj_rd=proj_a, wr=set_a, rd=set_b, tg=tg)


def _gdn_step(x_ref, g1_ref, win_ref, convw_ref, alog_ref, dtb_ref, og_ref, u_ref, y_ref, sfin_ref, convout_ref,
              s_s, *, proj_wr, proj_rd, wr, rd, tg):
    nc = tg // CHUNK
    nsys = nc * GDN_HEADS
    hcols = lambda h: slice(h * GDN_HEAD_DIM, (h + 1) * GDN_HEAD_DIM)
    og = og_ref[...]
    v = {}
    stage_a, stage_b, stage_c = [], [], []

    ext_w, z_w, ba_w = proj_wr
    o0 = D_S5
    o1 = o0 + 3 * D_GDN
    o2 = o1 + D_GDN

    def a_norm():
        v["n1"] = _rms(x_ref[...], g1_ref[...]).astype(BF16)
    stage_a.append(a_norm)

    def a_block(dst, rows, dst_c0, src_c0, width):
        def item():
            dst[rows, dst_c0:dst_c0 + width] = _dot(v["n1"], win_ref[:, src_c0:src_c0 + width])
        return item
    blk = 256
    for c0 in range(0, o0, blk):
        stage_a.append(a_block(u_ref, slice(None), c0, c0, blk))
    for c0 in range(0, o1 - o0, blk):
        stage_a.append(a_block(ext_w, slice(8, None), c0, o0 + c0, blk))
    for c0 in range(0, o2 - o1, blk):
        stage_a.append(a_block(z_w, slice(None), c0, o1 + c0, blk))
    stage_a.append(a_block(ba_w, slice(None), 0, o2, GATE_PAD))

    aq_r, b_r, o_r, gl_r, zs_r = rd
    states = [None] * GDN_HEADS

    def c_chunk(c):
        def item():
            rows = slice(c * CHUNK, (c + 1) * CHUNK)
            for h in range(GDN_HEADS):
                sys = c * GDN_HEADS + h
                st = s_s[h] if c == 0 else states[h]
                r = _dot(aq_r[sys], st.astype(BF16))
                o = r[GDN_HEAD_DIM:] + o_r[sys]
                states[h] = st * gl_r[sys, 0:1, :] + r[:GDN_HEAD_DIM] + b_r[sys]
                y_ref[rows, hcols(h)] = _rms(o, og) * zs_r[rows, hcols(h)]
                if c == nc - 1:
                    s_s[h] = states[h]
                    sfin_ref[0, h] = states[h]
        return item
    stage_c.extend(c_chunk(c) for c in range(nc))

    aq_s, b_s, o_s, gl_s, zs_s = wr
    ext_s, z_r, ba_r = proj_rd
    ii = lax.broadcasted_iota(jnp.int32, (CHUNK, CHUNK), 0)
    jj = lax.broadcasted_iota(jnp.int32, (CHUNK, CHUNK), 1)
    causal = jj <= ii
    strict = jj < ii
    scale = GDN_HEAD_DIM ** -0.5
    lmats = [None] * nsys
    rhss = [None] * nsys
    kgt_attn = [None] * nsys
    qgs = [None] * nsys

    def b_gates():
        beta_all, g_all = _gates(ba_r[...], alog_ref[...], dtb_ref[...])
        ri = lax.broadcasted_iota(jnp.int32, (tg, tg), 0)
        ci = lax.broadcasted_iota(jnp.int32, (tg, tg), 1)
        lbd = (((ri >> 6) == (ci >> 6)) & (ci <= ri)).astype(F32)
        v["beta_all"] = beta_all
        v["gc_col"] = _dot_hi(lbd, g_all)
        v["gc_row"] = v["gc_col"].T
    stage_b.append(b_gates)

    def b_silu_z(h):
        def item():
            zs_s[:, hcols(h)] = _silu(z_r[:, hcols(h)])
        return item

    def b_conv(h, part, name, norm):
        def item():
            c0 = part * D_GDN + h * GDN_HEAD_DIM
            cs = slice(c0, c0 + GDN_HEAD_DIM)
            ext = ext_s[:, cs]
            acc = ext[8:] * convw_ref[GDN_CONV - 1:GDN_CONV, cs]
            for j in range(GDN_CONV - 1):
                acc = acc + _rows_back(ext, GDN_CONV - 1 - j, tg) * convw_ref[j:j + 1, cs]
            acc = _silu(acc)
            v[name] = norm(acc)
        return item

    def b_prep(h, c):
        def item():
            beta_all, gc_col, gc_row = v["beta_all"], v["gc_col"], v["gc_row"]
            sys = c * GDN_HEADS + h
            rows = slice(c * CHUNK, (c + 1) * CHUNK)
            q = v["q"][rows]
            k = v["k"][rows]
            vv = v["v"][rows]
            beta = beta_all[rows, h:h + 1]
            gcc = gc_col[rows, GDN_HEADS + h:GDN_HEADS + h + 1]
            gcr = gc_row[GDN_HEADS + h:GDN_HEADS + h + 1, rows]
            decay = jnp.exp(jnp.where(causal, gcc - gcr, -jnp.inf))
            kb = k * beta
            egc = jnp.exp(gcc)
            g_last = gcc[CHUNK - 1:CHUNK, :]
            kq = _dot_nt(jnp.concatenate([kb, q], axis=0).astype(BF16), k.astype(BF16))
            lmats[sys] = jnp.where(strict, kq[:CHUNK] * decay, 0.0)
            attn = jnp.where(causal, kq[CHUNK:] * decay, 0.0)
            rhss[sys] = jnp.concatenate([vv * beta, kb * egc], axis=1)
            kgt_attn[sys] = jnp.concatenate([(k * jnp.exp(g_last - gcc)).T, attn], axis=0)
            qgs[sys] = q * egc
            gl_s[sys] = jnp.broadcast_to(jnp.exp(g_last), (8, GDN_HEAD_DIM))
        return item

    for h in range(GDN_HEADS):
        stage_b.append(b_silu_z(h))
        stage_b.append(b_conv(h, 0, "q", lambda a: _l2norm(a) * scale))
        stage_b.append(b_conv(h, 1, "k", _l2norm))
        stage_b.append(b_conv(h, 2, "v", lambda a: a))
        stage_b.extend(b_prep(h, c) for c in range(nc))

    def b_inverse_start():
        lmat = jnp.stack(lmats)
        v["lmat"] = lmat
        v["noff"] = -jnp.where(((ii >> 1) == (jj >> 1))[None], lmat, 0.0)
    stage_b.append(b_inverse_start)

    def b_inverse_level(shift):
        def item():
            v["noff"] = _tri_merge_level(v["noff"], v["lmat"], ii, jj, shift)
        return item
    shift = 1
    while (1 << shift) < CHUNK:
        stage_b.append(b_inverse_level(shift))
        shift += 1

    def b_solve():
        rhs = jnp.stack(rhss)
        sol = rhs + _bdot(v["noff"], rhs)
        v["prod"] = _bdot(jnp.stack(kgt_attn), sol)
    stage_b.append(b_solve)

    def b_store():
        prod = v["prod"]
        dk = GDN_HEAD_DIM
        for sys in range(nsys):
            aq_s[sys, :dk, :] = (-prod[sys, :dk, dk:]).astype(BF16)
            aq_s[sys, dk:, :] = (qgs[sys] - prod[sys, dk:, dk:]).astype(BF16)
            b_s[sys] = prod[sys, :dk, :dk]
            o_s[sys] = prod[sys, dk:, :dk]
    stage_b.append(b_store)

    for item in _interleave(stage_a, stage_b, stage_c):
        item()

    ext_w[0:8, :] = ext_s[pl.ds(tg, 8), :]
    convout_ref[0] = ext_s[pl.ds(tg + 8 - (GDN_CONV - 1), GDN_CONV - 1), :]


def _gdn_prompt(x, g1, w_in_bf, gw, bsz, seq, tg):
    nt = seq // tg
    ntiles = bsz * nt
    nsys = (tg // CHUNK) * GDN_HEADS
    tile_a = lambda s: jnp.minimum(s, ntiles - 1)
    tile_b = lambda s: jnp.clip(s - 1, 0, ntiles - 1)
    tile_c = lambda s: jnp.maximum(s - 2, 0)
    const = lambda s: (0, 0)
    proj_bufs = lambda: [pltpu.VMEM((tg + 8, 3 * D_GDN), F32),
                         pltpu.VMEM((tg, D_GDN), F32),
                         pltpu.VMEM((tg, GATE_PAD), F32)]
    resident = lambda shape: pl.BlockSpec(shape, const, pipeline_mode=pl.Buffered(1))
    handover = lambda: [pltpu.VMEM((nsys, GDN_HEAD_DIM + CHUNK, GDN_HEAD_DIM), BF16),
                        pltpu.VMEM((nsys, GDN_HEAD_DIM, GDN_HEAD_DIM), F32),
                        pltpu.VMEM((nsys, CHUNK, GDN_HEAD_DIM), F32),
                        pltpu.VMEM((nsys, 8, GDN_HEAD_DIM), F32),
                        pltpu.VMEM((tg, D_GDN), F32)]
    return pl.pallas_call(
        functools.partial(_gdn_kernel, tg=tg, nt=nt),
        grid=(ntiles + 2,),
        in_specs=[pl.BlockSpec((tg, D_MODEL), lambda s: (tile_a(s), 0)),
                  resident((1, D_MODEL)),
                  resident((D_MODEL, D_IN_PAD)),
                  resident((GDN_CONV, 3 * D_GDN)),
                  resident((1, GATE_PAD)),
                  resident((1, GATE_PAD)),
                  resident((1, GDN_HEAD_DIM))],
        out_specs=[pl.BlockSpec((tg, D_S5), lambda s: (tile_a(s), 0)),
                   pl.BlockSpec((tg, D_GDN), lambda s: (tile_c(s), 0)),
                   pl.BlockSpec((1, GDN_HEADS, GDN_HEAD_DIM, GDN_HEAD_DIM), lambda s: (tile_c(s) // nt, 0, 0, 0)),
                   pl.BlockSpec((1, GDN_CONV - 1, 3 * D_GDN), lambda s: (tile_b(s) // nt, 0, 0))],
        out_shape=[jax.ShapeDtypeStruct((bsz * seq, D_S5), F32),
                   jax.ShapeDtypeStruct((bsz * seq, D_GDN), F32),
                   jax.ShapeDtypeStruct((bsz, GDN_HEADS, GDN_HEAD_DIM, GDN_HEAD_DIM), F32),
                   jax.ShapeDtypeStruct((bsz, GDN_CONV - 1, 3 * D_GDN), F32)],
        scratch_shapes=[pltpu.VMEM((GDN_HEADS, GDN_HEAD_DIM, GDN_HEAD_DIM), F32),
                        proj_bufs(), proj_bufs(), handover(), handover()],
        compiler_params=pltpu.CompilerParams(dimension_semantics=("arbitrary",),
                                             vmem_limit_bytes=VMEM_LIMIT),
        name="proj_gdn_prompt",
    )(x, g1, w_in_bf, gw["conv_w"], gw["a_log"], gw["dt_bias"], gw["onorm_g"])


def _gdn_step_kernel(qkv_ref, st_ref, z_ref, ba_ref, s0_ref, convw_ref, alog_ref, dtb_ref, og_ref,
                     y_ref, s1_ref, *, bb):
    beta_all, g_all = _gates(ba_ref[...], alog_ref[...], dtb_ref[...])
    alpha_all = jnp.exp(g_all)
    scale = GDN_HEAD_DIM ** -0.5
    og = og_ref[...]

    for h in range(GDN_HEADS):
        def conv_cols(part, h=h):
            c0 = part * D_GDN + h * GDN_HEAD_DIM
            cs = slice(c0, c0 + GDN_HEAD_DIM)
            acc = qkv_ref[:, cs] * convw_ref[GDN_CONV - 1:GDN_CONV, cs]
            for j in range(GDN_CONV - 1):
                acc = acc + st_ref[j, :, cs] * convw_ref[j:j + 1, cs]
            return _silu(acc)

        q = _l2norm(conv_cols(0)) * scale
        k = _l2norm(conv_cols(1))
        v = conv_cols(2)
        beta = beta_all[:, h:h + 1]
        alpha = alpha_all[:, GDN_HEADS + h:GDN_HEADS + h + 1]
        qk = jnp.sum(q * k, axis=-1, keepdims=True)
        kt = k.T
        qt = q.T
        o_rows = []
        for b in range(bb):
            s0 = s0_ref[b, h]
            kcol = kt[:, b:b + 1]
            qcol = qt[:, b:b + 1]
            ks = jnp.sum(kcol * s0, axis=0, keepdims=True)
            qs = jnp.sum(qcol * s0, axis=0, keepdims=True)
            al = alpha[b:b + 1, :]
            v_new = beta[b:b + 1, :] * (v[b:b + 1, :] - al * ks)
            o_rows.append(al * qs + qk[b:b + 1, :] * v_new)
            s1_ref[b, h] = al * s0 + kcol * v_new
        o = jnp.concatenate(o_rows, axis=0)
        cs = slice(h * GDN_HEAD_DIM, (h + 1) * GDN_HEAD_DIM)
        y_ref[:, cs] = _out_gate(o, og, z_ref[:, cs])


def _gdn_sample(qkv, st_t, z, ba, s0, gw, bb):
    n = qkv.shape[0]
    row = lambda i: (i, 0)
    const = lambda i: (0, 0)
    state = lambda i: (i, 0, 0, 0)
    return pl.pallas_call(
        functools.partial(_gdn_step_kernel, bb=bb),
        grid=(n // bb,),
        in_specs=[pl.BlockSpec((bb, 3 * D_GDN), row),
                  pl.BlockSpec((GDN_CONV - 1, bb, 3 * D_GDN), lambda i: (0, i, 0)),
                  pl.BlockSpec((bb, D_GDN), row),
                  pl.BlockSpec((bb, GATE_PAD), row),
                  pl.BlockSpec((bb, GDN_HEADS, GDN_HEAD_DIM, GDN_HEAD_DIM), state),
                  pl.BlockSpec((GDN_CONV, 3 * D_GDN), const),
                  pl.BlockSpec((1, GATE_PAD), const),
                  pl.BlockSpec((1, GATE_PAD), const),
                  pl.BlockSpec((1, GDN_HEAD_DIM), const)],
        out_specs=[pl.BlockSpec((bb, D_GDN), row),
                   pl.BlockSpec((bb, GDN_HEADS, GDN_HEAD_DIM, GDN_HEAD_DIM), state)],
        out_shape=[jax.ShapeDtypeStruct((n, D_GDN), F32),
                   jax.ShapeDtypeStruct(s0.shape, F32)],
        compiler_params=pltpu.CompilerParams(dimension_semantics=("arbitrary",),
                                             vmem_limit_bytes=VMEM_LIMIT),
        name="gdn_sample",
    )(qkv, st_t, z, ba, s0, gw["conv_w"], gw["a_log"], gw["dt_bias"], gw["onorm_g"])


def _residual_in(x_ref, ys5_ref, ygdn_ref, wout_ref, g2_ref, x1_s, n2_s):
    x1 = (x_ref[...] + _dot(ys5_ref[...].astype(BF16), wout_ref[:D_S5, :])
          + _dot(ygdn_ref[...].astype(BF16), wout_ref[D_S5:, :]))
    x1_s[...] = x1
    n2_s[...] = _rms(x1, g2_ref[...]).astype(BF16)


def _ffn_prompt_kernel(x_ref, ys5_ref, ygdn_ref, wout_ref, g2_ref, wup_ref, cw_ref, wdn_ref, gf_ref,
                       y_ref, hlast_ref, x1_s, n2_s, act_s, carry_s, *, tm):
    ti = pl.program_id(1)
    nt = pl.num_programs(1)

    @pl.when(ti == 0)
    def _reset():
        carry_s[...] = jnp.zeros_like(carry_s)

    _residual_in(x_ref, ys5_ref, ygdn_ref, wout_ref, g2_ref, x1_s, n2_s)
    n2 = n2_s[...]
    for f in range(D_FF // FFN_COLS):
        halves = []
        for part in range(2):
            c0 = part * D_FF + f * FFN_COLS
            cs = slice(c0, c0 + FFN_COLS)
            hcur = _dot(n2, wup_ref[:, cs])
            ext = jnp.concatenate([carry_s[:, cs], hcur], axis=0)
            carry_s[:, cs] = hcur[tm - 8:, :]
            conv = (_rows_back(ext, 2, tm) * cw_ref[0:1, cs]
                    + _rows_back(ext, 1, tm) * cw_ref[1:2, cs]
                    + hcur * cw_ref[2:3, cs])
            halves.append(conv)
        act_s[:, f * FFN_COLS:(f + 1) * FFN_COLS] = (_silu(halves[0]) * halves[1]).astype(BF16)
    y_ref[...] = _rms(x1_s[...] + _dot(act_s[...], wdn_ref[...]), gf_ref[...])

    @pl.when(ti == nt - 1)
    def _fin():
        hlast_ref[0] = carry_s[8 - (FFN_CONV - 1):, :]


def _ffn_prompt(x, ys5, ygdn, fw, bsz, seq, tm):
    nt = seq // tm
    row = lambda b, i: (b * nt + i, 0)
    const = lambda b, i: (0, 0)
    resident = lambda shape: pl.BlockSpec(shape, const, pipeline_mode=pl.Buffered(1))
    return pl.pallas_call(
        functools.partial(_ffn_prompt_kernel, tm=tm),
        grid=(bsz, nt),
        in_specs=[pl.BlockSpec((tm, D_MODEL), row),
                  pl.BlockSpec((tm, D_S5), row),
                  pl.BlockSpec((tm, D_GDN), row),
                  resident((D_MODEL, D_MODEL)),
                  resident((1, D_MODEL)),
                  resident((D_MODEL, 2 * D_FF)),
                  resident((FFN_CONV, 2 * D_FF)),
                  resident((D_FF, D_MODEL)),
                  resident((1, D_MODEL))],
        out_specs=[pl.BlockSpec((tm, D_MODEL), row),
                   pl.BlockSpec((1, FFN_CONV - 1, 2 * D_FF), lambda b, i: (b, 0, 0))],
        out_shape=[jax.ShapeDtypeStruct((bsz * seq, D_MODEL), F32),
                   jax.ShapeDtypeStruct((bsz, FFN_CONV - 1, 2 * D_FF), F32)],
        scratch_shapes=[pltpu.VMEM((tm, D_MODEL), F32),
                        pltpu.VMEM((tm, D_MODEL), BF16),
                        pltpu.VMEM((tm, D_FF), BF16),
                        pltpu.VMEM((8, 2 * D_FF), F32)],
        compiler_params=pltpu.CompilerParams(dimension_semantics=("arbitrary", "arbitrary"),
                                             vmem_limit_bytes=VMEM_LIMIT),
        name="ffn_prompt",
    )(x, ys5, ygdn, fw["w_out"], fw["norm2_g"], fw["w_up"], fw["conv_w"], fw["w_down"], fw["normf_g"])


def _ffn_sample_kernel(x_ref, ys5_ref, ygdn_ref, st_ref, wout_ref, g2_ref, wup_ref, cw_ref, wdn_ref, gf_ref,
                       y_ref, h_ref, x1_s, n2_s, acc_s):
    _residual_in(x_ref, ys5_ref, ygdn_ref, wout_ref, g2_ref, x1_s, n2_s)
    n2 = n2_s[...]
    for f in range(D_FF // FFN_COLS):
        halves = []
        for part in range(2):
            c0 = part * D_FF + f * FFN_COLS
            cs = slice(c0, c0 + FFN_COLS)
            hcur = _dot(n2, wup_ref[:, cs])
            h_ref[:, cs] = hcur
            halves.append(st_ref[0, :, cs] * cw_ref[0:1, cs] + st_ref[1, :, cs] * cw_ref[1:2, cs]
                          + hcur * cw_ref[2:3, cs])
        act = (_silu(halves[0]) * halves[1]).astype(BF16)
        contrib = _dot(act, wdn_ref[f * FFN_COLS:(f + 1) * FFN_COLS, :])
        if f == 0:
            acc_s[...] = contrib
        else:
            acc_s[...] += contrib
    y_ref[...] = _rms(x1_s[...] + acc_s[...], gf_ref[...])


def _ffn_sample(x, ys5, ygdn, st_t, fw):
    n = x.shape[0]
    c2 = lambda i: (0, 0)
    c3 = lambda i: (0, 0, 0)
    return pl.pallas_call(
        _ffn_sample_kernel,
        grid=(1,),
        in_specs=[pl.BlockSpec((n, D_MODEL), c2),
                  pl.BlockSpec((n, D_S5), c2),
                  pl.BlockSpec((n, D_GDN), c2),
                  pl.BlockSpec((FFN_CONV - 1, n, 2 * D_FF), c3),
                  pl.BlockSpec((D_MODEL, D_MODEL), c2),
                  pl.BlockSpec((1, D_MODEL), c2),
                  pl.BlockSpec((D_MODEL, 2 * D_FF), c2),
                  pl.BlockSpec((FFN_CONV, 2 * D_FF), c2),
                  pl.BlockSpec((D_FF, D_MODEL), c2),
                  pl.BlockSpec((1, D_MODEL), c2)],
        out_specs=[pl.BlockSpec((n, D_MODEL), c2),
                   pl.BlockSpec((n, 2 * D_FF), c2)],
        out_shape=[jax.ShapeDtypeStruct((n, D_MODEL), F32),
                   jax.ShapeDtypeStruct((n, 2 * D_FF), F32)],
        scratch_shapes=[pltpu.VMEM((n, D_MODEL), F32),
                        pltpu.VMEM((n, D_MODEL), BF16),
                        pltpu.VMEM((n, D_MODEL), F32)],
        compiler_params=pltpu.CompilerParams(dimension_semantics=("arbitrary",),
                                             vmem_limit_bytes=VMEM_LIMIT),
        name="ffn_sample",
    )(x, ys5, ygdn, st_t, fw["w_out"], fw["norm2_g"], fw["w_up"], fw["conv_w"], fw["w_down"], fw["normf_g"])


PROJ_ROWS = 512
S5_STEPS = 64
GDN_ROWS = 256
FFN_ROWS = 512
GDN_SAMPLE_ROWS = 8


def kernel(x_prompt, x_sample, state_s5_re, state_s5_im, state_gdn, state_gdn_conv, state_ffn_conv, norm1_g, w_in, s5_a_re, s5_a_im, s5_log_dt, s5_b_re, s5_b_im, s5_c_re, s5_c_im, s5_d, s5_w_glu, gdn_conv_w, gdn_a_log, gdn_dt_bias, gdn_onorm_g, w_out, norm2_g, ffn_w_up, ffn_conv_w, ffn_w_down, normf_g):
    depth = w_in.shape[0]
    assert depth == 1, "the final rmsnorm is fused into the last layer's ffn kernel"
    bsz, seq, _ = x_prompt.shape
    nsmp = x_sample.shape[0]
    assert x_sample.shape[1] == 1
    l = 0

    xp = x_prompt.reshape(bsz * seq, D_MODEL)
    xs = x_sample.reshape(nsmp, D_MODEL)

    g1 = norm1_g[l].reshape(1, D_MODEL)
    w_in_bf = jnp.pad(w_in[l], ((0, 0), (0, D_IN_PAD - D_IN))).astype(BF16)
    sw = _s5_weights(s5_a_re[l], s5_a_im[l], s5_log_dt[l], s5_b_re[l], s5_b_im[l], s5_c_re[l], s5_c_im[l],
                     s5_d[l], s5_w_glu[l])
    gate_row = lambda v: jnp.pad(v, (GDN_HEADS, GATE_PAD - 2 * GDN_HEADS)).reshape(1, GATE_PAD)
    gw = {"conv_w": gdn_conv_w[l], "a_log": gate_row(gdn_a_log[l]), "dt_bias": gate_row(gdn_dt_bias[l]),
          "onorm_g": gdn_onorm_g[l].reshape(1, GDN_HEAD_DIM)}
    fw = {"w_out": w_out[l].astype(BF16), "norm2_g": norm2_g[l].reshape(1, D_MODEL),
          "w_up": ffn_w_up[l].astype(BF16), "conv_w": ffn_conv_w[l], "w_down": ffn_w_down[l].astype(BF16),
          "normf_g": normf_g.reshape(1, D_MODEL)}

    u_p, ygdn_p, p_gdn, p_gdn_conv = _gdn_prompt(xp, g1, w_in_bf, gw, bsz, seq, GDN_ROWS)
    u_tb = u_p.reshape(bsz, seq, D_S5).transpose(1, 0, 2).reshape(seq * bsz, D_S5)
    zeros_h = jnp.zeros((bsz, S5_CH), F32)
    ys5_tb, p_hr, p_hi = _s5(u_tb, zeros_h, zeros_h, sw, bsz, S5_STEPS)
    ys5_p = ys5_tb.reshape(seq, bsz, D_S5).transpose(1, 0, 2).reshape(bsz * seq, D_S5)
    y_p, p_ffn_conv = _ffn_prompt(xp, ys5_p, ygdn_p, fw, bsz, seq, FFN_ROWS)

    u_s, qkv_s, z_s, ba_s = _proj(xs, g1, w_in_bf, nsmp)
    ys5_s, s_hr, s_hi = _s5(u_s, state_s5_re[l].reshape(nsmp, S5_CH), state_s5_im[l].reshape(nsmp, S5_CH),
                            sw, nsmp, 1)
    gconv_t = state_gdn_conv[l].transpose(1, 0, 2)
    ygdn_s, s_gdn = _gdn_sample(qkv_s, gconv_t, z_s, ba_s, state_gdn[l], gw, GDN_SAMPLE_ROWS)
    fconv_t = state_ffn_conv[l].transpose(1, 0, 2)
    y_s, h_s = _ffn_sample(xs, ys5_s, ygdn_s, fconv_t, fw)
    s_gdn_conv = jnp.concatenate([state_gdn_conv[l][:, 1:], qkv_s[:, None, :]], axis=1)
    s_ffn_conv = jnp.concatenate([state_ffn_conv[l][:, 1:], h_s[:, None, :]], axis=1)

    st = lambda a: a[None]
    s5_shape = lambda a, n: a.reshape(1, n, S5_N_GROUPS, S5_STATE)
    return (y_p.reshape(bsz, seq, D_MODEL), y_s.reshape(nsmp, 1, D_MODEL),
            s5_shape(p_hr, bsz), s5_shape(p_hi, bsz), st(p_gdn), st(p_gdn_conv), st(p_ffn_conv),
            s5_shape(s_hr, nsmp), s5_shape(s_hi, nsmp), st(s_gdn), st(s_gdn_conv), st(s_ffn_conv))
```

```python
import functools

import jax
import jax.numpy as jnp
from jax import lax
from jax.experimental import pallas as pl
from jax.experimental.pallas import tpu as pltpu

F32 = jnp.float32
BF16 = jnp.bfloat16
NORM_EPS = 1e-6

D_MODEL = 1024
D_S5 = 512
S5_GROUP = 16
S5_N_GROUPS = 32
S5_STATE = 64
S5_CH = S5_N_GROUPS * S5_STATE
D_GDN = 512
GDN_HEADS = 4
GDN_HEAD_DIM = 128
GDN_CONV = 4
D_FF = 2816
FFN_CONV = 3
D_IN = D_S5 + 4 * D_GDN + 2 * GDN_HEADS
GATE_PAD = 128
D_IN_PAD = D_S5 + 4 * D_GDN + GATE_PAD

CHUNK = 64
FFN_COLS = 256
VMEM_LIMIT = 56 * 1024 * 1024


def _dot(a, b):
    return jnp.dot(a, b, preferred_element_type=F32)


def _dot_hi(a, b):
    return jnp.dot(a, b, preferred_element_type=F32, precision=lax.Precision.HIGHEST)


def _dot_nt(a, b):
    return lax.dot_general(a, b, (((1,), (1,)), ((), ())), preferred_element_type=F32)


def _rms(x, g):
    ms = jnp.mean(x * x, axis=-1, keepdims=True)
    return x * lax.rsqrt(ms + NORM_EPS) * g


def _silu(x):
    return x * jax.nn.sigmoid(x)


def _rows_back(ext, back, n):
    return pltpu.roll(ext, back, 0)[8:8 + n]


def _softplus(x):
    return jnp.maximum(x, 0.0) + jnp.log1p(jnp.exp(-jnp.abs(x)))


def _proj_kernel(x_ref, g_ref, w_ref, u_ref, qkv_ref, z_ref, ba_ref):
    n = _rms(x_ref[...], g_ref[...]).astype(BF16)
    o0 = D_S5
    o1 = o0 + 3 * D_GDN
    o2 = o1 + D_GDN
    u_ref[...] = _dot(n, w_ref[:, :o0])
    qkv_ref[...] = _dot(n, w_ref[:, o0:o1])
    z_ref[...] = _dot(n, w_ref[:, o1:o2])
    ba_ref[...] = _dot(n, w_ref[:, o2:])


def _proj(x, g1, w_in_bf, tm):
    n = x.shape[0]
    row = lambda i: (i, 0)
    const = lambda i: (0, 0)
    return pl.pallas_call(
        _proj_kernel,
        grid=(n // tm,),
        in_specs=[pl.BlockSpec((tm, D_MODEL), row),
                  pl.BlockSpec((1, D_MODEL), const),
                  pl.BlockSpec((D_MODEL, D_IN_PAD), const)],
        out_specs=[pl.BlockSpec((tm, D_S5), row),
                   pl.BlockSpec((tm, 3 * D_GDN), row),
                   pl.BlockSpec((tm, D_GDN), row),
                   pl.BlockSpec((tm, GATE_PAD), row)],
        out_shape=[jax.ShapeDtypeStruct((n, D_S5), F32),
                   jax.ShapeDtypeStruct((n, 3 * D_GDN), F32),
                   jax.ShapeDtypeStruct((n, D_GDN), F32),
                   jax.ShapeDtypeStruct((n, GATE_PAD), F32)],
        compiler_params=pltpu.CompilerParams(dimension_semantics=("arbitrary",),
                                             vmem_limit_bytes=VMEM_LIMIT),
        name="proj",
    )(x, g1, w_in_bf)


S5_HALF = S5_CH // 2
S5_COLS = 512


def _s5_kernel(u_ref, h0r_ref, h0i_ref, are_ref, aim_ref, ldt_ref, bre_ref, bim_ref, cre_ref, cim_ref,
               d_ref, wglu_ref, y_ref, hr_out, hi_out,
               abr_s, abi_s, wbr_s, wbi_s, hr_s, hi_s, xr_s, xi_s, *, nb, tt, batch_major):
    @pl.when(pl.program_id(0) == 0)
    def _init():
        ar = are_ref[...]
        ai = aim_ref[...]
        dt = jnp.exp(ldt_ref[...])
        mag = jnp.exp(ar * dt)
        abr = mag * jnp.cos(ai * dt)
        abi = mag * jnp.sin(ai * dt)
        den = ar * ar + ai * ai
        p = abr - 1.0
        fr = (p * ar + abi * ai) / den
        fi = (abi * ar - p * ai) / den
        abr_s[...] = abr
        abi_s[...] = abi
        for m in range(2):
            frm = fr[:, m * S5_HALF:(m + 1) * S5_HALF]
            fim = fi[:, m * S5_HALF:(m + 1) * S5_HALF]
            wr = bre_ref[m]
            wi = bim_ref[m]
            wbr_s[m] = (wr * frm - wi * fim).astype(BF16)
            wbi_s[m] = (wr * fim + wi * frm).astype(BF16)
        hr_s[...] = h0r_ref[...]
        hi_s[...] = h0i_ref[...]

    if batch_major:
        u = pltpu.einshape("btc->tbc", u_ref[...]).reshape(tt * nb, D_S5)
    else:
        u = u_ref[...]
    ub = u.astype(BF16)
    half = D_S5 // 2
    for m in range(2):
        um = ub[:, m * half:(m + 1) * half]
        xr_s[:, m * S5_HALF:(m + 1) * S5_HALF] = _dot(um, wbr_s[m])
        xi_s[:, m * S5_HALF:(m + 1) * S5_HALF] = _dot(um, wbi_s[m])

    for cb in range(S5_CH // S5_COLS):
        cols = slice(cb * S5_COLS, (cb + 1) * S5_COLS)
        a_r = jnp.broadcast_to(abr_s[:, cols], (nb, S5_COLS))
        a_i = jnp.broadcast_to(abi_s[:, cols], (nb, S5_COLS))

        def step(t, carry, cols=cols, a_r=a_r, a_i=a_i):
            hr, hi = carry
            r0 = pl.multiple_of(t * nb, nb)
            nhr = a_r * hr - a_i * hi + xr_s[pl.ds(r0, nb), cols]
            nhi = a_r * hi + a_i * hr + xi_s[pl.ds(r0, nb), cols]
            xr_s[pl.ds(r0, nb), cols] = nhr
            xi_s[pl.ds(r0, nb), cols] = nhi
            return nhr, nhi

        hr, hi = lax.fori_loop(0, tt, step, (hr_s[:, cols], hi_s[:, cols]), unroll=min(tt, 8))
        hr_s[:, cols] = hr
        hi_s[:, cols] = hi

    hr_out[...] = hr_s[...]
    hi_out[...] = hi_s[...]

    ys = []
    for m in range(2):
        hre = xr_s[:, m * S5_HALF:(m + 1) * S5_HALF].astype(BF16)
        him = xi_s[:, m * S5_HALF:(m + 1) * S5_HALF].astype(BF16)
        ych = _dot(hre, cre_ref[m]) - _dot(him, cim_ref[m])
        ys.append(ych + d_ref[:, m * half:(m + 1) * half] * u[:, m * half:(m + 1) * half])
    y = jax.nn.gelu(jnp.concatenate(ys, axis=-1)).astype(BF16)
    gl = _dot(y, wglu_ref[...])
    y = gl[:, :D_S5] * jax.nn.sigmoid(gl[:, D_S5:])
    if batch_major:
        y_ref[...] = pltpu.einshape("tbc->btc", y.reshape(tt, nb, D_S5))
    else:
        y_ref[...] = y


def _s5(u, h0r, h0i, sw, nb, tt, batch_major):
    rows = nb * tt
    const2 = lambda i: (0, 0)
    const3 = lambda i: (0, 0, 0)
    full2 = lambda shape: pl.BlockSpec(shape, const2)
    full3 = lambda shape: pl.BlockSpec(shape, const3)
    if batch_major:
        steps = u.shape[1] // tt
        io_spec = pl.BlockSpec((nb, tt, D_S5), lambda i: (0, i, 0))
    else:
        steps = u.shape[0] // rows
        io_spec = pl.BlockSpec((rows, D_S5), lambda i: (i, 0))
    return pl.pallas_call(
        functools.partial(_s5_kernel, nb=nb, tt=tt, batch_major=batch_major),
        grid=(steps,),
        in_specs=[io_spec,
                  full2((nb, S5_CH)), full2((nb, S5_CH)),
                  full2((1, S5_CH)), full2((1, S5_CH)), full2((1, S5_CH)),
                  full3((2, D_S5 // 2, S5_HALF)), full3((2, D_S5 // 2, S5_HALF)),
                  full3((2, S5_HALF, D_S5 // 2)), full3((2, S5_HALF, D_S5 // 2)),
                  full2((1, D_S5)), full2((D_S5, 2 * D_S5))],
        out_specs=[io_spec, full2((nb, S5_CH)), full2((nb, S5_CH))],
        out_shape=[jax.ShapeDtypeStruct(u.shape, F32),
                   jax.ShapeDtypeStruct((nb, S5_CH), F32),
                   jax.ShapeDtypeStruct((nb, S5_CH), F32)],
        scratch_shapes=[pltpu.VMEM((1, S5_CH), F32), pltpu.VMEM((1, S5_CH), F32),
                        pltpu.VMEM((2, D_S5 // 2, S5_HALF), BF16), pltpu.VMEM((2, D_S5 // 2, S5_HALF), BF16),
                        pltpu.VMEM((nb, S5_CH), F32), pltpu.VMEM((nb, S5_CH), F32),
                        pltpu.VMEM((rows, S5_CH), F32), pltpu.VMEM((rows, S5_CH), F32)],
        compiler_params=pltpu.CompilerParams(dimension_semantics=("arbitrary",),
                                             vmem_limit_bytes=VMEM_LIMIT),
        name="s5",
    )(u, h0r, h0i, sw["a_re"], sw["a_im"], sw["log_dt"], sw["b_re"], sw["b_im"],
      sw["c_re"], sw["c_im"], sw["d"], sw["w_glu"])


def _s5_weights(a_re, a_im, log_dt, b_re, b_im, c_re, c_im, d, w_glu):
    g, n, c = S5_N_GROUPS, S5_STATE, S5_GROUP
    eye = jnp.eye(g // 2, dtype=F32)

    def b_blockdiag(b):
        bb = b.reshape(2, g // 2, n, c)
        return jnp.einsum("mgnc,gh->mgchn", bb, eye).reshape(2, (g // 2) * c, (g // 2) * n)

    def c_blockdiag(cm):
        cc = cm.reshape(2, g // 2, c, n)
        return jnp.einsum("mgcn,gh->mgnhc", cc, eye).reshape(2, (g // 2) * n, (g // 2) * c)

    return {
        "a_re": a_re.reshape(1, S5_CH), "a_im": a_im.reshape(1, S5_CH),
        "log_dt": jnp.repeat(log_dt, n).reshape(1, S5_CH),
        "b_re": b_blockdiag(b_re), "b_im": b_blockdiag(b_im),
        "c_re": c_blockdiag(c_re).astype(BF16), "c_im": c_blockdiag(c_im).astype(BF16),
        "d": d.reshape(1, D_S5), "w_glu": w_glu.astype(BF16),
    }


def _l2norm(x):
    return x * lax.rsqrt(jnp.sum(x * x, axis=-1, keepdims=True) + NORM_EPS)


def _gates(ba, alog_row, dtb_row):
    beta = jax.nn.sigmoid(ba)
    g = -jnp.exp(alog_row) * _softplus(ba + dtb_row)
    return beta, g


def _out_gate(o, og_row, z):
    return _rms(o, og_row) * _silu(z)


def _bdot(a, b):
    return jnp.einsum("bij,bjk->bik", a.astype(BF16), b.astype(BF16), preferred_element_type=F32)


def _tri_merge_level(n, lmat, ii, jj, shift):
    same_big = (ii >> (shift + 1)) == (jj >> (shift + 1))
    same_small = (ii >> shift) == (jj >> shift)
    c = jnp.where((same_big & jnp.logical_not(same_small))[None], lmat, 0.0)
    w = c + _bdot(n, c)
    return n - (w + _bdot(w, n))


def _interleave(*stages):
    out = []
    pos = [0] * len(stages)
    while any(p < len(st) for p, st in zip(pos, stages)):
        k = min((i for i, st in enumerate(stages) if pos[i] < len(st)),
                key=lambda i: (pos[i] + 0.5) / len(stages[i]))
        out.append(stages[k][pos[k]])
        pos[k] += 1
    return out


def _gdn_kernel(x_ref, g1_ref, win_ref, convw_ref, alog_ref, dtb_ref, og_ref,
                u_ref, y_ref, sfin_ref, convout_ref, s_s, proj_a, proj_b, set_a, set_b, *, tg, nt):
    s = pl.program_id(0)

    @pl.when(s == 0)
    def _zero_buffers():
        for ref in (*proj_b, *set_a):
            ref[...] = jnp.zeros_like(ref)

    @pl.when(lax.rem(jnp.maximum(s - 1, 0), nt) == 0)
    def _new_sequence_conv():
        for ext in (proj_a[0], proj_b[0]):
            ext[0:8, :] = jnp.zeros((8, 3 * D_GDN), F32)

    @pl.when(lax.rem(jnp.maximum(s - 2, 0), nt) == 0)
    def _new_sequence_state():
        s_s[...] = jnp.zeros_like(s_s)

    args = (x_ref, g1_ref, win_ref, convw_ref, alog_ref, dtb_ref, og_ref, u_ref, y_ref, sfin_ref, convout_ref, s_s)

    @pl.when(lax.rem(s, 2) == 0)
    def _even():
        _gdn_step(*args, proj_wr=proj_a, proj_rd=proj_b, wr=set_b, rd=set_a, tg=tg)

    @pl.when(lax.rem(s, 2) == 1)
    def _odd():
        _gdn_step(*args, proj_wr=proj_b, proj_rd=proj_a, wr=set_a, rd=set_b, tg=tg)


def _gdn_step(x_ref, g1_ref, win_ref, convw_ref, alog_ref, dtb_ref, og_ref, u_ref, y_ref, sfin_ref, convout_ref,
              s_s, *, proj_wr, proj_rd, wr, rd, tg):
    nc = tg // CHUNK
    nsys = nc * GDN_HEADS
    hcols = lambda h: slice(h * GDN_HEAD_DIM, (h + 1) * GDN_HEAD_DIM)
    og = og_ref[...]
    v = {}
    stage_a, stage_b, stage_c = [], [], []

    ext_w, z_w, ba_w = proj_wr
    o0 = D_S5
    o1 = o0 + 3 * D_GDN
    o2 = o1 + D_GDN

    def a_norm():
        v["n1"] = _rms(x_ref[...], g1_ref[...]).astype(BF16)
    stage_a.append(a_norm)

    def a_block(dst, rows, dst_c0, src_c0, width):
        def item():
            dst[rows, dst_c0:dst_c0 + width] = _dot(v["n1"], win_ref[:, src_c0:src_c0 + width])
        return item
    blk = 256
    for c0 in range(0, o0, blk):
        stage_a.append(a_block(u_ref, slice(None), c0, c0, blk))
    for c0 in range(0, o1 - o0, blk):
        stage_a.append(a_block(ext_w, slice(8, None), c0, o0 + c0, blk))
    for c0 in range(0, o2 - o1, blk):
        stage_a.append(a_block(z_w, slice(None), c0, o1 + c0, blk))
    stage_a.append(a_block(ba_w, slice(None), 0, o2, GATE_PAD))

    aq_r, b_r, o_r, gl_r, zs_r = rd
    states = [None] * GDN_HEADS

    def c_chunk(c):
        def item():
            rows = slice(c * CHUNK, (c + 1) * CHUNK)
            for h in range(GDN_HEADS):
                sys = c * GDN_HEADS + h
                st = s_s[h] if c == 0 else states[h]
                r = _dot(aq_r[sys], st.astype(BF16))
                o = r[GDN_HEAD_DIM:] + o_r[sys]
                states[h] = st * gl_r[sys, 0:1, :] + r[:GDN_HEAD_DIM] + b_r[sys]
                y_ref[rows, hcols(h)] = _rms(o, og) * zs_r[rows, hcols(h)]
                if c == nc - 1:
                    s_s[h] = states[h]
                    sfin_ref[0, h] = states[h]
        return item
    stage_c.extend(c_chunk(c) for c in range(nc))

    aq_s, b_s, o_s, gl_s, zs_s = wr
    ext_s, z_r, ba_r = proj_rd
    ii = lax.broadcasted_iota(jnp.int32, (CHUNK, CHUNK), 0)
    jj = lax.broadcasted_iota(jnp.int32, (CHUNK, CHUNK), 1)
    causal = jj <= ii
    strict = jj < ii
    scale = GDN_HEAD_DIM ** -0.5
    lmats = [None] * nsys
    rhss = [None] * nsys
    kgt_attn = [None] * nsys
    qgs = [None] * nsys

    def b_gates():
        beta_all, g_all = _gates(ba_r[...], alog_ref[...], dtb_ref[...])
        ri = lax.broadcasted_iota(jnp.int32, (tg, tg), 0)
        ci = lax.broadcasted_iota(jnp.int32, (tg, tg), 1)
        lbd = (((ri >> 6) == (ci >> 6)) & (ci <= ri)).astype(F32)
        v["beta_all"] = beta_all
        v["gc_col"] = _dot_hi(lbd, g_all)
        v["gc_row"] = v["gc_col"].T
    stage_b.append(b_gates)

    def b_silu_z(h):
        def item():
            zs_s[:, hcols(h)] = _silu(z_r[:, hcols(h)])
        return item

    def b_conv(h, part, name, norm):
        def item():
            c0 = part * D_GDN + h * GDN_HEAD_DIM
            cs = slice(c0, c0 + GDN_HEAD_DIM)
            ext = ext_s[:, cs]
            acc = ext[8:] * convw_ref[GDN_CONV - 1:GDN_CONV, cs]
            for j in range(GDN_CONV - 1):
                acc = acc + _rows_back(ext, GDN_CONV - 1 - j, tg) * convw_ref[j:j + 1, cs]
            acc = _silu(acc)
            v[name] = norm(acc)
        return item

    def b_prep(h, c):
        def item():
            beta_all, gc_col, gc_row = v["beta_all"], v["gc_col"], v["gc_row"]
            sys = c * GDN_HEADS + h
            rows = slice(c * CHUNK, (c + 1) * CHUNK)
            q = v["q"][rows]
            k = v["k"][rows]
            vv = v["v"][rows]
            beta = beta_all[rows, h:h + 1]
            gcc = gc_col[rows, GDN_HEADS + h:GDN_HEADS + h + 1]
            gcr = gc_row[GDN_HEADS + h:GDN_HEADS + h + 1, rows]
            decay = jnp.exp(jnp.where(causal, gcc - gcr, -jnp.inf))
            kb = k * beta
            egc = jnp.exp(gcc)
            g_last = gcc[CHUNK - 1:CHUNK, :]
            kq = _dot_nt(jnp.concatenate([kb, q], axis=0).astype(BF16), k.astype(BF16))
            lmats[sys] = jnp.where(strict, kq[:CHUNK] * decay, 0.0)
            attn = jnp.where(causal, kq[CHUNK:] * decay, 0.0)
            rhss[sys] = jnp.concatenate([vv * beta, kb * egc], axis=1)
            kgt_attn[sys] = jnp.concatenate([(k * jnp.exp(g_last - gcc)).T, attn], axis=0)
            qgs[sys] = q * egc
            gl_s[sys] = jnp.broadcast_to(jnp.exp(g_last), (8, GDN_HEAD_DIM))
        return item

    for h in range(GDN_HEADS):
        stage_b.append(b_silu_z(h))
        stage_b.append(b_conv(h, 0, "q", lambda a: _l2norm(a) * scale))
        stage_b.append(b_conv(h, 1, "k", _l2norm))
        stage_b.append(b_conv(h, 2, "v", lambda a: a))
        stage_b.extend(b_prep(h, c) for c in range(nc))

    def b_inverse_start():
        lmat = jnp.stack(lmats)
        v["lmat"] = lmat
        v["noff"] = -jnp.where(((ii >> 1) == (jj >> 1))[None], lmat, 0.0)
    stage_b.append(b_inverse_start)

    def b_inverse_level(shift):
        def item():
            v["noff"] = _tri_merge_level(v["noff"], v["lmat"], ii, jj, shift)
        return item
    shift = 1
    while (1 << shift) < CHUNK:
        stage_b.append(b_inverse_level(shift))
        shift += 1

    def b_solve():
        rhs = jnp.stack(rhss)
        sol = rhs + _bdot(v["noff"], rhs)
        v["prod"] = _bdot(jnp.stack(kgt_attn), sol)
    stage_b.append(b_solve)

    def b_store():
        prod = v["prod"]
        dk = GDN_HEAD_DIM
        for sys in range(nsys):
            aq_s[sys, :dk, :] = (-prod[sys, :dk, dk:]).astype(BF16)
            aq_s[sys, dk:, :] = (qgs[sys] - prod[sys, dk:, dk:]).astype(BF16)
            b_s[sys] = prod[sys, :dk, :dk]
            o_s[sys] = prod[sys, dk:, :dk]
    stage_b.append(b_store)

    for item in _interleave(stage_a, stage_b, stage_c):
        item()

    ext_w[0:8, :] = ext_s[pl.ds(tg, 8), :]
    convout_ref[0] = ext_s[pl.ds(tg + 8 - (GDN_CONV - 1), GDN_CONV - 1), :]


def _gdn_prompt(x, g1, w_in_bf, gw, bsz, seq, tg):
    nt = seq // tg
    ntiles = bsz * nt
    nsys = (tg // CHUNK) * GDN_HEADS
    tile_a = lambda s: jnp.minimum(s, ntiles - 1)
    tile_b = lambda s: jnp.clip(s - 1, 0, ntiles - 1)
    tile_c = lambda s: jnp.maximum(s - 2, 0)
    const = lambda s: (0, 0)
    proj_bufs = lambda: [pltpu.VMEM((tg + 8, 3 * D_GDN), F32),
                         pltpu.VMEM((tg, D_GDN), F32),
                         pltpu.VMEM((tg, GATE_PAD), F32)]
    resident = lambda shape: pl.BlockSpec(shape, const, pipeline_mode=pl.Buffered(1))
    handover = lambda: [pltpu.VMEM((nsys, GDN_HEAD_DIM + CHUNK, GDN_HEAD_DIM), BF16),
                        pltpu.VMEM((nsys, GDN_HEAD_DIM, GDN_HEAD_DIM), F32),
                        pltpu.VMEM((nsys, CHUNK, GDN_HEAD_DIM), F32),
                        pltpu.VMEM((nsys, 8, GDN_HEAD_DIM), F32),
                        pltpu.VMEM((tg, D_GDN), F32)]
    return pl.pallas_call(
        functools.partial(_gdn_kernel, tg=tg, nt=nt),
        grid=(ntiles + 2,),
        in_specs=[pl.BlockSpec((tg, D_MODEL), lambda s: (tile_a(s), 0)),
                  resident((1, D_MODEL)),
                  resident((D_MODEL, D_IN_PAD)),
                  resident((GDN_CONV, 3 * D_GDN)),
                  resident((1, GATE_PAD)),
                  resident((1, GATE_PAD)),
                  resident((1, GDN_HEAD_DIM))],
        out_specs=[pl.BlockSpec((tg, D_S5), lambda s: (tile_a(s), 0)),
                   pl.BlockSpec((tg, D_GDN), lambda s: (tile_c(s), 0)),
                   pl.BlockSpec((1, GDN_HEADS, GDN_HEAD_DIM, GDN_HEAD_DIM), lambda s: (tile_c(s) // nt, 0, 0, 0)),
                   pl.BlockSpec((1, GDN_CONV - 1, 3 * D_GDN), lambda s: (tile_b(s) // nt, 0, 0))],
        out_shape=[jax.ShapeDtypeStruct((bsz * seq, D_S5), F32),
                   jax.ShapeDtypeStruct((bsz * seq, D_GDN), F32),
                   jax.ShapeDtypeStruct((bsz, GDN_HEADS, GDN_HEAD_DIM, GDN_HEAD_DIM), F32),
                   jax.ShapeDtypeStruct((bsz, GDN_CONV - 1, 3 * D_GDN), F32)],
        scratch_shapes=[pltpu.VMEM((GDN_HEADS, GDN_HEAD_DIM, GDN_HEAD_DIM), F32),
                        proj_bufs(), proj_bufs(), handover(), handover()],
        compiler_params=pltpu.CompilerParams(dimension_semantics=("arbitrary",),
                                             vmem_limit_bytes=VMEM_LIMIT),
        name="proj_gdn_prompt",
    )(x, g1, w_in_bf, gw["conv_w"], gw["a_log"], gw["dt_bias"], gw["onorm_g"])


def _gdn_step_kernel(qkv_ref, st_ref, z_ref, ba_ref, s0_ref, convw_ref, alog_ref, dtb_ref, og_ref,
                     y_ref, s1_ref, *, bb):
    beta_all, g_all = _gates(ba_ref[...], alog_ref[...], dtb_ref[...])
    alpha_all = jnp.exp(g_all)
    scale = GDN_HEAD_DIM ** -0.5
    og = og_ref[...]

    for h in range(GDN_HEADS):
        def conv_cols(part, h=h):
            c0 = part * D_GDN + h * GDN_HEAD_DIM
            cs = slice(c0, c0 + GDN_HEAD_DIM)
            acc = qkv_ref[:, cs] * convw_ref[GDN_CONV - 1:GDN_CONV, cs]
            for j in range(GDN_CONV - 1):
                acc = acc + st_ref[j, :, cs] * convw_ref[j:j + 1, cs]
            return _silu(acc)

        q = _l2norm(conv_cols(0)) * scale
        k = _l2norm(conv_cols(1))
        v = conv_cols(2)
        beta = beta_all[:, h:h + 1]
        alpha = alpha_all[:, GDN_HEADS + h:GDN_HEADS + h + 1]
        qk = jnp.sum(q * k, axis=-1, keepdims=True)
        kt = k.T
        kq = jnp.concatenate([k, q], axis=0).astype(BF16)
        o_rows = []
        for b in range(bb):
            s0 = s0_ref[b, h]
            kcol = kt[:, b:b + 1]
            r = _dot(kq, s0.astype(BF16))
            ks = r[b:b + 1, :]
            qs = r[bb + b:bb + b + 1, :]
            al = alpha[b:b + 1, :]
            v_new = beta[b:b + 1, :] * (v[b:b + 1, :] - al * ks)
            o_rows.append(al * qs + qk[b:b + 1, :] * v_new)
            s1_ref[b, h] = al * s0 + kcol * v_new
        o = jnp.concatenate(o_rows, axis=0)
        cs = slice(h * GDN_HEAD_DIM, (h + 1) * GDN_HEAD_DIM)
        y_ref[:, cs] = _out_gate(o, og, z_ref[:, cs])


def _gdn_sample(qkv, st_t, z, ba, s0, gw, bb):
    n = qkv.shape[0]
    row = lambda i: (i, 0)
    const = lambda i: (0, 0)
    state = lambda i: (i, 0, 0, 0)
    return pl.pallas_call(
        functools.partial(_gdn_step_kernel, bb=bb),
        grid=(n // bb,),
        in_specs=[pl.BlockSpec((bb, 3 * D_GDN), row),
                  pl.BlockSpec((GDN_CONV - 1, bb, 3 * D_GDN), lambda i: (0, i, 0)),
                  pl.BlockSpec((bb, D_GDN), row),
                  pl.BlockSpec((bb, GATE_PAD), row),
                  pl.BlockSpec((bb, GDN_HEADS, GDN_HEAD_DIM, GDN_HEAD_DIM), state),
                  pl.BlockSpec((GDN_CONV, 3 * D_GDN), const),
                  pl.BlockSpec((1, GATE_PAD), const),
                  pl.BlockSpec((1, GATE_PAD), const),
                  pl.BlockSpec((1, GDN_HEAD_DIM), const)],
        out_specs=[pl.BlockSpec((bb, D_GDN), row),
                   pl.BlockSpec((bb, GDN_HEADS, GDN_HEAD_DIM, GDN_HEAD_DIM), state)],
        out_shape=[jax.ShapeDtypeStruct((n, D_GDN), F32),
                   jax.ShapeDtypeStruct(s0.shape, F32)],
        compiler_params=pltpu.CompilerParams(dimension_semantics=("arbitrary",),
                                             vmem_limit_bytes=VMEM_LIMIT),
        name="gdn_sample",
    )(qkv, st_t, z, ba, s0, gw["conv_w"], gw["a_log"], gw["dt_bias"], gw["onorm_g"])


def _residual_in(x_ref, ys5_ref, ygdn_ref, wout_ref, g2_ref, x1_s, n2_s):
    x1 = (x_ref[...] + _dot(ys5_ref[...].astype(BF16), wout_ref[:D_S5, :])
          + _dot(ygdn_ref[...].astype(BF16), wout_ref[D_S5:, :]))
    x1_s[...] = x1
    n2_s[...] = _rms(x1, g2_ref[...]).astype(BF16)


def _ffn_prompt_kernel(x_ref, ys5_ref, ygdn_ref, wout_ref, g2_ref, wup_ref, cw_ref, wdn_ref, gf_ref,
                       y_ref, hlast_ref, x1_s, n2_s, act_s, carry_s, *, tm):
    ti = pl.program_id(1)
    nt = pl.num_programs(1)

    @pl.when(ti == 0)
    def _reset():
        carry_s[...] = jnp.zeros_like(carry_s)

    _residual_in(x_ref, ys5_ref, ygdn_ref, wout_ref, g2_ref, x1_s, n2_s)
    n2 = n2_s[...]
    for f in range(D_FF // FFN_COLS):
        halves = []
        for part in range(2):
            c0 = part * D_FF + f * FFN_COLS
            cs = slice(c0, c0 + FFN_COLS)
            hcur = _dot(n2, wup_ref[:, cs])
            ext = jnp.concatenate([carry_s[:, cs], hcur], axis=0)
            carry_s[:, cs] = hcur[tm - 8:, :]
            conv = (_rows_back(ext, 2, tm) * cw_ref[0:1, cs]
                    + _rows_back(ext, 1, tm) * cw_ref[1:2, cs]
                    + hcur * cw_ref[2:3, cs])
            halves.append(conv)
        act_s[:, f * FFN_COLS:(f + 1) * FFN_COLS] = (_silu(halves[0]) * halves[1]).astype(BF16)
    y_ref[...] = _rms(x1_s[...] + _dot(act_s[...], wdn_ref[...]), gf_ref[...])

    @pl.when(ti == nt - 1)
    def _fin():
        hlast_ref[0] = carry_s[8 - (FFN_CONV - 1):, :]


def _ffn_prompt(x, ys5, ygdn, fw, bsz, seq, tm):
    nt = seq // tm
    row = lambda b, i: (b * nt + i, 0)
    const = lambda b, i: (0, 0)
    resident = lambda shape: pl.BlockSpec(shape, const, pipeline_mode=pl.Buffered(1))
    return pl.pallas_call(
        functools.partial(_ffn_prompt_kernel, tm=tm),
        grid=(bsz, nt),
        in_specs=[pl.BlockSpec((tm, D_MODEL), row),
                  pl.BlockSpec((tm, D_S5), row),
                  pl.BlockSpec((tm, D_GDN), row),
                  resident((D_MODEL, D_MODEL)),
                  resident((1, D_MODEL)),
                  resident((D_MODEL, 2 * D_FF)),
                  resident((FFN_CONV, 2 * D_FF)),
                  resident((D_FF, D_MODEL)),
                  resident((1, D_MODEL))],
        out_specs=[pl.BlockSpec((tm, D_MODEL), row),
                   pl.BlockSpec((1, FFN_CONV - 1, 2 * D_FF), lambda b, i: (b, 0, 0))],
        out_shape=[jax.ShapeDtypeStruct((bsz * seq, D_MODEL), F32),
                   jax.ShapeDtypeStruct((bsz, FFN_CONV - 1, 2 * D_FF), F32)],
        scratch_shapes=[pltpu.VMEM((tm, D_MODEL), F32),
                        pltpu.VMEM((tm, D_MODEL), BF16),
                        pltpu.VMEM((tm, D_FF), BF16),
                        pltpu.VMEM((8, 2 * D_FF), F32)],
        compiler_params=pltpu.CompilerParams(dimension_semantics=("arbitrary", "arbitrary"),
                                             vmem_limit_bytes=VMEM_LIMIT),
        name="ffn_prompt",
    )(x, ys5, ygdn, fw["w_out"], fw["norm2_g"], fw["w_up"], fw["conv_w"], fw["w_down"], fw["normf_g"])


def _ffn_sample_kernel(x_ref, ys5_ref, ygdn_ref, st_ref, wout_ref, g2_ref, wup_ref, cw_ref, wdn_ref, gf_ref,
                       y_ref, h_ref, x1_s, n2_s, acc_s):
    _residual_in(x_ref, ys5_ref, ygdn_ref, wout_ref, g2_ref, x1_s, n2_s)
    n2 = n2_s[...]
    for f in range(D_FF // FFN_COLS):
        halves = []
        for part in range(2):
            c0 = part * D_FF + f * FFN_COLS
            cs = slice(c0, c0 + FFN_COLS)
            hcur = _dot(n2, wup_ref[:, cs])
            h_ref[:, cs] = hcur
            halves.append(st_ref[0, :, cs] * cw_ref[0:1, cs] + st_ref[1, :, cs] * cw_ref[1:2, cs]
                          + hcur * cw_ref[2:3, cs])
        act = (_silu(halves[0]) * halves[1]).astype(BF16)
        contrib = _dot(act, wdn_ref[f * FFN_COLS:(f + 1) * FFN_COLS, :])
        if f == 0:
            acc_s[...] = contrib
        else:
            acc_s[...] += contrib
    y_ref[...] = _rms(x1_s[...] + acc_s[...], gf_ref[...])


def _ffn_sample(x, ys5, ygdn, st_t, fw):
    n = x.shape[0]
    c2 = lambda i: (0, 0)
    c3 = lambda i: (0, 0, 0)
    return pl.pallas_call(
        _ffn_sample_kernel,
        grid=(1,),
        in_specs=[pl.BlockSpec((n, D_MODEL), c2),
                  pl.BlockSpec((n, D_S5), c2),
                  pl.BlockSpec((n, D_GDN), c2),
                  pl.BlockSpec((FFN_CONV - 1, n, 2 * D_FF), c3),
                  pl.BlockSpec((D_MODEL, D_MODEL), c2),
                  pl.BlockSpec((1, D_MODEL), c2),
                  pl.BlockSpec((D_MODEL, 2 * D_FF), c2),
                  pl.BlockSpec((FFN_CONV, 2 * D_FF), c2),
                  pl.BlockSpec((D_FF, D_MODEL), c2),
                  pl.BlockSpec((1, D_MODEL), c2)],
        out_specs=[pl.BlockSpec((n, D_MODEL), c2),
                   pl.BlockSpec((n, 2 * D_FF), c2)],
        out_shape=[jax.ShapeDtypeStruct((n, D_MODEL), F32),
                   jax.ShapeDtypeStruct((n, 2 * D_FF), F32)],
        scratch_shapes=[pltpu.VMEM((n, D_MODEL), F32),
                        pltpu.VMEM((n, D_MODEL), BF16),
                        pltpu.VMEM((n, D_MODEL), F32)],
        compiler_params=pltpu.CompilerParams(dimension_semantics=("arbitrary",),
                                             vmem_limit_bytes=VMEM_LIMIT),
        name="ffn_sample",
    )(x, ys5, ygdn, st_t, fw["w_out"], fw["norm2_g"], fw["w_up"], fw["conv_w"], fw["w_down"], fw["normf_g"])


PROJ_ROWS = 512
S5_STEPS = 64
GDN_ROWS = 256
FFN_ROWS = 512
GDN_SAMPLE_ROWS = 8


def kernel(x_prompt, x_sample, state_s5_re, state_s5_im, state_gdn, state_gdn_conv, state_ffn_conv, norm1_g, w_in, s5_a_re, s5_a_im, s5_log_dt, s5_b_re, s5_b_im, s5_c_re, s5_c_im, s5_d, s5_w_glu, gdn_conv_w, gdn_a_log, gdn_dt_bias, gdn_onorm_g, w_out, norm2_g, ffn_w_up, ffn_conv_w, ffn_w_down, normf_g):
    depth = w_in.shape[0]
    assert depth == 1, "the final rmsnorm is fused into the last layer's ffn kernel"
    bsz, seq, _ = x_prompt.shape
    nsmp = x_sample.shape[0]
    assert x_sample.shape[1] == 1
    l = 0

    xp = x_prompt.reshape(bsz * seq, D_MODEL)
    xs = x_sample.reshape(nsmp, D_MODEL)

    g1 = norm1_g[l].reshape(1, D_MODEL)
    w_in_bf = jnp.pad(w_in[l], ((0, 0), (0, D_IN_PAD - D_IN))).astype(BF16)
    sw = _s5_weights(s5_a_re[l], s5_a_im[l], s5_log_dt[l], s5_b_re[l], s5_b_im[l], s5_c_re[l], s5_c_im[l],
                     s5_d[l], s5_w_glu[l])
    gate_row = lambda v: jnp.pad(v, (GDN_HEADS, GATE_PAD - 2 * GDN_HEADS)).reshape(1, GATE_PAD)
    gw = {"conv_w": gdn_conv_w[l], "a_log": gate_row(gdn_a_log[l]), "dt_bias": gate_row(gdn_dt_bias[l]),
          "onorm_g": gdn_onorm_g[l].reshape(1, GDN_HEAD_DIM)}
    fw = {"w_out": w_out[l].astype(BF16), "norm2_g": norm2_g[l].reshape(1, D_MODEL),
          "w_up": ffn_w_up[l].astype(BF16), "conv_w": ffn_conv_w[l], "w_down": ffn_w_down[l].astype(BF16),
          "normf_g": normf_g.reshape(1, D_MODEL)}

    u_p, ygdn_p, p_gdn, p_gdn_conv = _gdn_prompt(xp, g1, w_in_bf, gw, bsz, seq, GDN_ROWS)
    zeros_h = jnp.zeros((bsz, S5_CH), F32)
    ys5_p, p_hr, p_hi = _s5(u_p.reshape(bsz, seq, D_S5), zeros_h, zeros_h, sw, bsz, S5_STEPS, True)
    ys5_p = ys5_p.reshape(bsz * seq, D_S5)
    y_p, p_ffn_conv = _ffn_prompt(xp, ys5_p, ygdn_p, fw, bsz, seq, FFN_ROWS)

    u_s, qkv_s, z_s, ba_s = _proj(xs, g1, w_in_bf, nsmp)
    ys5_s, s_hr, s_hi = _s5(u_s, state_s5_re[l].reshape(nsmp, S5_CH), state_s5_im[l].reshape(nsmp, S5_CH),
                            sw, nsmp, 1, False)
    gconv_t = state_gdn_conv[l].transpose(1, 0, 2)
    ygdn_s, s_gdn = _gdn_sample(qkv_s, gconv_t, z_s, ba_s, state_gdn[l], gw, GDN_SAMPLE_ROWS)
    fconv_t = state_ffn_conv[l].transpose(1, 0, 2)
    y_s, h_s = _ffn_sample(xs, ys5_s, ygdn_s, fconv_t, fw)
    s_gdn_conv = jnp.concatenate([state_gdn_conv[l][:, 1:], qkv_s[:, None, :]], axis=1)
    s_ffn_conv = jnp.concatenate([state_ffn_conv[l][:, 1:], h_s[:, None, :]], axis=1)

    st = lambda a: a[None]
    s5_shape = lambda a, n: a.reshape(1, n, S5_N_GROUPS, S5_STATE)
    return (y_p.reshape(bsz, seq, D_MODEL), y_s.reshape(nsmp, 1, D_MODEL),
            s5_shape(p_hr, bsz), s5_shape(p_hi, bsz), st(p_gdn), st(p_gdn_conv), st(p_ffn_conv),
            s5_shape(s_hr, nsmp), s5_shape(s_hi, nsmp), st(s_gdn), st(s_gdn_conv), st(s_ffn_conv))
```

```python
import functools

import jax
import jax.numpy as jnp
from jax import lax
from jax.experimental import pallas as pl
from jax.experimental.pallas import tpu as pltpu

F32 = jnp.float32
BF16 = jnp.bfloat16
NORM_EPS = 1e-6

D_MODEL = 1024
D_S5 = 512
S5_GROUP = 16
S5_N_GROUPS = 32
S5_STATE = 64
S5_CH = S5_N_GROUPS * S5_STATE
D_GDN = 512
GDN_HEADS = 4
GDN_HEAD_DIM = 128
GDN_CONV = 4
D_FF = 2816
FFN_CONV = 3
D_IN = D_S5 + 4 * D_GDN + 2 * GDN_HEADS
GATE_PAD = 128
D_IN_PAD = D_S5 + 4 * D_GDN + GATE_PAD

CHUNK = 64
FFN_COLS = 256
VMEM_LIMIT = 56 * 1024 * 1024


def _dot(a, b):
    return jnp.dot(a, b, preferred_element_type=F32)


def _dot_nt(a, b):
    return lax.dot_general(a, b, (((1,), (1,)), ((), ())), preferred_element_type=F32)


def _rms(x, g):
    ms = jnp.mean(x * x, axis=-1, keepdims=True)
    return x * lax.rsqrt(ms + NORM_EPS) * g


def _silu(x):
    return x * jax.nn.sigmoid(x)


def _rows_back(ext, back, n):
    return pltpu.roll(ext, back, 0)[8:8 + n]


def _softplus(x):
    return jnp.maximum(x, 0.0) + jnp.log1p(jnp.exp(-jnp.abs(x)))


def _proj_kernel(x_ref, g_ref, w_ref, u_ref, qkv_ref, z_ref, ba_ref):
    n = _rms(x_ref[...], g_ref[...]).astype(BF16)
    o0 = D_S5
    o1 = o0 + 3 * D_GDN
    o2 = o1 + D_GDN
    u_ref[...] = _dot(n, w_ref[:, :o0])
    qkv_ref[...] = _dot(n, w_ref[:, o0:o1])
    z_ref[...] = _dot(n, w_ref[:, o1:o2])
    ba_ref[...] = _dot(n, w_ref[:, o2:])


def _proj(x, g1, w_in_bf, tm):
    n = x.shape[0]
    row = lambda i: (i, 0)
    const = lambda i: (0, 0)
    return pl.pallas_call(
        _proj_kernel,
        grid=(n // tm,),
        in_specs=[pl.BlockSpec((tm, D_MODEL), row),
                  pl.BlockSpec((1, D_MODEL), const),
                  pl.BlockSpec((D_MODEL, D_IN_PAD), const)],
        out_specs=[pl.BlockSpec((tm, D_S5), row),
                   pl.BlockSpec((tm, 3 * D_GDN), row),
                   pl.BlockSpec((tm, D_GDN), row),
                   pl.BlockSpec((tm, GATE_PAD), row)],
        out_shape=[jax.ShapeDtypeStruct((n, D_S5), F32),
                   jax.ShapeDtypeStruct((n, 3 * D_GDN), F32),
                   jax.ShapeDtypeStruct((n, D_GDN), F32),
                   jax.ShapeDtypeStruct((n, GATE_PAD), F32)],
        compiler_params=pltpu.CompilerParams(dimension_semantics=("arbitrary",),
                                             vmem_limit_bytes=VMEM_LIMIT),
        name="proj",
    )(x, g1, w_in_bf)


S5_HALF = S5_CH // 2
S5_COLS = 1024


def _s5_kernel(u_ref, h0r_ref, h0i_ref, are_ref, aim_ref, ldt_ref, bre_ref, bim_ref, cre_ref, cim_ref,
               d_ref, wglu_ref, y_ref, hr_out, hi_out,
               abr_s, abi_s, wbr_s, wbi_s, hr_s, hi_s, xr_s, xi_s, *, nb, tt, batch_major):
    @pl.when(pl.program_id(0) == 0)
    def _init():
        ar = are_ref[...]
        ai = aim_ref[...]
        dt = jnp.exp(ldt_ref[...])
        mag = jnp.exp(ar * dt)
        abr = mag * jnp.cos(ai * dt)
        abi = mag * jnp.sin(ai * dt)
        den = ar * ar + ai * ai
        p = abr - 1.0
        fr = (p * ar + abi * ai) / den
        fi = (abi * ar - p * ai) / den
        abr_s[...] = abr
        abi_s[...] = abi
        for m in range(2):
            frm = fr[:, m * S5_HALF:(m + 1) * S5_HALF]
            fim = fi[:, m * S5_HALF:(m + 1) * S5_HALF]
            wr = bre_ref[m]
            wi = bim_ref[m]
            wbr_s[m] = (wr * frm - wi * fim).astype(BF16)
            wbi_s[m] = (wr * fim + wi * frm).astype(BF16)
        hr_s[...] = h0r_ref[...]
        hi_s[...] = h0i_ref[...]

    if batch_major:
        u = jnp.swapaxes(u_ref[...], 0, 1).reshape(tt * nb, D_S5)
    else:
        u = u_ref[...]
    ub = u.astype(BF16)
    half = D_S5 // 2
    for m in range(2):
        um = ub[:, m * half:(m + 1) * half]
        xr_s[:, m * S5_HALF:(m + 1) * S5_HALF] = _dot(um, wbr_s[m])
        xi_s[:, m * S5_HALF:(m + 1) * S5_HALF] = _dot(um, wbi_s[m])

    for cb in range(S5_CH // S5_COLS):
        cols = slice(cb * S5_COLS, (cb + 1) * S5_COLS)
        a_r = jnp.broadcast_to(abr_s[:, cols], (nb, S5_COLS))
        a_i = jnp.broadcast_to(abi_s[:, cols], (nb, S5_COLS))

        def step(t, carry, cols=cols, a_r=a_r, a_i=a_i):
            hr, hi = carry
            r0 = pl.multiple_of(t * nb, nb)
            nhr = a_r * hr - a_i * hi + xr_s[pl.ds(r0, nb), cols]
            nhi = a_r * hi + a_i * hr + xi_s[pl.ds(r0, nb), cols]
            xr_s[pl.ds(r0, nb), cols] = nhr
            xi_s[pl.ds(r0, nb), cols] = nhi
            return nhr, nhi

        hr, hi = lax.fori_loop(0, tt, step, (hr_s[:, cols], hi_s[:, cols]), unroll=min(tt, 8))
        hr_s[:, cols] = hr
        hi_s[:, cols] = hi

    hr_out[...] = hr_s[...]
    hi_out[...] = hi_s[...]

    ys = []
    for m in range(2):
        hre = xr_s[:, m * S5_HALF:(m + 1) * S5_HALF].astype(BF16)
        him = xi_s[:, m * S5_HALF:(m + 1) * S5_HALF].astype(BF16)
        ych = _dot(hre, cre_ref[m]) - _dot(him, cim_ref[m])
        ys.append(ych + d_ref[:, m * half:(m + 1) * half] * u[:, m * half:(m + 1) * half])
    y = jax.nn.gelu(jnp.concatenate(ys, axis=-1)).astype(BF16)
    gl = _dot(y, wglu_ref[...])
    y = gl[:, :D_S5] * jax.nn.sigmoid(gl[:, D_S5:])
    if batch_major:
        y_ref[...] = jnp.swapaxes(y.reshape(tt, nb, D_S5), 0, 1)
    else:
        y_ref[...] = y


def _s5(u, h0r, h0i, sw, nb, tt, batch_major):
    rows = nb * tt
    const2 = lambda i: (0, 0)
    const3 = lambda i: (0, 0, 0)
    full2 = lambda shape: pl.BlockSpec(shape, const2)
    full3 = lambda shape: pl.BlockSpec(shape, const3)
    if batch_major:
        steps = u.shape[1] // tt
        io_spec = pl.BlockSpec((nb, tt, D_S5), lambda i: (0, i, 0))
    else:
        steps = u.shape[0] // rows
        io_spec = pl.BlockSpec((rows, D_S5), lambda i: (i, 0))
    return pl.pallas_call(
        functools.partial(_s5_kernel, nb=nb, tt=tt, batch_major=batch_major),
        grid=(steps,),
        in_specs=[io_spec,
                  full2((nb, S5_CH)), full2((nb, S5_CH)),
                  full2((1, S5_CH)), full2((1, S5_CH)), full2((1, S5_CH)),
                  full3((2, D_S5 // 2, S5_HALF)), full3((2, D_S5 // 2, S5_HALF)),
                  full3((2, S5_HALF, D_S5 // 2)), full3((2, S5_HALF, D_S5 // 2)),
                  full2((1, D_S5)), full2((D_S5, 2 * D_S5))],
        out_specs=[io_spec, full2((nb, S5_CH)), full2((nb, S5_CH))],
        out_shape=[jax.ShapeDtypeStruct(u.shape, F32),
                   jax.ShapeDtypeStruct((nb, S5_CH), F32),
                   jax.ShapeDtypeStruct((nb, S5_CH), F32)],
        scratch_shapes=[pltpu.VMEM((1, S5_CH), F32), pltpu.VMEM((1, S5_CH), F32),
                        pltpu.VMEM((2, D_S5 // 2, S5_HALF), BF16), pltpu.VMEM((2, D_S5 // 2, S5_HALF), BF16),
                        pltpu.VMEM((nb, S5_CH), F32), pltpu.VMEM((nb, S5_CH), F32),
                        pltpu.VMEM((rows, S5_CH), F32), pltpu.VMEM((rows, S5_CH), F32)],
        compiler_params=pltpu.CompilerParams(dimension_semantics=("arbitrary",),
                                             vmem_limit_bytes=VMEM_LIMIT),
        name="s5",
    )(u, h0r, h0i, sw["a_re"], sw["a_im"], sw["log_dt"], sw["b_re"], sw["b_im"],
      sw["c_re"], sw["c_im"], sw["d"], sw["w_glu"])


def _s5_weights(a_re, a_im, log_dt, b_re, b_im, c_re, c_im, d, w_glu):
    g, n, c = S5_N_GROUPS, S5_STATE, S5_GROUP
    eye = jnp.eye(g // 2, dtype=F32)

    def b_blockdiag(b):
        bb = b.reshape(2, g // 2, n, c)
        return jnp.einsum("mgnc,gh->mgchn", bb, eye).reshape(2, (g // 2) * c, (g // 2) * n)

    def c_blockdiag(cm):
        cc = cm.reshape(2, g // 2, c, n)
        return jnp.einsum("mgcn,gh->mgnhc", cc, eye).reshape(2, (g // 2) * n, (g // 2) * c)

    return {
        "a_re": a_re.reshape(1, S5_CH), "a_im": a_im.reshape(1, S5_CH),
        "log_dt": jnp.repeat(log_dt, n).reshape(1, S5_CH),
        "b_re": b_blockdiag(b_re), "b_im": b_blockdiag(b_im),
        "c_re": c_blockdiag(c_re).astype(BF16), "c_im": c_blockdiag(c_im).astype(BF16),
        "d": d.reshape(1, D_S5), "w_glu": w_glu.astype(BF16),
    }


def _l2norm(x):
    return x * lax.rsqrt(jnp.sum(x * x, axis=-1, keepdims=True) + NORM_EPS)


def _gates(ba, alog_row, dtb_row):
    beta = jax.nn.sigmoid(ba)
    g = -jnp.exp(alog_row) * _softplus(ba + dtb_row)
    return beta, g


def _out_gate(o, og_row, z):
    return _rms(o, og_row) * _silu(z)


def _bdot(a, b):
    return jnp.einsum("bij,bjk->bik", a.astype(BF16), b.astype(BF16), preferred_element_type=F32)


def _tri_merge_level(n, lmat, ii, jj, shift):
    same_big = (ii >> (shift + 1)) == (jj >> (shift + 1))
    same_small = (ii >> shift) == (jj >> shift)
    c = jnp.where((same_big & jnp.logical_not(same_small))[None], lmat, 0.0)
    w = c + _bdot(n, c)
    return n - (w + _bdot(w, n))


def _interleave(*stages):
    out = []
    pos = [0] * len(stages)
    while any(p < len(st) for p, st in zip(pos, stages)):
        k = min((i for i, st in enumerate(stages) if pos[i] < len(st)),
                key=lambda i: (pos[i] + 0.5) / len(stages[i]))
        out.append(stages[k][pos[k]])
        pos[k] += 1
    return out


def _gdn_kernel(x_ref, g1_ref, win_ref, convw_ref, alog_ref, dtb_ref, og_ref,
                u_ref, y_ref, sfin_ref, convout_ref, s_s, proj_a, proj_b, set_a, set_b, *, tg, nt):
    s = pl.program_id(0)

    @pl.when(s == 0)
    def _zero_buffers():
        for ref in (*proj_b, *set_a):
            ref[...] = jnp.zeros_like(ref)

    @pl.when(lax.rem(jnp.maximum(s - 1, 0), nt) == 0)
    def _new_sequence_conv():
        for ext in (proj_a[0], proj_b[0]):
            ext[0:8, :] = jnp.zeros((8, 3 * D_GDN), F32)

    @pl.when(lax.rem(jnp.maximum(s - 2, 0), nt) == 0)
    def _new_sequence_state():
        s_s[...] = jnp.zeros_like(s_s)

    args = (x_ref, g1_ref, win_ref, convw_ref, alog_ref, dtb_ref, og_ref, u_ref, y_ref, sfin_ref, convout_ref, s_s)

    @pl.when(lax.rem(s, 2) == 0)
    def _even():
        _gdn_step(*args, proj_wr=proj_a, proj_rd=proj_b, wr=set_b, rd=set_a, tg=tg)

    @pl.when(lax.rem(s, 2) == 1)
    def _odd():
        _gdn_step(*args, proj_wr=proj_b, proj_rd=proj_a, wr=set_a, rd=set_b, tg=tg)


def _gdn_step(x_ref, g1_ref, win_ref, convw_ref, alog_ref, dtb_ref, og_ref, u_ref, y_ref, sfin_ref, convout_ref,
              s_s, *, proj_wr, proj_rd, wr, rd, tg):
    nc = tg // CHUNK
    nsys = nc * GDN_HEADS
    hcols = lambda h: slice(h * GDN_HEAD_DIM, (h + 1) * GDN_HEAD_DIM)
    og = og_ref[...]
    v = {}
    stage_a, stage_b, stage_c = [], [], []

    ext_w, z_w, ba_w = proj_wr
    o0 = D_S5
    o1 = o0 + 3 * D_GDN
    o2 = o1 + D_GDN

    def a_norm():
        v["n1"] = _rms(x_ref[...], g1_ref[...]).astype(BF16)
    stage_a.append(a_norm)

    def a_block(dst, rows, dst_c0, src_c0, width):
        def item():
            dst[rows, dst_c0:dst_c0 + width] = _dot(v["n1"], win_ref[:, src_c0:src_c0 + width])
        return item
    blk = 256
    for c0 in range(0, o0, blk):
        stage_a.append(a_block(u_ref, slice(None), c0, c0, blk))
    for c0 in range(0, o1 - o0, blk):
        stage_a.append(a_block(ext_w, slice(8, None), c0, o0 + c0, blk))
    for c0 in range(0, o2 - o1, blk):
        stage_a.append(a_block(z_w, slice(None), c0, o1 + c0, blk))
    stage_a.append(a_block(ba_w, slice(None), 0, o2, GATE_PAD))

    aq_r, b_r, o_r, gl_r, zs_r = rd
    states = [None] * GDN_HEADS

    def c_chunk(c):
        def item():
            rows = slice(c * CHUNK, (c + 1) * CHUNK)
            for h in range(GDN_HEADS):
                sys = c * GDN_HEADS + h
                st = s_s[h] if c == 0 else states[h]
                r = _dot(aq_r[sys], st.astype(BF16))
                o = r[GDN_HEAD_DIM:] + o_r[sys]
                states[h] = st * gl_r[sys, 0:1, :] + r[:GDN_HEAD_DIM] + b_r[sys]
                y_ref[rows, hcols(h)] = _rms(o, og) * zs_r[rows, hcols(h)]
                if c == nc - 1:
                    s_s[h] = states[h]
                    sfin_ref[0, h] = states[h]
        return item
    stage_c.extend(c_chunk(c) for c in range(nc))

    aq_s, b_s, o_s, gl_s, zs_s = wr
    ext_s, z_r, ba_r = proj_rd
    ii = lax.broadcasted_iota(jnp.int32, (CHUNK, CHUNK), 0)
    jj = lax.broadcasted_iota(jnp.int32, (CHUNK, CHUNK), 1)
    causal = jj <= ii
    strict = jj < ii
    scale = GDN_HEAD_DIM ** -0.5
    lmats = [None] * nsys
    rhss = [None] * nsys
    kgt_attn = [None] * nsys
    qgs = [None] * nsys

    def b_gates():
        raw = ba_r[...].T[:2 * GDN_HEADS, :]
        lanes = lambda p: jnp.concatenate([p] * (tg // GATE_PAD), axis=1)
        beta = jax.nn.sigmoid(raw)
        g = -jnp.exp(lanes(alog_ref[...])) * _softplus(raw + lanes(dtb_ref[...]))
        g1 = g.astype(BF16).astype(F32)
        g2 = (g - g1).astype(BF16).astype(F32)
        g3 = (g - g1 - g2).astype(BF16).astype(F32)
        ri = lax.broadcasted_iota(jnp.int32, (tg, tg), 0)
        ci = lax.broadcasted_iota(jnp.int32, (tg, tg), 1)
        ubd = jnp.where(((ri >> 6) == (ci >> 6)) & (ri <= ci), 1.0, 0.0).astype(BF16)
        parts = _dot(jnp.concatenate([g1, g2, g3], axis=0).astype(BF16), ubd)
        nh2 = 2 * GDN_HEADS
        gc = parts[:nh2] + parts[nh2:2 * nh2] + parts[2 * nh2:]
        row_id = lax.broadcasted_iota(jnp.int32, (nh2, tg), 0)
        rows = jnp.where(row_id < GDN_HEADS, beta, gc)
        cols = jnp.concatenate([rows, jnp.zeros((GATE_PAD - nh2, tg), F32)], axis=0).T
        v["beta_all"] = cols
        v["gc_col"] = cols
        v["gc_row"] = rows
    stage_b.append(b_gates)

    def b_silu_z(h):
        def item():
            zs_s[:, hcols(h)] = _silu(z_r[:, hcols(h)])
        return item

    def b_conv(h, part, name, norm):
        def item():
            c0 = part * D_GDN + h * GDN_HEAD_DIM
            cs = slice(c0, c0 + GDN_HEAD_DIM)
            ext = ext_s[:, cs]
            acc = ext[8:] * convw_ref[GDN_CONV - 1:GDN_CONV, cs]
            for j in range(GDN_CONV - 1):
                acc = acc + _rows_back(ext, GDN_CONV - 1 - j, tg) * convw_ref[j:j + 1, cs]
            acc = _silu(acc)
            v[name] = norm(acc)
        return item

    def b_prep(h, c):
        def item():
            beta_all, gc_col, gc_row = v["beta_all"], v["gc_col"], v["gc_row"]
            sys = c * GDN_HEADS + h
            rows = slice(c * CHUNK, (c + 1) * CHUNK)
            q = v["q"][rows]
            k = v["k"][rows]
            vv = v["v"][rows]
            beta = beta_all[rows, h:h + 1]
            gcc = gc_col[rows, GDN_HEADS + h:GDN_HEADS + h + 1]
            gcr = gc_row[GDN_HEADS + h:GDN_HEADS + h + 1, rows]
            decay = jnp.exp(jnp.where(causal, gcc - gcr, -jnp.inf))
            kb = k * beta
            egc = jnp.exp(gcc)
            g_last = gcc[CHUNK - 1:CHUNK, :]
            kq = _dot_nt(jnp.concatenate([kb, q], axis=0).astype(BF16), k.astype(BF16))
            lmats[sys] = jnp.where(strict, kq[:CHUNK] * decay, 0.0)
            attn = jnp.where(causal, kq[CHUNK:] * decay, 0.0)
            rhss[sys] = jnp.concatenate([vv * beta, kb * egc], axis=1)
            kgt_attn[sys] = jnp.concatenate([(k * jnp.exp(g_last - gcc)).T, attn], axis=0)
            qgs[sys] = q * egc
            gl_s[sys] = jnp.broadcast_to(jnp.exp(g_last), (8, GDN_HEAD_DIM))
        return item

    for h in range(GDN_HEADS):
        stage_b.append(b_silu_z(h))
        stage_b.append(b_conv(h, 0, "q", lambda a: _l2norm(a) * scale))
        stage_b.append(b_conv(h, 1, "k", _l2norm))
        stage_b.append(b_conv(h, 2, "v", lambda a: a))
        stage_b.extend(b_prep(h, c) for c in range(nc))

    def b_inverse_start():
        lmat = jnp.stack(lmats)
        v["lmat"] = lmat
        v["noff"] = -jnp.where(((ii >> 1) == (jj >> 1))[None], lmat, 0.0)
    stage_b.append(b_inverse_start)

    def b_inverse_level(shift):
        def item():
            v["noff"] = _tri_merge_level(v["noff"], v["lmat"], ii, jj, shift)
        return item
    shift = 1
    while (1 << shift) < CHUNK:
        stage_b.append(b_inverse_level(shift))
        shift += 1

    def b_solve():
        rhs = jnp.stack(rhss)
        sol = rhs + _bdot(v["noff"], rhs)
        v["prod"] = _bdot(jnp.stack(kgt_attn), sol)
    stage_b.append(b_solve)

    def b_store():
        prod = v["prod"]
        dk = GDN_HEAD_DIM
        for sys in range(nsys):
            aq_s[sys, :dk, :] = (-prod[sys, :dk, dk:]).astype(BF16)
            aq_s[sys, dk:, :] = (qgs[sys] - prod[sys, dk:, dk:]).astype(BF16)
            b_s[sys] = prod[sys, :dk, :dk]
            o_s[sys] = prod[sys, dk:, :dk]
    stage_b.append(b_store)

    for item in _interleave(stage_a, stage_b, stage_c):
        item()

    ext_w[0:8, :] = ext_s[pl.ds(tg, 8), :]
    convout_ref[0] = ext_s[pl.ds(tg + 8 - (GDN_CONV - 1), GDN_CONV - 1), :]


def _gdn_prompt(x, g1, w_in_bf, gw, bsz, seq, tg):
    nt = seq // tg
    ntiles = bsz * nt
    nsys = (tg // CHUNK) * GDN_HEADS
    tile_a = lambda s: jnp.minimum(s, ntiles - 1)
    tile_b = lambda s: jnp.clip(s - 1, 0, ntiles - 1)
    tile_c = lambda s: jnp.maximum(s - 2, 0)
    const = lambda s: (0, 0)
    proj_bufs = lambda: [pltpu.VMEM((tg + 8, 3 * D_GDN), F32),
                         pltpu.VMEM((tg, D_GDN), F32),
                         pltpu.VMEM((tg, GATE_PAD), F32)]
    resident = lambda shape: pl.BlockSpec(shape, const, pipeline_mode=pl.Buffered(1))
    handover = lambda: [pltpu.VMEM((nsys, GDN_HEAD_DIM + CHUNK, GDN_HEAD_DIM), BF16),
                        pltpu.VMEM((nsys, GDN_HEAD_DIM, GDN_HEAD_DIM), F32),
                        pltpu.VMEM((nsys, CHUNK, GDN_HEAD_DIM), F32),
                        pltpu.VMEM((nsys, 8, GDN_HEAD_DIM), F32),
                        pltpu.VMEM((tg, D_GDN), F32)]
    return pl.pallas_call(
        functools.partial(_gdn_kernel, tg=tg, nt=nt),
        grid=(ntiles + 2,),
        in_specs=[pl.BlockSpec((tg, D_MODEL), lambda s: (tile_a(s), 0)),
                  resident((1, D_MODEL)),
                  resident((D_MODEL, D_IN_PAD)),
                  resident((GDN_CONV, 3 * D_GDN)),
                  resident((2 * GDN_HEADS, GATE_PAD)),
                  resident((2 * GDN_HEADS, GATE_PAD)),
                  resident((1, GDN_HEAD_DIM))],
        out_specs=[pl.BlockSpec((tg, D_S5), lambda s: (tile_a(s), 0)),
                   pl.BlockSpec((tg, D_GDN), lambda s: (tile_c(s), 0)),
                   pl.BlockSpec((1, GDN_HEADS, GDN_HEAD_DIM, GDN_HEAD_DIM), lambda s: (tile_c(s) // nt, 0, 0, 0)),
                   pl.BlockSpec((1, GDN_CONV - 1, 3 * D_GDN), lambda s: (tile_b(s) // nt, 0, 0))],
        out_shape=[jax.ShapeDtypeStruct((bsz * seq, D_S5), F32),
                   jax.ShapeDtypeStruct((bsz * seq, D_GDN), F32),
                   jax.ShapeDtypeStruct((bsz, GDN_HEADS, GDN_HEAD_DIM, GDN_HEAD_DIM), F32),
                   jax.ShapeDtypeStruct((bsz, GDN_CONV - 1, 3 * D_GDN), F32)],
        scratch_shapes=[pltpu.VMEM((GDN_HEADS, GDN_HEAD_DIM, GDN_HEAD_DIM), F32),
                        proj_bufs(), proj_bufs(), handover(), handover()],
        compiler_params=pltpu.CompilerParams(dimension_semantics=("arbitrary",),
                                             vmem_limit_bytes=VMEM_LIMIT),
        name="proj_gdn_prompt",
    )(x, g1, w_in_bf, gw["conv_w"], gw["a_log_col"], gw["dt_bias_col"], gw["onorm_g"])


def _gdn_step_kernel(qkv_ref, st_ref, z_ref, ba_ref, s0_ref, convw_ref, alog_ref, dtb_ref, og_ref,
                     y_ref, s1_ref, *, bb):
    beta_all, g_all = _gates(ba_ref[...], alog_ref[...], dtb_ref[...])
    alpha_all = jnp.exp(g_all)
    scale = GDN_HEAD_DIM ** -0.5
    og = og_ref[...]

    for h in range(GDN_HEADS):
        def conv_cols(part, h=h):
            c0 = part * D_GDN + h * GDN_HEAD_DIM
            cs = slice(c0, c0 + GDN_HEAD_DIM)
            acc = qkv_ref[:, cs] * convw_ref[GDN_CONV - 1:GDN_CONV, cs]
            for j in range(GDN_CONV - 1):
                acc = acc + st_ref[j, :, cs] * convw_ref[j:j + 1, cs]
            return _silu(acc)

        q = _l2norm(conv_cols(0)) * scale
        k = _l2norm(conv_cols(1))
        v = conv_cols(2)
        beta = beta_all[:, h:h + 1]
        alpha = alpha_all[:, GDN_HEADS + h:GDN_HEADS + h + 1]
        qk = jnp.sum(q * k, axis=-1, keepdims=True)
        kt = k.T
        kq = jnp.concatenate([k, q], axis=0).astype(BF16)
        o_rows = []
        for b in range(bb):
            s0 = s0_ref[b, h]
            kcol = kt[:, b:b + 1]
            r = _dot(kq, s0.astype(BF16))
            ks = r[b:b + 1, :]
            qs = r[bb + b:bb + b + 1, :]
            al = alpha[b:b + 1, :]
            v_new = beta[b:b + 1, :] * (v[b:b + 1, :] - al * ks)
            o_rows.append(al * qs + qk[b:b + 1, :] * v_new)
            s1_ref[b, h] = al * s0 + kcol * v_new
        o = jnp.concatenate(o_rows, axis=0)
        cs = slice(h * GDN_HEAD_DIM, (h + 1) * GDN_HEAD_DIM)
        y_ref[:, cs] = _out_gate(o, og, z_ref[:, cs])


def _gdn_sample(qkv, st_t, z, ba, s0, gw, bb):
    n = qkv.shape[0]
    row = lambda i: (i, 0)
    const = lambda i: (0, 0)
    state = lambda i: (i, 0, 0, 0)
    return pl.pallas_call(
        functools.partial(_gdn_step_kernel, bb=bb),
        grid=(n // bb,),
        in_specs=[pl.BlockSpec((bb, 3 * D_GDN), row),
                  pl.BlockSpec((GDN_CONV - 1, bb, 3 * D_GDN), lambda i: (0, i, 0)),
                  pl.BlockSpec((bb, D_GDN), row),
                  pl.BlockSpec((bb, GATE_PAD), row),
                  pl.BlockSpec((bb, GDN_HEADS, GDN_HEAD_DIM, GDN_HEAD_DIM), state),
                  pl.BlockSpec((GDN_CONV, 3 * D_GDN), const),
                  pl.BlockSpec((1, GATE_PAD), const),
                  pl.BlockSpec((1, GATE_PAD), const),
                  pl.BlockSpec((1, GDN_HEAD_DIM), const)],
        out_specs=[pl.BlockSpec((bb, D_GDN), row),
                   pl.BlockSpec((bb, GDN_HEADS, GDN_HEAD_DIM, GDN_HEAD_DIM), state)],
        out_shape=[jax.ShapeDtypeStruct((n, D_GDN), F32),
                   jax.ShapeDtypeStruct(s0.shape, F32)],
        compiler_params=pltpu.CompilerParams(dimension_semantics=("arbitrary",),
                                             vmem_limit_bytes=VMEM_LIMIT),
        name="gdn_sample",
    )(qkv, st_t, z, ba, s0, gw["conv_w"], gw["a_log"], gw["dt_bias"], gw["onorm_g"])


def _residual_in(x_ref, ys5_ref, ygdn_ref, wout_ref, g2_ref, x1_s, n2_s):
    x1 = (x_ref[...] + _dot(ys5_ref[...].astype(BF16), wout_ref[:D_S5, :])
          + _dot(ygdn_ref[...].astype(BF16), wout_ref[D_S5:, :]))
    x1_s[...] = x1
    n2_s[...] = _rms(x1, g2_ref[...]).astype(BF16)


def _ffn_prompt_kernel(x_ref, ys5_ref, ygdn_ref, wout_ref, g2_ref, wup_ref, cw_ref, wdn_ref, gf_ref,
                       y_ref, hlast_ref, x1_s, n2_s, act_s, carry_s, *, tm):
    ti = pl.program_id(1)
    nt = pl.num_programs(1)

    @pl.when(ti == 0)
    def _reset():
        carry_s[...] = jnp.zeros_like(carry_s)

    _residual_in(x_ref, ys5_ref, ygdn_ref, wout_ref, g2_ref, x1_s, n2_s)
    n2 = n2_s[...]
    for f in range(D_FF // FFN_COLS):
        halves = []
        for part in range(2):
            c0 = part * D_FF + f * FFN_COLS
            cs = slice(c0, c0 + FFN_COLS)
            hcur = _dot(n2, wup_ref[:, cs])
            ext = jnp.concatenate([carry_s[:, cs], hcur], axis=0)
            carry_s[:, cs] = hcur[tm - 8:, :]
            conv = (_rows_back(ext, 2, tm) * cw_ref[0:1, cs]
                    + _rows_back(ext, 1, tm) * cw_ref[1:2, cs]
                    + hcur * cw_ref[2:3, cs])
            halves.append(conv)
        act_s[:, f * FFN_COLS:(f + 1) * FFN_COLS] = (_silu(halves[0]) * halves[1]).astype(BF16)
    y_ref[...] = _rms(x1_s[...] + _dot(act_s[...], wdn_ref[...]), gf_ref[...])

    @pl.when(ti == nt - 1)
    def _fin():
        hlast_ref[0] = carry_s[8 - (FFN_CONV - 1):, :]


def _ffn_prompt(x, ys5, ygdn, fw, bsz, seq, tm):
    nt = seq // tm
    row = lambda b, i: (b * nt + i, 0)
    const = lambda b, i: (0, 0)
    resident = lambda shape: pl.BlockSpec(shape, const, pipeline_mode=pl.Buffered(1))
    return pl.pallas_call(
        functools.partial(_ffn_prompt_kernel, tm=tm),
        grid=(bsz, nt),
        in_specs=[pl.BlockSpec((tm, D_MODEL), row),
                  pl.BlockSpec((tm, D_S5), row),
                  pl.BlockSpec((tm, D_GDN), row),
                  resident((D_MODEL, D_MODEL)),
                  resident((1, D_MODEL)),
                  resident((D_MODEL, 2 * D_FF)),
                  resident((FFN_CONV, 2 * D_FF)),
                  resident((D_FF, D_MODEL)),
                  resident((1, D_MODEL))],
        out_specs=[pl.BlockSpec((tm, D_MODEL), row),
                   pl.BlockSpec((1, FFN_CONV - 1, 2 * D_FF), lambda b, i: (b, 0, 0))],
        out_shape=[jax.ShapeDtypeStruct((bsz * seq, D_MODEL), F32),
                   jax.ShapeDtypeStruct((bsz, FFN_CONV - 1, 2 * D_FF), F32)],
        scratch_shapes=[pltpu.VMEM((tm, D_MODEL), F32),
                        pltpu.VMEM((tm, D_MODEL), BF16),
                        pltpu.VMEM((tm, D_FF), BF16),
                        pltpu.VMEM((8, 2 * D_FF), F32)],
        compiler_params=pltpu.CompilerParams(dimension_semantics=("arbitrary", "arbitrary"),
                                             vmem_limit_bytes=VMEM_LIMIT),
        name="ffn_prompt",
    )(x, ys5, ygdn, fw["w_out"], fw["norm2_g"], fw["w_up"], fw["conv_w"], fw["w_down"], fw["normf_g"])


def _ffn_sample_kernel(x_ref, ys5_ref, ygdn_ref, st_ref, wout_ref, g2_ref, wup_ref, cw_ref, wdn_ref, gf_ref,
                       y_ref, h_ref, x1_s, n2_s, acc_s):
    _residual_in(x_ref, ys5_ref, ygdn_ref, wout_ref, g2_ref, x1_s, n2_s)
    n2 = n2_s[...]
    for f in range(D_FF // FFN_COLS):
        halves = []
        for part in range(2):
            c0 = part * D_FF + f * FFN_COLS
            cs = slice(c0, c0 + FFN_COLS)
            hcur = _dot(n2, wup_ref[:, cs])
            h_ref[:, cs] = hcur
            halves.append(st_ref[0, :, cs] * cw_ref[0:1, cs] + st_ref[1, :, cs] * cw_ref[1:2, cs]
                          + hcur * cw_ref[2:3, cs])
        act = (_silu(halves[0]) * halves[1]).astype(BF16)
        contrib = _dot(act, wdn_ref[f * FFN_COLS:(f + 1) * FFN_COLS, :])
        if f == 0:
            acc_s[...] = contrib
        else:
            acc_s[...] += contrib
    y_ref[...] = _rms(x1_s[...] + acc_s[...], gf_ref[...])


def _ffn_sample(x, ys5, ygdn, st_t, fw):
    n = x.shape[0]
    c2 = lambda i: (0, 0)
    c3 = lambda i: (0, 0, 0)
    return pl.pallas_call(
        _ffn_sample_kernel,
        grid=(1,),
        in_specs=[pl.BlockSpec((n, D_MODEL), c2),
                  pl.BlockSpec((n, D_S5), c2),
                  pl.BlockSpec((n, D_GDN), c2),
                  pl.BlockSpec((FFN_CONV - 1, n, 2 * D_FF), c3),
                  pl.BlockSpec((D_MODEL, D_MODEL), c2),
                  pl.BlockSpec((1, D_MODEL), c2),
                  pl.BlockSpec((D_MODEL, 2 * D_FF), c2),
                  pl.BlockSpec((FFN_CONV, 2 * D_FF), c2),
                  pl.BlockSpec((D_FF, D_MODEL), c2),
                  pl.BlockSpec((1, D_MODEL), c2)],
        out_specs=[pl.BlockSpec((n, D_MODEL), c2),
                   pl.BlockSpec((n, 2 * D_FF), c2)],
        out_shape=[jax.ShapeDtypeStruct((n, D_MODEL), F32),
                   jax.ShapeDtypeStruct((n, 2 * D_FF), F32)],
        scratch_shapes=[pltpu.VMEM((n, D_MODEL), F32),
                        pltpu.VMEM((n, D_MODEL), BF16),
                        pltpu.VMEM((n, D_MODEL), F32)],
        compiler_params=pltpu.CompilerParams(dimension_semantics=("arbitrary",),
                                             vmem_limit_bytes=VMEM_LIMIT),
        name="ffn_sample",
    )(x, ys5, ygdn, st_t, fw["w_out"], fw["norm2_g"], fw["w_up"], fw["conv_w"], fw["w_down"], fw["normf_g"])


PROJ_ROWS = 512
S5_STEPS = 64
GDN_ROWS = 256
FFN_ROWS = 512
GDN_SAMPLE_ROWS = 8


def kernel(x_prompt, x_sample, state_s5_re, state_s5_im, state_gdn, state_gdn_conv, state_ffn_conv, norm1_g, w_in, s5_a_re, s5_a_im, s5_log_dt, s5_b_re, s5_b_im, s5_c_re, s5_c_im, s5_d, s5_w_glu, gdn_conv_w, gdn_a_log, gdn_dt_bias, gdn_onorm_g, w_out, norm2_g, ffn_w_up, ffn_conv_w, ffn_w_down, normf_g):
    depth = w_in.shape[0]
    assert depth == 1, "the final rmsnorm is fused into the last layer's ffn kernel"
    bsz, seq, _ = x_prompt.shape
    nsmp = x_sample.shape[0]
    assert x_sample.shape[1] == 1
    l = 0

    xp = x_prompt.reshape(bsz * seq, D_MODEL)
    xs = x_sample.reshape(nsmp, D_MODEL)

    g1 = norm1_g[l].reshape(1, D_MODEL)
    w_in_bf = jnp.pad(w_in[l], ((0, 0), (0, D_IN_PAD - D_IN))).astype(BF16)
    sw = _s5_weights(s5_a_re[l], s5_a_im[l], s5_log_dt[l], s5_b_re[l], s5_b_im[l], s5_c_re[l], s5_c_im[l],
                     s5_d[l], s5_w_glu[l])
    gate_row = lambda v: jnp.pad(v, (GDN_HEADS, GATE_PAD - 2 * GDN_HEADS)).reshape(1, GATE_PAD)
    gate_col = lambda v: jnp.broadcast_to(jnp.pad(v, (GDN_HEADS, 0))[:, None], (2 * GDN_HEADS, GATE_PAD))
    gw = {"conv_w": gdn_conv_w[l], "a_log": gate_row(gdn_a_log[l]), "dt_bias": gate_row(gdn_dt_bias[l]),
          "a_log_col": gate_col(gdn_a_log[l]), "dt_bias_col": gate_col(gdn_dt_bias[l]),
          "onorm_g": gdn_onorm_g[l].reshape(1, GDN_HEAD_DIM)}
    fw = {"w_out": w_out[l].astype(BF16), "norm2_g": norm2_g[l].reshape(1, D_MODEL),
          "w_up": ffn_w_up[l].astype(BF16), "conv_w": ffn_conv_w[l], "w_down": ffn_w_down[l].astype(BF16),
          "normf_g": normf_g.reshape(1, D_MODEL)}

    u_p, ygdn_p, p_gdn, p_gdn_conv = _gdn_prompt(xp, g1, w_in_bf, gw, bsz, seq, GDN_ROWS)
    zeros_h = jnp.zeros((bsz, S5_CH), F32)
    ys5_p, p_hr, p_hi = _s5(u_p.reshape(bsz, seq, D_S5), zeros_h, zeros_h, sw, bsz, S5_STEPS, True)
    ys5_p = ys5_p.reshape(bsz * seq, D_S5)
    y_p, p_ffn_conv = _ffn_prompt(xp, ys5_p, ygdn_p, fw, bsz, seq, FFN_ROWS)

    u_s, qkv_s, z_s, ba_s = _proj(xs, g1, w_in_bf, nsmp)
    ys5_s, s_hr, s_hi = _s5(u_s, state_s5_re[l].reshape(nsmp, S5_CH), state_s5_im[l].reshape(nsmp, S5_CH),
                            sw, nsmp, 1, False)
    gconv_t = state_gdn_conv[l].transpose(1, 0, 2)
    ygdn_s, s_gdn = _gdn_sample(qkv_s, gconv_t, z_s, ba_s, state_gdn[l], gw, GDN_SAMPLE_ROWS)
    fconv_t = state_ffn_conv[l].transpose(1, 0, 2)
    y_s, h_s = _ffn_sample(xs, ys5_s, ygdn_s, fconv_t, fw)
    s_gdn_conv = jnp.concatenate([state_gdn_conv[l][:, 1:], qkv_s[:, None, :]], axis=1)
    s_ffn_conv = jnp.concatenate([state_ffn_conv[l][:, 1:], h_s[:, None, :]], axis=1)

    st = lambda a: a[None]
    s5_shape = lambda a, n: a.reshape(1, n, S5_N_GROUPS, S5_STATE)
    return (y_p.reshape(bsz, seq, D_MODEL), y_s.reshape(nsmp, 1, D_MODEL),
            s5_shape(p_hr, bsz), s5_shape(p_hi, bsz), st(p_gdn), st(p_gdn_conv), st(p_ffn_conv),
            s5_shape(s_hr, nsmp), s5_shape(s_hi, nsmp), st(s_gdn), st(s_gdn_conv), st(s_ffn_conv))
```

```python
import functools

import jax
import jax.numpy as jnp
from jax import lax
from jax.experimental import pallas as pl
from jax.experimental.pallas import tpu as pltpu

F32 = jnp.float32
BF16 = jnp.bfloat16
NORM_EPS = 1e-6

D_MODEL = 1024
D_S5 = 512
S5_GROUP = 16
S5_N_GROUPS = 32
S5_STATE = 64
S5_CH = S5_N_GROUPS * S5_STATE
D_GDN = 512
GDN_HEADS = 4
GDN_HEAD_DIM = 128
GDN_CONV = 4
D_FF = 2816
FFN_CONV = 3
D_IN = D_S5 + 4 * D_GDN + 2 * GDN_HEADS
GATE_PAD = 128
D_IN_MAIN = D_S5 + 4 * D_GDN

CHUNK = 64
FFN_COLS = 256
VMEM_LIMIT = 56 * 1024 * 1024


def _dot(a, b):
    return jnp.dot(a, b, preferred_element_type=F32)


def _dot_nt(a, b):
    return lax.dot_general(a, b, (((1,), (1,)), ((), ())), preferred_element_type=F32)


def _rms(x, g):
    ms = jnp.mean(x * x, axis=-1, keepdims=True)
    return x * lax.rsqrt(ms + NORM_EPS) * g


def _silu(x):
    return x * jax.nn.sigmoid(x)


def _rows_back(ext, back, n):
    return pltpu.roll(ext, back, 0)[8:8 + n]


def _softplus(x):
    return jnp.maximum(x, 0.0) + jnp.log1p(jnp.exp(-jnp.abs(x)))


def _proj_kernel(x_ref, g_ref, w_ref, wgate_ref, u_ref, qkv_ref, z_ref, ba_ref):
    n = _rms(x_ref[...], g_ref[...]).astype(BF16)
    o0 = D_S5
    o1 = o0 + 3 * D_GDN
    u_ref[...] = _dot(n, w_ref[:, :o0])
    qkv_ref[...] = _dot(n, w_ref[:, o0:o1])
    z_ref[...] = _dot(n, w_ref[:, o1:])
    ba_ref[...] = _dot(n, wgate_ref[...])


def _proj(x, g1, w_in_bf, w_gate_bf, tm):
    n = x.shape[0]
    row = lambda i: (i, 0)
    const = lambda i: (0, 0)
    return pl.pallas_call(
        _proj_kernel,
        grid=(n // tm,),
        in_specs=[pl.BlockSpec((tm, D_MODEL), row),
                  pl.BlockSpec((1, D_MODEL), const),
                  pl.BlockSpec((D_MODEL, D_IN_MAIN), const),
                  pl.BlockSpec((D_MODEL, GATE_PAD), const)],
        out_specs=[pl.BlockSpec((tm, D_S5), row),
                   pl.BlockSpec((tm, 3 * D_GDN), row),
                   pl.BlockSpec((tm, D_GDN), row),
                   pl.BlockSpec((tm, GATE_PAD), row)],
        out_shape=[jax.ShapeDtypeStruct((n, D_S5), F32),
                   jax.ShapeDtypeStruct((n, 3 * D_GDN), F32),
                   jax.ShapeDtypeStruct((n, D_GDN), F32),
                   jax.ShapeDtypeStruct((n, GATE_PAD), F32)],
        compiler_params=pltpu.CompilerParams(dimension_semantics=("arbitrary",),
                                             vmem_limit_bytes=VMEM_LIMIT),
        name="proj",
    )(x, g1, w_in_bf, w_gate_bf)


S5_HALF = S5_CH // 2
S5_COLS = 1024


def _s5_kernel(u_ref, h0r_ref, h0i_ref, are_ref, aim_ref, ldt_ref, bre_ref, bim_ref, cre_ref, cim_ref,
               d_ref, wglu_ref, y_ref, hr_out, hi_out,
               abr_s, abi_s, wbr_s, wbi_s, hr_s, hi_s, xr_s, xi_s, *, nb, tt, batch_major):
    @pl.when(pl.program_id(0) == 0)
    def _init():
        ar = are_ref[...]
        ai = aim_ref[...]
        dt = jnp.exp(ldt_ref[...])
        mag = jnp.exp(ar * dt)
        abr = mag * jnp.cos(ai * dt)
        abi = mag * jnp.sin(ai * dt)
        den = ar * ar + ai * ai
        p = abr - 1.0
        fr = (p * ar + abi * ai) / den
        fi = (abi * ar - p * ai) / den
        abr_s[...] = abr
        abi_s[...] = abi
        for m in range(2):
            frm = fr[:, m * S5_HALF:(m + 1) * S5_HALF]
            fim = fi[:, m * S5_HALF:(m + 1) * S5_HALF]
            wr = bre_ref[m]
            wi = bim_ref[m]
            wbr_s[m] = (wr * frm - wi * fim).astype(BF16)
            wbi_s[m] = (wr * fim + wi * frm).astype(BF16)
        hr_s[...] = h0r_ref[...]
        hi_s[...] = h0i_ref[...]

    if batch_major:
        u = jnp.swapaxes(u_ref[...], 0, 1).reshape(tt * nb, D_S5)
    else:
        u = u_ref[...]
    ub = u.astype(BF16)
    half = D_S5 // 2
    for m in range(2):
        um = ub[:, m * half:(m + 1) * half]
        xr_s[:, m * S5_HALF:(m + 1) * S5_HALF] = _dot(um, wbr_s[m])
        xi_s[:, m * S5_HALF:(m + 1) * S5_HALF] = _dot(um, wbi_s[m])

    for cb in range(S5_CH // S5_COLS):
        cols = slice(cb * S5_COLS, (cb + 1) * S5_COLS)
        a_r = jnp.broadcast_to(abr_s[:, cols], (nb, S5_COLS))
        a_i = jnp.broadcast_to(abi_s[:, cols], (nb, S5_COLS))

        def step(t, carry, cols=cols, a_r=a_r, a_i=a_i):
            hr, hi = carry
            r0 = pl.multiple_of(t * nb, nb)
            nhr = a_r * hr - a_i * hi + xr_s[pl.ds(r0, nb), cols]
            nhi = a_r * hi + a_i * hr + xi_s[pl.ds(r0, nb), cols]
            xr_s[pl.ds(r0, nb), cols] = nhr
            xi_s[pl.ds(r0, nb), cols] = nhi
            return nhr, nhi

        hr, hi = lax.fori_loop(0, tt, step, (hr_s[:, cols], hi_s[:, cols]), unroll=min(tt, 8))
        hr_s[:, cols] = hr
        hi_s[:, cols] = hi

    hr_out[...] = hr_s[...]
    hi_out[...] = hi_s[...]

    ys = []
    for m in range(2):
        hre = xr_s[:, m * S5_HALF:(m + 1) * S5_HALF].astype(BF16)
        him = xi_s[:, m * S5_HALF:(m + 1) * S5_HALF].astype(BF16)
        ych = _dot(hre, cre_ref[m]) - _dot(him, cim_ref[m])
        ys.append(ych + d_ref[:, m * half:(m + 1) * half] * u[:, m * half:(m + 1) * half])
    y = jax.nn.gelu(jnp.concatenate(ys, axis=-1)).astype(BF16)
    gl = _dot(y, wglu_ref[...])
    y = gl[:, :D_S5] * jax.nn.sigmoid(gl[:, D_S5:])
    if batch_major:
        y_ref[...] = jnp.swapaxes(y.reshape(tt, nb, D_S5), 0, 1)
    else:
        y_ref[...] = y


def _s5(u, h0r, h0i, sw, nb, tt, batch_major):
    rows = nb * tt
    const2 = lambda i: (0, 0)
    const3 = lambda i: (0, 0, 0)
    full2 = lambda shape: pl.BlockSpec(shape, const2)
    full3 = lambda shape: pl.BlockSpec(shape, const3)
    if batch_major:
        steps = u.shape[1] // tt
        io_spec = pl.BlockSpec((nb, tt, D_S5), lambda i: (0, i, 0))
    else:
        steps = u.shape[0] // rows
        io_spec = pl.BlockSpec((rows, D_S5), lambda i: (i, 0))
    return pl.pallas_call(
        functools.partial(_s5_kernel, nb=nb, tt=tt, batch_major=batch_major),
        grid=(steps,),
        in_specs=[io_spec,
                  full2((nb, S5_CH)), full2((nb, S5_CH)),
                  full2((1, S5_CH)), full2((1, S5_CH)), full2((1, S5_CH)),
                  full3((2, D_S5 // 2, S5_HALF)), full3((2, D_S5 // 2, S5_HALF)),
                  full3((2, S5_HALF, D_S5 // 2)), full3((2, S5_HALF, D_S5 // 2)),
                  full2((1, D_S5)), full2((D_S5, 2 * D_S5))],
        out_specs=[io_spec, full2((nb, S5_CH)), full2((nb, S5_CH))],
        out_shape=[jax.ShapeDtypeStruct(u.shape, F32),
                   jax.ShapeDtypeStruct((nb, S5_CH), F32),
                   jax.ShapeDtypeStruct((nb, S5_CH), F32)],
        scratch_shapes=[pltpu.VMEM((1, S5_CH), F32), pltpu.VMEM((1, S5_CH), F32),
                        pltpu.VMEM((2, D_S5 // 2, S5_HALF), BF16), pltpu.VMEM((2, D_S5 // 2, S5_HALF), BF16),
                        pltpu.VMEM((nb, S5_CH), F32), pltpu.VMEM((nb, S5_CH), F32),
                        pltpu.VMEM((rows, S5_CH), F32), pltpu.VMEM((rows, S5_CH), F32)],
        compiler_params=pltpu.CompilerParams(dimension_semantics=("arbitrary",),
                                             vmem_limit_bytes=VMEM_LIMIT),
        name="s5",
    )(u, h0r, h0i, sw["a_re"], sw["a_im"], sw["log_dt"], sw["b_re"], sw["b_im"],
      sw["c_re"], sw["c_im"], sw["d"], sw["w_glu"])


def _s5_weights(a_re, a_im, log_dt, b_re, b_im, c_re, c_im, d, w_glu):
    g, n, c = S5_N_GROUPS, S5_STATE, S5_GROUP
    gh = g // 2

    def blockdiag(w, rows_per_group, cols_per_group):
        x = jnp.transpose(w, (0, 2, 1)).reshape(2, gh * rows_per_group, cols_per_group)
        rg = jnp.arange(gh * rows_per_group)[:, None] // rows_per_group
        cg = jnp.arange(gh * cols_per_group)[None, :] // cols_per_group
        return jnp.where(rg == cg, jnp.tile(x, (1, 1, gh)), 0.0)

    b_blockdiag = lambda b: blockdiag(b, c, n)
    c_blockdiag = lambda cm: blockdiag(cm, n, c)

    return {
        "a_re": a_re.reshape(1, S5_CH), "a_im": a_im.reshape(1, S5_CH),
        "log_dt": jnp.repeat(log_dt, n).reshape(1, S5_CH),
        "b_re": b_blockdiag(b_re), "b_im": b_blockdiag(b_im),
        "c_re": c_blockdiag(c_re).astype(BF16), "c_im": c_blockdiag(c_im).astype(BF16),
        "d": d.reshape(1, D_S5), "w_glu": w_glu.astype(BF16),
    }


def _l2norm(x):
    return x * lax.rsqrt(jnp.sum(x * x, axis=-1, keepdims=True) + NORM_EPS)


def _gates(ba, alog_row, dtb_row):
    beta = jax.nn.sigmoid(ba)
    g = -jnp.exp(alog_row) * _softplus(ba + dtb_row)
    return beta, g


def _out_gate(o, og_row, z):
    return _rms(o, og_row) * _silu(z)


def _bdot(a, b):
    return jnp.einsum("bij,bjk->bik", a.astype(BF16), b.astype(BF16), preferred_element_type=F32)


def _tri_merge_level(n, lmat, ii, jj, shift):
    same_big = (ii >> (shift + 1)) == (jj >> (shift + 1))
    same_small = (ii >> shift) == (jj >> shift)
    c = jnp.where((same_big & jnp.logical_not(same_small))[None], lmat, 0.0)
    w = c + _bdot(n, c)
    return n - (w + _bdot(w, n))


def _interleave(*stages):
    out = []
    pos = [0] * len(stages)
    while any(p < len(st) for p, st in zip(pos, stages)):
        k = min((i for i, st in enumerate(stages) if pos[i] < len(st)),
                key=lambda i: (pos[i] + 0.5) / len(stages[i]))
        out.append(stages[k][pos[k]])
        pos[k] += 1
    return out


def _gdn_kernel(x_ref, g1_ref, win_ref, wgate_ref, convw_ref, alog_ref, dtb_ref, og_ref,
                u_ref, y_ref, sfin_ref, convout_ref, s_s, proj_a, proj_b, set_a, set_b, *, tg, nt):
    s = pl.program_id(0)

    @pl.when(s == 0)
    def _zero_buffers():
        for ref in (*proj_b, *set_a):
            ref[...] = jnp.zeros_like(ref)

    @pl.when(lax.rem(jnp.maximum(s - 1, 0), nt) == 0)
    def _new_sequence_conv():
        for ext in (proj_a[0], proj_b[0]):
            ext[0:8, :] = jnp.zeros((8, 3 * D_GDN), F32)

    @pl.when(lax.rem(jnp.maximum(s - 2, 0), nt) == 0)
    def _new_sequence_state():
        s_s[...] = jnp.zeros_like(s_s)

    args = (x_ref, g1_ref, win_ref, wgate_ref, convw_ref, alog_ref, dtb_ref, og_ref, u_ref, y_ref, sfin_ref, convout_ref, s_s)

    @pl.when(lax.rem(s, 2) == 0)
    def _even():
        _gdn_step(*args, proj_wr=proj_a, proj_rd=proj_b, wr=set_b, rd=set_a, tg=tg)

    @pl.when(lax.rem(s, 2) == 1)
    def _odd():
        _gdn_step(*args, proj_wr=proj_b, proj_rd=proj_a, wr=set_a, rd=set_b, tg=tg)


def _gdn_step(x_ref, g1_ref, win_ref, wgate_ref, convw_ref, alog_ref, dtb_ref, og_ref, u_ref, y_ref, sfin_ref, convout_ref,
              s_s, *, proj_wr, proj_rd, wr, rd, tg):
    nc = tg // CHUNK
    nsys = nc * GDN_HEADS
    hcols = lambda h: slice(h * GDN_HEAD_DIM, (h + 1) * GDN_HEAD_DIM)
    og = og_ref[...]
    v = {}
    stage_a, stage_b, stage_c = [], [], []

    ext_w, z_w, ba_w = proj_wr
    o0 = D_S5
    o1 = o0 + 3 * D_GDN
    o2 = o1 + D_GDN

    def a_norm():
        v["n1"] = _rms(x_ref[...], g1_ref[...]).astype(BF16)
    stage_a.append(a_norm)

    def a_block(dst, rows, dst_c0, src_c0, width):
        def item():
            dst[rows, dst_c0:dst_c0 + width] = _dot(v["n1"], win_ref[:, src_c0:src_c0 + width])
        return item
    blk = 256
    for c0 in range(0, o0, blk):
        stage_a.append(a_block(u_ref, slice(None), c0, c0, blk))
    for c0 in range(0, o1 - o0, blk):
        stage_a.append(a_block(ext_w, slice(8, None), c0, o0 + c0, blk))
    for c0 in range(0, o2 - o1, blk):
        stage_a.append(a_block(z_w, slice(None), c0, o1 + c0, blk))

    def a_gates():
        ba_w[...] = _dot(v["n1"], wgate_ref[...])
    stage_a.append(a_gates)

    aq_r, b_r, o_r, gl_r, zs_r = rd
    states = [None] * GDN_HEADS

    def c_chunk(c):
        def item():
            rows = slice(c * CHUNK, (c + 1) * CHUNK)
            for h in range(GDN_HEADS):
                sys = c * GDN_HEADS + h
                st = s_s[h] if c == 0 else states[h]
                r = _dot(aq_r[sys], st.astype(BF16))
                o = r[GDN_HEAD_DIM:] + o_r[sys]
                states[h] = st * gl_r[sys, 0:1, :] + r[:GDN_HEAD_DIM] + b_r[sys]
                y_ref[rows, hcols(h)] = _rms(o, og) * zs_r[rows, hcols(h)]
                if c == nc - 1:
                    s_s[h] = states[h]
                    sfin_ref[0, h] = states[h]
        return item
    stage_c.extend(c_chunk(c) for c in range(nc))

    aq_s, b_s, o_s, gl_s, zs_s = wr
    ext_s, z_r, ba_r = proj_rd
    ii = lax.broadcasted_iota(jnp.int32, (CHUNK, CHUNK), 0)
    jj = lax.broadcasted_iota(jnp.int32, (CHUNK, CHUNK), 1)
    causal = jj <= ii
    strict = jj < ii
    scale = GDN_HEAD_DIM ** -0.5
    lmats = [None] * nsys
    rhss = [None] * nsys
    kgt_attn = [None] * nsys
    qgs = [None] * nsys

    def b_gates():
        raw = ba_r[...].T[:2 * GDN_HEADS, :]
        lanes = lambda p: jnp.concatenate([p] * (tg // GATE_PAD), axis=1)
        beta = jax.nn.sigmoid(raw)
        g = -jnp.exp(lanes(alog_ref[...])) * _softplus(raw + lanes(dtb_ref[...]))
        g1 = g.astype(BF16).astype(F32)
        g2 = (g - g1).astype(BF16).astype(F32)
        g3 = (g - g1 - g2).astype(BF16).astype(F32)
        ri = lax.broadcasted_iota(jnp.int32, (tg, tg), 0)
        ci = lax.broadcasted_iota(jnp.int32, (tg, tg), 1)
        ubd = jnp.where(((ri >> 6) == (ci >> 6)) & (ri <= ci), 1.0, 0.0).astype(BF16)
        parts = _dot(jnp.concatenate([g1, g2, g3], axis=0).astype(BF16), ubd)
        nh2 = 2 * GDN_HEADS
        gc = parts[:nh2] + parts[nh2:2 * nh2] + parts[2 * nh2:]
        row_id = lax.broadcasted_iota(jnp.int32, (nh2, tg), 0)
        rows = jnp.where(row_id < GDN_HEADS, beta, gc)
        cols = jnp.concatenate([rows, jnp.zeros((GATE_PAD - nh2, tg), F32)], axis=0).T
        v["beta_all"] = cols
        v["gc_col"] = cols
        v["gc_row"] = rows
    stage_b.append(b_gates)

    def b_silu_z(h):
        def item():
            zs_s[:, hcols(h)] = _silu(z_r[:, hcols(h)])
        return item

    def b_conv(h, part, name, norm):
        def item():
            c0 = part * D_GDN + h * GDN_HEAD_DIM
            cs = slice(c0, c0 + GDN_HEAD_DIM)
            ext = ext_s[:, cs]
            acc = ext[8:] * convw_ref[GDN_CONV - 1:GDN_CONV, cs]
            for j in range(GDN_CONV - 1):
                acc = acc + _rows_back(ext, GDN_CONV - 1 - j, tg) * convw_ref[j:j + 1, cs]
            acc = _silu(acc)
            v[name] = norm(acc)
        return item

    def b_prep(h, c):
        def item():
            beta_all, gc_col, gc_row = v["beta_all"], v["gc_col"], v["gc_row"]
            sys = c * GDN_HEADS + h
            rows = slice(c * CHUNK, (c + 1) * CHUNK)
            q = v["q"][rows]
            k = v["k"][rows]
            vv = v["v"][rows]
            beta = beta_all[rows, h:h + 1]
            gcc = gc_col[rows, GDN_HEADS + h:GDN_HEADS + h + 1]
            gcr = gc_row[GDN_HEADS + h:GDN_HEADS + h + 1, rows]
            decay = jnp.exp(jnp.where(causal, gcc - gcr, -jnp.inf))
            kb = k * beta
            egc = jnp.exp(gcc)
            g_last = gcc[CHUNK - 1:CHUNK, :]
            kq = _dot_nt(jnp.concatenate([kb, q], axis=0).astype(BF16), k.astype(BF16))
            lmats[sys] = jnp.where(strict, kq[:CHUNK] * decay, 0.0)
            attn = jnp.where(causal, kq[CHUNK:] * decay, 0.0)
            rhss[sys] = jnp.concatenate([vv * beta, kb * egc], axis=1)
            kgt_attn[sys] = jnp.concatenate([(k * jnp.exp(g_last - gcc)).T, attn], axis=0)
            qgs[sys] = q * egc
            gl_s[sys] = jnp.broadcast_to(jnp.exp(g_last), (8, GDN_HEAD_DIM))
        return item

    for h in range(GDN_HEADS):
        stage_b.append(b_silu_z(h))
        stage_b.append(b_conv(h, 0, "q", lambda a: _l2norm(a) * scale))
        stage_b.append(b_conv(h, 1, "k", _l2norm))
        stage_b.append(b_conv(h, 2, "v", lambda a: a))
        stage_b.extend(b_prep(h, c) for c in range(nc))

    def b_inverse_start():
        lmat = jnp.stack(lmats)
        v["lmat"] = lmat
        v["noff"] = -jnp.where(((ii >> 1) == (jj >> 1))[None], lmat, 0.0)
    stage_b.append(b_inverse_start)

    def b_inverse_level(shift):
        def item():
            v["noff"] = _tri_merge_level(v["noff"], v["lmat"], ii, jj, shift)
        return item
    shift = 1
    while (1 << shift) < CHUNK:
        stage_b.append(b_inverse_level(shift))
        shift += 1

    def b_solve():
        rhs = jnp.stack(rhss)
        sol = rhs + _bdot(v["noff"], rhs)
        v["prod"] = _bdot(jnp.stack(kgt_attn), sol)
    stage_b.append(b_solve)

    def b_store():
        prod = v["prod"]
        dk = GDN_HEAD_DIM
        for sys in range(nsys):
            aq_s[sys, :dk, :] = (-prod[sys, :dk, dk:]).astype(BF16)
            aq_s[sys, dk:, :] = (qgs[sys] - prod[sys, dk:, dk:]).astype(BF16)
            b_s[sys] = prod[sys, :dk, :dk]
            o_s[sys] = prod[sys, dk:, :dk]
    stage_b.append(b_store)

    for item in _interleave(stage_a, stage_b, stage_c):
        item()

    ext_w[0:8, :] = ext_s[pl.ds(tg, 8), :]
    convout_ref[0] = ext_s[pl.ds(tg + 8 - (GDN_CONV - 1), GDN_CONV - 1), :]


def _gdn_prompt(x, g1, w_in_bf, w_gate_bf, gw, bsz, seq, tg):
    nt = seq // tg
    ntiles = bsz * nt
    nsys = (tg // CHUNK) * GDN_HEADS
    tile_a = lambda s: jnp.minimum(s, ntiles - 1)
    tile_b = lambda s: jnp.clip(s - 1, 0, ntiles - 1)
    tile_c = lambda s: jnp.maximum(s - 2, 0)
    const = lambda s: (0, 0)
    proj_bufs = lambda: [pltpu.VMEM((tg + 8, 3 * D_GDN), F32),
                         pltpu.VMEM((tg, D_GDN), F32),
                         pltpu.VMEM((tg, GATE_PAD), F32)]
    resident = lambda shape: pl.BlockSpec(shape, const, pipeline_mode=pl.Buffered(1))
    handover = lambda: [pltpu.VMEM((nsys, GDN_HEAD_DIM + CHUNK, GDN_HEAD_DIM), BF16),
                        pltpu.VMEM((nsys, GDN_HEAD_DIM, GDN_HEAD_DIM), F32),
                        pltpu.VMEM((nsys, CHUNK, GDN_HEAD_DIM), F32),
                        pltpu.VMEM((nsys, 8, GDN_HEAD_DIM), F32),
                        pltpu.VMEM((tg, D_GDN), F32)]
    return pl.pallas_call(
        functools.partial(_gdn_kernel, tg=tg, nt=nt),
        grid=(ntiles + 2,),
        in_specs=[pl.BlockSpec((tg, D_MODEL), lambda s: (tile_a(s), 0)),
                  resident((1, D_MODEL)),
                  resident((D_MODEL, D_IN_MAIN)),
                  resident((D_MODEL, GATE_PAD)),
                  resident((GDN_CONV, 3 * D_GDN)),
                  resident((2 * GDN_HEADS, GATE_PAD)),
                  resident((2 * GDN_HEADS, GATE_PAD)),
                  resident((1, GDN_HEAD_DIM))],
        out_specs=[pl.BlockSpec((tg, D_S5), lambda s: (tile_a(s), 0)),
                   pl.BlockSpec((tg, D_GDN), lambda s: (tile_c(s), 0)),
                   pl.BlockSpec((1, GDN_HEADS, GDN_HEAD_DIM, GDN_HEAD_DIM), lambda s: (tile_c(s) // nt, 0, 0, 0)),
                   pl.BlockSpec((1, GDN_CONV - 1, 3 * D_GDN), lambda s: (tile_b(s) // nt, 0, 0))],
        out_shape=[jax.ShapeDtypeStruct((bsz * seq, D_S5), F32),
                   jax.ShapeDtypeStruct((bsz * seq, D_GDN), F32),
                   jax.ShapeDtypeStruct((bsz, GDN_HEADS, GDN_HEAD_DIM, GDN_HEAD_DIM), F32),
                   jax.ShapeDtypeStruct((bsz, GDN_CONV - 1, 3 * D_GDN), F32)],
        scratch_shapes=[pltpu.VMEM((GDN_HEADS, GDN_HEAD_DIM, GDN_HEAD_DIM), F32),
                        proj_bufs(), proj_bufs(), handover(), handover()],
        compiler_params=pltpu.CompilerParams(dimension_semantics=("arbitrary",),
                                             vmem_limit_bytes=VMEM_LIMIT),
        name="proj_gdn_prompt",
    )(x, g1, w_in_bf, w_gate_bf, gw["conv_w"], gw["a_log_col"], gw["dt_bias_col"], gw["onorm_g"])


def _gdn_step_kernel(qkv_ref, st_ref, z_ref, ba_ref, s0_ref, convw_ref, alog_ref, dtb_ref, og_ref,
                     y_ref, s1_ref, *, bb):
    beta_all, g_all = _gates(ba_ref[...], alog_ref[...], dtb_ref[...])
    alpha_all = jnp.exp(g_all)
    scale = GDN_HEAD_DIM ** -0.5
    og = og_ref[...]

    for h in range(GDN_HEADS):
        def conv_cols(part, h=h):
            c0 = part * D_GDN + h * GDN_HEAD_DIM
            cs = slice(c0, c0 + GDN_HEAD_DIM)
            acc = qkv_ref[:, cs] * convw_ref[GDN_CONV - 1:GDN_CONV, cs]
            for j in range(GDN_CONV - 1):
                acc = acc + st_ref[j, :, cs] * convw_ref[j:j + 1, cs]
            return _silu(acc)

        q = _l2norm(conv_cols(0)) * scale
        k = _l2norm(conv_cols(1))
        v = conv_cols(2)
        beta = beta_all[:, h:h + 1]
        alpha = alpha_all[:, GDN_HEADS + h:GDN_HEADS + h + 1]
        qk = jnp.sum(q * k, axis=-1, keepdims=True)
        kt = k.T
        kq = jnp.concatenate([k, q], axis=0).astype(BF16)
        o_rows = []
        for b in range(bb):
            s0 = s0_ref[b, h]
            kcol = kt[:, b:b + 1]
            r = _dot(kq, s0.astype(BF16))
            ks = r[b:b + 1, :]
            qs = r[bb + b:bb + b + 1, :]
            al = alpha[b:b + 1, :]
            v_new = beta[b:b + 1, :] * (v[b:b + 1, :] - al * ks)
            o_rows.append(al * qs + qk[b:b + 1, :] * v_new)
            s1_ref[b, h] = al * s0 + kcol * v_new
        o = jnp.concatenate(o_rows, axis=0)
        cs = slice(h * GDN_HEAD_DIM, (h + 1) * GDN_HEAD_DIM)
        y_ref[:, cs] = _out_gate(o, og, z_ref[:, cs])


def _gdn_sample(qkv, st_t, z, ba, s0, gw, bb):
    n = qkv.shape[0]
    row = lambda i: (i, 0)
    const = lambda i: (0, 0)
    state = lambda i: (i, 0, 0, 0)
    return pl.pallas_call(
        functools.partial(_gdn_step_kernel, bb=bb),
        grid=(n // bb,),
        in_specs=[pl.BlockSpec((bb, 3 * D_GDN), row),
                  pl.BlockSpec((GDN_CONV - 1, bb, 3 * D_GDN), lambda i: (0, i, 0)),
                  pl.BlockSpec((bb, D_GDN), row),
                  pl.BlockSpec((bb, GATE_PAD), row),
                  pl.BlockSpec((bb, GDN_HEADS, GDN_HEAD_DIM, GDN_HEAD_DIM), state),
                  pl.BlockSpec((GDN_CONV, 3 * D_GDN), const),
                  pl.BlockSpec((1, GATE_PAD), const),
                  pl.BlockSpec((1, GATE_PAD), const),
                  pl.BlockSpec((1, GDN_HEAD_DIM), const)],
        out_specs=[pl.BlockSpec((bb, D_GDN), row),
                   pl.BlockSpec((bb, GDN_HEADS, GDN_HEAD_DIM, GDN_HEAD_DIM), state)],
        out_shape=[jax.ShapeDtypeStruct((n, D_GDN), F32),
                   jax.ShapeDtypeStruct(s0.shape, F32)],
        compiler_params=pltpu.CompilerParams(dimension_semantics=("arbitrary",),
                                             vmem_limit_bytes=VMEM_LIMIT),
        name="gdn_sample",
    )(qkv, st_t, z, ba, s0, gw["conv_w"], gw["a_log"], gw["dt_bias"], gw["onorm_g"])


def _residual_in(x_ref, ys5_ref, ygdn_ref, wout_ref, g2_ref, x1_s, n2_s):
    x1 = (x_ref[...] + _dot(ys5_ref[...].astype(BF16), wout_ref[:D_S5, :])
          + _dot(ygdn_ref[...].astype(BF16), wout_ref[D_S5:, :]))
    x1_s[...] = x1
    n2_s[...] = _rms(x1, g2_ref[...]).astype(BF16)


def _ffn_prompt_kernel(x_ref, ys5_ref, ygdn_ref, wout_ref, g2_ref, wup_ref, cw_ref, wdn_ref, gf_ref,
                       y_ref, hlast_ref, x1_s, n2_s, act_s, carry_s, *, tm):
    ti = pl.program_id(1)
    nt = pl.num_programs(1)

    @pl.when(ti == 0)
    def _reset():
        carry_s[...] = jnp.zeros_like(carry_s)

    _residual_in(x_ref, ys5_ref, ygdn_ref, wout_ref, g2_ref, x1_s, n2_s)
    n2 = n2_s[...]
    for f in range(D_FF // FFN_COLS):
        halves = []
        for part in range(2):
            c0 = part * D_FF + f * FFN_COLS
            cs = slice(c0, c0 + FFN_COLS)
            hcur = _dot(n2, wup_ref[:, cs])
            ext = jnp.concatenate([carry_s[:, cs], hcur], axis=0)
            carry_s[:, cs] = hcur[tm - 8:, :]
            conv = (_rows_back(ext, 2, tm) * cw_ref[0:1, cs]
                    + _rows_back(ext, 1, tm) * cw_ref[1:2, cs]
                    + hcur * cw_ref[2:3, cs])
            halves.append(conv)
        act_s[:, f * FFN_COLS:(f + 1) * FFN_COLS] = (_silu(halves[0]) * halves[1]).astype(BF16)
    y_ref[...] = _rms(x1_s[...] + _dot(act_s[...], wdn_ref[...]), gf_ref[...])

    @pl.when(ti == nt - 1)
    def _fin():
        hlast_ref[0] = carry_s[8 - (FFN_CONV - 1):, :]


def _ffn_prompt(x, ys5, ygdn, fw, bsz, seq, tm):
    nt = seq // tm
    row = lambda b, i: (b * nt + i, 0)
    const = lambda b, i: (0, 0)
    resident = lambda shape: pl.BlockSpec(shape, const, pipeline_mode=pl.Buffered(1))
    return pl.pallas_call(
        functools.partial(_ffn_prompt_kernel, tm=tm),
        grid=(bsz, nt),
        in_specs=[pl.BlockSpec((tm, D_MODEL), row),
                  pl.BlockSpec((tm, D_S5), row),
                  pl.BlockSpec((tm, D_GDN), row),
                  resident((D_MODEL, D_MODEL)),
                  resident((1, D_MODEL)),
                  resident((D_MODEL, 2 * D_FF)),
                  resident((FFN_CONV, 2 * D_FF)),
                  resident((D_FF, D_MODEL)),
                  resident((1, D_MODEL))],
        out_specs=[pl.BlockSpec((tm, D_MODEL), row),
                   pl.BlockSpec((1, FFN_CONV - 1, 2 * D_FF), lambda b, i: (b, 0, 0))],
        out_shape=[jax.ShapeDtypeStruct((bsz * seq, D_MODEL), F32),
                   jax.ShapeDtypeStruct((bsz, FFN_CONV - 1, 2 * D_FF), F32)],
        scratch_shapes=[pltpu.VMEM((tm, D_MODEL), F32),
                        pltpu.VMEM((tm, D_MODEL), BF16),
                        pltpu.VMEM((tm, D_FF), BF16),
                        pltpu.VMEM((8, 2 * D_FF), F32)],
        compiler_params=pltpu.CompilerParams(dimension_semantics=("arbitrary", "arbitrary"),
                                             vmem_limit_bytes=VMEM_LIMIT),
        name="ffn_prompt",
    )(x, ys5, ygdn, fw["w_out"], fw["norm2_g"], fw["w_up"], fw["conv_w"], fw["w_down"], fw["normf_g"])


def _ffn_sample_kernel(x_ref, ys5_ref, ygdn_ref, st_ref, wout_ref, g2_ref, wup_ref, cw_ref, wdn_ref, gf_ref,
                       y_ref, h_ref, x1_s, n2_s, acc_s):
    _residual_in(x_ref, ys5_ref, ygdn_ref, wout_ref, g2_ref, x1_s, n2_s)
    n2 = n2_s[...]
    for f in range(D_FF // FFN_COLS):
        halves = []
        for part in range(2):
            c0 = part * D_FF + f * FFN_COLS
            cs = slice(c0, c0 + FFN_COLS)
            hcur = _dot(n2, wup_ref[:, cs])
            h_ref[:, cs] = hcur
            halves.append(st_ref[0, :, cs] * cw_ref[0:1, cs] + st_ref[1, :, cs] * cw_ref[1:2, cs]
                          + hcur * cw_ref[2:3, cs])
        act = (_silu(halves[0]) * halves[1]).astype(BF16)
        contrib = _dot(act, wdn_ref[f * FFN_COLS:(f + 1) * FFN_COLS, :])
        if f == 0:
            acc_s[...] = contrib
        else:
            acc_s[...] += contrib
    y_ref[...] = _rms(x1_s[...] + acc_s[...], gf_ref[...])


def _ffn_sample(x, ys5, ygdn, st_t, fw):
    n = x.shape[0]
    c2 = lambda i: (0, 0)
    c3 = lambda i: (0, 0, 0)
    return pl.pallas_call(
        _ffn_sample_kernel,
        grid=(1,),
        in_specs=[pl.BlockSpec((n, D_MODEL), c2),
                  pl.BlockSpec((n, D_S5), c2),
                  pl.BlockSpec((n, D_GDN), c2),
                  pl.BlockSpec((FFN_CONV - 1, n, 2 * D_FF), c3),
                  pl.BlockSpec((D_MODEL, D_MODEL), c2),
                  pl.BlockSpec((1, D_MODEL), c2),
                  pl.BlockSpec((D_MODEL, 2 * D_FF), c2),
                  pl.BlockSpec((FFN_CONV, 2 * D_FF), c2),
                  pl.BlockSpec((D_FF, D_MODEL), c2),
                  pl.BlockSpec((1, D_MODEL), c2)],
        out_specs=[pl.BlockSpec((n, D_MODEL), c2),
                   pl.BlockSpec((n, 2 * D_FF), c2)],
        out_shape=[jax.ShapeDtypeStruct((n, D_MODEL), F32),
                   jax.ShapeDtypeStruct((n, 2 * D_FF), F32)],
        scratch_shapes=[pltpu.VMEM((n, D_MODEL), F32),
                        pltpu.VMEM((n, D_MODEL), BF16),
                        pltpu.VMEM((n, D_MODEL), F32)],
        compiler_params=pltpu.CompilerParams(dimension_semantics=("arbitrary",),
                                             vmem_limit_bytes=VMEM_LIMIT),
        name="ffn_sample",
    )(x, ys5, ygdn, st_t, fw["w_out"], fw["norm2_g"], fw["w_up"], fw["conv_w"], fw["w_down"], fw["normf_g"])


PROJ_ROWS = 512
S5_STEPS = 128
GDN_ROWS = 256
FFN_ROWS = 512
GDN_SAMPLE_ROWS = 8


def kernel(x_prompt, x_sample, state_s5_re, state_s5_im, state_gdn, state_gdn_conv, state_ffn_conv, norm1_g, w_in, s5_a_re, s5_a_im, s5_log_dt, s5_b_re, s5_b_im, s5_c_re, s5_c_im, s5_d, s5_w_glu, gdn_conv_w, gdn_a_log, gdn_dt_bias, gdn_onorm_g, w_out, norm2_g, ffn_w_up, ffn_conv_w, ffn_w_down, normf_g):
    depth = w_in.shape[0]
    assert depth == 1, "the final rmsnorm is fused into the last layer's ffn kernel"
    bsz, seq, _ = x_prompt.shape
    nsmp = x_sample.shape[0]
    assert x_sample.shape[1] == 1
    l = 0

    xp = x_prompt.reshape(bsz * seq, D_MODEL)
    xs = x_sample.reshape(nsmp, D_MODEL)

    g1 = norm1_g[l].reshape(1, D_MODEL)
    w_in_bf = w_in[l][:, :D_IN_MAIN].astype(BF16)
    w_gate_bf = jnp.pad(w_in[l][:, D_IN_MAIN:], ((0, 0), (0, GATE_PAD - 2 * GDN_HEADS))).astype(BF16)
    sw = _s5_weights(s5_a_re[l], s5_a_im[l], s5_log_dt[l], s5_b_re[l], s5_b_im[l], s5_c_re[l], s5_c_im[l],
                     s5_d[l], s5_w_glu[l])
    gate_row = lambda v: jnp.pad(v, (GDN_HEADS, GATE_PAD - 2 * GDN_HEADS)).reshape(1, GATE_PAD)
    gate_col = lambda v: jnp.broadcast_to(jnp.pad(v, (GDN_HEADS, 0))[:, None], (2 * GDN_HEADS, GATE_PAD))
    gw = {"conv_w": gdn_conv_w[l], "a_log": gate_row(gdn_a_log[l]), "dt_bias": gate_row(gdn_dt_bias[l]),
          "a_log_col": gate_col(gdn_a_log[l]), "dt_bias_col": gate_col(gdn_dt_bias[l]),
          "onorm_g": gdn_onorm_g[l].reshape(1, GDN_HEAD_DIM)}
    fw = {"w_out": w_out[l].astype(BF16), "norm2_g": norm2_g[l].reshape(1, D_MODEL),
          "w_up": ffn_w_up[l].astype(BF16), "conv_w": ffn_conv_w[l], "w_down": ffn_w_down[l].astype(BF16),
          "normf_g": normf_g.reshape(1, D_MODEL)}

    u_p, ygdn_p, p_gdn, p_gdn_conv = _gdn_prompt(xp, g1, w_in_bf, w_gate_bf, gw, bsz, seq, GDN_ROWS)
    zeros_h = jnp.zeros((bsz, S5_CH), F32)
    ys5_p, p_hr, p_hi = _s5(u_p.reshape(bsz, seq, D_S5), zeros_h, zeros_h, sw, bsz, S5_STEPS, True)
    ys5_p = ys5_p.reshape(bsz * seq, D_S5)
    y_p, p_ffn_conv = _ffn_prompt(xp, ys5_p, ygdn_p, fw, bsz, seq, FFN_ROWS)

    u_s, qkv_s, z_s, ba_s = _proj(xs, g1, w_in_bf, w_gate_bf, nsmp)
    ys5_s, s_hr, s_hi = _s5(u_s, state_s5_re[l].reshape(nsmp, S5_CH), state_s5_im[l].reshape(nsmp, S5_CH),
                            sw, nsmp, 1, False)
    gconv_t = state_gdn_conv[l].transpose(1, 0, 2)
    ygdn_s, s_gdn = _gdn_sample(qkv_s, gconv_t, z_s, ba_s, state_gdn[l], gw, GDN_SAMPLE_ROWS)
    fconv_t = state_ffn_conv[l].transpose(1, 0, 2)
    y_s, h_s = _ffn_sample(xs, ys5_s, ygdn_s, fconv_t, fw)
    s_gdn_conv = jnp.concatenate([state_gdn_conv[l][:, 1:], qkv_s[:, None, :]], axis=1)
    s_ffn_conv = jnp.concatenate([state_ffn_conv[l][:, 1:], h_s[:, None, :]], axis=1)

    st = lambda a: a[None]
    s5_shape = lambda a, n: a.reshape(1, n, S5_N_GROUPS, S5_STATE)
    return (y_p.reshape(bsz, seq, D_MODEL), y_s.reshape(nsmp, 1, D_MODEL),
            s5_shape(p_hr, bsz), s5_shape(p_hi, bsz), st(p_gdn), st(p_gdn_conv), st(p_ffn_conv),
            s5_shape(s_hr, nsmp), s5_shape(s_hi, nsmp), st(s_gdn), st(s_gdn_conv), st(s_ffn_conv))
```

```python
import functools

import jax
import jax.numpy as jnp
from jax import lax
from jax.experimental import pallas as pl
from jax.experimental.pallas import tpu as pltpu

F32 = jnp.float32
BF16 = jnp.bfloat16
NORM_EPS = 1e-6

D_MODEL = 1024
D_S5 = 512
S5_GROUP = 16
S5_N_GROUPS = 32
S5_STATE = 64
S5_CH = S5_N_GROUPS * S5_STATE
D_GDN = 512
GDN_HEADS = 4
GDN_HEAD_DIM = 128
GDN_CONV = 4
D_FF = 2816
FFN_CONV = 3
D_IN = D_S5 + 4 * D_GDN + 2 * GDN_HEADS
GATE_PAD = 128
D_IN_MAIN = D_S5 + 4 * D_GDN

CHUNK = 64
FFN_COLS = 256
VMEM_LIMIT = 56 * 1024 * 1024


def _dot(a, b):
    return jnp.dot(a, b, preferred_element_type=F32)


def _dot_nt(a, b):
    return lax.dot_general(a, b, (((1,), (1,)), ((), ())), preferred_element_type=F32)


def _rms(x, g):
    ms = jnp.mean(x * x, axis=-1, keepdims=True)
    return x * lax.rsqrt(ms + NORM_EPS) * g


def _silu(x):
    return x * jax.nn.sigmoid(x)


def _rows_back(ext, back, n):
    return pltpu.roll(ext, back, 0)[8:8 + n]


def _softplus(x):
    return jnp.maximum(x, 0.0) + jnp.log1p(jnp.exp(-jnp.abs(x)))


def _proj_kernel(x_ref, g_ref, w_ref, wgate_ref, u_ref, qkv_ref, z_ref, ba_ref):
    n = _rms(x_ref[...], g_ref[...]).astype(BF16)
    o0 = D_S5
    o1 = o0 + 3 * D_GDN
    u_ref[...] = _dot(n, w_ref[:, :o0])
    qkv_ref[...] = _dot(n, w_ref[:, o0:o1])
    z_ref[...] = _dot(n, w_ref[:, o1:])
    ba_ref[...] = _dot(n, wgate_ref[...])


def _proj(x, g1, w_in_bf, w_gate_bf, tm):
    n = x.shape[0]
    row = lambda i: (i, 0)
    const = lambda i: (0, 0)
    return pl.pallas_call(
        _proj_kernel,
        grid=(n // tm,),
        in_specs=[pl.BlockSpec((tm, D_MODEL), row),
                  pl.BlockSpec((1, D_MODEL), const),
                  pl.BlockSpec((D_MODEL, D_IN_MAIN), const),
                  pl.BlockSpec((D_MODEL, GATE_PAD), const)],
        out_specs=[pl.BlockSpec((tm, D_S5), row),
                   pl.BlockSpec((tm, 3 * D_GDN), row),
                   pl.BlockSpec((tm, D_GDN), row),
                   pl.BlockSpec((tm, GATE_PAD), row)],
        out_shape=[jax.ShapeDtypeStruct((n, D_S5), F32),
                   jax.ShapeDtypeStruct((n, 3 * D_GDN), F32),
                   jax.ShapeDtypeStruct((n, D_GDN), F32),
                   jax.ShapeDtypeStruct((n, GATE_PAD), F32)],
        compiler_params=pltpu.CompilerParams(dimension_semantics=("arbitrary",),
                                             vmem_limit_bytes=VMEM_LIMIT),
        name="proj",
    )(x, g1, w_in_bf, w_gate_bf)


S5_HALF = S5_CH // 2
S5_COLS = 1024
S5_SCAN_GROUP = 8
S5_PARTS = 4


def _s5_kernel(u_ref, h0r_ref, h0i_ref, are_ref, aim_ref, ldt_ref, bre_ref, bim_ref, cre_ref, cim_ref,
               d_ref, wglu_ref, y_ref, hr_out, hi_out,
               abr_s, abi_s, wbr_s, wbi_s, hr_s, hi_s, xr_s, xi_s, y_s, *, nb, tt, batch_major):
    @pl.when(pl.program_id(0) == 0)
    def _init():
        ar = are_ref[...]
        ai = aim_ref[...]
        dt = jnp.exp(ldt_ref[...])
        mag = jnp.exp(ar * dt)
        abr = mag * jnp.cos(ai * dt)
        abi = mag * jnp.sin(ai * dt)
        den = ar * ar + ai * ai
        p = abr - 1.0
        fr = (p * ar + abi * ai) / den
        fi = (abi * ar - p * ai) / den
        abr_s[...] = abr
        abi_s[...] = abi
        for m in range(2):
            frm = fr[:, m * S5_HALF:(m + 1) * S5_HALF]
            fim = fi[:, m * S5_HALF:(m + 1) * S5_HALF]
            wr = bre_ref[m]
            wi = bim_ref[m]
            wbr_s[m] = (wr * frm - wi * fim).astype(BF16)
            wbi_s[m] = (wr * fim + wi * frm).astype(BF16)
        hr_s[...] = h0r_ref[...]
        hi_s[...] = h0i_ref[...]

    if batch_major:
        u = jnp.swapaxes(u_ref[...], 0, 1).reshape(tt * nb, D_S5)
    else:
        u = u_ref[...]
    ub = u.astype(BF16)
    half = D_S5 // 2

    nparts = S5_PARTS if tt % (S5_PARTS * S5_SCAN_GROUP) == 0 else 1
    tp = tt // nparts
    part_rows = lambda p: slice(p * tp * nb, (p + 1) * tp * nb)

    blk = 256

    def in_proj(p):
        def block(m, ws, xs, c0):
            def item():
                xs[part_rows(p), m * S5_HALF + c0:m * S5_HALF + c0 + blk] = _dot(
                    ub[part_rows(p), m * half:(m + 1) * half], ws[m, :, c0:c0 + blk])
            return item
        return [block(m, ws, xs, c0) for m in range(2) for ws, xs in ((wbr_s, xr_s), (wbi_s, xi_s))
                for c0 in range(0, S5_HALF, blk)]

    def scan(p):
        def group(cb, t0, steps):
            def item():
                cols = slice(cb * S5_COLS, (cb + 1) * S5_COLS)
                a_r = jnp.broadcast_to(abr_s[:, cols], (nb, S5_COLS))
                a_i = jnp.broadcast_to(abi_s[:, cols], (nb, S5_COLS))
                hr = hr_s[:, cols]
                hi = hi_s[:, cols]
                for t in range(t0, t0 + steps):
                    rows = slice(t * nb, (t + 1) * nb)
                    hr, hi = (a_r * hr - a_i * hi + xr_s[rows, cols],
                              a_r * hi + a_i * hr + xi_s[rows, cols])
                    xr_s[rows, cols] = hr
                    xi_s[rows, cols] = hi
                hr_s[:, cols] = hr
                hi_s[:, cols] = hi
            return item
        steps = min(tp, S5_SCAN_GROUP)
        return [group(cb, t0, steps) for t0 in range(p * tp, (p + 1) * tp, steps)
                for cb in range(S5_CH // S5_COLS)]

    def out_proj(p):
        ys = {}

        def re_block(m):
            def item():
                hre = xr_s[part_rows(p), m * S5_HALF:(m + 1) * S5_HALF].astype(BF16)
                ys[m] = (_dot(hre, cre_ref[m])
                         + d_ref[:, m * half:(m + 1) * half] * u[part_rows(p), m * half:(m + 1) * half])
            return item

        def im_block(m):
            def item():
                him = xi_s[part_rows(p), m * S5_HALF:(m + 1) * S5_HALF].astype(BF16)
                ys[m] = ys[m] - _dot(him, cim_ref[m])
            return item

        def act():
            ys["act"] = jax.nn.gelu(jnp.concatenate([ys[0], ys[1]], axis=-1)).astype(BF16)

        def glu(c0):
            def item():
                val = _dot(ys["act"], wglu_ref[:, c0:c0 + blk])
                gate = _dot(ys["act"], wglu_ref[:, D_S5 + c0:D_S5 + c0 + blk])
                y_s[part_rows(p), c0:c0 + blk] = val * jax.nn.sigmoid(gate)
            return item
        return ([re_block(0), im_block(0), re_block(1), im_block(1), act]
                + [glu(c0) for c0 in range(0, D_S5, blk)])

    order = in_proj(0)
    for p in range(nparts):
        nxt = in_proj(p + 1) if p + 1 < nparts else []
        prv = out_proj(p - 1) if p > 0 else []
        order += _interleave(nxt + prv, scan(p)) if (nxt or prv) else scan(p)
    order += out_proj(nparts - 1)
    for item in order:
        item()

    hr_out[...] = hr_s[...]
    hi_out[...] = hi_s[...]
    if batch_major:
        y_ref[...] = jnp.swapaxes(y_s[...].reshape(tt, nb, D_S5), 0, 1)
    else:
        y_ref[...] = y_s[...]


def _s5(u, h0r, h0i, sw, nb, tt, batch_major):
    rows = nb * tt
    const2 = lambda i: (0, 0)
    const3 = lambda i: (0, 0, 0)
    full2 = lambda shape: pl.BlockSpec(shape, const2)
    full3 = lambda shape: pl.BlockSpec(shape, const3)
    if batch_major:
        steps = u.shape[1] // tt
        io_spec = pl.BlockSpec((nb, tt, D_S5), lambda i: (0, i, 0))
    else:
        steps = u.shape[0] // rows
        io_spec = pl.BlockSpec((rows, D_S5), lambda i: (i, 0))
    return pl.pallas_call(
        functools.partial(_s5_kernel, nb=nb, tt=tt, batch_major=batch_major),
        grid=(steps,),
        in_specs=[io_spec,
                  full2((nb, S5_CH)), full2((nb, S5_CH)),
                  full2((1, S5_CH)), full2((1, S5_CH)), full2((1, S5_CH)),
                  full3((2, D_S5 // 2, S5_HALF)), full3((2, D_S5 // 2, S5_HALF)),
                  full3((2, S5_HALF, D_S5 // 2)), full3((2, S5_HALF, D_S5 // 2)),
                  full2((1, D_S5)), full2((D_S5, 2 * D_S5))],
        out_specs=[io_spec, full2((nb, S5_CH)), full2((nb, S5_CH))],
        out_shape=[jax.ShapeDtypeStruct(u.shape, F32),
                   jax.ShapeDtypeStruct((nb, S5_CH), F32),
                   jax.ShapeDtypeStruct((nb, S5_CH), F32)],
        scratch_shapes=[pltpu.VMEM((1, S5_CH), F32), pltpu.VMEM((1, S5_CH), F32),
                        pltpu.VMEM((2, D_S5 // 2, S5_HALF), BF16), pltpu.VMEM((2, D_S5 // 2, S5_HALF), BF16),
                        pltpu.VMEM((nb, S5_CH), F32), pltpu.VMEM((nb, S5_CH), F32),
                        pltpu.VMEM((rows, S5_CH), F32), pltpu.VMEM((rows, S5_CH), F32),
                        pltpu.VMEM((rows, D_S5), F32)],
        compiler_params=pltpu.CompilerParams(dimension_semantics=("arbitrary",),
                                             vmem_limit_bytes=VMEM_LIMIT),
        name="s5",
    )(u, h0r, h0i, sw["a_re"], sw["a_im"], sw["log_dt"], sw["b_re"], sw["b_im"],
      sw["c_re"], sw["c_im"], sw["d"], sw["w_glu"])


def _s5_weights(a_re, a_im, log_dt, b_re, b_im, c_re, c_im, d, w_glu):
    g, n, c = S5_N_GROUPS, S5_STATE, S5_GROUP
    gh = g // 2

    def blockdiag(w, rows_per_group, cols_per_group):
        x = jnp.transpose(w, (0, 2, 1)).reshape(2, gh * rows_per_group, cols_per_group)
        rg = jnp.arange(gh * rows_per_group)[:, None] // rows_per_group
        cg = jnp.arange(gh * cols_per_group)[None, :] // cols_per_group
        return jnp.where(rg == cg, jnp.tile(x, (1, 1, gh)), 0.0)

    b_blockdiag = lambda b: blockdiag(b, c, n)
    c_blockdiag = lambda cm: blockdiag(cm, n, c)

    return {
        "a_re": a_re.reshape(1, S5_CH), "a_im": a_im.reshape(1, S5_CH),
        "log_dt": jnp.repeat(log_dt, n).reshape(1, S5_CH),
        "b_re": b_blockdiag(b_re), "b_im": b_blockdiag(b_im),
        "c_re": c_blockdiag(c_re).astype(BF16), "c_im": c_blockdiag(c_im).astype(BF16),
        "d": d.reshape(1, D_S5), "w_glu": w_glu.astype(BF16),
    }


def _l2norm(x):
    return x * lax.rsqrt(jnp.sum(x * x, axis=-1, keepdims=True) + NORM_EPS)


def _gates(ba, alog_row, dtb_row):
    beta = jax.nn.sigmoid(ba)
    g = -jnp.exp(alog_row) * _softplus(ba + dtb_row)
    return beta, g


def _out_gate(o, og_row, z):
    return _rms(o, og_row) * _silu(z)


def _bdot(a, b):
    return jnp.einsum("bij,bjk->bik", a.astype(BF16), b.astype(BF16), preferred_element_type=F32)


def _tri_merge_level(n, lmat, ii, jj, shift):
    same_big = (ii >> (shift + 1)) == (jj >> (shift + 1))
    same_small = (ii >> shift) == (jj >> shift)
    c = jnp.where((same_big & jnp.logical_not(same_small))[None], lmat, 0.0)
    w = c + _bdot(n, c)
    return n - (w + _bdot(w, n))


def _interleave(*stages):
    out = []
    pos = [0] * len(stages)
    while any(p < len(st) for p, st in zip(pos, stages)):
        k = min((i for i, st in enumerate(stages) if pos[i] < len(st)),
                key=lambda i: (pos[i] + 0.5) / len(stages[i]))
        out.append(stages[k][pos[k]])
        pos[k] += 1
    return out


def _gdn_kernel(x_ref, g1_ref, win_ref, wgate_ref, convw_ref, alog_ref, dtb_ref, og_ref,
                u_ref, y_ref, sfin_ref, convout_ref, s_s, proj_a, proj_b, set_a, set_b, *, tg, nt):
    s = pl.program_id(0)

    @pl.when(s == 0)
    def _zero_buffers():
        for ref in (*proj_b, *set_a):
            ref[...] = jnp.zeros_like(ref)

    @pl.when(lax.rem(jnp.maximum(s - 1, 0), nt) == 0)
    def _new_sequence_conv():
        for ext in (proj_a[0], proj_b[0]):
            ext[0:8, :] = jnp.zeros((8, 3 * D_GDN), F32)

    @pl.when(lax.rem(jnp.maximum(s - 2, 0), nt) == 0)
    def _new_sequence_state():
        s_s[...] = jnp.zeros_like(s_s)

    args = (x_ref, g1_ref, win_ref, wgate_ref, convw_ref, alog_ref, dtb_ref, og_ref, u_ref, y_ref, sfin_ref, convout_ref, s_s)

    @pl.when(lax.rem(s, 2) == 0)
    def _even():
        _gdn_step(*args, proj_wr=proj_a, proj_rd=proj_b, wr=set_b, rd=set_a, tg=tg)

    @pl.when(lax.rem(s, 2) == 1)
    def _odd():
        _gdn_step(*args, proj_wr=proj_b, proj_rd=proj_a, wr=set_a, rd=set_b, tg=tg)


def _gdn_step(x_ref, g1_ref, win_ref, wgate_ref, convw_ref, alog_ref, dtb_ref, og_ref, u_ref, y_ref, sfin_ref, convout_ref,
              s_s, *, proj_wr, proj_rd, wr, rd, tg):
    nc = tg // CHUNK
    nsys = nc * GDN_HEADS
    hcols = lambda h: slice(h * GDN_HEAD_DIM, (h + 1) * GDN_HEAD_DIM)
    og = og_ref[...]
    v = {}
    stage_a, stage_b, stage_c = [], [], []

    ext_w, z_w, ba_w = proj_wr
    o0 = D_S5
    o1 = o0 + 3 * D_GDN
    o2 = o1 + D_GDN

    def a_norm():
        v["n1"] = _rms(x_ref[...], g1_ref[...]).astype(BF16)
    stage_a.append(a_norm)

    def a_block(dst, rows, dst_c0, src_c0, width):
        def item():
            dst[rows, dst_c0:dst_c0 + width] = _dot(v["n1"], win_ref[:, src_c0:src_c0 + width])
        return item
    blk = 256
    for c0 in range(0, o0, blk):
        stage_a.append(a_block(u_ref, slice(None), c0, c0, blk))
    for c0 in range(0, o1 - o0, blk):
        stage_a.append(a_block(ext_w, slice(8, None), c0, o0 + c0, blk))
    for c0 in range(0, o2 - o1, blk):
        stage_a.append(a_block(z_w, slice(None), c0, o1 + c0, blk))

    def a_gates():
        ba_w[...] = _dot(v["n1"], wgate_ref[...])
    stage_a.append(a_gates)

    aq_r, b_r, o_r, gl_r, zs_r = rd
    states = [None] * GDN_HEADS

    def c_chunk(c):
        def item():
            rows = slice(c * CHUNK, (c + 1) * CHUNK)
            for h in range(GDN_HEADS):
                sys = c * GDN_HEADS + h
                st = s_s[h] if c == 0 else states[h]
                r = _dot(aq_r[sys], st.astype(BF16))
                o = r[GDN_HEAD_DIM:] + o_r[sys]
                states[h] = st * gl_r[sys, 0:1, :] + r[:GDN_HEAD_DIM] + b_r[sys]
                y_ref[rows, hcols(h)] = _rms(o, og) * zs_r[rows, hcols(h)]
                if c == nc - 1:
                    s_s[h] = states[h]
                    sfin_ref[0, h] = states[h]
        return item
    stage_c.extend(c_chunk(c) for c in range(nc))

    aq_s, b_s, o_s, gl_s, zs_s = wr
    ext_s, z_r, ba_r = proj_rd
    ii = lax.broadcasted_iota(jnp.int32, (CHUNK, CHUNK), 0)
    jj = lax.broadcasted_iota(jnp.int32, (CHUNK, CHUNK), 1)
    causal = jj <= ii
    strict = jj < ii
    scale = GDN_HEAD_DIM ** -0.5
    lmats = [None] * nsys
    rhss = [None] * nsys
    kgt_attn = [None] * nsys
    qgs = [None] * nsys

    def b_gates():
        raw = ba_r[...].T[:2 * GDN_HEADS, :]
        lanes = lambda p: jnp.concatenate([p] * (tg // GATE_PAD), axis=1)
        beta = jax.nn.sigmoid(raw)
        g = -jnp.exp(lanes(alog_ref[...])) * _softplus(raw + lanes(dtb_ref[...]))
        g1 = g.astype(BF16).astype(F32)
        g2 = (g - g1).astype(BF16).astype(F32)
        g3 = (g - g1 - g2).astype(BF16).astype(F32)
        ri = lax.broadcasted_iota(jnp.int32, (tg, tg), 0)
        ci = lax.broadcasted_iota(jnp.int32, (tg, tg), 1)
        ubd = jnp.where(((ri >> 6) == (ci >> 6)) & (ri <= ci), 1.0, 0.0).astype(BF16)
        parts = _dot(jnp.concatenate([g1, g2, g3], axis=0).astype(BF16), ubd)
        nh2 = 2 * GDN_HEADS
        gc = parts[:nh2] + parts[nh2:2 * nh2] + parts[2 * nh2:]
        row_id = lax.broadcasted_iota(jnp.int32, (nh2, tg), 0)
        rows = jnp.where(row_id < GDN_HEADS, beta, gc)
        cols = jnp.concatenate([rows, jnp.zeros((GATE_PAD - nh2, tg), F32)], axis=0).T
        v["beta_all"] = cols
        v["gc_col"] = cols
        v["gc_row"] = rows
    stage_b.append(b_gates)

    def b_silu_z(h):
        def item():
            zs_s[:, hcols(h)] = _silu(z_r[:, hcols(h)])
        return item

    def b_conv(h, part, name, norm):
        def item():
            c0 = part * D_GDN + h * GDN_HEAD_DIM
            cs = slice(c0, c0 + GDN_HEAD_DIM)
            ext = ext_s[:, cs]
            acc = ext[8:] * convw_ref[GDN_CONV - 1:GDN_CONV, cs]
            for j in range(GDN_CONV - 1):
                acc = acc + _rows_back(ext, GDN_CONV - 1 - j, tg) * convw_ref[j:j + 1, cs]
            acc = _silu(acc)
            v[name] = norm(acc)
        return item

    def b_prep(h, c):
        def item():
            beta_all, gc_col, gc_row = v["beta_all"], v["gc_col"], v["gc_row"]
            sys = c * GDN_HEADS + h
            rows = slice(c * CHUNK, (c + 1) * CHUNK)
            q = v["q"][rows]
            k = v["k"][rows]
            vv = v["v"][rows]
            beta = beta_all[rows, h:h + 1]
            gcc = gc_col[rows, GDN_HEADS + h:GDN_HEADS + h + 1]
            gcr = gc_row[GDN_HEADS + h:GDN_HEADS + h + 1, rows]
            decay = jnp.exp(jnp.where(causal, gcc - gcr, -jnp.inf))
            kb = k * beta
            egc = jnp.exp(gcc)
            g_last = gcc[CHUNK - 1:CHUNK, :]
            kq = _dot_nt(jnp.concatenate([kb, q], axis=0).astype(BF16), k.astype(BF16))
            lmats[sys] = jnp.where(strict, kq[:CHUNK] * decay, 0.0)
            attn = jnp.where(causal, kq[CHUNK:] * decay, 0.0)
            rhss[sys] = jnp.concatenate([vv * beta, kb * egc], axis=1)
            kgt_attn[sys] = jnp.concatenate([(k * jnp.exp(g_last - gcc)).T, attn], axis=0)
            qgs[sys] = q * egc
            gl_s[sys] = jnp.broadcast_to(jnp.exp(g_last), (8, GDN_HEAD_DIM))
        return item

    for h in range(GDN_HEADS):
        stage_b.append(b_silu_z(h))
        stage_b.append(b_conv(h, 0, "q", lambda a: _l2norm(a) * scale))
        stage_b.append(b_conv(h, 1, "k", _l2norm))
        stage_b.append(b_conv(h, 2, "v", lambda a: a))
        stage_b.extend(b_prep(h, c) for c in range(nc))

    def b_inverse_start():
        lmat = jnp.stack(lmats)
        v["lmat"] = lmat
        v["noff"] = -jnp.where(((ii >> 1) == (jj >> 1))[None], lmat, 0.0)
    stage_b.append(b_inverse_start)

    def b_inverse_level(shift):
        def item():
            v["noff"] = _tri_merge_level(v["noff"], v["lmat"], ii, jj, shift)
        return item
    shift = 1
    while (1 << shift) < CHUNK:
        stage_b.append(b_inverse_level(shift))
        shift += 1

    def b_solve():
        rhs = jnp.stack(rhss)
        sol = rhs + _bdot(v["noff"], rhs)
        v["prod"] = _bdot(jnp.stack(kgt_attn), sol)
    stage_b.append(b_solve)

    def b_store():
        prod = v["prod"]
        dk = GDN_HEAD_DIM
        for sys in range(nsys):
            aq_s[sys, :dk, :] = (-prod[sys, :dk, dk:]).astype(BF16)
            aq_s[sys, dk:, :] = (qgs[sys] - prod[sys, dk:, dk:]).astype(BF16)
            b_s[sys] = prod[sys, :dk, :dk]
            o_s[sys] = prod[sys, dk:, :dk]
    stage_b.append(b_store)

    for item in _interleave(stage_a, stage_b, stage_c):
        item()

    ext_w[0:8, :] = ext_s[pl.ds(tg, 8), :]
    convout_ref[0] = ext_s[pl.ds(tg + 8 - (GDN_CONV - 1), GDN_CONV - 1), :]


def _gdn_prompt(x, g1, w_in_bf, w_gate_bf, gw, bsz, seq, tg):
    nt = seq // tg
    ntiles = bsz * nt
    nsys = (tg // CHUNK) * GDN_HEADS
    tile_a = lambda s: jnp.minimum(s, ntiles - 1)
    tile_b = lambda s: jnp.clip(s - 1, 0, ntiles - 1)
    tile_c = lambda s: jnp.maximum(s - 2, 0)
    const = lambda s: (0, 0)
    proj_bufs = lambda: [pltpu.VMEM((tg + 8, 3 * D_GDN), F32),
                         pltpu.VMEM((tg, D_GDN), F32),
                         pltpu.VMEM((tg, GATE_PAD), F32)]
    resident = lambda shape: pl.BlockSpec(shape, const, pipeline_mode=pl.Buffered(1))
    handover = lambda: [pltpu.VMEM((nsys, GDN_HEAD_DIM + CHUNK, GDN_HEAD_DIM), BF16),
                        pltpu.VMEM((nsys, GDN_HEAD_DIM, GDN_HEAD_DIM), F32),
                        pltpu.VMEM((nsys, CHUNK, GDN_HEAD_DIM), F32),
                        pltpu.VMEM((nsys, 8, GDN_HEAD_DIM), F32),
                        pltpu.VMEM((tg, D_GDN), F32)]
    return pl.pallas_call(
        functools.partial(_gdn_kernel, tg=tg, nt=nt),
        grid=(ntiles + 2,),
        in_specs=[pl.BlockSpec((tg, D_MODEL), lambda s: (tile_a(s), 0)),
                  resident((1, D_MODEL)),
                  resident((D_MODEL, D_IN_MAIN)),
                  resident((D_MODEL, GATE_PAD)),
                  resident((GDN_CONV, 3 * D_GDN)),
                  resident((2 * GDN_HEADS, GATE_PAD)),
                  resident((2 * GDN_HEADS, GATE_PAD)),
                  resident((1, GDN_HEAD_DIM))],
        out_specs=[pl.BlockSpec((tg, D_S5), lambda s: (tile_a(s), 0)),
                   pl.BlockSpec((tg, D_GDN), lambda s: (tile_c(s), 0)),
                   pl.BlockSpec((1, GDN_HEADS, GDN_HEAD_DIM, GDN_HEAD_DIM), lambda s: (tile_c(s) // nt, 0, 0, 0)),
                   pl.BlockSpec((1, GDN_CONV - 1, 3 * D_GDN), lambda s: (tile_b(s) // nt, 0, 0))],
        out_shape=[jax.ShapeDtypeStruct((bsz * seq, D_S5), F32),
                   jax.ShapeDtypeStruct((bsz * seq, D_GDN), F32),
                   jax.ShapeDtypeStruct((bsz, GDN_HEADS, GDN_HEAD_DIM, GDN_HEAD_DIM), F32),
                   jax.ShapeDtypeStruct((bsz, GDN_CONV - 1, 3 * D_GDN), F32)],
        scratch_shapes=[pltpu.VMEM((GDN_HEADS, GDN_HEAD_DIM, GDN_HEAD_DIM), F32),
                        proj_bufs(), proj_bufs(), handover(), handover()],
        compiler_params=pltpu.CompilerParams(dimension_semantics=("arbitrary",),
                                             vmem_limit_bytes=VMEM_LIMIT),
        name="proj_gdn_prompt",
    )(x, g1, w_in_bf, w_gate_bf, gw["conv_w"], gw["a_log_col"], gw["dt_bias_col"], gw["onorm_g"])


def _gdn_step_kernel(qkv_ref, st_ref, z_ref, ba_ref, s0_ref, convw_ref, alog_ref, dtb_ref, og_ref,
                     y_ref, s1_ref, *, bb):
    beta_all, g_all = _gates(ba_ref[...], alog_ref[...], dtb_ref[...])
    alpha_all = jnp.exp(g_all)
    scale = GDN_HEAD_DIM ** -0.5
    og = og_ref[...]

    for h in range(GDN_HEADS):
        def conv_cols(part, h=h):
            c0 = part * D_GDN + h * GDN_HEAD_DIM
            cs = slice(c0, c0 + GDN_HEAD_DIM)
            acc = qkv_ref[:, cs] * convw_ref[GDN_CONV - 1:GDN_CONV, cs]
            for j in range(GDN_CONV - 1):
                acc = acc + st_ref[j, :, cs] * convw_ref[j:j + 1, cs]
            return _silu(acc)

        q = _l2norm(conv_cols(0)) * scale
        k = _l2norm(conv_cols(1))
        v = conv_cols(2)
        beta = beta_all[:, h:h + 1]
        alpha = alpha_all[:, GDN_HEADS + h:GDN_HEADS + h + 1]
        qk = jnp.sum(q * k, axis=-1, keepdims=True)
        kt = k.T
        kq = jnp.concatenate([k, q], axis=0).astype(BF16)
        o_rows = []
        for b in range(bb):
            s0 = s0_ref[b, h]
            kcol = kt[:, b:b + 1]
            r = _dot(kq, s0.astype(BF16))
            ks = r[b:b + 1, :]
            qs = r[bb + b:bb + b + 1, :]
            al = alpha[b:b + 1, :]
            v_new = beta[b:b + 1, :] * (v[b:b + 1, :] - al * ks)
            o_rows.append(al * qs + qk[b:b + 1, :] * v_new)
            s1_ref[b, h] = al * s0 + kcol * v_new
        o = jnp.concatenate(o_rows, axis=0)
        cs = slice(h * GDN_HEAD_DIM, (h + 1) * GDN_HEAD_DIM)
        y_ref[:, cs] = _out_gate(o, og, z_ref[:, cs])


def _gdn_sample(qkv, st_t, z, ba, s0, gw, bb):
    n = qkv.shape[0]
    row = lambda i: (i, 0)
    const = lambda i: (0, 0)
    state = lambda i: (i, 0, 0, 0)
    return pl.pallas_call(
        functools.partial(_gdn_step_kernel, bb=bb),
        grid=(n // bb,),
        in_specs=[pl.BlockSpec((bb, 3 * D_GDN), row),
                  pl.BlockSpec((GDN_CONV - 1, bb, 3 * D_GDN), lambda i: (0, i, 0)),
                  pl.BlockSpec((bb, D_GDN), row),
                  pl.BlockSpec((bb, GATE_PAD), row),
                  pl.BlockSpec((bb, GDN_HEADS, GDN_HEAD_DIM, GDN_HEAD_DIM), state),
                  pl.BlockSpec((GDN_CONV, 3 * D_GDN), const),
                  pl.BlockSpec((1, GATE_PAD), const),
                  pl.BlockSpec((1, GATE_PAD), const),
                  pl.BlockSpec((1, GDN_HEAD_DIM), const)],
        out_specs=[pl.BlockSpec((bb, D_GDN), row),
                   pl.BlockSpec((bb, GDN_HEADS, GDN_HEAD_DIM, GDN_HEAD_DIM), state)],
        out_shape=[jax.ShapeDtypeStruct((n, D_GDN), F32),
                   jax.ShapeDtypeStruct(s0.shape, F32)],
        compiler_params=pltpu.CompilerParams(dimension_semantics=("arbitrary",),
                                             vmem_limit_bytes=VMEM_LIMIT),
        name="gdn_sample",
    )(qkv, st_t, z, ba, s0, gw["conv_w"], gw["a_log"], gw["dt_bias"], gw["onorm_g"])


def _residual_in(x_ref, ys5_ref, ygdn_ref, wout_ref, g2_ref, x1_s, n2_s):
    x1 = (x_ref[...] + _dot(ys5_ref[...].astype(BF16), wout_ref[:D_S5, :])
          + _dot(ygdn_ref[...].astype(BF16), wout_ref[D_S5:, :]))
    x1_s[...] = x1
    n2_s[...] = _rms(x1, g2_ref[...]).astype(BF16)


def _ffn_prompt_kernel(x_ref, ys5_ref, ygdn_ref, wout_ref, g2_ref, wup_ref, cw_ref, wdn_ref, gf_ref,
                       y_ref, hlast_ref, x1_s, n2_s, act_s, carry_s, *, tm):
    ti = pl.program_id(1)
    nt = pl.num_programs(1)

    @pl.when(ti == 0)
    def _reset():
        carry_s[...] = jnp.zeros_like(carry_s)

    _residual_in(x_ref, ys5_ref, ygdn_ref, wout_ref, g2_ref, x1_s, n2_s)
    n2 = n2_s[...]
    for f in range(D_FF // FFN_COLS):
        halves = []
        for part in range(2):
            c0 = part * D_FF + f * FFN_COLS
            cs = slice(c0, c0 + FFN_COLS)
            hcur = _dot(n2, wup_ref[:, cs])
            ext = jnp.concatenate([carry_s[:, cs], hcur], axis=0)
            carry_s[:, cs] = hcur[tm - 8:, :]
            conv = (_rows_back(ext, 2, tm) * cw_ref[0:1, cs]
                    + _rows_back(ext, 1, tm) * cw_ref[1:2, cs]
                    + hcur * cw_ref[2:3, cs])
            halves.append(conv)
        act_s[:, f * FFN_COLS:(f + 1) * FFN_COLS] = (_silu(halves[0]) * halves[1]).astype(BF16)
    y_ref[...] = _rms(x1_s[...] + _dot(act_s[...], wdn_ref[...]), gf_ref[...])

    @pl.when(ti == nt - 1)
    def _fin():
        hlast_ref[0] = carry_s[8 - (FFN_CONV - 1):, :]


def _ffn_prompt(x, ys5, ygdn, fw, bsz, seq, tm):
    nt = seq // tm
    row = lambda b, i: (b * nt + i, 0)
    const = lambda b, i: (0, 0)
    resident = lambda shape: pl.BlockSpec(shape, const, pipeline_mode=pl.Buffered(1))
    return pl.pallas_call(
        functools.partial(_ffn_prompt_kernel, tm=tm),
        grid=(bsz, nt),
        in_specs=[pl.BlockSpec((tm, D_MODEL), row),
                  pl.BlockSpec((tm, D_S5), row),
                  pl.BlockSpec((tm, D_GDN), row),
                  resident((D_MODEL, D_MODEL)),
                  resident((1, D_MODEL)),
                  resident((D_MODEL, 2 * D_FF)),
                  resident((FFN_CONV, 2 * D_FF)),
                  resident((D_FF, D_MODEL)),
                  resident((1, D_MODEL))],
        out_specs=[pl.BlockSpec((tm, D_MODEL), row),
                   pl.BlockSpec((1, FFN_CONV - 1, 2 * D_FF), lambda b, i: (b, 0, 0))],
        out_shape=[jax.ShapeDtypeStruct((bsz * seq, D_MODEL), F32),
                   jax.ShapeDtypeStruct((bsz, FFN_CONV - 1, 2 * D_FF), F32)],
        scratch_shapes=[pltpu.VMEM((tm, D_MODEL), F32),
                        pltpu.VMEM((tm, D_MODEL), BF16),
                        pltpu.VMEM((tm, D_FF), BF16),
                        pltpu.VMEM((8, 2 * D_FF), F32)],
        compiler_params=pltpu.CompilerParams(dimension_semantics=("arbitrary", "arbitrary"),
                                             vmem_limit_bytes=VMEM_LIMIT),
        name="ffn_prompt",
    )(x, ys5, ygdn, fw["w_out"], fw["norm2_g"], fw["w_up"], fw["conv_w"], fw["w_down"], fw["normf_g"])


def _ffn_sample_kernel(x_ref, ys5_ref, ygdn_ref, st_ref, wout_ref, g2_ref, wup_ref, cw_ref, wdn_ref, gf_ref,
                       y_ref, h_ref, x1_s, n2_s, acc_s):
    _residual_in(x_ref, ys5_ref, ygdn_ref, wout_ref, g2_ref, x1_s, n2_s)
    n2 = n2_s[...]
    for f in range(D_FF // FFN_COLS):
        halves = []
        for part in range(2):
            c0 = part * D_FF + f * FFN_COLS
            cs = slice(c0, c0 + FFN_COLS)
            hcur = _dot(n2, wup_ref[:, cs])
            h_ref[:, cs] = hcur
            halves.append(st_ref[0, :, cs] * cw_ref[0:1, cs] + st_ref[1, :, cs] * cw_ref[1:2, cs]
                          + hcur * cw_ref[2:3, cs])
        act = (_silu(halves[0]) * halves[1]).astype(BF16)
        contrib = _dot(act, wdn_ref[f * FFN_COLS:(f + 1) * FFN_COLS, :])
        if f == 0:
            acc_s[...] = contrib
        else:
            acc_s[...] += contrib
    y_ref[...] = _rms(x1_s[...] + acc_s[...], gf_ref[...])


def _ffn_sample(x, ys5, ygdn, st_t, fw):
    n = x.shape[0]
    c2 = lambda i: (0, 0)
    c3 = lambda i: (0, 0, 0)
    return pl.pallas_call(
        _ffn_sample_kernel,
        grid=(1,),
        in_specs=[pl.BlockSpec((n, D_MODEL), c2),
                  pl.BlockSpec((n, D_S5), c2),
                  pl.BlockSpec((n, D_GDN), c2),
                  pl.BlockSpec((FFN_CONV - 1, n, 2 * D_FF), c3),
                  pl.BlockSpec((D_MODEL, D_MODEL), c2),
                  pl.BlockSpec((1, D_MODEL), c2),
                  pl.BlockSpec((D_MODEL, 2 * D_FF), c2),
                  pl.BlockSpec((FFN_CONV, 2 * D_FF), c2),
                  pl.BlockSpec((D_FF, D_MODEL), c2),
                  pl.BlockSpec((1, D_MODEL), c2)],
        out_specs=[pl.BlockSpec((n, D_MODEL), c2),
                   pl.BlockSpec((n, 2 * D_FF), c2)],
        out_shape=[jax.ShapeDtypeStruct((n, D_MODEL), F32),
                   jax.ShapeDtypeStruct((n, 2 * D_FF), F32)],
        scratch_shapes=[pltpu.VMEM((n, D_MODEL), F32),
                        pltpu.VMEM((n, D_MODEL), BF16),
                        pltpu.VMEM((n, D_MODEL), F32)],
        compiler_params=pltpu.CompilerParams(dimension_semantics=("arbitrary",),
                                             vmem_limit_bytes=VMEM_LIMIT),
        name="ffn_sample",
    )(x, ys5, ygdn, st_t, fw["w_out"], fw["norm2_g"], fw["w_up"], fw["conv_w"], fw["w_down"], fw["normf_g"])


PROJ_ROWS = 512
S5_STEPS = 128
GDN_ROWS = 256
FFN_ROWS = 512
GDN_SAMPLE_ROWS = 8


def kernel(x_prompt, x_sample, state_s5_re, state_s5_im, state_gdn, state_gdn_conv, state_ffn_conv, norm1_g, w_in, s5_a_re, s5_a_im, s5_log_dt, s5_b_re, s5_b_im, s5_c_re, s5_c_im, s5_d, s5_w_glu, gdn_conv_w, gdn_a_log, gdn_dt_bias, gdn_onorm_g, w_out, norm2_g, ffn_w_up, ffn_conv_w, ffn_w_down, normf_g):
    depth = w_in.shape[0]
    assert depth == 1, "the final rmsnorm is fused into the last layer's ffn kernel"
    bsz, seq, _ = x_prompt.shape
    nsmp = x_sample.shape[0]
    assert x_sample.shape[1] == 1
    l = 0

    xp = x_prompt.reshape(bsz * seq, D_MODEL)
    xs = x_sample.reshape(nsmp, D_MODEL)

    g1 = norm1_g[l].reshape(1, D_MODEL)
    w_in_bf = w_in[l][:, :D_IN_MAIN].astype(BF16)
    w_gate_bf = jnp.pad(w_in[l][:, D_IN_MAIN:], ((0, 0), (0, GATE_PAD - 2 * GDN_HEADS))).astype(BF16)
    sw = _s5_weights(s5_a_re[l], s5_a_im[l], s5_log_dt[l], s5_b_re[l], s5_b_im[l], s5_c_re[l], s5_c_im[l],
                     s5_d[l], s5_w_glu[l])
    gate_row = lambda v: jnp.pad(v, (GDN_HEADS, GATE_PAD - 2 * GDN_HEADS)).reshape(1, GATE_PAD)
    gate_col = lambda v: jnp.broadcast_to(jnp.pad(v, (GDN_HEADS, 0))[:, None], (2 * GDN_HEADS, GATE_PAD))
    gw = {"conv_w": gdn_conv_w[l], "a_log": gate_row(gdn_a_log[l]), "dt_bias": gate_row(gdn_dt_bias[l]),
          "a_log_col": gate_col(gdn_a_log[l]), "dt_bias_col": gate_col(gdn_dt_bias[l]),
          "onorm_g": gdn_onorm_g[l].reshape(1, GDN_HEAD_DIM)}
    fw = {"w_out": w_out[l].astype(BF16), "norm2_g": norm2_g[l].reshape(1, D_MODEL),
          "w_up": ffn_w_up[l].astype(BF16), "conv_w": ffn_conv_w[l], "w_down": ffn_w_down[l].astype(BF16),
          "normf_g": normf_g.reshape(1, D_MODEL)}

    u_p, ygdn_p, p_gdn, p_gdn_conv = _gdn_prompt(xp, g1, w_in_bf, w_gate_bf, gw, bsz, seq, GDN_ROWS)
    zeros_h = jnp.zeros((bsz, S5_CH), F32)
    ys5_p, p_hr, p_hi = _s5(u_p.reshape(bsz, seq, D_S5), zeros_h, zeros_h, sw, bsz, S5_STEPS, True)
    ys5_p = ys5_p.reshape(bsz * seq, D_S5)
    y_p, p_ffn_conv = _ffn_prompt(xp, ys5_p, ygdn_p, fw, bsz, seq, FFN_ROWS)

    u_s, qkv_s, z_s, ba_s = _proj(xs, g1, w_in_bf, w_gate_bf, nsmp)
    ys5_s, s_hr, s_hi = _s5(u_s, state_s5_re[l].reshape(nsmp, S5_CH), state_s5_im[l].reshape(nsmp, S5_CH),
                            sw, nsmp, 1, False)
    gconv_t = state_gdn_conv[l].transpose(1, 0, 2)
    ygdn_s, s_gdn = _gdn_sample(qkv_s, gconv_t, z_s, ba_s, state_gdn[l], gw, GDN_SAMPLE_ROWS)
    fconv_t = state_ffn_conv[l].transpose(1, 0, 2)
    y_s, h_s = _ffn_sample(xs, ys5_s, ygdn_s, fconv_t, fw)
    s_gdn_conv = jnp.concatenate([state_gdn_conv[l][:, 1:], qkv_s[:, None, :]], axis=1)
    s_ffn_conv = jnp.concatenate([state_ffn_conv[l][:, 1:], h_s[:, None, :]], axis=1)

    st = lambda a: a[None]
    s5_shape = lambda a, n: a.reshape(1, n, S5_N_GROUPS, S5_STATE)
    return (y_p.reshape(bsz, seq, D_MODEL), y_s.reshape(nsmp, 1, D_MODEL),
            s5_shape(p_hr, bsz), s5_shape(p_hi, bsz), st(p_gdn), st(p_gdn_conv), st(p_ffn_conv),
            s5_shape(s_hr, nsmp), s5_shape(s_hi, nsmp), st(s_gdn), st(s_gdn_conv), st(s_ffn_conv))
```

```python
import functools

import jax
import jax.numpy as jnp
from jax import lax
from jax.experimental import pallas as pl
from jax.experimental.pallas import tpu as pltpu

F32 = jnp.float32
BF16 = jnp.bfloat16
NORM_EPS = 1e-6

D_MODEL = 1024
D_S5 = 512
S5_GROUP = 16
S5_N_GROUPS = 32
S5_STATE = 64
S5_CH = S5_N_GROUPS * S5_STATE
D_GDN = 512
GDN_HEADS = 4
GDN_HEAD_DIM = 128
GDN_CONV = 4
D_FF = 2816
FFN_CONV = 3
D_IN = D_S5 + 4 * D_GDN + 2 * GDN_HEADS
GATE_PAD = 128
D_IN_MAIN = D_S5 + 4 * D_GDN

CHUNK = 64
FFN_COLS = 256
VMEM_LIMIT = 56 * 1024 * 1024


def _dot(a, b):
    return jnp.dot(a, b, preferred_element_type=F32)


def _dot_nt(a, b):
    return lax.dot_general(a, b, (((1,), (1,)), ((), ())), preferred_element_type=F32)


def _rms(x, g):
    ms = jnp.mean(x * x, axis=-1, keepdims=True)
    return x * lax.rsqrt(ms + NORM_EPS) * g


def _silu(x):
    return x * jax.nn.sigmoid(x)


def _rows_back(ext, back, n):
    return pltpu.roll(ext, back, 0)[8:8 + n]


def _softplus(x):
    return jnp.maximum(x, 0.0) + jnp.log1p(jnp.exp(-jnp.abs(x)))


def _proj_kernel(x_ref, g_ref, w_ref, wgate_ref, u_ref, qkv_ref, z_ref, ba_ref):
    n = _rms(x_ref[...], g_ref[...]).astype(BF16)
    o0 = D_S5
    o1 = o0 + 3 * D_GDN
    u_ref[...] = _dot(n, w_ref[:, :o0])
    qkv_ref[...] = _dot(n, w_ref[:, o0:o1])
    z_ref[...] = _dot(n, w_ref[:, o1:])
    ba_ref[...] = _dot(n, wgate_ref[...])


def _proj(x, g1, w_in_bf, w_gate_bf, tm):
    n = x.shape[0]
    row = lambda i: (i, 0)
    const = lambda i: (0, 0)
    return pl.pallas_call(
        _proj_kernel,
        grid=(n // tm,),
        in_specs=[pl.BlockSpec((tm, D_MODEL), row),
                  pl.BlockSpec((1, D_MODEL), const),
                  pl.BlockSpec((D_MODEL, D_IN_MAIN), const),
                  pl.BlockSpec((D_MODEL, GATE_PAD), const)],
        out_specs=[pl.BlockSpec((tm, D_S5), row),
                   pl.BlockSpec((tm, 3 * D_GDN), row),
                   pl.BlockSpec((tm, D_GDN), row),
                   pl.BlockSpec((tm, GATE_PAD), row)],
        out_shape=[jax.ShapeDtypeStruct((n, D_S5), F32),
                   jax.ShapeDtypeStruct((n, 3 * D_GDN), F32),
                   jax.ShapeDtypeStruct((n, D_GDN), F32),
                   jax.ShapeDtypeStruct((n, GATE_PAD), F32)],
        compiler_params=pltpu.CompilerParams(dimension_semantics=("arbitrary",),
                                             vmem_limit_bytes=VMEM_LIMIT),
        name="proj",
    )(x, g1, w_in_bf, w_gate_bf)


S5_HALF = S5_CH // 2
S5_COLS = 1024
S5_SCAN_GROUP = 8
S5_PARTS = 4


def _s5_kernel(u_ref, h0r_ref, h0i_ref, are_ref, aim_ref, ldt_ref, bre_ref, bim_ref, cre_ref, cim_ref,
               d_ref, wglu_ref, y_ref, hr_out, hi_out,
               abr_s, abi_s, wbr_s, wbi_s, hr_s, hi_s, xr_s, xi_s, y_s, *, nb, tt, batch_major):
    @pl.when(pl.program_id(0) == 0)
    def _init():
        ar = are_ref[...]
        ai = aim_ref[...]
        dt = jnp.exp(ldt_ref[...])
        mag = jnp.exp(ar * dt)
        abr = mag * jnp.cos(ai * dt)
        abi = mag * jnp.sin(ai * dt)
        den = ar * ar + ai * ai
        p = abr - 1.0
        fr = (p * ar + abi * ai) / den
        fi = (abi * ar - p * ai) / den
        abr_s[...] = abr
        abi_s[...] = abi
        for m in range(2):
            frm = fr[:, m * S5_HALF:(m + 1) * S5_HALF]
            fim = fi[:, m * S5_HALF:(m + 1) * S5_HALF]
            wr = bre_ref[m]
            wi = bim_ref[m]
            wbr_s[m] = (wr * frm - wi * fim).astype(BF16)
            wbi_s[m] = (wr * fim + wi * frm).astype(BF16)
        hr_s[...] = h0r_ref[...]
        hi_s[...] = h0i_ref[...]

    if batch_major:
        u = jnp.swapaxes(u_ref[...], 0, 1).reshape(tt * nb, D_S5)
    else:
        u = u_ref[...]
    ub = u.astype(BF16)
    half = D_S5 // 2

    nparts = S5_PARTS if tt % (S5_PARTS * S5_SCAN_GROUP) == 0 else 1
    tp = tt // nparts
    part_rows = lambda p: slice(p * tp * nb, (p + 1) * tp * nb)

    blk = 256

    def in_proj(p):
        def block(m, ws, xs, c0):
            def item():
                xs[part_rows(p), m * S5_HALF + c0:m * S5_HALF + c0 + blk] = _dot(
                    ub[part_rows(p), m * half:(m + 1) * half], ws[m, :, c0:c0 + blk])
            return item
        return [block(m, ws, xs, c0) for m in range(2) for ws, xs in ((wbr_s, xr_s), (wbi_s, xi_s))
                for c0 in range(0, S5_HALF, blk)]

    def scan(p):
        def group(cb, t0, steps):
            def item():
                cols = slice(cb * S5_COLS, (cb + 1) * S5_COLS)
                a_r = jnp.broadcast_to(abr_s[:, cols], (nb, S5_COLS))
                a_i = jnp.broadcast_to(abi_s[:, cols], (nb, S5_COLS))
                hr = hr_s[:, cols]
                hi = hi_s[:, cols]
                for t in range(t0, t0 + steps):
                    rows = slice(t * nb, (t + 1) * nb)
                    hr, hi = (a_r * hr - a_i * hi + xr_s[rows, cols],
                              a_r * hi + a_i * hr + xi_s[rows, cols])
                    xr_s[rows, cols] = hr
                    xi_s[rows, cols] = hi
                hr_s[:, cols] = hr
                hi_s[:, cols] = hi
            return item
        steps = min(tp, S5_SCAN_GROUP)
        return [group(cb, t0, steps) for t0 in range(p * tp, (p + 1) * tp, steps)
                for cb in range(S5_CH // S5_COLS)]

    def out_proj(p):
        ys = {}

        def re_block(m):
            def item():
                hre = xr_s[part_rows(p), m * S5_HALF:(m + 1) * S5_HALF].astype(BF16)
                ys[m] = (_dot(hre, cre_ref[m])
                         + d_ref[:, m * half:(m + 1) * half] * u[part_rows(p), m * half:(m + 1) * half])
            return item

        def im_block(m):
            def item():
                him = xi_s[part_rows(p), m * S5_HALF:(m + 1) * S5_HALF].astype(BF16)
                ys[m] = ys[m] - _dot(him, cim_ref[m])
            return item

        def act():
            ys["act"] = jax.nn.gelu(jnp.concatenate([ys[0], ys[1]], axis=-1)).astype(BF16)

        def glu(c0):
            def item():
                val = _dot(ys["act"], wglu_ref[:, c0:c0 + blk])
                gate = _dot(ys["act"], wglu_ref[:, D_S5 + c0:D_S5 + c0 + blk])
                y_s[part_rows(p), c0:c0 + blk] = val * jax.nn.sigmoid(gate)
            return item
        return ([re_block(0), im_block(0), re_block(1), im_block(1), act]
                + [glu(c0) for c0 in range(0, D_S5, blk)])

    order = in_proj(0)
    for p in range(nparts):
        nxt = in_proj(p + 1) if p + 1 < nparts else []
        prv = out_proj(p - 1) if p > 0 else []
        order += _interleave(nxt + prv, scan(p)) if (nxt or prv) else scan(p)
    order += out_proj(nparts - 1)
    for item in order:
        item()

    hr_out[...] = hr_s[...]
    hi_out[...] = hi_s[...]
    if batch_major:
        y_ref[...] = jnp.swapaxes(y_s[...].reshape(tt, nb, D_S5), 0, 1)
    else:
        y_ref[...] = y_s[...]


def _s5(u, h0r, h0i, sw, nb, tt, batch_major):
    rows = nb * tt
    const2 = lambda i: (0, 0)
    const3 = lambda i: (0, 0, 0)
    full2 = lambda shape: pl.BlockSpec(shape, const2)
    full3 = lambda shape: pl.BlockSpec(shape, const3)
    if batch_major:
        steps = u.shape[1] // tt
        io_spec = pl.BlockSpec((nb, tt, D_S5), lambda i: (0, i, 0))
    else:
        steps = u.shape[0] // rows
        io_spec = pl.BlockSpec((rows, D_S5), lambda i: (i, 0))
    return pl.pallas_call(
        functools.partial(_s5_kernel, nb=nb, tt=tt, batch_major=batch_major),
        grid=(steps,),
        in_specs=[io_spec,
                  full2((nb, S5_CH)), full2((nb, S5_CH)),
                  full2((1, S5_CH)), full2((1, S5_CH)), full2((1, S5_CH)),
                  full3((2, D_S5 // 2, S5_HALF)), full3((2, D_S5 // 2, S5_HALF)),
                  full3((2, S5_HALF, D_S5 // 2)), full3((2, S5_HALF, D_S5 // 2)),
                  full2((1, D_S5)), full2((D_S5, 2 * D_S5))],
        out_specs=[io_spec, full2((nb, S5_CH)), full2((nb, S5_CH))],
        out_shape=[jax.ShapeDtypeStruct(u.shape, F32),
                   jax.ShapeDtypeStruct((nb, S5_CH), F32),
                   jax.ShapeDtypeStruct((nb, S5_CH), F32)],
        scratch_shapes=[pltpu.VMEM((1, S5_CH), F32), pltpu.VMEM((1, S5_CH), F32),
                        pltpu.VMEM((2, D_S5 // 2, S5_HALF), BF16), pltpu.VMEM((2, D_S5 // 2, S5_HALF), BF16),
                        pltpu.VMEM((nb, S5_CH), F32), pltpu.VMEM((nb, S5_CH), F32),
                        pltpu.VMEM((rows, S5_CH), F32), pltpu.VMEM((rows, S5_CH), F32),
                        pltpu.VMEM((rows, D_S5), F32)],
        compiler_params=pltpu.CompilerParams(dimension_semantics=("arbitrary",),
                                             vmem_limit_bytes=VMEM_LIMIT),
        name="s5",
    )(u, h0r, h0i, sw["a_re"], sw["a_im"], sw["log_dt"], sw["b_re"], sw["b_im"],
      sw["c_re"], sw["c_im"], sw["d"], sw["w_glu"])


def _s5_weights(a_re, a_im, log_dt, b_re, b_im, c_re, c_im, d, w_glu):
    g, n, c = S5_N_GROUPS, S5_STATE, S5_GROUP
    gh = g // 2

    def blockdiag(w, rows_per_group, cols_per_group):
        x = jnp.transpose(w, (0, 2, 1)).reshape(2, gh * rows_per_group, cols_per_group)
        rg = jnp.arange(gh * rows_per_group)[:, None] // rows_per_group
        cg = jnp.arange(gh * cols_per_group)[None, :] // cols_per_group
        return jnp.where(rg == cg, jnp.tile(x, (1, 1, gh)), 0.0)

    b_blockdiag = lambda b: blockdiag(b, c, n)
    c_blockdiag = lambda cm: blockdiag(cm, n, c)

    return {
        "a_re": a_re.reshape(1, S5_CH), "a_im": a_im.reshape(1, S5_CH),
        "log_dt": jnp.repeat(log_dt, n).reshape(1, S5_CH),
        "b_re": b_blockdiag(b_re), "b_im": b_blockdiag(b_im),
        "c_re": c_blockdiag(c_re).astype(BF16), "c_im": c_blockdiag(c_im).astype(BF16),
        "d": d.reshape(1, D_S5), "w_glu": w_glu.astype(BF16),
    }


def _l2norm(x):
    return x * lax.rsqrt(jnp.sum(x * x, axis=-1, keepdims=True) + NORM_EPS)


def _gates(ba, alog_row, dtb_row):
    beta = jax.nn.sigmoid(ba)
    g = -jnp.exp(alog_row) * _softplus(ba + dtb_row)
    return beta, g


def _out_gate(o, og_row, z):
    return _rms(o, og_row) * _silu(z)


def _bdot(a, b):
    return jnp.einsum("bij,bjk->bik", a.astype(BF16), b.astype(BF16), preferred_element_type=F32)


def _tri_merge_level(n, lmat, ii, jj, shift):
    same_big = (ii >> (shift + 1)) == (jj >> (shift + 1))
    same_small = (ii >> shift) == (jj >> shift)
    c = jnp.where((same_big & jnp.logical_not(same_small))[None], lmat, 0.0)
    w = c + _bdot(n, c)
    return n - (w + _bdot(w, n))


def _interleave(*stages):
    out = []
    pos = [0] * len(stages)
    while any(p < len(st) for p, st in zip(pos, stages)):
        k = min((i for i, st in enumerate(stages) if pos[i] < len(st)),
                key=lambda i: (pos[i] + 0.5) / len(stages[i]))
        out.append(stages[k][pos[k]])
        pos[k] += 1
    return out


def _gdn_kernel(x_ref, g1_ref, win_ref, wgate_ref, convw_ref, alog_ref, dtb_ref, og_ref,
                u_ref, y_ref, sfin_ref, convout_ref, s_s, proj_a, proj_b, set_a, set_b, *, tg, nt):
    s = pl.program_id(0)

    @pl.when(s == 0)
    def _zero_buffers():
        for ref in (*proj_b, *set_a):
            ref[...] = jnp.zeros_like(ref)

    @pl.when(lax.rem(jnp.maximum(s - 1, 0), nt) == 0)
    def _new_sequence_conv():
        for ext in (proj_a[0], proj_b[0]):
            ext[0:8, :] = jnp.zeros((8, 3 * D_GDN), F32)

    @pl.when(lax.rem(jnp.maximum(s - 2, 0), nt) == 0)
    def _new_sequence_state():
        s_s[...] = jnp.zeros_like(s_s)

    args = (x_ref, g1_ref, win_ref, wgate_ref, convw_ref, alog_ref, dtb_ref, og_ref, u_ref, y_ref, sfin_ref, convout_ref, s_s)

    @pl.when(lax.rem(s, 2) == 0)
    def _even():
        _gdn_step(*args, proj_wr=proj_a, proj_rd=proj_b, wr=set_b, rd=set_a, tg=tg)

    @pl.when(lax.rem(s, 2) == 1)
    def _odd():
        _gdn_step(*args, proj_wr=proj_b, proj_rd=proj_a, wr=set_a, rd=set_b, tg=tg)


def _gdn_step(x_ref, g1_ref, win_ref, wgate_ref, convw_ref, alog_ref, dtb_ref, og_ref, u_ref, y_ref, sfin_ref, convout_ref,
              s_s, *, proj_wr, proj_rd, wr, rd, tg):
    nc = tg // CHUNK
    nsys = nc * GDN_HEADS
    hcols = lambda h: slice(h * GDN_HEAD_DIM, (h + 1) * GDN_HEAD_DIM)
    og = og_ref[...]
    v = {}
    stage_a, stage_b, stage_c = [], [], []

    ext_w, z_w, ba_w = proj_wr
    o0 = D_S5
    o1 = o0 + 3 * D_GDN
    o2 = o1 + D_GDN

    def a_norm():
        v["n1"] = _rms(x_ref[...], g1_ref[...]).astype(BF16)
    stage_a.append(a_norm)

    def a_block(dst, rows, dst_c0, src_c0, width):
        def item():
            dst[rows, dst_c0:dst_c0 + width] = _dot(v["n1"], win_ref[:, src_c0:src_c0 + width])
        return item
    blk = 256
    for c0 in range(0, o0, blk):
        stage_a.append(a_block(u_ref, slice(None), c0, c0, blk))
    for c0 in range(0, o1 - o0, blk):
        stage_a.append(a_block(ext_w, slice(8, None), c0, o0 + c0, blk))
    for c0 in range(0, o2 - o1, blk):
        stage_a.append(a_block(z_w, slice(None), c0, o1 + c0, blk))

    def a_gates():
        ba_w[...] = _dot(v["n1"], wgate_ref[...])
    stage_a.append(a_gates)

    aq_r, b_r, o_r, gl_r, zs_r = rd
    states = [None] * GDN_HEADS

    def c_chunk(c):
        def item():
            rows = slice(c * CHUNK, (c + 1) * CHUNK)
            for h in range(GDN_HEADS):
                sys = c * GDN_HEADS + h
                st = s_s[h] if c == 0 else states[h]
                r = _dot(aq_r[sys], st.astype(BF16))
                o = r[GDN_HEAD_DIM:] + o_r[sys]
                states[h] = st * gl_r[sys, 0:1, :] + r[:GDN_HEAD_DIM] + b_r[sys]
                y_ref[rows, hcols(h)] = _rms(o, og) * zs_r[rows, hcols(h)]
                if c == nc - 1:
                    s_s[h] = states[h]
                    sfin_ref[0, h] = states[h]
        return item
    stage_c.extend(c_chunk(c) for c in range(nc))

    aq_s, b_s, o_s, gl_s, zs_s = wr
    ext_s, z_r, ba_r = proj_rd
    ii = lax.broadcasted_iota(jnp.int32, (CHUNK, CHUNK), 0)
    jj = lax.broadcasted_iota(jnp.int32, (CHUNK, CHUNK), 1)
    causal = jj <= ii
    strict = jj < ii
    scale = GDN_HEAD_DIM ** -0.5
    lmats = [None] * nsys
    rhss = [None] * nsys
    kgt_attn = [None] * nsys
    qgs = [None] * nsys

    def b_gates():
        raw = ba_r[...].T[:2 * GDN_HEADS, :]
        lanes = lambda p: jnp.concatenate([p] * (tg // GATE_PAD), axis=1)
        beta = jax.nn.sigmoid(raw)
        g = -jnp.exp(lanes(alog_ref[...])) * _softplus(raw + lanes(dtb_ref[...]))
        g1 = g.astype(BF16).astype(F32)
        g2 = (g - g1).astype(BF16).astype(F32)
        g3 = (g - g1 - g2).astype(BF16).astype(F32)
        ri = lax.broadcasted_iota(jnp.int32, (tg, tg), 0)
        ci = lax.broadcasted_iota(jnp.int32, (tg, tg), 1)
        ubd = jnp.where(((ri >> 6) == (ci >> 6)) & (ri <= ci), 1.0, 0.0).astype(BF16)
        parts = _dot(jnp.concatenate([g1, g2, g3], axis=0).astype(BF16), ubd)
        nh2 = 2 * GDN_HEADS
        gc = parts[:nh2] + parts[nh2:2 * nh2] + parts[2 * nh2:]
        row_id = lax.broadcasted_iota(jnp.int32, (nh2, tg), 0)
        rows = jnp.where(row_id < GDN_HEADS, beta, gc)
        cols = jnp.concatenate([rows, jnp.zeros((GATE_PAD - nh2, tg), F32)], axis=0).T
        v["beta_all"] = cols
        v["gc_col"] = cols
        v["gc_row"] = rows
    stage_b.append(b_gates)

    def b_silu_z(h):
        def item():
            zs_s[:, hcols(h)] = _silu(z_r[:, hcols(h)])
        return item

    def b_conv(h, part, name, norm):
        def item():
            c0 = part * D_GDN + h * GDN_HEAD_DIM
            cs = slice(c0, c0 + GDN_HEAD_DIM)
            ext = ext_s[:, cs]
            acc = ext[8:] * convw_ref[GDN_CONV - 1:GDN_CONV, cs]
            for j in range(GDN_CONV - 1):
                acc = acc + _rows_back(ext, GDN_CONV - 1 - j, tg) * convw_ref[j:j + 1, cs]
            acc = _silu(acc)
            v[name] = norm(acc)
        return item

    def b_prep(h, c):
        def item():
            beta_all, gc_col, gc_row = v["beta_all"], v["gc_col"], v["gc_row"]
            sys = c * GDN_HEADS + h
            rows = slice(c * CHUNK, (c + 1) * CHUNK)
            q = v["q"][rows]
            k = v["k"][rows]
            vv = v["v"][rows]
            beta = beta_all[rows, h:h + 1]
            gcc = gc_col[rows, GDN_HEADS + h:GDN_HEADS + h + 1]
            gcr = gc_row[GDN_HEADS + h:GDN_HEADS + h + 1, rows]
            decay = jnp.exp(jnp.where(causal, gcc - gcr, -jnp.inf))
            kb = k * beta
            egc = jnp.exp(gcc)
            g_last = gcc[CHUNK - 1:CHUNK, :]
            kq = _dot_nt(jnp.concatenate([kb, q], axis=0).astype(BF16), k.astype(BF16))
            lmats[sys] = jnp.where(strict, kq[:CHUNK] * decay, 0.0)
            attn = jnp.where(causal, kq[CHUNK:] * decay, 0.0)
            rhss[sys] = jnp.concatenate([vv * beta, kb * egc], axis=1)
            kgt_attn[sys] = jnp.concatenate([(k * jnp.exp(g_last - gcc)).T, attn], axis=0)
            qgs[sys] = q * egc
            gl_s[sys] = jnp.broadcast_to(jnp.exp(g_last), (8, GDN_HEAD_DIM))
        return item

    for h in range(GDN_HEADS):
        stage_b.append(b_silu_z(h))
        stage_b.append(b_conv(h, 0, "q", lambda a: _l2norm(a) * scale))
        stage_b.append(b_conv(h, 1, "k", _l2norm))
        stage_b.append(b_conv(h, 2, "v", lambda a: a))
        stage_b.extend(b_prep(h, c) for c in range(nc))

    def b_inverse_start():
        lmat = jnp.stack(lmats)
        v["lmat"] = lmat
        v["noff"] = -jnp.where(((ii >> 1) == (jj >> 1))[None], lmat, 0.0)
    stage_b.append(b_inverse_start)

    def b_inverse_level(shift):
        def item():
            v["noff"] = _tri_merge_level(v["noff"], v["lmat"], ii, jj, shift)
        return item
    shift = 1
    while (1 << shift) < CHUNK:
        stage_b.append(b_inverse_level(shift))
        shift += 1

    def b_solve():
        rhs = jnp.stack(rhss)
        sol = rhs + _bdot(v["noff"], rhs)
        v["prod"] = _bdot(jnp.stack(kgt_attn), sol)
    stage_b.append(b_solve)

    def b_store():
        prod = v["prod"]
        dk = GDN_HEAD_DIM
        for sys in range(nsys):
            aq_s[sys, :dk, :] = (-prod[sys, :dk, dk:]).astype(BF16)
            aq_s[sys, dk:, :] = (qgs[sys] - prod[sys, dk:, dk:]).astype(BF16)
            b_s[sys] = prod[sys, :dk, :dk]
            o_s[sys] = prod[sys, dk:, :dk]
    stage_b.append(b_store)

    for item in _interleave(stage_a, stage_b, stage_c):
        item()

    ext_w[0:8, :] = ext_s[pl.ds(tg, 8), :]
    convout_ref[0] = ext_s[pl.ds(tg + 8 - (GDN_CONV - 1), GDN_CONV - 1), :]


def _gdn_prompt(x, g1, w_in_bf, w_gate_bf, gw, bsz, seq, tg):
    nt = seq // tg
    ntiles = bsz * nt
    nsys = (tg // CHUNK) * GDN_HEADS
    tile_a = lambda s: jnp.minimum(s, ntiles - 1)
    tile_b = lambda s: jnp.clip(s - 1, 0, ntiles - 1)
    tile_c = lambda s: jnp.maximum(s - 2, 0)
    const = lambda s: (0, 0)
    proj_bufs = lambda: [pltpu.VMEM((tg + 8, 3 * D_GDN), F32),
                         pltpu.VMEM((tg, D_GDN), F32),
                         pltpu.VMEM((tg, GATE_PAD), F32)]
    resident = lambda shape: pl.BlockSpec(shape, const, pipeline_mode=pl.Buffered(1))
    handover = lambda: [pltpu.VMEM((nsys, GDN_HEAD_DIM + CHUNK, GDN_HEAD_DIM), BF16),
                        pltpu.VMEM((nsys, GDN_HEAD_DIM, GDN_HEAD_DIM), F32),
                        pltpu.VMEM((nsys, CHUNK, GDN_HEAD_DIM), F32),
                        pltpu.VMEM((nsys, 8, GDN_HEAD_DIM), F32),
                        pltpu.VMEM((tg, D_GDN), F32)]
    return pl.pallas_call(
        functools.partial(_gdn_kernel, tg=tg, nt=nt),
        grid=(ntiles + 2,),
        in_specs=[pl.BlockSpec((tg, D_MODEL), lambda s: (tile_a(s), 0)),
                  resident((1, D_MODEL)),
                  resident((D_MODEL, D_IN_MAIN)),
                  resident((D_MODEL, GATE_PAD)),
                  resident((GDN_CONV, 3 * D_GDN)),
                  resident((2 * GDN_HEADS, GATE_PAD)),
                  resident((2 * GDN_HEADS, GATE_PAD)),
                  resident((1, GDN_HEAD_DIM))],
        out_specs=[pl.BlockSpec((tg, D_S5), lambda s: (tile_a(s), 0)),
                   pl.BlockSpec((tg, D_GDN), lambda s: (tile_c(s), 0)),
                   pl.BlockSpec((1, GDN_HEADS, GDN_HEAD_DIM, GDN_HEAD_DIM), lambda s: (tile_c(s) // nt, 0, 0, 0)),
                   pl.BlockSpec((1, GDN_CONV - 1, 3 * D_GDN), lambda s: (tile_b(s) // nt, 0, 0))],
        out_shape=[jax.ShapeDtypeStruct((bsz * seq, D_S5), F32),
                   jax.ShapeDtypeStruct((bsz * seq, D_GDN), F32),
                   jax.ShapeDtypeStruct((bsz, GDN_HEADS, GDN_HEAD_DIM, GDN_HEAD_DIM), F32),
                   jax.ShapeDtypeStruct((bsz, GDN_CONV - 1, 3 * D_GDN), F32)],
        scratch_shapes=[pltpu.VMEM((GDN_HEADS, GDN_HEAD_DIM, GDN_HEAD_DIM), F32),
                        proj_bufs(), proj_bufs(), handover(), handover()],
        compiler_params=pltpu.CompilerParams(dimension_semantics=("arbitrary",),
                                             vmem_limit_bytes=VMEM_LIMIT),
        name="proj_gdn_prompt",
    )(x, g1, w_in_bf, w_gate_bf, gw["conv_w"], gw["a_log_col"], gw["dt_bias_col"], gw["onorm_g"])


def _gdn_step_kernel(qkv_ref, st_ref, z_ref, ba_ref, s0_ref, convw_ref, alog_ref, dtb_ref, og_ref,
                     y_ref, s1_ref, *, bb):
    beta_all, g_all = _gates(ba_ref[...], alog_ref[...], dtb_ref[...])
    alpha_all = jnp.exp(g_all)
    scale = GDN_HEAD_DIM ** -0.5
    og = og_ref[...]

    for h in range(GDN_HEADS):
        def conv_cols(part, h=h):
            c0 = part * D_GDN + h * GDN_HEAD_DIM
            cs = slice(c0, c0 + GDN_HEAD_DIM)
            acc = qkv_ref[:, cs] * convw_ref[GDN_CONV - 1:GDN_CONV, cs]
            for j in range(GDN_CONV - 1):
                acc = acc + st_ref[j, :, cs] * convw_ref[j:j + 1, cs]
            return _silu(acc)

        q = _l2norm(conv_cols(0)) * scale
        k = _l2norm(conv_cols(1))
        v = conv_cols(2)
        beta = beta_all[:, h:h + 1]
        alpha = alpha_all[:, GDN_HEADS + h:GDN_HEADS + h + 1]
        qk = jnp.sum(q * k, axis=-1, keepdims=True)
        kt = k.T
        kq = jnp.concatenate([k, q], axis=0).astype(BF16)
        o_rows = []
        for b in range(bb):
            s0 = s0_ref[b, h]
            kcol = kt[:, b:b + 1]
            r = _dot(kq, s0.astype(BF16))
            ks = r[b:b + 1, :]
            qs = r[bb + b:bb + b + 1, :]
            al = alpha[b:b + 1, :]
            v_new = beta[b:b + 1, :] * (v[b:b + 1, :] - al * ks)
            o_rows.append(al * qs + qk[b:b + 1, :] * v_new)
            s1_ref[b, h] = al * s0 + kcol * v_new
        o = jnp.concatenate(o_rows, axis=0)
        cs = slice(h * GDN_HEAD_DIM, (h + 1) * GDN_HEAD_DIM)
        y_ref[:, cs] = _out_gate(o, og, z_ref[:, cs])


def _gdn_sample(qkv, st_t, z, ba, s0, gw, bb):
    n = qkv.shape[0]
    row = lambda i: (i, 0)
    const = lambda i: (0, 0)
    state = lambda i: (i, 0, 0, 0)
    return pl.pallas_call(
        functools.partial(_gdn_step_kernel, bb=bb),
        grid=(n // bb,),
        in_specs=[pl.BlockSpec((bb, 3 * D_GDN), row),
                  pl.BlockSpec((GDN_CONV - 1, bb, 3 * D_GDN), lambda i: (0, i, 0)),
                  pl.BlockSpec((bb, D_GDN), row),
                  pl.BlockSpec((bb, GATE_PAD), row),
                  pl.BlockSpec((bb, GDN_HEADS, GDN_HEAD_DIM, GDN_HEAD_DIM), state),
                  pl.BlockSpec((GDN_CONV, 3 * D_GDN), const),
                  pl.BlockSpec((1, GATE_PAD), const),
                  pl.BlockSpec((1, GATE_PAD), const),
                  pl.BlockSpec((1, GDN_HEAD_DIM), const)],
        out_specs=[pl.BlockSpec((bb, D_GDN), row),
                   pl.BlockSpec((bb, GDN_HEADS, GDN_HEAD_DIM, GDN_HEAD_DIM), state)],
        out_shape=[jax.ShapeDtypeStruct((n, D_GDN), F32),
                   jax.ShapeDtypeStruct(s0.shape, F32)],
        compiler_params=pltpu.CompilerParams(dimension_semantics=("arbitrary",),
                                             vmem_limit_bytes=VMEM_LIMIT),
        name="gdn_sample",
    )(qkv, st_t, z, ba, s0, gw["conv_w"], gw["a_log"], gw["dt_bias"], gw["onorm_g"])


def _residual_in(x_ref, ys5_ref, ygdn_ref, wout_ref, g2_ref, x1_s, n2_s):
    x1 = (x_ref[...] + _dot(ys5_ref[...].astype(BF16), wout_ref[:D_S5, :])
          + _dot(ygdn_ref[...].astype(BF16), wout_ref[D_S5:, :]))
    x1_s[...] = x1
    n2_s[...] = _rms(x1, g2_ref[...]).astype(BF16)


def _ffn_prompt_kernel(x_ref, ys5_ref, ygdn_ref, wout_ref, g2_ref, wup_ref, cw_ref, wdn_ref, gf_ref,
                       y_ref, hlast_ref, x1_s, n2_s, act_s, carry_s, *, tm):
    ti = pl.program_id(1)
    nt = pl.num_programs(1)

    @pl.when(ti == 0)
    def _reset():
        carry_s[...] = jnp.zeros_like(carry_s)

    _residual_in(x_ref, ys5_ref, ygdn_ref, wout_ref, g2_ref, x1_s, n2_s)
    n2 = n2_s[...]
    for f in range(D_FF // FFN_COLS):
        halves = []
        for part in range(2):
            c0 = part * D_FF + f * FFN_COLS
            cs = slice(c0, c0 + FFN_COLS)
            hcur = _dot(n2, wup_ref[:, cs])
            ext = jnp.concatenate([carry_s[:, cs], hcur], axis=0)
            carry_s[:, cs] = hcur[tm - 8:, :]
            conv = (_rows_back(ext, 2, tm) * cw_ref[0:1, cs]
                    + _rows_back(ext, 1, tm) * cw_ref[1:2, cs]
                    + hcur * cw_ref[2:3, cs])
            halves.append(conv)
        act_s[:, f * FFN_COLS:(f + 1) * FFN_COLS] = (_silu(halves[0]) * halves[1]).astype(BF16)
    y_ref[...] = _rms(x1_s[...] + _dot(act_s[...], wdn_ref[...]), gf_ref[...])

    @pl.when(ti == nt - 1)
    def _fin():
        hlast_ref[0] = carry_s[8 - (FFN_CONV - 1):, :]


def _ffn_prompt(x, ys5, ygdn, fw, bsz, seq, tm):
    nt = seq // tm
    row = lambda b, i: (b * nt + i, 0)
    const = lambda b, i: (0, 0)
    resident = lambda shape: pl.BlockSpec(shape, const, pipeline_mode=pl.Buffered(1))
    return pl.pallas_call(
        functools.partial(_ffn_prompt_kernel, tm=tm),
        grid=(bsz, nt),
        in_specs=[pl.BlockSpec((tm, D_MODEL), row),
                  pl.BlockSpec((tm, D_S5), row),
                  pl.BlockSpec((tm, D_GDN), row),
                  resident((D_MODEL, D_MODEL)),
                  resident((1, D_MODEL)),
                  resident((D_MODEL, 2 * D_FF)),
                  resident((FFN_CONV, 2 * D_FF)),
                  resident((D_FF, D_MODEL)),
                  resident((1, D_MODEL))],
        out_specs=[pl.BlockSpec((tm, D_MODEL), row),
                   pl.BlockSpec((1, FFN_CONV - 1, 2 * D_FF), lambda b, i: (b, 0, 0))],
        out_shape=[jax.ShapeDtypeStruct((bsz * seq, D_MODEL), F32),
                   jax.ShapeDtypeStruct((bsz, FFN_CONV - 1, 2 * D_FF), F32)],
        scratch_shapes=[pltpu.VMEM((tm, D_MODEL), F32),
                        pltpu.VMEM((tm, D_MODEL), BF16),
                        pltpu.VMEM((tm, D_FF), BF16),
                        pltpu.VMEM((8, 2 * D_FF), F32)],
        compiler_params=pltpu.CompilerParams(dimension_semantics=("arbitrary", "arbitrary"),
                                             vmem_limit_bytes=VMEM_LIMIT),
        name="ffn_prompt",
    )(x, ys5, ygdn, fw["w_out"], fw["norm2_g"], fw["w_up"], fw["conv_w"], fw["w_down"], fw["normf_g"])


def _ffn_sample_kernel(x_ref, ys5_ref, ygdn_ref, stg_ref, stu_ref, wout_ref, g2_ref, wgate_ref, wup_ref,
                       cwg_ref, cwu_ref, wdn_ref, gf_ref, y_ref, hg_ref, hu_ref, x1_s, n2_s, acc_s):
    f = pl.program_id(0)

    @pl.when(f == 0)
    def _first():
        _residual_in(x_ref, ys5_ref, ygdn_ref, wout_ref, g2_ref, x1_s, n2_s)
        acc_s[...] = jnp.zeros_like(acc_s)

    n2 = n2_s[...]
    halves = []
    for w_ref, st_ref, cw_ref, h_ref in ((wgate_ref, stg_ref, cwg_ref, hg_ref), (wup_ref, stu_ref, cwu_ref, hu_ref)):
        hcur = _dot(n2, w_ref[...])
        h_ref[...] = hcur
        halves.append(st_ref[0] * cw_ref[0:1, :] + st_ref[1] * cw_ref[1:2, :] + hcur * cw_ref[2:3, :])
    act = (_silu(halves[0]) * halves[1]).astype(BF16)
    acc_s[...] += _dot(act, wdn_ref[...])

    @pl.when(f == pl.num_programs(0) - 1)
    def _last():
        y_ref[...] = _rms(x1_s[...] + acc_s[...], gf_ref[...])


def _ffn_sample(x, ys5, ygdn, st_t, fw):
    n = x.shape[0]
    nf = D_FF // FFN_COLS
    c2 = lambda f: (0, 0)
    gate_cols = lambda f: (0, f)
    up_cols = lambda f: (0, nf + f)
    return pl.pallas_call(
        _ffn_sample_kernel,
        grid=(nf,),
        in_specs=[pl.BlockSpec((n, D_MODEL), c2),
                  pl.BlockSpec((n, D_S5), c2),
                  pl.BlockSpec((n, D_GDN), c2),
                  pl.BlockSpec((FFN_CONV - 1, n, FFN_COLS), lambda f: (0, 0, f)),
                  pl.BlockSpec((FFN_CONV - 1, n, FFN_COLS), lambda f: (0, 0, nf + f)),
                  pl.BlockSpec((D_MODEL, D_MODEL), c2),
                  pl.BlockSpec((1, D_MODEL), c2),
                  pl.BlockSpec((D_MODEL, FFN_COLS), gate_cols),
                  pl.BlockSpec((D_MODEL, FFN_COLS), up_cols),
                  pl.BlockSpec((FFN_CONV, FFN_COLS), gate_cols),
                  pl.BlockSpec((FFN_CONV, FFN_COLS), up_cols),
                  pl.BlockSpec((FFN_COLS, D_MODEL), lambda f: (f, 0)),
                  pl.BlockSpec((1, D_MODEL), c2)],
        out_specs=[pl.BlockSpec((n, D_MODEL), c2),
                   pl.BlockSpec((n, FFN_COLS), gate_cols),
                   pl.BlockSpec((n, FFN_COLS), gate_cols)],
        out_shape=[jax.ShapeDtypeStruct((n, D_MODEL), F32),
                   jax.ShapeDtypeStruct((n, D_FF), F32),
                   jax.ShapeDtypeStruct((n, D_FF), F32)],
        scratch_shapes=[pltpu.VMEM((n, D_MODEL), F32),
                        pltpu.VMEM((n, D_MODEL), BF16),
                        pltpu.VMEM((n, D_MODEL), F32)],
        compiler_params=pltpu.CompilerParams(dimension_semantics=("arbitrary",),
                                             vmem_limit_bytes=VMEM_LIMIT),
        name="ffn_sample",
    )(x, ys5, ygdn, st_t, st_t, fw["w_out"], fw["norm2_g"], fw["w_up"], fw["w_up"], fw["conv_w"], fw["conv_w"],
      fw["w_down"], fw["normf_g"])


PROJ_ROWS = 512
S5_STEPS = 128
GDN_ROWS = 256
FFN_ROWS = 512
GDN_SAMPLE_ROWS = 8


def kernel(x_prompt, x_sample, state_s5_re, state_s5_im, state_gdn, state_gdn_conv, state_ffn_conv, norm1_g, w_in, s5_a_re, s5_a_im, s5_log_dt, s5_b_re, s5_b_im, s5_c_re, s5_c_im, s5_d, s5_w_glu, gdn_conv_w, gdn_a_log, gdn_dt_bias, gdn_onorm_g, w_out, norm2_g, ffn_w_up, ffn_conv_w, ffn_w_down, normf_g):
    depth = w_in.shape[0]
    assert depth == 1, "the final rmsnorm is fused into the last layer's ffn kernel"
    bsz, seq, _ = x_prompt.shape
    nsmp = x_sample.shape[0]
    assert x_sample.shape[1] == 1
    l = 0

    xp = x_prompt.reshape(bsz * seq, D_MODEL)
    xs = x_sample.reshape(nsmp, D_MODEL)

    g1 = norm1_g[l].reshape(1, D_MODEL)
    w_in_bf = w_in[l][:, :D_IN_MAIN].astype(BF16)
    w_gate_bf = jnp.pad(w_in[l][:, D_IN_MAIN:], ((0, 0), (0, GATE_PAD - 2 * GDN_HEADS))).astype(BF16)
    sw = _s5_weights(s5_a_re[l], s5_a_im[l], s5_log_dt[l], s5_b_re[l], s5_b_im[l], s5_c_re[l], s5_c_im[l],
                     s5_d[l], s5_w_glu[l])
    gate_row = lambda v: jnp.pad(v, (GDN_HEADS, GATE_PAD - 2 * GDN_HEADS)).reshape(1, GATE_PAD)
    gate_col = lambda v: jnp.broadcast_to(jnp.pad(v, (GDN_HEADS, 0))[:, None], (2 * GDN_HEADS, GATE_PAD))
    gw = {"conv_w": gdn_conv_w[l], "a_log": gate_row(gdn_a_log[l]), "dt_bias": gate_row(gdn_dt_bias[l]),
          "a_log_col": gate_col(gdn_a_log[l]), "dt_bias_col": gate_col(gdn_dt_bias[l]),
          "onorm_g": gdn_onorm_g[l].reshape(1, GDN_HEAD_DIM)}
    fw = {"w_out": w_out[l].astype(BF16), "norm2_g": norm2_g[l].reshape(1, D_MODEL),
          "w_up": ffn_w_up[l].astype(BF16), "conv_w": ffn_conv_w[l], "w_down": ffn_w_down[l].astype(BF16),
          "normf_g": normf_g.reshape(1, D_MODEL)}

    u_p, ygdn_p, p_gdn, p_gdn_conv = _gdn_prompt(xp, g1, w_in_bf, w_gate_bf, gw, bsz, seq, GDN_ROWS)
    zeros_h = jnp.zeros((bsz, S5_CH), F32)
    ys5_p, p_hr, p_hi = _s5(u_p.reshape(bsz, seq, D_S5), zeros_h, zeros_h, sw, bsz, S5_STEPS, True)
    ys5_p = ys5_p.reshape(bsz * seq, D_S5)
    y_p, p_ffn_conv = _ffn_prompt(xp, ys5_p, ygdn_p, fw, bsz, seq, FFN_ROWS)

    u_s, qkv_s, z_s, ba_s = _proj(xs, g1, w_in_bf, w_gate_bf, nsmp)
    ys5_s, s_hr, s_hi = _s5(u_s, state_s5_re[l].reshape(nsmp, S5_CH), state_s5_im[l].reshape(nsmp, S5_CH),
                            sw, nsmp, 1, False)
    gconv_t = state_gdn_conv[l].transpose(1, 0, 2)
    ygdn_s, s_gdn = _gdn_sample(qkv_s, gconv_t, z_s, ba_s, state_gdn[l], gw, GDN_SAMPLE_ROWS)
    fconv_t = state_ffn_conv[l].transpose(1, 0, 2)
    y_s, hg_s, hu_s = _ffn_sample(xs, ys5_s, ygdn_s, fconv_t, fw)
    h_s = jnp.concatenate([hg_s, hu_s], axis=1)
    s_gdn_conv = jnp.concatenate([state_gdn_conv[l][:, 1:], qkv_s[:, None, :]], axis=1)
    s_ffn_conv = jnp.concatenate([state_ffn_conv[l][:, 1:], h_s[:, None, :]], axis=1)

    st = lambda a: a[None]
    s5_shape = lambda a, n: a.reshape(1, n, S5_N_GROUPS, S5_STATE)
    return (y_p.reshape(bsz, seq, D_MODEL), y_s.reshape(nsmp, 1, D_MODEL),
            s5_shape(p_hr, bsz), s5_shape(p_hi, bsz), st(p_gdn), st(p_gdn_conv), st(p_ffn_conv),
            s5_shape(s_hr, nsmp), s5_shape(s_hi, nsmp), st(s_gdn), st(s_gdn_conv), st(s_ffn_conv))
```

```python
import functools

import jax
import jax.numpy as jnp
from jax import lax
from jax.experimental import pallas as pl
from jax.experimental.pallas import tpu as pltpu

F32 = jnp.float32
BF16 = jnp.bfloat16
NORM_EPS = 1e-6

D_MODEL = 1024
D_S5 = 512
S5_GROUP = 16
S5_N_GROUPS = 32
S5_STATE = 64
S5_CH = S5_N_GROUPS * S5_STATE
D_GDN = 512
GDN_HEADS = 4
GDN_HEAD_DIM = 128
GDN_CONV = 4
D_FF = 2816
FFN_CONV = 3
D_IN = D_S5 + 4 * D_GDN + 2 * GDN_HEADS
GATE_PAD = 128
D_IN_MAIN = D_S5 + 4 * D_GDN

CHUNK = 64
FFN_COLS = 256
VMEM_LIMIT = 56 * 1024 * 1024


def _dot(a, b):
    return jnp.dot(a, b, preferred_element_type=F32)


def _dot_nt(a, b):
    return lax.dot_general(a, b, (((1,), (1,)), ((), ())), preferred_element_type=F32)


def _rms(x, g):
    ms = jnp.mean(x * x, axis=-1, keepdims=True)
    return x * lax.rsqrt(ms + NORM_EPS) * g


def _silu(x):
    return x * jax.nn.sigmoid(x)


def _rows_back(ext, back, n):
    return pltpu.roll(ext, back, 0)[8:8 + n]


def _softplus(x):
    return jnp.maximum(x, 0.0) + jnp.log1p(jnp.exp(-jnp.abs(x)))


def _proj_kernel(x_ref, g_ref, w_ref, wgate_ref, u_ref, qkv_ref, z_ref, ba_ref):
    n = _rms(x_ref[...], g_ref[...]).astype(BF16)
    o0 = D_S5
    o1 = o0 + 3 * D_GDN
    u_ref[...] = _dot(n, w_ref[:, :o0])
    qkv_ref[...] = _dot(n, w_ref[:, o0:o1])
    z_ref[...] = _dot(n, w_ref[:, o1:])
    ba_ref[...] = _dot(n, wgate_ref[...])


def _proj(x, g1, w_in_bf, w_gate_bf, tm):
    n = x.shape[0]
    row = lambda i: (i, 0)
    const = lambda i: (0, 0)
    return pl.pallas_call(
        _proj_kernel,
        grid=(n // tm,),
        in_specs=[pl.BlockSpec((tm, D_MODEL), row),
                  pl.BlockSpec((1, D_MODEL), const),
                  pl.BlockSpec((D_MODEL, D_IN_MAIN), const),
                  pl.BlockSpec((D_MODEL, GATE_PAD), const)],
        out_specs=[pl.BlockSpec((tm, D_S5), row),
                   pl.BlockSpec((tm, 3 * D_GDN), row),
                   pl.BlockSpec((tm, D_GDN), row),
                   pl.BlockSpec((tm, GATE_PAD), row)],
        out_shape=[jax.ShapeDtypeStruct((n, D_S5), F32),
                   jax.ShapeDtypeStruct((n, 3 * D_GDN), F32),
                   jax.ShapeDtypeStruct((n, D_GDN), F32),
                   jax.ShapeDtypeStruct((n, GATE_PAD), F32)],
        compiler_params=pltpu.CompilerParams(dimension_semantics=("arbitrary",),
                                             vmem_limit_bytes=VMEM_LIMIT),
        name="proj",
    )(x, g1, w_in_bf, w_gate_bf)


S5_HALF = S5_CH // 2
S5_COLS = 1024
S5_SCAN_GROUP = 8
S5_PARTS = 4


def _s5_kernel(u_ref, h0r_ref, h0i_ref, are_ref, aim_ref, ldt_ref, bre_ref, bim_ref, cre_ref, cim_ref,
               d_ref, wglu_ref, y_ref, hr_out, hi_out,
               abr_s, abi_s, wbr_s, wbi_s, hr_s, hi_s, xr_s, xi_s, y_s, *, nb, tt, batch_major):
    @pl.when(pl.program_id(0) == 0)
    def _init():
        ar = are_ref[...]
        ai = aim_ref[...]
        dt = jnp.exp(ldt_ref[...])
        mag = jnp.exp(ar * dt)
        abr = mag * jnp.cos(ai * dt)
        abi = mag * jnp.sin(ai * dt)
        den = ar * ar + ai * ai
        p = abr - 1.0
        fr = (p * ar + abi * ai) / den
        fi = (abi * ar - p * ai) / den
        abr_s[...] = abr
        abi_s[...] = abi
        for m in range(2):
            frm = fr[:, m * S5_HALF:(m + 1) * S5_HALF]
            fim = fi[:, m * S5_HALF:(m + 1) * S5_HALF]
            wr = bre_ref[m]
            wi = bim_ref[m]
            wbr_s[m] = (wr * frm - wi * fim).astype(BF16)
            wbi_s[m] = (wr * fim + wi * frm).astype(BF16)
        hr_s[...] = h0r_ref[...]
        hi_s[...] = h0i_ref[...]

    if batch_major:
        u = jnp.swapaxes(u_ref[...], 0, 1).reshape(tt * nb, D_S5)
    else:
        u = u_ref[...]
    ub = u.astype(BF16)
    half = D_S5 // 2

    nparts = S5_PARTS if tt % (S5_PARTS * S5_SCAN_GROUP) == 0 else 1
    tp = tt // nparts
    part_rows = lambda p: slice(p * tp * nb, (p + 1) * tp * nb)

    blk = 256

    def in_proj(p):
        def block(m, ws, xs, c0):
            def item():
                xs[part_rows(p), m * S5_HALF + c0:m * S5_HALF + c0 + blk] = _dot(
                    ub[part_rows(p), m * half:(m + 1) * half], ws[m, :, c0:c0 + blk])
            return item
        return [block(m, ws, xs, c0) for m in range(2) for ws, xs in ((wbr_s, xr_s), (wbi_s, xi_s))
                for c0 in range(0, S5_HALF, blk)]

    def scan(p):
        def group(cb, t0, steps):
            def item():
                cols = slice(cb * S5_COLS, (cb + 1) * S5_COLS)
                a_r = jnp.broadcast_to(abr_s[:, cols], (nb, S5_COLS))
                a_i = jnp.broadcast_to(abi_s[:, cols], (nb, S5_COLS))
                hr = hr_s[:, cols]
                hi = hi_s[:, cols]
                for t in range(t0, t0 + steps):
                    rows = slice(t * nb, (t + 1) * nb)
                    hr, hi = (a_r * hr - a_i * hi + xr_s[rows, cols],
                              a_r * hi + a_i * hr + xi_s[rows, cols])
                    xr_s[rows, cols] = hr
                    xi_s[rows, cols] = hi
                hr_s[:, cols] = hr
                hi_s[:, cols] = hi
            return item
        steps = min(tp, S5_SCAN_GROUP)
        return [group(cb, t0, steps) for t0 in range(p * tp, (p + 1) * tp, steps)
                for cb in range(S5_CH // S5_COLS)]

    def out_proj(p):
        ys = {}

        def re_block(m):
            def item():
                hre = xr_s[part_rows(p), m * S5_HALF:(m + 1) * S5_HALF].astype(BF16)
                ys[m] = (_dot(hre, cre_ref[m])
                         + d_ref[:, m * half:(m + 1) * half] * u[part_rows(p), m * half:(m + 1) * half])
            return item

        def im_block(m):
            def item():
                him = xi_s[part_rows(p), m * S5_HALF:(m + 1) * S5_HALF].astype(BF16)
                ys[m] = ys[m] - _dot(him, cim_ref[m])
            return item

        def act():
            ys["act"] = jax.nn.gelu(jnp.concatenate([ys[0], ys[1]], axis=-1)).astype(BF16)

        def glu(c0):
            def item():
                val = _dot(ys["act"], wglu_ref[:, c0:c0 + blk])
                gate = _dot(ys["act"], wglu_ref[:, D_S5 + c0:D_S5 + c0 + blk])
                y_s[part_rows(p), c0:c0 + blk] = val * jax.nn.sigmoid(gate)
            return item
        return ([re_block(0), im_block(0), re_block(1), im_block(1), act]
                + [glu(c0) for c0 in range(0, D_S5, blk)])

    order = in_proj(0)
    for p in range(nparts):
        nxt = in_proj(p + 1) if p + 1 < nparts else []
        prv = out_proj(p - 1) if p > 0 else []
        order += _interleave(nxt + prv, scan(p)) if (nxt or prv) else scan(p)
    order += out_proj(nparts - 1)
    for item in order:
        item()

    hr_out[...] = hr_s[...]
    hi_out[...] = hi_s[...]
    if batch_major:
        y_ref[...] = jnp.swapaxes(y_s[...].reshape(tt, nb, D_S5), 0, 1)
    else:
        y_ref[...] = y_s[...]


def _s5(u, h0r, h0i, sw, nb, tt, batch_major):
    rows = nb * tt
    const2 = lambda i: (0, 0)
    const3 = lambda i: (0, 0, 0)
    full2 = lambda shape: pl.BlockSpec(shape, const2)
    full3 = lambda shape: pl.BlockSpec(shape, const3)
    if batch_major:
        steps = u.shape[1] // tt
        io_spec = pl.BlockSpec((nb, tt, D_S5), lambda i: (0, i, 0))
    else:
        steps = u.shape[0] // rows
        io_spec = pl.BlockSpec((rows, D_S5), lambda i: (i, 0))
    return pl.pallas_call(
        functools.partial(_s5_kernel, nb=nb, tt=tt, batch_major=batch_major),
        grid=(steps,),
        in_specs=[io_spec,
                  full2((nb, S5_CH)), full2((nb, S5_CH)),
                  full2((1, S5_CH)), full2((1, S5_CH)), full2((1, S5_CH)),
                  full3((2, D_S5 // 2, S5_HALF)), full3((2, D_S5 // 2, S5_HALF)),
                  full3((2, S5_HALF, D_S5 // 2)), full3((2, S5_HALF, D_S5 // 2)),
                  full2((1, D_S5)), full2((D_S5, 2 * D_S5))],
        out_specs=[io_spec, full2((nb, S5_CH)), full2((nb, S5_CH))],
        out_shape=[jax.ShapeDtypeStruct(u.shape, F32),
                   jax.ShapeDtypeStruct((nb, S5_CH), F32),
                   jax.ShapeDtypeStruct((nb, S5_CH), F32)],
        scratch_shapes=[pltpu.VMEM((1, S5_CH), F32), pltpu.VMEM((1, S5_CH), F32),
                        pltpu.VMEM((2, D_S5 // 2, S5_HALF), BF16), pltpu.VMEM((2, D_S5 // 2, S5_HALF), BF16),
                        pltpu.VMEM((nb, S5_CH), F32), pltpu.VMEM((nb, S5_CH), F32),
                        pltpu.VMEM((rows, S5_CH), F32), pltpu.VMEM((rows, S5_CH), F32),
                        pltpu.VMEM((rows, D_S5), F32)],
        compiler_params=pltpu.CompilerParams(dimension_semantics=("arbitrary",),
                                             vmem_limit_bytes=VMEM_LIMIT),
        name="s5",
    )(u, h0r, h0i, sw["a_re"], sw["a_im"], sw["log_dt"], sw["b_re"], sw["b_im"],
      sw["c_re"], sw["c_im"], sw["d"], sw["w_glu"])


def _s5_weights(a_re, a_im, log_dt, b_re, b_im, c_re, c_im, d, w_glu):
    g, n, c = S5_N_GROUPS, S5_STATE, S5_GROUP
    gh = g // 2

    def blockdiag(w, rows_per_group, cols_per_group):
        x = jnp.transpose(w, (0, 2, 1)).reshape(2, gh * rows_per_group, cols_per_group)
        rg = jnp.arange(gh * rows_per_group)[:, None] // rows_per_group
        cg = jnp.arange(gh * cols_per_group)[None, :] // cols_per_group
        return jnp.where(rg == cg, jnp.tile(x, (1, 1, gh)), 0.0)

    b_blockdiag = lambda b: blockdiag(b, c, n)
    c_blockdiag = lambda cm: blockdiag(cm, n, c)

    return {
        "a_re": a_re.reshape(1, S5_CH), "a_im": a_im.reshape(1, S5_CH),
        "log_dt": jnp.repeat(log_dt, n).reshape(1, S5_CH),
        "b_re": b_blockdiag(b_re), "b_im": b_blockdiag(b_im),
        "c_re": c_blockdiag(c_re).astype(BF16), "c_im": c_blockdiag(c_im).astype(BF16),
        "d": d.reshape(1, D_S5), "w_glu": w_glu.astype(BF16),
    }


def _l2norm(x):
    return x * lax.rsqrt(jnp.sum(x * x, axis=-1, keepdims=True) + NORM_EPS)


def _gates(ba, alog_row, dtb_row):
    beta = jax.nn.sigmoid(ba)
    g = -jnp.exp(alog_row) * _softplus(ba + dtb_row)
    return beta, g


def _out_gate(o, og_row, z):
    return _rms(o, og_row) * _silu(z)


def _bdot(a, b):
    return jnp.einsum("bij,bjk->bik", a.astype(BF16), b.astype(BF16), preferred_element_type=F32)


def _tri_merge_level(n, lmat, ii, jj, shift):
    same_big = (ii >> (shift + 1)) == (jj >> (shift + 1))
    same_small = (ii >> shift) == (jj >> shift)
    c = jnp.where((same_big & jnp.logical_not(same_small))[None], lmat, 0.0)
    w = c + _bdot(n, c)
    return n - (w + _bdot(w, n))


def _interleave(*stages):
    out = []
    pos = [0] * len(stages)
    while any(p < len(st) for p, st in zip(pos, stages)):
        k = min((i for i, st in enumerate(stages) if pos[i] < len(st)),
                key=lambda i: (pos[i] + 0.5) / len(stages[i]))
        out.append(stages[k][pos[k]])
        pos[k] += 1
    return out


def _gdn_kernel(x_ref, g1_ref, win_ref, wgate_ref, convw_ref, alog_ref, dtb_ref, og_ref,
                u_ref, y_ref, sfin_ref, convout_ref, s_s, proj_a, proj_b, set_a, set_b, *, tg, nt):
    s = pl.program_id(0)

    @pl.when(s == 0)
    def _zero_buffers():
        for ref in (*proj_b, *set_a):
            ref[...] = jnp.zeros_like(ref)

    @pl.when(lax.rem(jnp.maximum(s - 1, 0), nt) == 0)
    def _new_sequence_conv():
        for ext in (proj_a[0], proj_b[0]):
            ext[0:8, :] = jnp.zeros((8, 3 * D_GDN), F32)

    @pl.when(lax.rem(jnp.maximum(s - 2, 0), nt) == 0)
    def _new_sequence_state():
        s_s[...] = jnp.zeros_like(s_s)

    args = (x_ref, g1_ref, win_ref, wgate_ref, convw_ref, alog_ref, dtb_ref, og_ref, u_ref, y_ref, sfin_ref, convout_ref, s_s)

    @pl.when(lax.rem(s, 2) == 0)
    def _even():
        _gdn_step(*args, proj_wr=proj_a, proj_rd=proj_b, wr=set_b, rd=set_a, tg=tg)

    @pl.when(lax.rem(s, 2) == 1)
    def _odd():
        _gdn_step(*args, proj_wr=proj_b, proj_rd=proj_a, wr=set_a, rd=set_b, tg=tg)


def _gdn_step(x_ref, g1_ref, win_ref, wgate_ref, convw_ref, alog_ref, dtb_ref, og_ref, u_ref, y_ref, sfin_ref, convout_ref,
              s_s, *, proj_wr, proj_rd, wr, rd, tg):
    nc = tg // CHUNK
    nsys = nc * GDN_HEADS
    hcols = lambda h: slice(h * GDN_HEAD_DIM, (h + 1) * GDN_HEAD_DIM)
    og = og_ref[...]
    v = {}
    stage_a, stage_b, stage_c = [], [], []

    ext_w, z_w, ba_w = proj_wr
    o0 = D_S5
    o1 = o0 + 3 * D_GDN
    o2 = o1 + D_GDN

    def a_norm():
        v["n1"] = _rms(x_ref[...], g1_ref[...]).astype(BF16)
    stage_a.append(a_norm)

    def a_block(dst, rows, dst_c0, src_c0, width):
        def item():
            dst[rows, dst_c0:dst_c0 + width] = _dot(v["n1"], win_ref[:, src_c0:src_c0 + width])
        return item
    blk = 256
    for c0 in range(0, o0, blk):
        stage_a.append(a_block(u_ref, slice(None), c0, c0, blk))
    for c0 in range(0, o1 - o0, blk):
        stage_a.append(a_block(ext_w, slice(8, None), c0, o0 + c0, blk))
    for c0 in range(0, o2 - o1, blk):
        stage_a.append(a_block(z_w, slice(None), c0, o1 + c0, blk))

    def a_gates():
        ba_w[...] = _dot(v["n1"], wgate_ref[...])
    stage_a.append(a_gates)

    aq_r, b_r, o_r, gl_r, zs_r = rd
    states = [None] * GDN_HEADS

    def c_chunk(c):
        def item():
            rows = slice(c * CHUNK, (c + 1) * CHUNK)
            for h in range(GDN_HEADS):
                sys = c * GDN_HEADS + h
                st = s_s[h] if c == 0 else states[h]
                r = _dot(aq_r[sys], st.astype(BF16))
                o = r[GDN_HEAD_DIM:] + o_r[sys]
                states[h] = st * gl_r[sys, 0:1, :] + r[:GDN_HEAD_DIM] + b_r[sys]
                y_ref[rows, hcols(h)] = _rms(o, og) * zs_r[rows, hcols(h)]
                if c == nc - 1:
                    s_s[h] = states[h]
                    sfin_ref[0, h] = states[h]
        return item
    stage_c.extend(c_chunk(c) for c in range(nc))

    aq_s, b_s, o_s, gl_s, zs_s = wr
    ext_s, z_r, ba_r = proj_rd
    ii = lax.broadcasted_iota(jnp.int32, (CHUNK, CHUNK), 0)
    jj = lax.broadcasted_iota(jnp.int32, (CHUNK, CHUNK), 1)
    causal = jj <= ii
    strict = jj < ii
    scale = GDN_HEAD_DIM ** -0.5
    lmats = [None] * nsys
    rhss = [None] * nsys
    kgt_attn = [None] * nsys
    qgs = [None] * nsys

    def b_gates():
        raw = ba_r[...].T[:2 * GDN_HEADS, :]
        lanes = lambda p: jnp.concatenate([p] * (tg // GATE_PAD), axis=1)
        beta = jax.nn.sigmoid(raw)
        g = -jnp.exp(lanes(alog_ref[...])) * _softplus(raw + lanes(dtb_ref[...]))
        g1 = g.astype(BF16).astype(F32)
        g2 = (g - g1).astype(BF16).astype(F32)
        g3 = (g - g1 - g2).astype(BF16).astype(F32)
        ri = lax.broadcasted_iota(jnp.int32, (tg, tg), 0)
        ci = lax.broadcasted_iota(jnp.int32, (tg, tg), 1)
        ubd = jnp.where(((ri >> 6) == (ci >> 6)) & (ri <= ci), 1.0, 0.0).astype(BF16)
        parts = _dot(jnp.concatenate([g1, g2, g3], axis=0).astype(BF16), ubd)
        nh2 = 2 * GDN_HEADS
        gc = parts[:nh2] + parts[nh2:2 * nh2] + parts[2 * nh2:]
        row_id = lax.broadcasted_iota(jnp.int32, (nh2, tg), 0)
        rows = jnp.where(row_id < GDN_HEADS, beta, gc)
        cols = jnp.concatenate([rows, jnp.zeros((GATE_PAD - nh2, tg), F32)], axis=0).T
        v["beta_all"] = cols
        v["gc_col"] = cols
        v["gc_row"] = rows
    stage_b.append(b_gates)

    def b_silu_z(h):
        def item():
            zs_s[:, hcols(h)] = _silu(z_r[:, hcols(h)])
        return item

    def b_conv(h, part, name, norm):
        def item():
            c0 = part * D_GDN + h * GDN_HEAD_DIM
            cs = slice(c0, c0 + GDN_HEAD_DIM)
            ext = ext_s[:, cs]
            acc = ext[8:] * convw_ref[GDN_CONV - 1:GDN_CONV, cs]
            for j in range(GDN_CONV - 1):
                acc = acc + _rows_back(ext, GDN_CONV - 1 - j, tg) * convw_ref[j:j + 1, cs]
            acc = _silu(acc)
            v[name] = norm(acc)
        return item

    def b_prep(h, c):
        def item():
            beta_all, gc_col, gc_row = v["beta_all"], v["gc_col"], v["gc_row"]
            sys = c * GDN_HEADS + h
            rows = slice(c * CHUNK, (c + 1) * CHUNK)
            q = v["q"][rows]
            k = v["k"][rows]
            vv = v["v"][rows]
            beta = beta_all[rows, h:h + 1]
            gcc = gc_col[rows, GDN_HEADS + h:GDN_HEADS + h + 1]
            gcr = gc_row[GDN_HEADS + h:GDN_HEADS + h + 1, rows]
            decay = jnp.exp(jnp.where(causal, gcc - gcr, -jnp.inf))
            kb = k * beta
            egc = jnp.exp(gcc)
            g_last = gcc[CHUNK - 1:CHUNK, :]
            kq = _dot_nt(jnp.concatenate([kb, q], axis=0).astype(BF16), k.astype(BF16))
            lmats[sys] = jnp.where(strict, kq[:CHUNK] * decay, 0.0)
            attn = jnp.where(causal, kq[CHUNK:] * decay, 0.0)
            rhss[sys] = jnp.concatenate([vv * beta, kb * egc], axis=1)
            kgt_attn[sys] = jnp.concatenate([(k * jnp.exp(g_last - gcc)).T, attn], axis=0)
            qgs[sys] = q * egc
            gl_s[sys] = jnp.broadcast_to(jnp.exp(g_last), (8, GDN_HEAD_DIM))
        return item

    for h in range(GDN_HEADS):
        stage_b.append(b_silu_z(h))
        stage_b.append(b_conv(h, 0, "q", lambda a: _l2norm(a) * scale))
        stage_b.append(b_conv(h, 1, "k", _l2norm))
        stage_b.append(b_conv(h, 2, "v", lambda a: a))
        stage_b.extend(b_prep(h, c) for c in range(nc))

    def b_inverse_start():
        lmat = jnp.stack(lmats)
        v["lmat"] = lmat
        v["noff"] = -jnp.where(((ii >> 1) == (jj >> 1))[None], lmat, 0.0)
    stage_b.append(b_inverse_start)

    def b_inverse_level(shift):
        def item():
            v["noff"] = _tri_merge_level(v["noff"], v["lmat"], ii, jj, shift)
        return item
    shift = 1
    while (1 << shift) < CHUNK:
        stage_b.append(b_inverse_level(shift))
        shift += 1

    def b_solve():
        rhs = jnp.stack(rhss)
        sol = rhs + _bdot(v["noff"], rhs)
        v["prod"] = _bdot(jnp.stack(kgt_attn), sol)
    stage_b.append(b_solve)

    def b_store():
        prod = v["prod"]
        dk = GDN_HEAD_DIM
        for sys in range(nsys):
            aq_s[sys, :dk, :] = (-prod[sys, :dk, dk:]).astype(BF16)
            aq_s[sys, dk:, :] = (qgs[sys] - prod[sys, dk:, dk:]).astype(BF16)
            b_s[sys] = prod[sys, :dk, :dk]
            o_s[sys] = prod[sys, dk:, :dk]
    stage_b.append(b_store)

    for item in _interleave(stage_a, stage_b, stage_c):
        item()

    ext_w[0:8, :] = ext_s[pl.ds(tg, 8), :]
    convout_ref[0] = ext_s[pl.ds(tg + 8 - (GDN_CONV - 1), GDN_CONV - 1), :]


def _gdn_prompt(x, g1, w_in_bf, w_gate_bf, gw, bsz, seq, tg):
    nt = seq // tg
    ntiles = bsz * nt
    nsys = (tg // CHUNK) * GDN_HEADS
    tile_a = lambda s: jnp.minimum(s, ntiles - 1)
    tile_b = lambda s: jnp.clip(s - 1, 0, ntiles - 1)
    tile_c = lambda s: jnp.maximum(s - 2, 0)
    const = lambda s: (0, 0)
    proj_bufs = lambda: [pltpu.VMEM((tg + 8, 3 * D_GDN), F32),
                         pltpu.VMEM((tg, D_GDN), F32),
                         pltpu.VMEM((tg, GATE_PAD), F32)]
    resident = lambda shape: pl.BlockSpec(shape, const, pipeline_mode=pl.Buffered(1))
    handover = lambda: [pltpu.VMEM((nsys, GDN_HEAD_DIM + CHUNK, GDN_HEAD_DIM), BF16),
                        pltpu.VMEM((nsys, GDN_HEAD_DIM, GDN_HEAD_DIM), F32),
                        pltpu.VMEM((nsys, CHUNK, GDN_HEAD_DIM), F32),
                        pltpu.VMEM((nsys, 8, GDN_HEAD_DIM), F32),
                        pltpu.VMEM((tg, D_GDN), F32)]
    return pl.pallas_call(
        functools.partial(_gdn_kernel, tg=tg, nt=nt),
        grid=(ntiles + 2,),
        in_specs=[pl.BlockSpec((tg, D_MODEL), lambda s: (tile_a(s), 0)),
                  resident((1, D_MODEL)),
                  resident((D_MODEL, D_IN_MAIN)),
                  resident((D_MODEL, GATE_PAD)),
                  resident((GDN_CONV, 3 * D_GDN)),
                  resident((2 * GDN_HEADS, GATE_PAD)),
                  resident((2 * GDN_HEADS, GATE_PAD)),
                  resident((1, GDN_HEAD_DIM))],
        out_specs=[pl.BlockSpec((tg, D_S5), lambda s: (tile_a(s), 0)),
                   pl.BlockSpec((tg, D_GDN), lambda s: (tile_c(s), 0)),
                   pl.BlockSpec((1, GDN_HEADS, GDN_HEAD_DIM, GDN_HEAD_DIM), lambda s: (tile_c(s) // nt, 0, 0, 0)),
                   pl.BlockSpec((1, GDN_CONV - 1, 3 * D_GDN), lambda s: (tile_b(s) // nt, 0, 0))],
        out_shape=[jax.ShapeDtypeStruct((bsz * seq, D_S5), F32),
                   jax.ShapeDtypeStruct((bsz * seq, D_GDN), F32),
                   jax.ShapeDtypeStruct((bsz, GDN_HEADS, GDN_HEAD_DIM, GDN_HEAD_DIM), F32),
                   jax.ShapeDtypeStruct((bsz, GDN_CONV - 1, 3 * D_GDN), F32)],
        scratch_shapes=[pltpu.VMEM((GDN_HEADS, GDN_HEAD_DIM, GDN_HEAD_DIM), F32),
                        proj_bufs(), proj_bufs(), handover(), handover()],
        compiler_params=pltpu.CompilerParams(dimension_semantics=("arbitrary",),
                                             vmem_limit_bytes=VMEM_LIMIT),
        name="proj_gdn_prompt",
    )(x, g1, w_in_bf, w_gate_bf, gw["conv_w"], gw["a_log_col"], gw["dt_bias_col"], gw["onorm_g"])


def _gdn_step_kernel(qkv_ref, st_ref, z_ref, ba_ref, s0_ref, convw_ref, alog_ref, dtb_ref, og_ref,
                     y_ref, s1_ref, *, bb):
    beta_all, g_all = _gates(ba_ref[...], alog_ref[...], dtb_ref[...])
    alpha_all = jnp.exp(g_all)
    scale = GDN_HEAD_DIM ** -0.5
    og = og_ref[...]

    for h in range(GDN_HEADS):
        def conv_cols(part, h=h):
            c0 = part * D_GDN + h * GDN_HEAD_DIM
            cs = slice(c0, c0 + GDN_HEAD_DIM)
            acc = qkv_ref[:, cs] * convw_ref[GDN_CONV - 1:GDN_CONV, cs]
            for j in range(GDN_CONV - 1):
                acc = acc + st_ref[j, :, cs] * convw_ref[j:j + 1, cs]
            return _silu(acc)

        q = _l2norm(conv_cols(0)) * scale
        k = _l2norm(conv_cols(1))
        v = conv_cols(2)
        beta = beta_all[:, h:h + 1]
        alpha = alpha_all[:, GDN_HEADS + h:GDN_HEADS + h + 1]
        qk = jnp.sum(q * k, axis=-1, keepdims=True)
        kt = k.T
        kq = jnp.concatenate([k, q], axis=0).astype(BF16)
        o_rows = []
        for b in range(bb):
            s0 = s0_ref[b, h]
            kcol = kt[:, b:b + 1]
            r = _dot(kq, s0.astype(BF16))
            ks = r[b:b + 1, :]
            qs = r[bb + b:bb + b + 1, :]
            al = alpha[b:b + 1, :]
            v_new = beta[b:b + 1, :] * (v[b:b + 1, :] - al * ks)
            o_rows.append(al * qs + qk[b:b + 1, :] * v_new)
            s1_ref[b, h] = al * s0 + kcol * v_new
        o = jnp.concatenate(o_rows, axis=0)
        cs = slice(h * GDN_HEAD_DIM, (h + 1) * GDN_HEAD_DIM)
        y_ref[:, cs] = _out_gate(o, og, z_ref[:, cs])


def _gdn_sample(qkv, st_t, z, ba, s0, gw, bb):
    n = qkv.shape[0]
    row = lambda i: (i, 0)
    const = lambda i: (0, 0)
    state = lambda i: (i, 0, 0, 0)
    return pl.pallas_call(
        functools.partial(_gdn_step_kernel, bb=bb),
        grid=(n // bb,),
        in_specs=[pl.BlockSpec((bb, 3 * D_GDN), row),
                  pl.BlockSpec((GDN_CONV - 1, bb, 3 * D_GDN), lambda i: (0, i, 0)),
                  pl.BlockSpec((bb, D_GDN), row),
                  pl.BlockSpec((bb, GATE_PAD), row),
                  pl.BlockSpec((bb, GDN_HEADS, GDN_HEAD_DIM, GDN_HEAD_DIM), state),
                  pl.BlockSpec((GDN_CONV, 3 * D_GDN), const),
                  pl.BlockSpec((1, GATE_PAD), const),
                  pl.BlockSpec((1, GATE_PAD), const),
                  pl.BlockSpec((1, GDN_HEAD_DIM), const)],
        out_specs=[pl.BlockSpec((bb, D_GDN), row),
                   pl.BlockSpec((bb, GDN_HEADS, GDN_HEAD_DIM, GDN_HEAD_DIM), state)],
        out_shape=[jax.ShapeDtypeStruct((n, D_GDN), F32),
                   jax.ShapeDtypeStruct(s0.shape, F32)],
        compiler_params=pltpu.CompilerParams(dimension_semantics=("arbitrary",),
                                             vmem_limit_bytes=VMEM_LIMIT),
        name="gdn_sample",
    )(qkv, st_t, z, ba, s0, gw["conv_w"], gw["a_log"], gw["dt_bias"], gw["onorm_g"])


def _residual_in(x_ref, ys5_ref, ygdn_ref, wout_ref, g2_ref, x1_s, n2_s):
    x1 = (x_ref[...] + _dot(ys5_ref[...].astype(BF16), wout_ref[:D_S5, :])
          + _dot(ygdn_ref[...].astype(BF16), wout_ref[D_S5:, :]))
    x1_s[...] = x1
    n2_s[...] = _rms(x1, g2_ref[...]).astype(BF16)


def _ffn_prompt_kernel(x_ref, ys5_ref, ygdn_ref, wout_ref, g2_ref, wup_ref, cw_ref, wdn_ref, gf_ref,
                       y_ref, hlast_ref, x1_s, n2_s, act_s, carry_s, *, tm):
    ti = pl.program_id(1)
    nt = pl.num_programs(1)

    @pl.when(ti == 0)
    def _reset():
        carry_s[...] = jnp.zeros_like(carry_s)

    _residual_in(x_ref, ys5_ref, ygdn_ref, wout_ref, g2_ref, x1_s, n2_s)
    n2 = n2_s[...]
    for f in range(D_FF // FFN_COLS):
        halves = []
        for part in range(2):
            c0 = part * D_FF + f * FFN_COLS
            cs = slice(c0, c0 + FFN_COLS)
            hcur = _dot(n2, wup_ref[:, cs])
            ext = jnp.concatenate([carry_s[:, cs], hcur], axis=0)
            carry_s[:, cs] = hcur[tm - 8:, :]
            conv = (_rows_back(ext, 2, tm) * cw_ref[0:1, cs]
                    + _rows_back(ext, 1, tm) * cw_ref[1:2, cs]
                    + hcur * cw_ref[2:3, cs])
            halves.append(conv)
        act_s[:, f * FFN_COLS:(f + 1) * FFN_COLS] = (_silu(halves[0]) * halves[1]).astype(BF16)
    y_ref[...] = _rms(x1_s[...] + _dot(act_s[...], wdn_ref[...]), gf_ref[...])

    @pl.when(ti == nt - 1)
    def _fin():
        hlast_ref[0] = carry_s[8 - (FFN_CONV - 1):, :]


def _ffn_prompt(x, ys5, ygdn, fw, bsz, seq, tm):
    nt = seq // tm
    row = lambda b, i: (b * nt + i, 0)
    const = lambda b, i: (0, 0)
    resident = lambda shape: pl.BlockSpec(shape, const, pipeline_mode=pl.Buffered(1))
    return pl.pallas_call(
        functools.partial(_ffn_prompt_kernel, tm=tm),
        grid=(bsz, nt),
        in_specs=[pl.BlockSpec((tm, D_MODEL), row),
                  pl.BlockSpec((tm, D_S5), row),
                  pl.BlockSpec((tm, D_GDN), row),
                  resident((D_MODEL, D_MODEL)),
                  resident((1, D_MODEL)),
                  resident((D_MODEL, 2 * D_FF)),
                  resident((FFN_CONV, 2 * D_FF)),
                  resident((D_FF, D_MODEL)),
                  resident((1, D_MODEL))],
        out_specs=[pl.BlockSpec((tm, D_MODEL), row),
                   pl.BlockSpec((1, FFN_CONV - 1, 2 * D_FF), lambda b, i: (b, 0, 0))],
        out_shape=[jax.ShapeDtypeStruct((bsz * seq, D_MODEL), F32),
                   jax.ShapeDtypeStruct((bsz, FFN_CONV - 1, 2 * D_FF), F32)],
        scratch_shapes=[pltpu.VMEM((tm, D_MODEL), F32),
                        pltpu.VMEM((tm, D_MODEL), BF16),
                        pltpu.VMEM((tm, D_FF), BF16),
                        pltpu.VMEM((8, 2 * D_FF), F32)],
        compiler_params=pltpu.CompilerParams(dimension_semantics=("arbitrary", "arbitrary"),
                                             vmem_limit_bytes=VMEM_LIMIT),
        name="ffn_prompt",
    )(x, ys5, ygdn, fw["w_out"], fw["norm2_g"], fw["w_up"], fw["conv_w"], fw["w_down"], fw["normf_g"])


def _ffn_sample_kernel(x_ref, ys5_ref, ygdn_ref, st_ref, wout_ref, g2_ref, wup_ref, cw_ref, wdn_ref, gf_ref,
                       y_ref, h_ref, x1_s, n2_s, acc_s):
    _residual_in(x_ref, ys5_ref, ygdn_ref, wout_ref, g2_ref, x1_s, n2_s)
    n2 = n2_s[...]
    for f in range(D_FF // FFN_COLS):
        halves = []
        for part in range(2):
            c0 = part * D_FF + f * FFN_COLS
            cs = slice(c0, c0 + FFN_COLS)
            hcur = _dot(n2, wup_ref[:, cs])
            h_ref[:, cs] = hcur
            halves.append(st_ref[0, :, cs] * cw_ref[0:1, cs] + st_ref[1, :, cs] * cw_ref[1:2, cs]
                          + hcur * cw_ref[2:3, cs])
        act = (_silu(halves[0]) * halves[1]).astype(BF16)
        contrib = _dot(act, wdn_ref[f * FFN_COLS:(f + 1) * FFN_COLS, :])
        if f == 0:
            acc_s[...] = contrib
        else:
            acc_s[...] += contrib
    y_ref[...] = _rms(x1_s[...] + acc_s[...], gf_ref[...])


def _ffn_sample(x, ys5, ygdn, st_t, fw):
    n = x.shape[0]
    c2 = lambda i: (0, 0)
    c3 = lambda i: (0, 0, 0)
    return pl.pallas_call(
        _ffn_sample_kernel,
        grid=(1,),
        in_specs=[pl.BlockSpec((n, D_MODEL), c2),
                  pl.BlockSpec((n, D_S5), c2),
                  pl.BlockSpec((n, D_GDN), c2),
                  pl.BlockSpec((FFN_CONV - 1, n, 2 * D_FF), c3),
                  pl.BlockSpec((D_MODEL, D_MODEL), c2),
                  pl.BlockSpec((1, D_MODEL), c2),
                  pl.BlockSpec((D_MODEL, 2 * D_FF), c2),
                  pl.BlockSpec((FFN_CONV, 2 * D_FF), c2),
                  pl.BlockSpec((D_FF, D_MODEL), c2),
                  pl.BlockSpec((1, D_MODEL), c2)],
        out_specs=[pl.BlockSpec((n, D_MODEL), c2),
                   pl.BlockSpec((n, 2 * D_FF), c2)],
        out_shape=[jax.ShapeDtypeStruct((n, D_MODEL), F32),
                   jax.ShapeDtypeStruct((n, 2 * D_FF), F32)],
        scratch_shapes=[pltpu.VMEM((n, D_MODEL), F32),
                        pltpu.VMEM((n, D_MODEL), BF16),
                        pltpu.VMEM((n, D_MODEL), F32)],
        compiler_params=pltpu.CompilerParams(dimension_semantics=("arbitrary",),
                                             vmem_limit_bytes=VMEM_LIMIT),
        name="ffn_sample",
    )(x, ys5, ygdn, st_t, fw["w_out"], fw["norm2_g"], fw["w_up"], fw["conv_w"], fw["w_down"], fw["normf_g"])


PROJ_ROWS = 512
S5_STEPS = 128
GDN_ROWS = 512
FFN_ROWS = 512
GDN_SAMPLE_ROWS = 8


def kernel(x_prompt, x_sample, state_s5_re, state_s5_im, state_gdn, state_gdn_conv, state_ffn_conv, norm1_g, w_in, s5_a_re, s5_a_im, s5_log_dt, s5_b_re, s5_b_im, s5_c_re, s5_c_im, s5_d, s5_w_glu, gdn_conv_w, gdn_a_log, gdn_dt_bias, gdn_onorm_g, w_out, norm2_g, ffn_w_up, ffn_conv_w, ffn_w_down, normf_g):
    depth = w_in.shape[0]
    assert depth == 1, "the final rmsnorm is fused into the last layer's ffn kernel"
    bsz, seq, _ = x_prompt.shape
    nsmp = x_sample.shape[0]
    assert x_sample.shape[1] == 1
    l = 0

    xp = x_prompt.reshape(bsz * seq, D_MODEL)
    xs = x_sample.reshape(nsmp, D_MODEL)

    g1 = norm1_g[l].reshape(1, D_MODEL)
    w_in_bf = w_in[l][:, :D_IN_MAIN].astype(BF16)
    w_gate_bf = jnp.pad(w_in[l][:, D_IN_MAIN:], ((0, 0), (0, GATE_PAD - 2 * GDN_HEADS))).astype(BF16)
    sw = _s5_weights(s5_a_re[l], s5_a_im[l], s5_log_dt[l], s5_b_re[l], s5_b_im[l], s5_c_re[l], s5_c_im[l],
                     s5_d[l], s5_w_glu[l])
    gate_row = lambda v: jnp.pad(v, (GDN_HEADS, GATE_PAD - 2 * GDN_HEADS)).reshape(1, GATE_PAD)
    gate_col = lambda v: jnp.broadcast_to(jnp.pad(v, (GDN_HEADS, 0))[:, None], (2 * GDN_HEADS, GATE_PAD))
    gw = {"conv_w": gdn_conv_w[l], "a_log": gate_row(gdn_a_log[l]), "dt_bias": gate_row(gdn_dt_bias[l]),
          "a_log_col": gate_col(gdn_a_log[l]), "dt_bias_col": gate_col(gdn_dt_bias[l]),
          "onorm_g": gdn_onorm_g[l].reshape(1, GDN_HEAD_DIM)}
    fw = {"w_out": w_out[l].astype(BF16), "norm2_g": norm2_g[l].reshape(1, D_MODEL),
          "w_up": ffn_w_up[l].astype(BF16), "conv_w": ffn_conv_w[l], "w_down": ffn_w_down[l].astype(BF16),
          "normf_g": normf_g.reshape(1, D_MODEL)}

    u_p, ygdn_p, p_gdn, p_gdn_conv = _gdn_prompt(xp, g1, w_in_bf, w_gate_bf, gw, bsz, seq, GDN_ROWS)
    zeros_h = jnp.zeros((bsz, S5_CH), F32)
    ys5_p, p_hr, p_hi = _s5(u_p.reshape(bsz, seq, D_S5), zeros_h, zeros_h, sw, bsz, S5_STEPS, True)
    ys5_p = ys5_p.reshape(bsz * seq, D_S5)
    y_p, p_ffn_conv = _ffn_prompt(xp, ys5_p, ygdn_p, fw, bsz, seq, FFN_ROWS)

    u_s, qkv_s, z_s, ba_s = _proj(xs, g1, w_in_bf, w_gate_bf, nsmp)
    ys5_s, s_hr, s_hi = _s5(u_s, state_s5_re[l].reshape(nsmp, S5_CH), state_s5_im[l].reshape(nsmp, S5_CH),
                            sw, nsmp, 1, False)
    gconv_t = state_gdn_conv[l].transpose(1, 0, 2)
    ygdn_s, s_gdn = _gdn_sample(qkv_s, gconv_t, z_s, ba_s, state_gdn[l], gw, GDN_SAMPLE_ROWS)
    fconv_t = state_ffn_conv[l].transpose(1, 0, 2)
    y_s, h_s = _ffn_sample(xs, ys5_s, ygdn_s, fconv_t, fw)
    s_gdn_conv = jnp.concatenate([state_gdn_conv[l][:, 1:], qkv_s[:, None, :]], axis=1)
    s_ffn_conv = jnp.concatenate([state_ffn_conv[l][:, 1:], h_s[:, None, :]], axis=1)

    st = lambda a: a[None]
    s5_shape = lambda a, n: a.reshape(1, n, S5_N_GROUPS, S5_STATE)
    return (y_p.reshape(bsz, seq, D_MODEL), y_s.reshape(nsmp, 1, D_MODEL),
            s5_shape(p_hr, bsz), s5_shape(p_hi, bsz), st(p_gdn), st(p_gdn_conv), st(p_ffn_conv),
            s5_shape(s_hr, nsmp), s5_shape(s_hi, nsmp), st(s_gdn), st(s_gdn_conv), st(s_ffn_conv))
```

```python
import functools

import jax
import jax.numpy as jnp
from jax import lax
from jax.experimental import pallas as pl
from jax.experimental.pallas import tpu as pltpu

F32 = jnp.float32
BF16 = jnp.bfloat16
NORM_EPS = 1e-6

D_MODEL = 1024
D_S5 = 512
S5_GROUP = 16
S5_N_GROUPS = 32
S5_STATE = 64
S5_CH = S5_N_GROUPS * S5_STATE
D_GDN = 512
GDN_HEADS = 4
GDN_HEAD_DIM = 128
GDN_CONV = 4
D_FF = 2816
FFN_CONV = 3
D_IN = D_S5 + 4 * D_GDN + 2 * GDN_HEADS
GATE_PAD = 128
D_IN_MAIN = D_S5 + 4 * D_GDN

CHUNK = 64
FFN_COLS = 256
VMEM_LIMIT = 56 * 1024 * 1024


def _dot(a, b):
    return jnp.dot(a, b, preferred_element_type=F32)


def _dot_nt(a, b):
    return lax.dot_general(a, b, (((1,), (1,)), ((), ())), preferred_element_type=F32)


def _rms(x, g):
    ms = jnp.mean(x * x, axis=-1, keepdims=True)
    return x * lax.rsqrt(ms + NORM_EPS) * g


def _silu(x):
    return x * jax.nn.sigmoid(x)


def _rows_back(ext, back, n):
    return pltpu.roll(ext, back, 0)[8:8 + n]


def _softplus(x):
    return jnp.maximum(x, 0.0) + jnp.log1p(jnp.exp(-jnp.abs(x)))


def _proj_kernel(x_ref, g_ref, w_ref, wgate_ref, u_ref, qkv_ref, z_ref, ba_ref):
    n = _rms(x_ref[...], g_ref[...]).astype(BF16)
    o0 = D_S5
    o1 = o0 + 3 * D_GDN
    u_ref[...] = _dot(n, w_ref[:, :o0])
    qkv_ref[...] = _dot(n, w_ref[:, o0:o1])
    z_ref[...] = _dot(n, w_ref[:, o1:])
    ba_ref[...] = _dot(n, wgate_ref[...])


def _proj(x, g1, w_in_bf, w_gate_bf, tm):
    n = x.shape[0]
    row = lambda i: (i, 0)
    const = lambda i: (0, 0)
    return pl.pallas_call(
        _proj_kernel,
        grid=(n // tm,),
        in_specs=[pl.BlockSpec((tm, D_MODEL), row),
                  pl.BlockSpec((1, D_MODEL), const),
                  pl.BlockSpec((D_MODEL, D_IN_MAIN), const),
                  pl.BlockSpec((D_MODEL, GATE_PAD), const)],
        out_specs=[pl.BlockSpec((tm, D_S5), row),
                   pl.BlockSpec((tm, 3 * D_GDN), row),
                   pl.BlockSpec((tm, D_GDN), row),
                   pl.BlockSpec((tm, GATE_PAD), row)],
        out_shape=[jax.ShapeDtypeStruct((n, D_S5), F32),
                   jax.ShapeDtypeStruct((n, 3 * D_GDN), F32),
                   jax.ShapeDtypeStruct((n, D_GDN), F32),
                   jax.ShapeDtypeStruct((n, GATE_PAD), F32)],
        compiler_params=pltpu.CompilerParams(dimension_semantics=("arbitrary",),
                                             vmem_limit_bytes=VMEM_LIMIT),
        name="proj",
    )(x, g1, w_in_bf, w_gate_bf)


S5_HALF = S5_CH // 2
S5_COLS = 1024
S5_SCAN_GROUP = 8
S5_PARTS = 4


def _s5_kernel(u_ref, h0r_ref, h0i_ref, are_ref, aim_ref, ldt_ref, bre_ref, bim_ref, cre_ref, cim_ref,
               d_ref, wglu_ref, y_ref, hr_out, hi_out,
               abr_s, abi_s, wbr_s, wbi_s, hr_s, hi_s, xr_s, xi_s, y_s, *, nb, tt, batch_major):
    @pl.when(pl.program_id(0) == 0)
    def _init():
        ar = are_ref[...]
        ai = aim_ref[...]
        dt = jnp.exp(ldt_ref[...])
        mag = jnp.exp(ar * dt)
        abr = mag * jnp.cos(ai * dt)
        abi = mag * jnp.sin(ai * dt)
        den = ar * ar + ai * ai
        p = abr - 1.0
        fr = (p * ar + abi * ai) / den
        fi = (abi * ar - p * ai) / den
        abr_s[...] = abr
        abi_s[...] = abi
        for m in range(2):
            frm = fr[:, m * S5_HALF:(m + 1) * S5_HALF]
            fim = fi[:, m * S5_HALF:(m + 1) * S5_HALF]
            wr = bre_ref[m]
            wi = bim_ref[m]
            wbr_s[m] = (wr * frm - wi * fim).astype(BF16)
            wbi_s[m] = (wr * fim + wi * frm).astype(BF16)
        hr_s[...] = h0r_ref[...]
        hi_s[...] = h0i_ref[...]

    if batch_major:
        u = jnp.swapaxes(u_ref[...], 0, 1).reshape(tt * nb, D_S5)
    else:
        u = u_ref[...]
    ub = u.astype(BF16)
    half = D_S5 // 2

    nparts = S5_PARTS if tt % (S5_PARTS * S5_SCAN_GROUP) == 0 else 1
    tp = tt // nparts
    part_rows = lambda p: slice(p * tp * nb, (p + 1) * tp * nb)

    blk = 256

    def in_proj(p):
        def block(m, ws, xs, c0):
            def item():
                xs[part_rows(p), m * S5_HALF + c0:m * S5_HALF + c0 + blk] = _dot(
                    ub[part_rows(p), m * half:(m + 1) * half], ws[m, :, c0:c0 + blk])
            return item
        return [block(m, ws, xs, c0) for m in range(2) for ws, xs in ((wbr_s, xr_s), (wbi_s, xi_s))
                for c0 in range(0, S5_HALF, blk)]

    def scan(p):
        def group(cb, t0, steps):
            def item():
                cols = slice(cb * S5_COLS, (cb + 1) * S5_COLS)
                a_r = jnp.broadcast_to(abr_s[:, cols], (nb, S5_COLS))
                a_i = jnp.broadcast_to(abi_s[:, cols], (nb, S5_COLS))
                hr = hr_s[:, cols]
                hi = hi_s[:, cols]
                for t in range(t0, t0 + steps):
                    rows = slice(t * nb, (t + 1) * nb)
                    hr, hi = (a_r * hr - a_i * hi + xr_s[rows, cols],
                              a_r * hi + a_i * hr + xi_s[rows, cols])
                    xr_s[rows, cols] = hr
                    xi_s[rows, cols] = hi
                hr_s[:, cols] = hr
                hi_s[:, cols] = hi
            return item
        steps = min(tp, S5_SCAN_GROUP)
        return [group(cb, t0, steps) for t0 in range(p * tp, (p + 1) * tp, steps)
                for cb in range(S5_CH // S5_COLS)]

    def out_proj(p):
        ys = {}

        def re_block(m):
            def item():
                hre = xr_s[part_rows(p), m * S5_HALF:(m + 1) * S5_HALF].astype(BF16)
                ys[m] = (_dot(hre, cre_ref[m])
                         + d_ref[:, m * half:(m + 1) * half] * u[part_rows(p), m * half:(m + 1) * half])
            return item

        def im_block(m):
            def item():
                him = xi_s[part_rows(p), m * S5_HALF:(m + 1) * S5_HALF].astype(BF16)
                ys[m] = ys[m] - _dot(him, cim_ref[m])
            return item

        def act():
            ys["act"] = jax.nn.gelu(jnp.concatenate([ys[0], ys[1]], axis=-1)).astype(BF16)

        def glu(c0):
            def item():
                val = _dot(ys["act"], wglu_ref[:, c0:c0 + blk])
                gate = _dot(ys["act"], wglu_ref[:, D_S5 + c0:D_S5 + c0 + blk])
                y_s[part_rows(p), c0:c0 + blk] = val * jax.nn.sigmoid(gate)
            return item
        return ([re_block(0), im_block(0), re_block(1), im_block(1), act]
                + [glu(c0) for c0 in range(0, D_S5, blk)])

    order = in_proj(0)
    for p in range(nparts):
        nxt = in_proj(p + 1) if p + 1 < nparts else []
        prv = out_proj(p - 1) if p > 0 else []
        order += _interleave(nxt + prv, scan(p)) if (nxt or prv) else scan(p)
    order += out_proj(nparts - 1)
    for item in order:
        item()

    hr_out[...] = hr_s[...]
    hi_out[...] = hi_s[...]
    if batch_major:
        y_ref[...] = jnp.swapaxes(y_s[...].reshape(tt, nb, D_S5), 0, 1)
    else:
        y_ref[...] = y_s[...]


def _s5(u, h0r, h0i, sw, nb, tt, batch_major):
    rows = nb * tt
    const2 = lambda i: (0, 0)
    const3 = lambda i: (0, 0, 0)
    full2 = lambda shape: pl.BlockSpec(shape, const2)
    full3 = lambda shape: pl.BlockSpec(shape, const3)
    if batch_major:
        steps = u.shape[1] // tt
        io_spec = pl.BlockSpec((nb, tt, D_S5), lambda i: (0, i, 0))
    else:
        steps = u.shape[0] // rows
        io_spec = pl.BlockSpec((rows, D_S5), lambda i: (i, 0))
    return pl.pallas_call(
        functools.partial(_s5_kernel, nb=nb, tt=tt, batch_major=batch_major),
        grid=(steps,),
        in_specs=[io_spec,
                  full2((nb, S5_CH)), full2((nb, S5_CH)),
                  full2((1, S5_CH)), full2((1, S5_CH)), full2((1, S5_CH)),
                  full3((2, D_S5 // 2, S5_HALF)), full3((2, D_S5 // 2, S5_HALF)),
                  full3((2, S5_HALF, D_S5 // 2)), full3((2, S5_HALF, D_S5 // 2)),
                  full2((1, D_S5)), full2((D_S5, 2 * D_S5))],
        out_specs=[io_spec, full2((nb, S5_CH)), full2((nb, S5_CH))],
        out_shape=[jax.ShapeDtypeStruct(u.shape, F32),
                   jax.ShapeDtypeStruct((nb, S5_CH), F32),
                   jax.ShapeDtypeStruct((nb, S5_CH), F32)],
        scratch_shapes=[pltpu.VMEM((1, S5_CH), F32), pltpu.VMEM((1, S5_CH), F32),
                        pltpu.VMEM((2, D_S5 // 2, S5_HALF), BF16), pltpu.VMEM((2, D_S5 // 2, S5_HALF), BF16),
                        pltpu.VMEM((nb, S5_CH), F32), pltpu.VMEM((nb, S5_CH), F32),
                        pltpu.VMEM((rows, S5_CH), F32), pltpu.VMEM((rows, S5_CH), F32),
                        pltpu.VMEM((rows, D_S5), F32)],
        compiler_params=pltpu.CompilerParams(dimension_semantics=("arbitrary",),
                                             vmem_limit_bytes=VMEM_LIMIT),
        name="s5",
    )(u, h0r, h0i, sw["a_re"], sw["a_im"], sw["log_dt"], sw["b_re"], sw["b_im"],
      sw["c_re"], sw["c_im"], sw["d"], sw["w_glu"])


def _s5_weights(a_re, a_im, log_dt, b_re, b_im, c_re, c_im, d, w_glu):
    g, n, c = S5_N_GROUPS, S5_STATE, S5_GROUP
    gh = g // 2

    def blockdiag(w, rows_per_group, cols_per_group):
        x = jnp.transpose(w, (0, 2, 1)).reshape(2, gh * rows_per_group, cols_per_group)
        rg = jnp.arange(gh * rows_per_group)[:, None] // rows_per_group
        cg = jnp.arange(gh * cols_per_group)[None, :] // cols_per_group
        return jnp.where(rg == cg, jnp.tile(x, (1, 1, gh)), 0.0)

    b_blockdiag = lambda b: blockdiag(b, c, n)
    c_blockdiag = lambda cm: blockdiag(cm, n, c)

    return {
        "a_re": a_re.reshape(1, S5_CH), "a_im": a_im.reshape(1, S5_CH),
        "log_dt": jnp.repeat(log_dt, n).reshape(1, S5_CH),
        "b_re": b_blockdiag(b_re), "b_im": b_blockdiag(b_im),
        "c_re": c_blockdiag(c_re).astype(BF16), "c_im": c_blockdiag(c_im).astype(BF16),
        "d": d.reshape(1, D_S5), "w_glu": w_glu.astype(BF16),
    }


def _l2norm(x):
    return x * lax.rsqrt(jnp.sum(x * x, axis=-1, keepdims=True) + NORM_EPS)


def _gates(ba, alog_row, dtb_row):
    beta = jax.nn.sigmoid(ba)
    g = -jnp.exp(alog_row) * _softplus(ba + dtb_row)
    return beta, g


def _out_gate(o, og_row, z):
    return _rms(o, og_row) * _silu(z)


def _bdot(a, b):
    return jnp.einsum("bij,bjk->bik", a.astype(BF16), b.astype(BF16), preferred_element_type=F32)


def _tri_merge_level(n, lmat, ii, jj, shift):
    same_big = (ii >> (shift + 1)) == (jj >> (shift + 1))
    same_small = (ii >> shift) == (jj >> shift)
    c = jnp.where((same_big & jnp.logical_not(same_small))[None], lmat, 0.0)
    w = c + _bdot(n, c)
    return n - (w + _bdot(w, n))


def _interleave(*stages):
    out = []
    pos = [0] * len(stages)
    while any(p < len(st) for p, st in zip(pos, stages)):
        k = min((i for i, st in enumerate(stages) if pos[i] < len(st)),
                key=lambda i: (pos[i] + 0.5) / len(stages[i]))
        out.append(stages[k][pos[k]])
        pos[k] += 1
    return out


def _gdn_kernel(x_ref, g1_ref, win_ref, wgate_ref, convw_ref, alog_ref, dtb_ref, og_ref,
                u_ref, y_ref, sfin_ref, convout_ref, s_s, proj_a, proj_b, set_a, set_b, *, tg, nt):
    s = pl.program_id(0)

    @pl.when(s == 0)
    def _zero_buffers():
        for ref in (*proj_b, *set_a):
            ref[...] = jnp.zeros_like(ref)

    @pl.when(lax.rem(jnp.maximum(s - 1, 0), nt) == 0)
    def _new_sequence_conv():
        for ext in (proj_a[0], proj_b[0]):
            ext[0:8, :] = jnp.zeros((8, 3 * D_GDN), F32)

    @pl.when(lax.rem(jnp.maximum(s - 2, 0), nt) == 0)
    def _new_sequence_state():
        s_s[...] = jnp.zeros_like(s_s)

    args = (x_ref, g1_ref, win_ref, wgate_ref, convw_ref, alog_ref, dtb_ref, og_ref, u_ref, y_ref, sfin_ref, convout_ref, s_s)

    @pl.when(lax.rem(s, 2) == 0)
    def _even():
        _gdn_step(*args, proj_wr=proj_a, proj_rd=proj_b, wr=set_b, rd=set_a, tg=tg)

    @pl.when(lax.rem(s, 2) == 1)
    def _odd():
        _gdn_step(*args, proj_wr=proj_b, proj_rd=proj_a, wr=set_a, rd=set_b, tg=tg)


def _gdn_step(x_ref, g1_ref, win_ref, wgate_ref, convw_ref, alog_ref, dtb_ref, og_ref, u_ref, y_ref, sfin_ref, convout_ref,
              s_s, *, proj_wr, proj_rd, wr, rd, tg):
    nc = tg // CHUNK
    nsys = nc * GDN_HEADS
    hcols = lambda h: slice(h * GDN_HEAD_DIM, (h + 1) * GDN_HEAD_DIM)
    og = og_ref[...]
    v = {}
    stage_a, stage_b, stage_c = [], [], []

    ext_w, z_w, ba_w = proj_wr
    o0 = D_S5
    o1 = o0 + 3 * D_GDN
    o2 = o1 + D_GDN

    def a_norm():
        v["n1"] = _rms(x_ref[...], g1_ref[...]).astype(BF16)
    stage_a.append(a_norm)

    def a_block(dst, rows, dst_c0, src_c0, width):
        def item():
            dst[rows, dst_c0:dst_c0 + width] = _dot(v["n1"], win_ref[:, src_c0:src_c0 + width])
        return item
    blk = 256
    for c0 in range(0, o0, blk):
        stage_a.append(a_block(u_ref, slice(None), c0, c0, blk))
    for c0 in range(0, o1 - o0, blk):
        stage_a.append(a_block(ext_w, slice(8, None), c0, o0 + c0, blk))
    for c0 in range(0, o2 - o1, blk):
        stage_a.append(a_block(z_w, slice(None), c0, o1 + c0, blk))

    def a_gates():
        ba_w[...] = _dot(v["n1"], wgate_ref[...])
    stage_a.append(a_gates)

    aq_r, b_r, o_r, gl_r, zs_r = rd
    states = [None] * GDN_HEADS

    def c_chunk(c):
        def item():
            rows = slice(c * CHUNK, (c + 1) * CHUNK)
            for h in range(GDN_HEADS):
                sys = c * GDN_HEADS + h
                st = s_s[h] if c == 0 else states[h]
                r = _dot(aq_r[sys], st.astype(BF16))
                o = r[GDN_HEAD_DIM:] + o_r[sys]
                states[h] = st * gl_r[sys, 0:1, :] + r[:GDN_HEAD_DIM] + b_r[sys]
                y_ref[rows, hcols(h)] = _rms(o, og) * zs_r[rows, hcols(h)]
                if c == nc - 1:
                    s_s[h] = states[h]
                    sfin_ref[0, h] = states[h]
        return item
    stage_c.extend(c_chunk(c) for c in range(nc))

    aq_s, b_s, o_s, gl_s, zs_s = wr
    ext_s, z_r, ba_r = proj_rd
    ii = lax.broadcasted_iota(jnp.int32, (CHUNK, CHUNK), 0)
    jj = lax.broadcasted_iota(jnp.int32, (CHUNK, CHUNK), 1)
    causal = jj <= ii
    strict = jj < ii
    scale = GDN_HEAD_DIM ** -0.5
    lmats = [None] * nsys
    rhss = [None] * nsys
    kgt_attn = [None] * nsys
    qgs = [None] * nsys

    def b_gates():
        raw = ba_r[...].T[:2 * GDN_HEADS, :]
        lanes = lambda p: jnp.concatenate([p] * (tg // GATE_PAD), axis=1)
        beta = jax.nn.sigmoid(raw)
        g = -jnp.exp(lanes(alog_ref[...])) * _softplus(raw + lanes(dtb_ref[...]))
        g1 = g.astype(BF16).astype(F32)
        g2 = (g - g1).astype(BF16).astype(F32)
        g3 = (g - g1 - g2).astype(BF16).astype(F32)
        ri = lax.broadcasted_iota(jnp.int32, (tg, tg), 0)
        ci = lax.broadcasted_iota(jnp.int32, (tg, tg), 1)
        ubd = jnp.where(((ri >> 6) == (ci >> 6)) & (ri <= ci), 1.0, 0.0).astype(BF16)
        parts = _dot(jnp.concatenate([g1, g2, g3], axis=0).astype(BF16), ubd)
        nh2 = 2 * GDN_HEADS
        gc = parts[:nh2] + parts[nh2:2 * nh2] + parts[2 * nh2:]
        row_id = lax.broadcasted_iota(jnp.int32, (nh2, tg), 0)
        rows = jnp.where(row_id < GDN_HEADS, beta, gc)
        cols = jnp.concatenate([rows, jnp.zeros((GATE_PAD - nh2, tg), F32)], axis=0).T
        v["beta_all"] = cols
        v["gc_col"] = cols
        v["gc_row"] = rows
    stage_b.append(b_gates)

    def b_silu_z(h):
        def item():
            zs_s[:, hcols(h)] = _silu(z_r[:, hcols(h)])
        return item

    def b_conv(h, part, name, norm):
        def item():
            c0 = part * D_GDN + h * GDN_HEAD_DIM
            cs = slice(c0, c0 + GDN_HEAD_DIM)
            ext = ext_s[:, cs]
            acc = ext[8:] * convw_ref[GDN_CONV - 1:GDN_CONV, cs]
            for j in range(GDN_CONV - 1):
                acc = acc + _rows_back(ext, GDN_CONV - 1 - j, tg) * convw_ref[j:j + 1, cs]
            acc = _silu(acc)
            v[name] = norm(acc)
        return item

    def b_prep(h, c):
        def item():
            beta_all, gc_col, gc_row = v["beta_all"], v["gc_col"], v["gc_row"]
            sys = c * GDN_HEADS + h
            rows = slice(c * CHUNK, (c + 1) * CHUNK)
            q = v["q"][rows]
            k = v["k"][rows]
            vv = v["v"][rows]
            beta = beta_all[rows, h:h + 1]
            gcc = gc_col[rows, GDN_HEADS + h:GDN_HEADS + h + 1]
            gcr = gc_row[GDN_HEADS + h:GDN_HEADS + h + 1, rows]
            decay = jnp.exp(jnp.where(causal, gcc - gcr, -jnp.inf))
            kb = k * beta
            egc = jnp.exp(gcc)
            g_last = gcc[CHUNK - 1:CHUNK, :]
            kq = _dot_nt(jnp.concatenate([kb, q], axis=0).astype(BF16), k.astype(BF16))
            lmats[sys] = jnp.where(strict, kq[:CHUNK] * decay, 0.0)
            attn = jnp.where(causal, kq[CHUNK:] * decay, 0.0)
            rhss[sys] = jnp.concatenate([vv * beta, kb * egc], axis=1)
            kgt_attn[sys] = jnp.concatenate([(k * jnp.exp(g_last - gcc)).T, attn], axis=0)
            qgs[sys] = q * egc
            gl_s[sys] = jnp.broadcast_to(jnp.exp(g_last), (8, GDN_HEAD_DIM))
        return item

    for h in range(GDN_HEADS):
        stage_b.append(b_silu_z(h))
        stage_b.append(b_conv(h, 0, "q", lambda a: _l2norm(a) * scale))
        stage_b.append(b_conv(h, 1, "k", _l2norm))
        stage_b.append(b_conv(h, 2, "v", lambda a: a))
        stage_b.extend(b_prep(h, c) for c in range(nc))

    def b_inverse_start():
        lmat = jnp.stack(lmats)
        v["lmat"] = lmat
        v["noff"] = -jnp.where(((ii >> 1) == (jj >> 1))[None], lmat, 0.0)
    stage_b.append(b_inverse_start)

    def b_inverse_level(shift):
        def item():
            v["noff"] = _tri_merge_level(v["noff"], v["lmat"], ii, jj, shift)
        return item
    shift = 1
    while (1 << shift) < CHUNK:
        stage_b.append(b_inverse_level(shift))
        shift += 1

    def b_solve():
        rhs = jnp.stack(rhss)
        sol = rhs + _bdot(v["noff"], rhs)
        v["prod"] = _bdot(jnp.stack(kgt_attn), sol)
    stage_b.append(b_solve)

    def b_store():
        prod = v["prod"]
        dk = GDN_HEAD_DIM
        for sys in range(nsys):
            aq_s[sys, :dk, :] = (-prod[sys, :dk, dk:]).astype(BF16)
            aq_s[sys, dk:, :] = (qgs[sys] - prod[sys, dk:, dk:]).astype(BF16)
            b_s[sys] = prod[sys, :dk, :dk]
            o_s[sys] = prod[sys, dk:, :dk]
    stage_b.append(b_store)

    for item in _interleave(stage_a, stage_b, stage_c):
        item()

    ext_w[0:8, :] = ext_s[pl.ds(tg, 8), :]
    convout_ref[0] = ext_s[pl.ds(tg + 8 - (GDN_CONV - 1), GDN_CONV - 1), :]


def _gdn_prompt(x, g1, w_in_bf, w_gate_bf, gw, bsz, seq, tg):
    nt = seq // tg
    ntiles = bsz * nt
    nsys = (tg // CHUNK) * GDN_HEADS
    tile_a = lambda s: jnp.minimum(s, ntiles - 1)
    tile_b = lambda s: jnp.clip(s - 1, 0, ntiles - 1)
    tile_c = lambda s: jnp.maximum(s - 2, 0)
    const = lambda s: (0, 0)
    proj_bufs = lambda: [pltpu.VMEM((tg + 8, 3 * D_GDN), F32),
                         pltpu.VMEM((tg, D_GDN), F32),
                         pltpu.VMEM((tg, GATE_PAD), F32)]
    resident = lambda shape: pl.BlockSpec(shape, const, pipeline_mode=pl.Buffered(1))
    handover = lambda: [pltpu.VMEM((nsys, GDN_HEAD_DIM + CHUNK, GDN_HEAD_DIM), BF16),
                        pltpu.VMEM((nsys, GDN_HEAD_DIM, GDN_HEAD_DIM), F32),
                        pltpu.VMEM((nsys, CHUNK, GDN_HEAD_DIM), F32),
                        pltpu.VMEM((nsys, 8, GDN_HEAD_DIM), F32),
                        pltpu.VMEM((tg, D_GDN), F32)]
    return pl.pallas_call(
        functools.partial(_gdn_kernel, tg=tg, nt=nt),
        grid=(ntiles + 2,),
        in_specs=[pl.BlockSpec((tg, D_MODEL), lambda s: (tile_a(s), 0)),
                  resident((1, D_MODEL)),
                  resident((D_MODEL, D_IN_MAIN)),
                  resident((D_MODEL, GATE_PAD)),
                  resident((GDN_CONV, 3 * D_GDN)),
                  resident((2 * GDN_HEADS, GATE_PAD)),
                  resident((2 * GDN_HEADS, GATE_PAD)),
                  resident((1, GDN_HEAD_DIM))],
        out_specs=[pl.BlockSpec((tg, D_S5), lambda s: (tile_a(s), 0)),
                   pl.BlockSpec((tg, D_GDN), lambda s: (tile_c(s), 0)),
                   pl.BlockSpec((1, GDN_HEADS, GDN_HEAD_DIM, GDN_HEAD_DIM), lambda s: (tile_c(s) // nt, 0, 0, 0)),
                   pl.BlockSpec((1, GDN_CONV - 1, 3 * D_GDN), lambda s: (tile_b(s) // nt, 0, 0))],
        out_shape=[jax.ShapeDtypeStruct((bsz * seq, D_S5), F32),
                   jax.ShapeDtypeStruct((bsz * seq, D_GDN), F32),
                   jax.ShapeDtypeStruct((bsz, GDN_HEADS, GDN_HEAD_DIM, GDN_HEAD_DIM), F32),
                   jax.ShapeDtypeStruct((bsz, GDN_CONV - 1, 3 * D_GDN), F32)],
        scratch_shapes=[pltpu.VMEM((GDN_HEADS, GDN_HEAD_DIM, GDN_HEAD_DIM), F32),
                        proj_bufs(), proj_bufs(), handover(), handover()],
        compiler_params=pltpu.CompilerParams(dimension_semantics=("arbitrary",),
                                             vmem_limit_bytes=VMEM_LIMIT),
        name="proj_gdn_prompt",
    )(x, g1, w_in_bf, w_gate_bf, gw["conv_w"], gw["a_log_col"], gw["dt_bias_col"], gw["onorm_g"])


def _gdn_step_kernel(qkv_ref, st_ref, z_ref, ba_ref, s0_ref, convw_ref, alog_ref, dtb_ref, og_ref,
                     y_ref, s1_ref, *, bb):
    beta_all, g_all = _gates(ba_ref[...], alog_ref[...], dtb_ref[...])
    alpha_all = jnp.exp(g_all)
    scale = GDN_HEAD_DIM ** -0.5
    og = og_ref[...]

    for h in range(GDN_HEADS):
        def conv_cols(part, h=h):
            c0 = part * D_GDN + h * GDN_HEAD_DIM
            cs = slice(c0, c0 + GDN_HEAD_DIM)
            acc = qkv_ref[:, cs] * convw_ref[GDN_CONV - 1:GDN_CONV, cs]
            for j in range(GDN_CONV - 1):
                acc = acc + st_ref[j, :, cs] * convw_ref[j:j + 1, cs]
            return _silu(acc)

        q = _l2norm(conv_cols(0)) * scale
        k = _l2norm(conv_cols(1))
        v = conv_cols(2)
        beta = beta_all[:, h:h + 1]
        alpha = alpha_all[:, GDN_HEADS + h:GDN_HEADS + h + 1]
        qk = jnp.sum(q * k, axis=-1, keepdims=True)
        kt = k.T
        kq = jnp.concatenate([k, q], axis=0).astype(BF16)
        o_rows = []
        for b in range(bb):
            s0 = s0_ref[b, h]
            kcol = kt[:, b:b + 1]
            r = _dot(kq, s0.astype(BF16))
            ks = r[b:b + 1, :]
            qs = r[bb + b:bb + b + 1, :]
            al = alpha[b:b + 1, :]
            v_new = beta[b:b + 1, :] * (v[b:b + 1, :] - al * ks)
            o_rows.append(al * qs + qk[b:b + 1, :] * v_new)
            s1_ref[b, h] = al * s0 + kcol * v_new
        o = jnp.concatenate(o_rows, axis=0)
        cs = slice(h * GDN_HEAD_DIM, (h + 1) * GDN_HEAD_DIM)
        y_ref[:, cs] = _out_gate(o, og, z_ref[:, cs])


def _gdn_sample(qkv, st_t, z, ba, s0, gw, bb):
    n = qkv.shape[0]
    row = lambda i: (i, 0)
    const = lambda i: (0, 0)
    state = lambda i: (i, 0, 0, 0)
    return pl.pallas_call(
        functools.partial(_gdn_step_kernel, bb=bb),
        grid=(n // bb,),
        in_specs=[pl.BlockSpec((bb, 3 * D_GDN), row),
                  pl.BlockSpec((GDN_CONV - 1, bb, 3 * D_GDN), lambda i: (0, i, 0)),
                  pl.BlockSpec((bb, D_GDN), row),
                  pl.BlockSpec((bb, GATE_PAD), row),
                  pl.BlockSpec((bb, GDN_HEADS, GDN_HEAD_DIM, GDN_HEAD_DIM), state),
                  pl.BlockSpec((GDN_CONV, 3 * D_GDN), const),
                  pl.BlockSpec((1, GATE_PAD), const),
                  pl.BlockSpec((1, GATE_PAD), const),
                  pl.BlockSpec((1, GDN_HEAD_DIM), const)],
        out_specs=[pl.BlockSpec((bb, D_GDN), row),
                   pl.BlockSpec((bb, GDN_HEADS, GDN_HEAD_DIM, GDN_HEAD_DIM), state)],
        out_shape=[jax.ShapeDtypeStruct((n, D_GDN), F32),
                   jax.ShapeDtypeStruct(s0.shape, F32)],
        compiler_params=pltpu.CompilerParams(dimension_semantics=("arbitrary",),
                                             vmem_limit_bytes=VMEM_LIMIT),
        name="gdn_sample",
    )(qkv, st_t, z, ba, s0, gw["conv_w"], gw["a_log"], gw["dt_bias"], gw["onorm_g"])


def _residual_in(x_ref, ys5_ref, ygdn_ref, wout_ref, g2_ref, x1_s, n2_s):
    x1 = (x_ref[...] + _dot(ys5_ref[...].astype(BF16), wout_ref[:D_S5, :])
          + _dot(ygdn_ref[...].astype(BF16), wout_ref[D_S5:, :]))
    x1_s[...] = x1
    n2_s[...] = _rms(x1, g2_ref[...]).astype(BF16)


def _ffn_kernel(x_ref, ys5_ref, ygdn_ref, xs_ref, ys5s_ref, ygdns_ref, st_ref,
                wout_ref, g2_ref, wup_ref, cw_ref, wdn_ref, gf_ref,
                y_ref, hlast_ref, ysmp_ref, hsmp_ref, x1_s, n2_s, act_s, carry_s, *, tm, nt, ns):
    s = pl.program_id(0)
    last = pl.num_programs(0) - 1
    nf = D_FF // FFN_COLS

    @pl.when(s < last)
    def _prompt_tile():
        ti = lax.rem(s, nt)

        @pl.when(ti == 0)
        def _reset():
            carry_s[...] = jnp.zeros_like(carry_s)

        _residual_in(x_ref, ys5_ref, ygdn_ref, wout_ref, g2_ref, x1_s, n2_s)
        n2 = n2_s[...]
        for f in range(nf):
            halves = []
            for part in range(2):
                c0 = part * D_FF + f * FFN_COLS
                cs = slice(c0, c0 + FFN_COLS)
                hcur = _dot(n2, wup_ref[:, cs])
                ext = jnp.concatenate([carry_s[:, cs], hcur], axis=0)
                carry_s[:, cs] = hcur[tm - 8:, :]
                conv = (_rows_back(ext, 2, tm) * cw_ref[0:1, cs]
                        + _rows_back(ext, 1, tm) * cw_ref[1:2, cs]
                        + hcur * cw_ref[2:3, cs])
                halves.append(conv)
            act_s[:, f * FFN_COLS:(f + 1) * FFN_COLS] = (_silu(halves[0]) * halves[1]).astype(BF16)
        y_ref[...] = _rms(x1_s[...] + _dot(act_s[...], wdn_ref[...]), gf_ref[...])

        @pl.when(ti == nt - 1)
        def _fin():
            hlast_ref[0] = carry_s[8 - (FFN_CONV - 1):, :]

    @pl.when(s == last)
    def _sample_rows():
        x1_v, n2_v, act_v = x1_s.at[0:ns], n2_s.at[0:ns], act_s.at[0:ns]
        _residual_in(xs_ref, ys5s_ref, ygdns_ref, wout_ref, g2_ref, x1_v, n2_v)
        n2 = n2_v[...]
        for f in range(nf):
            halves = []
            for part in range(2):
                c0 = part * D_FF + f * FFN_COLS
                cs = slice(c0, c0 + FFN_COLS)
                hcur = _dot(n2, wup_ref[:, cs])
                hsmp_ref[:, cs] = hcur
                halves.append(st_ref[0, :, cs] * cw_ref[0:1, cs] + st_ref[1, :, cs] * cw_ref[1:2, cs]
                              + hcur * cw_ref[2:3, cs])
            act_v[:, f * FFN_COLS:(f + 1) * FFN_COLS] = (_silu(halves[0]) * halves[1]).astype(BF16)
        ysmp_ref[...] = _rms(x1_v[...] + _dot(act_v[...], wdn_ref[...]), gf_ref[...])


def _ffn(x, ys5, ygdn, xs, ys5s, ygdns, st_t, fw, bsz, seq, tm):
    nt = seq // tm
    ntiles = bsz * nt
    ns = xs.shape[0]
    tile = lambda s: jnp.minimum(s, ntiles - 1)
    row = lambda s: (tile(s), 0)
    const = lambda s: (0, 0)
    resident = lambda shape: pl.BlockSpec(shape, lambda s: (0,) * len(shape), pipeline_mode=pl.Buffered(1))
    return pl.pallas_call(
        functools.partial(_ffn_kernel, tm=tm, nt=nt, ns=ns),
        grid=(ntiles + 1,),
        in_specs=[pl.BlockSpec((tm, D_MODEL), row),
                  pl.BlockSpec((tm, D_S5), row),
                  pl.BlockSpec((tm, D_GDN), row),
                  resident((ns, D_MODEL)),
                  resident((ns, D_S5)),
                  resident((ns, D_GDN)),
                  resident((FFN_CONV - 1, ns, 2 * D_FF)),
                  resident((D_MODEL, D_MODEL)),
                  resident((1, D_MODEL)),
                  resident((D_MODEL, 2 * D_FF)),
                  resident((FFN_CONV, 2 * D_FF)),
                  resident((D_FF, D_MODEL)),
                  resident((1, D_MODEL))],
        out_specs=[pl.BlockSpec((tm, D_MODEL), row),
                   pl.BlockSpec((1, FFN_CONV - 1, 2 * D_FF), lambda s: (tile(s) // nt, 0, 0)),
                   pl.BlockSpec((ns, D_MODEL), const),
                   pl.BlockSpec((ns, 2 * D_FF), const)],
        out_shape=[jax.ShapeDtypeStruct((bsz * seq, D_MODEL), F32),
                   jax.ShapeDtypeStruct((bsz, FFN_CONV - 1, 2 * D_FF), F32),
                   jax.ShapeDtypeStruct((ns, D_MODEL), F32),
                   jax.ShapeDtypeStruct((ns, 2 * D_FF), F32)],
        scratch_shapes=[pltpu.VMEM((tm, D_MODEL), F32),
                        pltpu.VMEM((tm, D_MODEL), BF16),
                        pltpu.VMEM((tm, D_FF), BF16),
                        pltpu.VMEM((8, 2 * D_FF), F32)],
        compiler_params=pltpu.CompilerParams(dimension_semantics=("arbitrary",),
                                             vmem_limit_bytes=VMEM_LIMIT),
        name="ffn",
    )(x, ys5, ygdn, xs, ys5s, ygdns, st_t, fw["w_out"], fw["norm2_g"], fw["w_up"], fw["conv_w"], fw["w_down"],
      fw["normf_g"])


PROJ_ROWS = 512
S5_STEPS = 128
GDN_ROWS = 512
FFN_ROWS = 512
GDN_SAMPLE_ROWS = 8


def kernel(x_prompt, x_sample, state_s5_re, state_s5_im, state_gdn, state_gdn_conv, state_ffn_conv, norm1_g, w_in, s5_a_re, s5_a_im, s5_log_dt, s5_b_re, s5_b_im, s5_c_re, s5_c_im, s5_d, s5_w_glu, gdn_conv_w, gdn_a_log, gdn_dt_bias, gdn_onorm_g, w_out, norm2_g, ffn_w_up, ffn_conv_w, ffn_w_down, normf_g):
    depth = w_in.shape[0]
    assert depth == 1, "the final rmsnorm is fused into the last layer's ffn kernel"
    bsz, seq, _ = x_prompt.shape
    nsmp = x_sample.shape[0]
    assert x_sample.shape[1] == 1
    l = 0

    xp = x_prompt.reshape(bsz * seq, D_MODEL)
    xs = x_sample.reshape(nsmp, D_MODEL)

    g1 = norm1_g[l].reshape(1, D_MODEL)
    w_in_bf = w_in[l][:, :D_IN_MAIN].astype(BF16)
    w_gate_bf = jnp.pad(w_in[l][:, D_IN_MAIN:], ((0, 0), (0, GATE_PAD - 2 * GDN_HEADS))).astype(BF16)
    sw = _s5_weights(s5_a_re[l], s5_a_im[l], s5_log_dt[l], s5_b_re[l], s5_b_im[l], s5_c_re[l], s5_c_im[l],
                     s5_d[l], s5_w_glu[l])
    gate_row = lambda v: jnp.pad(v, (GDN_HEADS, GATE_PAD - 2 * GDN_HEADS)).reshape(1, GATE_PAD)
    gate_col = lambda v: jnp.broadcast_to(jnp.pad(v, (GDN_HEADS, 0))[:, None], (2 * GDN_HEADS, GATE_PAD))
    gw = {"conv_w": gdn_conv_w[l], "a_log": gate_row(gdn_a_log[l]), "dt_bias": gate_row(gdn_dt_bias[l]),
          "a_log_col": gate_col(gdn_a_log[l]), "dt_bias_col": gate_col(gdn_dt_bias[l]),
          "onorm_g": gdn_onorm_g[l].reshape(1, GDN_HEAD_DIM)}
    fw = {"w_out": w_out[l].astype(BF16), "norm2_g": norm2_g[l].reshape(1, D_MODEL),
          "w_up": ffn_w_up[l].astype(BF16), "conv_w": ffn_conv_w[l], "w_down": ffn_w_down[l].astype(BF16),
          "normf_g": normf_g.reshape(1, D_MODEL)}

    u_p, ygdn_p, p_gdn, p_gdn_conv = _gdn_prompt(xp, g1, w_in_bf, w_gate_bf, gw, bsz, seq, GDN_ROWS)
    zeros_h = jnp.zeros((bsz, S5_CH), F32)
    ys5_p, p_hr, p_hi = _s5(u_p.reshape(bsz, seq, D_S5), zeros_h, zeros_h, sw, bsz, S5_STEPS, True)
    ys5_p = ys5_p.reshape(bsz * seq, D_S5)

    u_s, qkv_s, z_s, ba_s = _proj(xs, g1, w_in_bf, w_gate_bf, nsmp)
    ys5_s, s_hr, s_hi = _s5(u_s, state_s5_re[l].reshape(nsmp, S5_CH), state_s5_im[l].reshape(nsmp, S5_CH),
                            sw, nsmp, 1, False)
    gconv_t = state_gdn_conv[l].transpose(1, 0, 2)
    ygdn_s, s_gdn = _gdn_sample(qkv_s, gconv_t, z_s, ba_s, state_gdn[l], gw, GDN_SAMPLE_ROWS)

    fconv_t = state_ffn_conv[l].transpose(1, 0, 2)
    y_p, p_ffn_conv, y_s, h_s = _ffn(xp, ys5_p, ygdn_p, xs, ys5_s, ygdn_s, fconv_t, fw, bsz, seq, FFN_ROWS)
    s_gdn_conv = jnp.concatenate([state_gdn_conv[l][:, 1:], qkv_s[:, None, :]], axis=1)
    s_ffn_conv = jnp.concatenate([state_ffn_conv[l][:, 1:], h_s[:, None, :]], axis=1)

    st = lambda a: a[None]
    s5_shape = lambda a, n: a.reshape(1, n, S5_N_GROUPS, S5_STATE)
    return (y_p.reshape(bsz, seq, D_MODEL), y_s.reshape(nsmp, 1, D_MODEL),
            s5_shape(p_hr, bsz), s5_shape(p_hi, bsz), st(p_gdn), st(p_gdn_conv), st(p_ffn_conv),
            s5_shape(s_hr, nsmp), s5_shape(s_hi, nsmp), st(s_gdn), st(s_gdn_conv), st(s_ffn_conv))
```

```python
import functools

import jax
import jax.numpy as jnp
from jax import lax
from jax.experimental import pallas as pl
from jax.experimental.pallas import tpu as pltpu

F32 = jnp.float32
BF16 = jnp.bfloat16
NORM_EPS = 1e-6

D_MODEL = 1024
D_S5 = 512
S5_GROUP = 16
S5_N_GROUPS = 32
S5_STATE = 64
S5_CH = S5_N_GROUPS * S5_STATE
D_GDN = 512
GDN_HEADS = 4
GDN_HEAD_DIM = 128
GDN_CONV = 4
D_FF = 2816
FFN_CONV = 3
D_IN = D_S5 + 4 * D_GDN + 2 * GDN_HEADS
GATE_PAD = 128
D_IN_MAIN = D_S5 + 4 * D_GDN

CHUNK = 64
FFN_COLS = 256
VMEM_LIMIT = 56 * 1024 * 1024
VMEM_LIMIT_SMALL = 24 * 1024 * 1024


def _dot(a, b):
    return jnp.dot(a, b, preferred_element_type=F32)


def _dot_nt(a, b):
    return lax.dot_general(a, b, (((1,), (1,)), ((), ())), preferred_element_type=F32)


def _rms(x, g):
    ms = jnp.mean(x * x, axis=-1, keepdims=True)
    return x * lax.rsqrt(ms + NORM_EPS) * g


def _silu(x):
    return x * jax.nn.sigmoid(x)


def _rows_back(ext, back, n):
    return pltpu.roll(ext, back, 0)[8:8 + n]


def _softplus(x):
    return jnp.maximum(x, 0.0) + jnp.log1p(jnp.exp(-jnp.abs(x)))


def _proj_kernel(x_ref, g_ref, w_ref, wgate_ref, u_ref, qkv_ref, z_ref, ba_ref):
    n = _rms(x_ref[...], g_ref[...]).astype(BF16)
    o0 = D_S5
    o1 = o0 + 3 * D_GDN
    u_ref[...] = _dot(n, w_ref[:, :o0])
    qkv_ref[...] = _dot(n, w_ref[:, o0:o1])
    z_ref[...] = _dot(n, w_ref[:, o1:])
    ba_ref[...] = _dot(n, wgate_ref[...])


def _proj(x, g1, w_in_bf, w_gate_bf, tm):
    n = x.shape[0]
    row = lambda i: (i, 0)
    const = lambda i: (0, 0)
    return pl.pallas_call(
        _proj_kernel,
        grid=(n // tm,),
        in_specs=[pl.BlockSpec((tm, D_MODEL), row),
                  pl.BlockSpec((1, D_MODEL), const),
                  pl.BlockSpec((D_MODEL, D_IN_MAIN), const),
                  pl.BlockSpec((D_MODEL, GATE_PAD), const)],
        out_specs=[pl.BlockSpec((tm, D_S5), row),
                   pl.BlockSpec((tm, 3 * D_GDN), row),
                   pl.BlockSpec((tm, D_GDN), row),
                   pl.BlockSpec((tm, GATE_PAD), row)],
        out_shape=[jax.ShapeDtypeStruct((n, D_S5), F32),
                   jax.ShapeDtypeStruct((n, 3 * D_GDN), F32),
                   jax.ShapeDtypeStruct((n, D_GDN), F32),
                   jax.ShapeDtypeStruct((n, GATE_PAD), F32)],
        compiler_params=pltpu.CompilerParams(dimension_semantics=("arbitrary",),
                                             vmem_limit_bytes=VMEM_LIMIT_SMALL),
        name="proj",
    )(x, g1, w_in_bf, w_gate_bf)


S5_HALF = S5_CH // 2
S5_COLS = 1024
S5_SCAN_GROUP = 8
S5_PARTS = 4


def _s5_kernel(u_ref, h0r_ref, h0i_ref, are_ref, aim_ref, ldt_ref, bre_ref, bim_ref, cre_ref, cim_ref,
               d_ref, wglu_ref, y_ref, hr_out, hi_out,
               abr_s, abi_s, wbr_s, wbi_s, hr_s, hi_s, xr_s, xi_s, y_s, *, nb, tt, batch_major):
    @pl.when(pl.program_id(0) == 0)
    def _init():
        ar = are_ref[...]
        ai = aim_ref[...]
        dt = jnp.exp(ldt_ref[...])
        mag = jnp.exp(ar * dt)
        abr = mag * jnp.cos(ai * dt)
        abi = mag * jnp.sin(ai * dt)
        den = ar * ar + ai * ai
        p = abr - 1.0
        fr = (p * ar + abi * ai) / den
        fi = (abi * ar - p * ai) / den
        abr_s[...] = abr
        abi_s[...] = abi
        for m in range(2):
            frm = fr[:, m * S5_HALF:(m + 1) * S5_HALF]
            fim = fi[:, m * S5_HALF:(m + 1) * S5_HALF]
            wr = bre_ref[m]
            wi = bim_ref[m]
            wbr_s[m] = (wr * frm - wi * fim).astype(BF16)
            wbi_s[m] = (wr * fim + wi * frm).astype(BF16)
        hr_s[...] = h0r_ref[...]
        hi_s[...] = h0i_ref[...]

    if batch_major:
        u = jnp.swapaxes(u_ref[...], 0, 1).reshape(tt * nb, D_S5)
    else:
        u = u_ref[...]
    ub = u.astype(BF16)
    half = D_S5 // 2

    nparts = S5_PARTS if tt % (S5_PARTS * S5_SCAN_GROUP) == 0 else 1
    tp = tt // nparts
    part_rows = lambda p: slice(p * tp * nb, (p + 1) * tp * nb)

    blk = 256

    def in_proj(p):
        def block(m, ws, xs, c0):
            def item():
                xs[part_rows(p), m * S5_HALF + c0:m * S5_HALF + c0 + blk] = _dot(
                    ub[part_rows(p), m * half:(m + 1) * half], ws[m, :, c0:c0 + blk])
            return item
        return [block(m, ws, xs, c0) for m in range(2) for ws, xs in ((wbr_s, xr_s), (wbi_s, xi_s))
                for c0 in range(0, S5_HALF, blk)]

    def scan(p):
        def group(cb, t0, steps):
            def item():
                cols = slice(cb * S5_COLS, (cb + 1) * S5_COLS)
                a_r = jnp.broadcast_to(abr_s[:, cols], (nb, S5_COLS))
                a_i = jnp.broadcast_to(abi_s[:, cols], (nb, S5_COLS))
                hr = hr_s[:, cols]
                hi = hi_s[:, cols]
                for t in range(t0, t0 + steps):
                    rows = slice(t * nb, (t + 1) * nb)
                    hr, hi = (a_r * hr - a_i * hi + xr_s[rows, cols],
                              a_r * hi + a_i * hr + xi_s[rows, cols])
                    xr_s[rows, cols] = hr
                    xi_s[rows, cols] = hi
                hr_s[:, cols] = hr
                hi_s[:, cols] = hi
            return item
        steps = min(tp, S5_SCAN_GROUP)
        return [group(cb, t0, steps) for t0 in range(p * tp, (p + 1) * tp, steps)
                for cb in range(S5_CH // S5_COLS)]

    def out_proj(p):
        ys = {}

        def re_block(m):
            def item():
                hre = xr_s[part_rows(p), m * S5_HALF:(m + 1) * S5_HALF].astype(BF16)
                ys[m] = (_dot(hre, cre_ref[m])
                         + d_ref[:, m * half:(m + 1) * half] * u[part_rows(p), m * half:(m + 1) * half])
            return item

        def im_block(m):
            def item():
                him = xi_s[part_rows(p), m * S5_HALF:(m + 1) * S5_HALF].astype(BF16)
                ys[m] = ys[m] - _dot(him, cim_ref[m])
            return item

        def act():
            ys["act"] = jax.nn.gelu(jnp.concatenate([ys[0], ys[1]], axis=-1)).astype(BF16)

        def glu(c0):
            def item():
                val = _dot(ys["act"], wglu_ref[:, c0:c0 + blk])
                gate = _dot(ys["act"], wglu_ref[:, D_S5 + c0:D_S5 + c0 + blk])
                y_s[part_rows(p), c0:c0 + blk] = val * jax.nn.sigmoid(gate)
            return item
        return ([re_block(0), im_block(0), re_block(1), im_block(1), act]
                + [glu(c0) for c0 in range(0, D_S5, blk)])

    order = in_proj(0)
    for p in range(nparts):
        nxt = in_proj(p + 1) if p + 1 < nparts else []
        prv = out_proj(p - 1) if p > 0 else []
        order += _interleave(nxt + prv, scan(p)) if (nxt or prv) else scan(p)
    order += out_proj(nparts - 1)
    for item in order:
        item()

    hr_out[...] = hr_s[...]
    hi_out[...] = hi_s[...]
    if batch_major:
        y_ref[...] = jnp.swapaxes(y_s[...].reshape(tt, nb, D_S5), 0, 1)
    else:
        y_ref[...] = y_s[...]


def _s5(u, h0r, h0i, sw, nb, tt, batch_major):
    rows = nb * tt
    const2 = lambda i: (0, 0)
    const3 = lambda i: (0, 0, 0)
    full2 = lambda shape: pl.BlockSpec(shape, const2)
    full3 = lambda shape: pl.BlockSpec(shape, const3)
    if batch_major:
        steps = u.shape[1] // tt
        io_spec = pl.BlockSpec((nb, tt, D_S5), lambda i: (0, i, 0))
    else:
        steps = u.shape[0] // rows
        io_spec = pl.BlockSpec((rows, D_S5), lambda i: (i, 0))
    return pl.pallas_call(
        functools.partial(_s5_kernel, nb=nb, tt=tt, batch_major=batch_major),
        grid=(steps,),
        in_specs=[io_spec,
                  full2((nb, S5_CH)), full2((nb, S5_CH)),
                  full2((1, S5_CH)), full2((1, S5_CH)), full2((1, S5_CH)),
                  full3((2, D_S5 // 2, S5_HALF)), full3((2, D_S5 // 2, S5_HALF)),
                  full3((2, S5_HALF, D_S5 // 2)), full3((2, S5_HALF, D_S5 // 2)),
                  full2((1, D_S5)), full2((D_S5, 2 * D_S5))],
        out_specs=[io_spec, full2((nb, S5_CH)), full2((nb, S5_CH))],
        out_shape=[jax.ShapeDtypeStruct(u.shape, F32),
                   jax.ShapeDtypeStruct((nb, S5_CH), F32),
                   jax.ShapeDtypeStruct((nb, S5_CH), F32)],
        scratch_shapes=[pltpu.VMEM((1, S5_CH), F32), pltpu.VMEM((1, S5_CH), F32),
                        pltpu.VMEM((2, D_S5 // 2, S5_HALF), BF16), pltpu.VMEM((2, D_S5 // 2, S5_HALF), BF16),
                        pltpu.VMEM((nb, S5_CH), F32), pltpu.VMEM((nb, S5_CH), F32),
                        pltpu.VMEM((rows, S5_CH), F32), pltpu.VMEM((rows, S5_CH), F32),
                        pltpu.VMEM((rows, D_S5), F32)],
        compiler_params=pltpu.CompilerParams(dimension_semantics=("arbitrary",),
                                             vmem_limit_bytes=VMEM_LIMIT if batch_major else VMEM_LIMIT_SMALL),
        name="s5",
    )(u, h0r, h0i, sw["a_re"], sw["a_im"], sw["log_dt"], sw["b_re"], sw["b_im"],
      sw["c_re"], sw["c_im"], sw["d"], sw["w_glu"])


def _s5_weights(a_re, a_im, log_dt, b_re, b_im, c_re, c_im, d, w_glu):
    g, n, c = S5_N_GROUPS, S5_STATE, S5_GROUP
    gh = g // 2

    def blockdiag(w, rows_per_group, cols_per_group):
        x = jnp.transpose(w, (0, 2, 1)).reshape(2, gh * rows_per_group, cols_per_group)
        rg = jnp.arange(gh * rows_per_group)[:, None] // rows_per_group
        cg = jnp.arange(gh * cols_per_group)[None, :] // cols_per_group
        return jnp.where(rg == cg, jnp.tile(x, (1, 1, gh)), 0.0)

    b_blockdiag = lambda b: blockdiag(b, c, n)
    c_blockdiag = lambda cm: blockdiag(cm, n, c)

    return {
        "a_re": a_re.reshape(1, S5_CH), "a_im": a_im.reshape(1, S5_CH),
        "log_dt": jnp.repeat(log_dt, n).reshape(1, S5_CH),
        "b_re": b_blockdiag(b_re), "b_im": b_blockdiag(b_im),
        "c_re": c_blockdiag(c_re).astype(BF16), "c_im": c_blockdiag(c_im).astype(BF16),
        "d": d.reshape(1, D_S5), "w_glu": w_glu.astype(BF16),
    }


def _l2norm(x):
    return x * lax.rsqrt(jnp.sum(x * x, axis=-1, keepdims=True) + NORM_EPS)


def _gates(ba, alog_row, dtb_row):
    beta = jax.nn.sigmoid(ba)
    g = -jnp.exp(alog_row) * _softplus(ba + dtb_row)
    return beta, g


def _out_gate(o, og_row, z):
    return _rms(o, og_row) * _silu(z)


def _bdot(a, b):
    return jnp.einsum("bij,bjk->bik", a.astype(BF16), b.astype(BF16), preferred_element_type=F32)


def _tri_merge_level(n, lmat, ii, jj, shift):
    same_big = (ii >> (shift + 1)) == (jj >> (shift + 1))
    same_small = (ii >> shift) == (jj >> shift)
    c = jnp.where((same_big & jnp.logical_not(same_small))[None], lmat, 0.0)
    w = c + _bdot(n, c)
    return n - (w + _bdot(w, n))


def _interleave(*stages):
    out = []
    pos = [0] * len(stages)
    while any(p < len(st) for p, st in zip(pos, stages)):
        k = min((i for i, st in enumerate(stages) if pos[i] < len(st)),
                key=lambda i: (pos[i] + 0.5) / len(stages[i]))
        out.append(stages[k][pos[k]])
        pos[k] += 1
    return out


def _gdn_kernel(x_ref, g1_ref, win_ref, wgate_ref, convw_ref, alog_ref, dtb_ref, og_ref,
                u_ref, y_ref, sfin_ref, convout_ref, s_s, proj_a, proj_b, set_a, set_b, *, tg, nt):
    s = pl.program_id(0)

    @pl.when(s == 0)
    def _zero_buffers():
        for ref in (*proj_b, *set_a):
            ref[...] = jnp.zeros_like(ref)

    @pl.when(lax.rem(jnp.maximum(s - 1, 0), nt) == 0)
    def _new_sequence_conv():
        for ext in (proj_a[0], proj_b[0]):
            ext[0:8, :] = jnp.zeros((8, 3 * D_GDN), F32)

    @pl.when(lax.rem(jnp.maximum(s - 2, 0), nt) == 0)
    def _new_sequence_state():
        s_s[...] = jnp.zeros_like(s_s)

    args = (x_ref, g1_ref, win_ref, wgate_ref, convw_ref, alog_ref, dtb_ref, og_ref, u_ref, y_ref, sfin_ref, convout_ref, s_s)

    @pl.when(lax.rem(s, 2) == 0)
    def _even():
        _gdn_step(*args, proj_wr=proj_a, proj_rd=proj_b, wr=set_b, rd=set_a, tg=tg)

    @pl.when(lax.rem(s, 2) == 1)
    def _odd():
        _gdn_step(*args, proj_wr=proj_b, proj_rd=proj_a, wr=set_a, rd=set_b, tg=tg)


def _gdn_step(x_ref, g1_ref, win_ref, wgate_ref, convw_ref, alog_ref, dtb_ref, og_ref, u_ref, y_ref, sfin_ref, convout_ref,
              s_s, *, proj_wr, proj_rd, wr, rd, tg):
    nc = tg // CHUNK
    nsys = nc * GDN_HEADS
    hcols = lambda h: slice(h * GDN_HEAD_DIM, (h + 1) * GDN_HEAD_DIM)
    og = og_ref[...]
    v = {}
    stage_a, stage_b, stage_c = [], [], []

    ext_w, z_w, ba_w = proj_wr
    o0 = D_S5
    o1 = o0 + 3 * D_GDN
    o2 = o1 + D_GDN

    def a_norm():
        v["n1"] = _rms(x_ref[...], g1_ref[...]).astype(BF16)
    stage_a.append(a_norm)

    def a_block(dst, rows, dst_c0, src_c0, width):
        def item():
            dst[rows, dst_c0:dst_c0 + width] = _dot(v["n1"], win_ref[:, src_c0:src_c0 + width])
        return item
    blk = 256
    for c0 in range(0, o0, blk):
        stage_a.append(a_block(u_ref, slice(None), c0, c0, blk))
    for c0 in range(0, o1 - o0, blk):
        stage_a.append(a_block(ext_w, slice(8, None), c0, o0 + c0, blk))
    for c0 in range(0, o2 - o1, blk):
        stage_a.append(a_block(z_w, slice(None), c0, o1 + c0, blk))

    def a_gates():
        ba_w[...] = _dot(v["n1"], wgate_ref[...])
    stage_a.append(a_gates)

    aq_r, b_r, o_r, gl_r, zs_r = rd
    states = [None] * GDN_HEADS

    def c_chunk(c):
        def item():
            rows = slice(c * CHUNK, (c + 1) * CHUNK)
            for h in range(GDN_HEADS):
                sys = c * GDN_HEADS + h
                st = s_s[h] if c == 0 else states[h]
                r = _dot(aq_r[sys], st.astype(BF16))
                o = r[GDN_HEAD_DIM:] + o_r[sys]
                states[h] = st * gl_r[sys, 0:1, :] + r[:GDN_HEAD_DIM] + b_r[sys]
                y_ref[rows, hcols(h)] = _rms(o, og) * zs_r[rows, hcols(h)]
                if c == nc - 1:
                    s_s[h] = states[h]
                    sfin_ref[0, h] = states[h]
        return item
    stage_c.extend(c_chunk(c) for c in range(nc))

    aq_s, b_s, o_s, gl_s, zs_s = wr
    ext_s, z_r, ba_r = proj_rd
    ii = lax.broadcasted_iota(jnp.int32, (CHUNK, CHUNK), 0)
    jj = lax.broadcasted_iota(jnp.int32, (CHUNK, CHUNK), 1)
    causal = jj <= ii
    strict = jj < ii
    scale = GDN_HEAD_DIM ** -0.5
    lmats = [None] * nsys
    rhss = [None] * nsys
    kgt_attn = [None] * nsys
    qgs = [None] * nsys

    def b_gates():
        raw = ba_r[...].T[:2 * GDN_HEADS, :]
        lanes = lambda p: jnp.concatenate([p] * (tg // GATE_PAD), axis=1)
        beta = jax.nn.sigmoid(raw)
        g = -jnp.exp(lanes(alog_ref[...])) * _softplus(raw + lanes(dtb_ref[...]))
        g1 = g.astype(BF16).astype(F32)
        g2 = (g - g1).astype(BF16).astype(F32)
        g3 = (g - g1 - g2).astype(BF16).astype(F32)
        ri = lax.broadcasted_iota(jnp.int32, (tg, tg), 0)
        ci = lax.broadcasted_iota(jnp.int32, (tg, tg), 1)
        ubd = jnp.where(((ri >> 6) == (ci >> 6)) & (ri <= ci), 1.0, 0.0).astype(BF16)
        parts = _dot(jnp.concatenate([g1, g2, g3], axis=0).astype(BF16), ubd)
        nh2 = 2 * GDN_HEADS
        gc = parts[:nh2] + parts[nh2:2 * nh2] + parts[2 * nh2:]
        row_id = lax.broadcasted_iota(jnp.int32, (nh2, tg), 0)
        rows = jnp.where(row_id < GDN_HEADS, beta, gc)
        cols = jnp.concatenate([rows, jnp.zeros((GATE_PAD - nh2, tg), F32)], axis=0).T
        v["beta_all"] = cols
        v["gc_col"] = cols
        v["gc_row"] = rows
    stage_b.append(b_gates)

    def b_silu_z(h):
        def item():
            zs_s[:, hcols(h)] = _silu(z_r[:, hcols(h)])
        return item

    def b_conv(h, part, name, norm):
        def item():
            c0 = part * D_GDN + h * GDN_HEAD_DIM
            cs = slice(c0, c0 + GDN_HEAD_DIM)
            ext = ext_s[:, cs]
            acc = ext[8:] * convw_ref[GDN_CONV - 1:GDN_CONV, cs]
            for j in range(GDN_CONV - 1):
                acc = acc + _rows_back(ext, GDN_CONV - 1 - j, tg) * convw_ref[j:j + 1, cs]
            acc = _silu(acc)
            v[name] = norm(acc)
        return item

    def b_prep(h, c):
        def item():
            beta_all, gc_col, gc_row = v["beta_all"], v["gc_col"], v["gc_row"]
            sys = c * GDN_HEADS + h
            rows = slice(c * CHUNK, (c + 1) * CHUNK)
            q = v["q"][rows]
            k = v["k"][rows]
            vv = v["v"][rows]
            beta = beta_all[rows, h:h + 1]
            gcc = gc_col[rows, GDN_HEADS + h:GDN_HEADS + h + 1]
            gcr = gc_row[GDN_HEADS + h:GDN_HEADS + h + 1, rows]
            decay = jnp.exp(jnp.where(causal, gcc - gcr, -jnp.inf))
            kb = k * beta
            egc = jnp.exp(gcc)
            g_last = gcc[CHUNK - 1:CHUNK, :]
            kq = _dot_nt(jnp.concatenate([kb, q], axis=0).astype(BF16), k.astype(BF16))
            lmats[sys] = jnp.where(strict, kq[:CHUNK] * decay, 0.0)
            attn = jnp.where(causal, kq[CHUNK:] * decay, 0.0)
            rhss[sys] = jnp.concatenate([vv * beta, kb * egc], axis=1)
            kgt_attn[sys] = jnp.concatenate([(k * jnp.exp(g_last - gcc)).T, attn], axis=0)
            qgs[sys] = q * egc
            gl_s[sys] = jnp.broadcast_to(jnp.exp(g_last), (8, GDN_HEAD_DIM))
        return item

    for h in range(GDN_HEADS):
        stage_b.append(b_silu_z(h))
        stage_b.append(b_conv(h, 0, "q", lambda a: _l2norm(a) * scale))
        stage_b.append(b_conv(h, 1, "k", _l2norm))
        stage_b.append(b_conv(h, 2, "v", lambda a: a))
        stage_b.extend(b_prep(h, c) for c in range(nc))

    def b_inverse_start():
        lmat = jnp.stack(lmats)
        v["lmat"] = lmat
        v["noff"] = -jnp.where(((ii >> 1) == (jj >> 1))[None], lmat, 0.0)
    stage_b.append(b_inverse_start)

    def b_inverse_level(shift):
        def item():
            v["noff"] = _tri_merge_level(v["noff"], v["lmat"], ii, jj, shift)
        return item
    shift = 1
    while (1 << shift) < CHUNK:
        stage_b.append(b_inverse_level(shift))
        shift += 1

    def b_solve():
        rhs = jnp.stack(rhss)
        sol = rhs + _bdot(v["noff"], rhs)
        v["prod"] = _bdot(jnp.stack(kgt_attn), sol)
    stage_b.append(b_solve)

    def b_store():
        prod = v["prod"]
        dk = GDN_HEAD_DIM
        for sys in range(nsys):
            aq_s[sys, :dk, :] = (-prod[sys, :dk, dk:]).astype(BF16)
            aq_s[sys, dk:, :] = (qgs[sys] - prod[sys, dk:, dk:]).astype(BF16)
            b_s[sys] = prod[sys, :dk, :dk]
            o_s[sys] = prod[sys, dk:, :dk]
    stage_b.append(b_store)

    for item in _interleave(stage_a, stage_b, stage_c):
        item()

    ext_w[0:8, :] = ext_s[pl.ds(tg, 8), :]
    convout_ref[0] = ext_s[pl.ds(tg + 8 - (GDN_CONV - 1), GDN_CONV - 1), :]


def _gdn_prompt(x, g1, w_in_bf, w_gate_bf, gw, bsz, seq, tg):
    nt = seq // tg
    ntiles = bsz * nt
    nsys = (tg // CHUNK) * GDN_HEADS
    tile_a = lambda s: jnp.minimum(s, ntiles - 1)
    tile_b = lambda s: jnp.clip(s - 1, 0, ntiles - 1)
    tile_c = lambda s: jnp.maximum(s - 2, 0)
    const = lambda s: (0, 0)
    proj_bufs = lambda: [pltpu.VMEM((tg + 8, 3 * D_GDN), F32),
                         pltpu.VMEM((tg, D_GDN), F32),
                         pltpu.VMEM((tg, GATE_PAD), F32)]
    resident = lambda shape: pl.BlockSpec(shape, const, pipeline_mode=pl.Buffered(1))
    handover = lambda: [pltpu.VMEM((nsys, GDN_HEAD_DIM + CHUNK, GDN_HEAD_DIM), BF16),
                        pltpu.VMEM((nsys, GDN_HEAD_DIM, GDN_HEAD_DIM), F32),
                        pltpu.VMEM((nsys, CHUNK, GDN_HEAD_DIM), F32),
                        pltpu.VMEM((nsys, 8, GDN_HEAD_DIM), F32),
                        pltpu.VMEM((tg, D_GDN), F32)]
    return pl.pallas_call(
        functools.partial(_gdn_kernel, tg=tg, nt=nt),
        grid=(ntiles + 2,),
        in_specs=[pl.BlockSpec((tg, D_MODEL), lambda s: (tile_a(s), 0)),
                  resident((1, D_MODEL)),
                  resident((D_MODEL, D_IN_MAIN)),
                  resident((D_MODEL, GATE_PAD)),
                  resident((GDN_CONV, 3 * D_GDN)),
                  resident((2 * GDN_HEADS, GATE_PAD)),
                  resident((2 * GDN_HEADS, GATE_PAD)),
                  resident((1, GDN_HEAD_DIM))],
        out_specs=[pl.BlockSpec((tg, D_S5), lambda s: (tile_a(s), 0)),
                   pl.BlockSpec((tg, D_GDN), lambda s: (tile_c(s), 0)),
                   pl.BlockSpec((1, GDN_HEADS, GDN_HEAD_DIM, GDN_HEAD_DIM), lambda s: (tile_c(s) // nt, 0, 0, 0)),
                   pl.BlockSpec((1, GDN_CONV - 1, 3 * D_GDN), lambda s: (tile_b(s) // nt, 0, 0))],
        out_shape=[jax.ShapeDtypeStruct((bsz * seq, D_S5), F32),
                   jax.ShapeDtypeStruct((bsz * seq, D_GDN), F32),
                   jax.ShapeDtypeStruct((bsz, GDN_HEADS, GDN_HEAD_DIM, GDN_HEAD_DIM), F32),
                   jax.ShapeDtypeStruct((bsz, GDN_CONV - 1, 3 * D_GDN), F32)],
        scratch_shapes=[pltpu.VMEM((GDN_HEADS, GDN_HEAD_DIM, GDN_HEAD_DIM), F32),
                        proj_bufs(), proj_bufs(), handover(), handover()],
        compiler_params=pltpu.CompilerParams(dimension_semantics=("arbitrary",),
                                             vmem_limit_bytes=VMEM_LIMIT),
        name="proj_gdn_prompt",
    )(x, g1, w_in_bf, w_gate_bf, gw["conv_w"], gw["a_log_col"], gw["dt_bias_col"], gw["onorm_g"])


def _gdn_step_kernel(qkv_ref, st_ref, z_ref, ba_ref, s0_ref, convw_ref, alog_ref, dtb_ref, og_ref,
                     y_ref, s1_ref, *, bb):
    beta_all, g_all = _gates(ba_ref[...], alog_ref[...], dtb_ref[...])
    alpha_all = jnp.exp(g_all)
    scale = GDN_HEAD_DIM ** -0.5
    og = og_ref[...]

    for h in range(GDN_HEADS):
        def conv_cols(part, h=h):
            c0 = part * D_GDN + h * GDN_HEAD_DIM
            cs = slice(c0, c0 + GDN_HEAD_DIM)
            acc = qkv_ref[:, cs] * convw_ref[GDN_CONV - 1:GDN_CONV, cs]
            for j in range(GDN_CONV - 1):
                acc = acc + st_ref[j, :, cs] * convw_ref[j:j + 1, cs]
            return _silu(acc)

        q = _l2norm(conv_cols(0)) * scale
        k = _l2norm(conv_cols(1))
        v = conv_cols(2)
        beta = beta_all[:, h:h + 1]
        alpha = alpha_all[:, GDN_HEADS + h:GDN_HEADS + h + 1]
        qk = jnp.sum(q * k, axis=-1, keepdims=True)
        kt = k.T
        kq = jnp.concatenate([k, q], axis=0).astype(BF16)
        o_rows = []
        for b in range(bb):
            s0 = s0_ref[b, h]
            kcol = kt[:, b:b + 1]
            r = _dot(kq, s0.astype(BF16))
            ks = r[b:b + 1, :]
            qs = r[bb + b:bb + b + 1, :]
            al = alpha[b:b + 1, :]
            v_new = beta[b:b + 1, :] * (v[b:b + 1, :] - al * ks)
            o_rows.append(al * qs + qk[b:b + 1, :] * v_new)
            s1_ref[b, h] = al * s0 + kcol * v_new
        o = jnp.concatenate(o_rows, axis=0)
        cs = slice(h * GDN_HEAD_DIM, (h + 1) * GDN_HEAD_DIM)
        y_ref[:, cs] = _out_gate(o, og, z_ref[:, cs])


def _gdn_sample(qkv, st_t, z, ba, s0, gw, bb):
    n = qkv.shape[0]
    row = lambda i: (i, 0)
    const = lambda i: (0, 0)
    state = lambda i: (i, 0, 0, 0)
    return pl.pallas_call(
        functools.partial(_gdn_step_kernel, bb=bb),
        grid=(n // bb,),
        in_specs=[pl.BlockSpec((bb, 3 * D_GDN), row),
                  pl.BlockSpec((GDN_CONV - 1, bb, 3 * D_GDN), lambda i: (0, i, 0)),
                  pl.BlockSpec((bb, D_GDN), row),
                  pl.BlockSpec((bb, GATE_PAD), row),
                  pl.BlockSpec((bb, GDN_HEADS, GDN_HEAD_DIM, GDN_HEAD_DIM), state),
                  pl.BlockSpec((GDN_CONV, 3 * D_GDN), const),
                  pl.BlockSpec((1, GATE_PAD), const),
                  pl.BlockSpec((1, GATE_PAD), const),
                  pl.BlockSpec((1, GDN_HEAD_DIM), const)],
        out_specs=[pl.BlockSpec((bb, D_GDN), row),
                   pl.BlockSpec((bb, GDN_HEADS, GDN_HEAD_DIM, GDN_HEAD_DIM), state)],
        out_shape=[jax.ShapeDtypeStruct((n, D_GDN), F32),
                   jax.ShapeDtypeStruct(s0.shape, F32)],
        compiler_params=pltpu.CompilerParams(dimension_semantics=("arbitrary",),
                                             vmem_limit_bytes=VMEM_LIMIT_SMALL),
        name="gdn_sample",
    )(qkv, st_t, z, ba, s0, gw["conv_w"], gw["a_log"], gw["dt_bias"], gw["onorm_g"])


def _residual_in(x_ref, ys5_ref, ygdn_ref, wout_ref, g2_ref, x1_s, n2_s):
    x1 = (x_ref[...] + _dot(ys5_ref[...].astype(BF16), wout_ref[:D_S5, :])
          + _dot(ygdn_ref[...].astype(BF16), wout_ref[D_S5:, :]))
    x1_s[...] = x1
    n2_s[...] = _rms(x1, g2_ref[...]).astype(BF16)


def _ffn_prompt_kernel(x_ref, ys5_ref, ygdn_ref, wout_ref, g2_ref, wup_ref, cw_ref, wdn_ref, gf_ref,
                       y_ref, hlast_ref, x1_s, n2_s, act_s, carry_s, *, tm):
    ti = pl.program_id(1)
    nt = pl.num_programs(1)

    @pl.when(ti == 0)
    def _reset():
        carry_s[...] = jnp.zeros_like(carry_s)

    _residual_in(x_ref, ys5_ref, ygdn_ref, wout_ref, g2_ref, x1_s, n2_s)
    n2 = n2_s[...]
    for f in range(D_FF // FFN_COLS):
        halves = []
        for part in range(2):
            c0 = part * D_FF + f * FFN_COLS
            cs = slice(c0, c0 + FFN_COLS)
            hcur = _dot(n2, wup_ref[:, cs])
            ext = jnp.concatenate([carry_s[:, cs], hcur], axis=0)
            carry_s[:, cs] = hcur[tm - 8:, :]
            conv = (_rows_back(ext, 2, tm) * cw_ref[0:1, cs]
                    + _rows_back(ext, 1, tm) * cw_ref[1:2, cs]
                    + hcur * cw_ref[2:3, cs])
            halves.append(conv)
        act_s[:, f * FFN_COLS:(f + 1) * FFN_COLS] = (_silu(halves[0]) * halves[1]).astype(BF16)
    y_ref[...] = _rms(x1_s[...] + _dot(act_s[...], wdn_ref[...]), gf_ref[...])

    @pl.when(ti == nt - 1)
    def _fin():
        hlast_ref[0] = carry_s[8 - (FFN_CONV - 1):, :]


def _ffn_prompt(x, ys5, ygdn, fw, bsz, seq, tm):
    nt = seq // tm
    row = lambda b, i: (b * nt + i, 0)
    const = lambda b, i: (0, 0)
    resident = lambda shape: pl.BlockSpec(shape, const, pipeline_mode=pl.Buffered(1))
    return pl.pallas_call(
        functools.partial(_ffn_prompt_kernel, tm=tm),
        grid=(bsz, nt),
        in_specs=[pl.BlockSpec((tm, D_MODEL), row),
                  pl.BlockSpec((tm, D_S5), row),
                  pl.BlockSpec((tm, D_GDN), row),
                  resident((D_MODEL, D_MODEL)),
                  resident((1, D_MODEL)),
                  resident((D_MODEL, 2 * D_FF)),
                  resident((FFN_CONV, 2 * D_FF)),
                  resident((D_FF, D_MODEL)),
                  resident((1, D_MODEL))],
        out_specs=[pl.BlockSpec((tm, D_MODEL), row),
                   pl.BlockSpec((1, FFN_CONV - 1, 2 * D_FF), lambda b, i: (b, 0, 0))],
        out_shape=[jax.ShapeDtypeStruct((bsz * seq, D_MODEL), F32),
                   jax.ShapeDtypeStruct((bsz, FFN_CONV - 1, 2 * D_FF), F32)],
        scratch_shapes=[pltpu.VMEM((tm, D_MODEL), F32),
                        pltpu.VMEM((tm, D_MODEL), BF16),
                        pltpu.VMEM((tm, D_FF), BF16),
                        pltpu.VMEM((8, 2 * D_FF), F32)],
        compiler_params=pltpu.CompilerParams(dimension_semantics=("arbitrary", "arbitrary"),
                                             vmem_limit_bytes=VMEM_LIMIT),
        name="ffn_prompt",
    )(x, ys5, ygdn, fw["w_out"], fw["norm2_g"], fw["w_up"], fw["conv_w"], fw["w_down"], fw["normf_g"])


def _ffn_sample_kernel(x_ref, ys5_ref, ygdn_ref, st_ref, wout_ref, g2_ref, wup_ref, cw_ref, wdn_ref, gf_ref,
                       y_ref, h_ref, x1_s, n2_s, acc_s):
    _residual_in(x_ref, ys5_ref, ygdn_ref, wout_ref, g2_ref, x1_s, n2_s)
    n2 = n2_s[...]
    for f in range(D_FF // FFN_COLS):
        halves = []
        for part in range(2):
            c0 = part * D_FF + f * FFN_COLS
            cs = slice(c0, c0 + FFN_COLS)
            hcur = _dot(n2, wup_ref[:, cs])
            h_ref[:, cs] = hcur
            halves.append(st_ref[0, :, cs] * cw_ref[0:1, cs] + st_ref[1, :, cs] * cw_ref[1:2, cs]
                          + hcur * cw_ref[2:3, cs])
        act = (_silu(halves[0]) * halves[1]).astype(BF16)
        contrib = _dot(act, wdn_ref[f * FFN_COLS:(f + 1) * FFN_COLS, :])
        if f == 0:
            acc_s[...] = contrib
        else:
            acc_s[...] += contrib
    y_ref[...] = _rms(x1_s[...] + acc_s[...], gf_ref[...])


def _ffn_sample(x, ys5, ygdn, st_t, fw):
    n = x.shape[0]
    c2 = lambda i: (0, 0)
    c3 = lambda i: (0, 0, 0)
    return pl.pallas_call(
        _ffn_sample_kernel,
        grid=(1,),
        in_specs=[pl.BlockSpec((n, D_MODEL), c2),
                  pl.BlockSpec((n, D_S5), c2),
                  pl.BlockSpec((n, D_GDN), c2),
                  pl.BlockSpec((FFN_CONV - 1, n, 2 * D_FF), c3),
                  pl.BlockSpec((D_MODEL, D_MODEL), c2),
                  pl.BlockSpec((1, D_MODEL), c2),
                  pl.BlockSpec((D_MODEL, 2 * D_FF), c2),
                  pl.BlockSpec((FFN_CONV, 2 * D_FF), c2),
                  pl.BlockSpec((D_FF, D_MODEL), c2),
                  pl.BlockSpec((1, D_MODEL), c2)],
        out_specs=[pl.BlockSpec((n, D_MODEL), c2),
                   pl.BlockSpec((n, 2 * D_FF), c2)],
        out_shape=[jax.ShapeDtypeStruct((n, D_MODEL), F32),
                   jax.ShapeDtypeStruct((n, 2 * D_FF), F32)],
        scratch_shapes=[pltpu.VMEM((n, D_MODEL), F32),
                        pltpu.VMEM((n, D_MODEL), BF16),
                        pltpu.VMEM((n, D_MODEL), F32)],
        compiler_params=pltpu.CompilerParams(dimension_semantics=("arbitrary",),
                                             vmem_limit_bytes=VMEM_LIMIT),
        name="ffn_sample",
    )(x, ys5, ygdn, st_t, fw["w_out"], fw["norm2_g"], fw["w_up"], fw["conv_w"], fw["w_down"], fw["normf_g"])


PROJ_ROWS = 512
S5_STEPS = 128
GDN_ROWS = 512
FFN_ROWS = 512
GDN_SAMPLE_ROWS = 8


def kernel(x_prompt, x_sample, state_s5_re, state_s5_im, state_gdn, state_gdn_conv, state_ffn_conv, norm1_g, w_in, s5_a_re, s5_a_im, s5_log_dt, s5_b_re, s5_b_im, s5_c_re, s5_c_im, s5_d, s5_w_glu, gdn_conv_w, gdn_a_log, gdn_dt_bias, gdn_onorm_g, w_out, norm2_g, ffn_w_up, ffn_conv_w, ffn_w_down, normf_g):
    depth = w_in.shape[0]
    assert depth == 1, "the final rmsnorm is fused into the last layer's ffn kernel"
    bsz, seq, _ = x_prompt.shape
    nsmp = x_sample.shape[0]
    assert x_sample.shape[1] == 1
    l = 0

    xp = x_prompt.reshape(bsz * seq, D_MODEL)
    xs = x_sample.reshape(nsmp, D_MODEL)

    g1 = norm1_g[l].reshape(1, D_MODEL)
    w_in_bf = w_in[l][:, :D_IN_MAIN].astype(BF16)
    w_gate_bf = jnp.pad(w_in[l][:, D_IN_MAIN:], ((0, 0), (0, GATE_PAD - 2 * GDN_HEADS))).astype(BF16)
    sw = _s5_weights(s5_a_re[l], s5_a_im[l], s5_log_dt[l], s5_b_re[l], s5_b_im[l], s5_c_re[l], s5_c_im[l],
                     s5_d[l], s5_w_glu[l])
    gate_row = lambda v: jnp.pad(v, (GDN_HEADS, GATE_PAD - 2 * GDN_HEADS)).reshape(1, GATE_PAD)
    gate_col = lambda v: jnp.broadcast_to(jnp.pad(v, (GDN_HEADS, 0))[:, None], (2 * GDN_HEADS, GATE_PAD))
    gw = {"conv_w": gdn_conv_w[l], "a_log": gate_row(gdn_a_log[l]), "dt_bias": gate_row(gdn_dt_bias[l]),
          "a_log_col": gate_col(gdn_a_log[l]), "dt_bias_col": gate_col(gdn_dt_bias[l]),
          "onorm_g": gdn_onorm_g[l].reshape(1, GDN_HEAD_DIM)}
    fw = {"w_out": w_out[l].astype(BF16), "norm2_g": norm2_g[l].reshape(1, D_MODEL),
          "w_up": ffn_w_up[l].astype(BF16), "conv_w": ffn_conv_w[l], "w_down": ffn_w_down[l].astype(BF16),
          "normf_g": normf_g.reshape(1, D_MODEL)}

    u_p, ygdn_p, p_gdn, p_gdn_conv = _gdn_prompt(xp, g1, w_in_bf, w_gate_bf, gw, bsz, seq, GDN_ROWS)
    zeros_h = jnp.zeros((bsz, S5_CH), F32)
    ys5_p, p_hr, p_hi = _s5(u_p.reshape(bsz, seq, D_S5), zeros_h, zeros_h, sw, bsz, S5_STEPS, True)
    ys5_p = ys5_p.reshape(bsz * seq, D_S5)
    y_p, p_ffn_conv = _ffn_prompt(xp, ys5_p, ygdn_p, fw, bsz, seq, FFN_ROWS)

    u_s, qkv_s, z_s, ba_s = _proj(xs, g1, w_in_bf, w_gate_bf, nsmp)
    ys5_s, s_hr, s_hi = _s5(u_s, state_s5_re[l].reshape(nsmp, S5_CH), state_s5_im[l].reshape(nsmp, S5_CH),
                            sw, nsmp, 1, False)
    gconv_t = state_gdn_conv[l].transpose(1, 0, 2)
    ygdn_s, s_gdn = _gdn_sample(qkv_s, gconv_t, z_s, ba_s, state_gdn[l], gw, GDN_SAMPLE_ROWS)
    fconv_t = state_ffn_conv[l].transpose(1, 0, 2)
    y_s, h_s = _ffn_sample(xs, ys5_s, ygdn_s, fconv_t, fw)
    s_gdn_conv = jnp.concatenate([state_gdn_conv[l][:, 1:], qkv_s[:, None, :]], axis=1)
    s_ffn_conv = jnp.concatenate([state_ffn_conv[l][:, 1:], h_s[:, None, :]], axis=1)

    st = lambda a: a[None]
    s5_shape = lambda a, n: a.reshape(1, n, S5_N_GROUPS, S5_STATE)
    return (y_p.reshape(bsz, seq, D_MODEL), y_s.reshape(nsmp, 1, D_MODEL),
            s5_shape(p_hr, bsz), s5_shape(p_hi, bsz), st(p_gdn), st(p_gdn_conv), st(p_ffn_conv),
            s5_shape(s_hr, nsmp), s5_shape(s_hi, nsmp), st(s_gdn), st(s_gdn_conv), st(s_ffn_conv))
```

```python
import functools

import jax
import jax.numpy as jnp
from jax import lax
from jax.experimental import pallas as pl
from jax.experimental.pallas import tpu as pltpu

F32 = jnp.float32
BF16 = jnp.bfloat16
NORM_EPS = 1e-6

D_MODEL = 1024
D_S5 = 512
S5_GROUP = 16
S5_N_GROUPS = 32
S5_STATE = 64
S5_CH = S5_N_GROUPS * S5_STATE
D_GDN = 512
GDN_HEADS = 4
GDN_HEAD_DIM = 128
GDN_CONV = 4
D_FF = 2816
FFN_CONV = 3
D_IN = D_S5 + 4 * D_GDN + 2 * GDN_HEADS
GATE_PAD = 128
D_IN_MAIN = D_S5 + 4 * D_GDN

CHUNK = 64
FFN_COLS = 256
VMEM_LIMIT = 56 * 1024 * 1024


def _dot(a, b):
    return jnp.dot(a, b, preferred_element_type=F32)


def _dot_nt(a, b):
    return lax.dot_general(a, b, (((1,), (1,)), ((), ())), preferred_element_type=F32)


def _rms(x, g):
    ms = jnp.mean(x * x, axis=-1, keepdims=True)
    return x * lax.rsqrt(ms + NORM_EPS) * g


def _silu(x):
    return x * jax.nn.sigmoid(x)


def _rows_back(ext, back, n):
    return pltpu.roll(ext, back, 0)[8:8 + n]


def _softplus(x):
    return jnp.maximum(x, 0.0) + jnp.log1p(jnp.exp(-jnp.abs(x)))


def _proj_kernel(x_ref, g_ref, w_ref, wgate_ref, u_ref, qkv_ref, z_ref, ba_ref):
    n = _rms(x_ref[...], g_ref[...]).astype(BF16)
    o0 = D_S5
    o1 = o0 + 3 * D_GDN
    u_ref[...] = _dot(n, w_ref[:, :o0])
    qkv_ref[...] = _dot(n, w_ref[:, o0:o1])
    z_ref[...] = _dot(n, w_ref[:, o1:])
    ba_ref[...] = _dot(n, wgate_ref[...])


def _proj(x, g1, w_in_bf, w_gate_bf, tm):
    n = x.shape[0]
    row = lambda i: (i, 0)
    const = lambda i: (0, 0)
    return pl.pallas_call(
        _proj_kernel,
        grid=(n // tm,),
        in_specs=[pl.BlockSpec((tm, D_MODEL), row),
                  pl.BlockSpec((1, D_MODEL), const),
                  pl.BlockSpec((D_MODEL, D_IN_MAIN), const),
                  pl.BlockSpec((D_MODEL, GATE_PAD), const)],
        out_specs=[pl.BlockSpec((tm, D_S5), row),
                   pl.BlockSpec((tm, 3 * D_GDN), row),
                   pl.BlockSpec((tm, D_GDN), row),
                   pl.BlockSpec((tm, GATE_PAD), row)],
        out_shape=[jax.ShapeDtypeStruct((n, D_S5), F32),
                   jax.ShapeDtypeStruct((n, 3 * D_GDN), F32),
                   jax.ShapeDtypeStruct((n, D_GDN), F32),
                   jax.ShapeDtypeStruct((n, GATE_PAD), F32)],
        compiler_params=pltpu.CompilerParams(dimension_semantics=("arbitrary",),
                                             vmem_limit_bytes=VMEM_LIMIT),
        name="proj",
    )(x, g1, w_in_bf, w_gate_bf)


S5_HALF = S5_CH // 2
S5_COLS = 1024
S5_SCAN_GROUP = 8
S5_PARTS = 4


def _s5_kernel(u_ref, h0r_ref, h0i_ref, are_ref, aim_ref, ldt_ref, bre_ref, bim_ref, cre_ref, cim_ref,
               d_ref, wglu_ref, y_ref, hr_out, hi_out,
               abr_s, abi_s, wbr_s, wbi_s, hr_s, hi_s, xr_s, xi_s, y_s, *, nb, tt, batch_major):
    @pl.when(pl.program_id(0) == 0)
    def _init():
        ar = are_ref[...]
        ai = aim_ref[...]
        dt = jnp.exp(ldt_ref[...])
        mag = jnp.exp(ar * dt)
        abr = mag * jnp.cos(ai * dt)
        abi = mag * jnp.sin(ai * dt)
        den = ar * ar + ai * ai
        p = abr - 1.0
        fr = (p * ar + abi * ai) / den
        fi = (abi * ar - p * ai) / den
        abr_s[...] = abr
        abi_s[...] = abi
        for m in range(2):
            frm = fr[:, m * S5_HALF:(m + 1) * S5_HALF]
            fim = fi[:, m * S5_HALF:(m + 1) * S5_HALF]
            wr = bre_ref[m]
            wi = bim_ref[m]
            wbr_s[m] = (wr * frm - wi * fim).astype(BF16)
            wbi_s[m] = (wr * fim + wi * frm).astype(BF16)
        hr_s[...] = h0r_ref[...]
        hi_s[...] = h0i_ref[...]

    if batch_major:
        u = jnp.swapaxes(u_ref[...], 0, 1).reshape(tt * nb, D_S5)
    else:
        u = u_ref[...]
    ub = u.astype(BF16)
    half = D_S5 // 2

    nparts = S5_PARTS if tt % (S5_PARTS * S5_SCAN_GROUP) == 0 else 1
    tp = tt // nparts
    part_rows = lambda p: slice(p * tp * nb, (p + 1) * tp * nb)

    blk = 256

    def in_proj(p):
        def block(m, ws, xs, c0):
            def item():
                xs[part_rows(p), m * S5_HALF + c0:m * S5_HALF + c0 + blk] = _dot(
                    ub[part_rows(p), m * half:(m + 1) * half], ws[m, :, c0:c0 + blk])
            return item
        return [block(m, ws, xs, c0) for m in range(2) for ws, xs in ((wbr_s, xr_s), (wbi_s, xi_s))
                for c0 in range(0, S5_HALF, blk)]

    def scan(p):
        def group(cb, t0, steps):
            def item():
                cols = slice(cb * S5_COLS, (cb + 1) * S5_COLS)
                a_r = jnp.broadcast_to(abr_s[:, cols], (nb, S5_COLS))
                a_i = jnp.broadcast_to(abi_s[:, cols], (nb, S5_COLS))
                hr = hr_s[:, cols]
                hi = hi_s[:, cols]
                for t in range(t0, t0 + steps):
                    rows = slice(t * nb, (t + 1) * nb)
                    hr, hi = (a_r * hr - a_i * hi + xr_s[rows, cols],
                              a_r * hi + a_i * hr + xi_s[rows, cols])
                    xr_s[rows, cols] = hr
                    xi_s[rows, cols] = hi
                hr_s[:, cols] = hr
                hi_s[:, cols] = hi
            return item
        steps = min(tp, S5_SCAN_GROUP)
        return [group(cb, t0, steps) for t0 in range(p * tp, (p + 1) * tp, steps)
                for cb in range(S5_CH // S5_COLS)]

    def out_proj(p):
        ys = {}

        def re_block(m):
            def item():
                hre = xr_s[part_rows(p), m * S5_HALF:(m + 1) * S5_HALF].astype(BF16)
                ys[m] = (_dot(hre, cre_ref[m])
                         + d_ref[:, m * half:(m + 1) * half] * u[part_rows(p), m * half:(m + 1) * half])
            return item

        def im_block(m):
            def item():
                him = xi_s[part_rows(p), m * S5_HALF:(m + 1) * S5_HALF].astype(BF16)
                ys[m] = ys[m] - _dot(him, cim_ref[m])
            return item

        def act():
            ys["act"] = jax.nn.gelu(jnp.concatenate([ys[0], ys[1]], axis=-1)).astype(BF16)

        def glu(c0):
            def item():
                val = _dot(ys["act"], wglu_ref[:, c0:c0 + blk])
                gate = _dot(ys["act"], wglu_ref[:, D_S5 + c0:D_S5 + c0 + blk])
                y_s[part_rows(p), c0:c0 + blk] = val * jax.nn.sigmoid(gate)
            return item
        return ([re_block(0), im_block(0), re_block(1), im_block(1), act]
                + [glu(c0) for c0 in range(0, D_S5, blk)])

    order = in_proj(0)
    for p in range(nparts):
        nxt = in_proj(p + 1) if p + 1 < nparts else []
        prv = out_proj(p - 1) if p > 0 else []
        order += _interleave(nxt + prv, scan(p)) if (nxt or prv) else scan(p)
    order += out_proj(nparts - 1)
    for item in order:
        item()

    hr_out[...] = hr_s[...]
    hi_out[...] = hi_s[...]
    if batch_major:
        y_ref[...] = jnp.swapaxes(y_s[...].reshape(tt, nb, D_S5), 0, 1)
    else:
        y_ref[...] = y_s[...]


def _s5(u, h0r, h0i, sw, nb, tt, batch_major):
    rows = nb * tt
    const2 = lambda i: (0, 0)
    const3 = lambda i: (0, 0, 0)
    full2 = lambda shape: pl.BlockSpec(shape, const2)
    full3 = lambda shape: pl.BlockSpec(shape, const3)
    if batch_major:
        steps = u.shape[1] // tt
        io_spec = pl.BlockSpec((nb, tt, D_S5), lambda i: (0, i, 0))
    else:
        steps = u.shape[0] // rows
        io_spec = pl.BlockSpec((rows, D_S5), lambda i: (i, 0))
    return pl.pallas_call(
        functools.partial(_s5_kernel, nb=nb, tt=tt, batch_major=batch_major),
        grid=(steps,),
        in_specs=[io_spec,
                  full2((nb, S5_CH)), full2((nb, S5_CH)),
                  full2((1, S5_CH)), full2((1, S5_CH)), full2((1, S5_CH)),
                  full3((2, D_S5 // 2, S5_HALF)), full3((2, D_S5 // 2, S5_HALF)),
                  full3((2, S5_HALF, D_S5 // 2)), full3((2, S5_HALF, D_S5 // 2)),
                  full2((1, D_S5)), full2((D_S5, 2 * D_S5))],
        out_specs=[io_spec, full2((nb, S5_CH)), full2((nb, S5_CH))],
        out_shape=[jax.ShapeDtypeStruct(u.shape, F32),
                   jax.ShapeDtypeStruct((nb, S5_CH), F32),
                   jax.ShapeDtypeStruct((nb, S5_CH), F32)],
        scratch_shapes=[pltpu.VMEM((1, S5_CH), F32), pltpu.VMEM((1, S5_CH), F32),
                        pltpu.VMEM((2, D_S5 // 2, S5_HALF), BF16), pltpu.VMEM((2, D_S5 // 2, S5_HALF), BF16),
                        pltpu.VMEM((nb, S5_CH), F32), pltpu.VMEM((nb, S5_CH), F32),
                        pltpu.VMEM((rows, S5_CH), F32), pltpu.VMEM((rows, S5_CH), F32),
                        pltpu.VMEM((rows, D_S5), F32)],
        compiler_params=pltpu.CompilerParams(dimension_semantics=("arbitrary",),
                                             vmem_limit_bytes=VMEM_LIMIT),
        name="s5",
    )(u, h0r, h0i, sw["a_re"], sw["a_im"], sw["log_dt"], sw["b_re"], sw["b_im"],
      sw["c_re"], sw["c_im"], sw["d"], sw["w_glu"])


def _s5_weights(a_re, a_im, log_dt, b_re, b_im, c_re, c_im, d, w_glu):
    g, n, c = S5_N_GROUPS, S5_STATE, S5_GROUP
    gh = g // 2

    def blockdiag(w, rows_per_group, cols_per_group):
        x = jnp.transpose(w, (0, 2, 1)).reshape(2, gh * rows_per_group, cols_per_group)
        rg = jnp.arange(gh * rows_per_group)[:, None] // rows_per_group
        cg = jnp.arange(gh * cols_per_group)[None, :] // cols_per_group
        return jnp.where(rg == cg, jnp.tile(x, (1, 1, gh)), 0.0)

    b_blockdiag = lambda b: blockdiag(b, c, n)
    c_blockdiag = lambda cm: blockdiag(cm, n, c)

    return {
        "a_re": a_re.reshape(1, S5_CH), "a_im": a_im.reshape(1, S5_CH),
        "log_dt": jnp.repeat(log_dt, n).reshape(1, S5_CH),
        "b_re": b_blockdiag(b_re), "b_im": b_blockdiag(b_im),
        "c_re": c_blockdiag(c_re).astype(BF16), "c_im": c_blockdiag(c_im).astype(BF16),
        "d": d.reshape(1, D_S5), "w_glu": w_glu.astype(BF16),
    }


def _l2norm(x):
    return x * lax.rsqrt(jnp.sum(x * x, axis=-1, keepdims=True) + NORM_EPS)


def _gates(ba, alog_row, dtb_row):
    beta = jax.nn.sigmoid(ba)
    g = -jnp.exp(alog_row) * _softplus(ba + dtb_row)
    return beta, g


def _out_gate(o, og_row, z):
    return _rms(o, og_row) * _silu(z)


def _bdot(a, b):
    return jnp.einsum("bij,bjk->bik", a.astype(BF16), b.astype(BF16), preferred_element_type=F32)


def _tri_merge_level(n, lmat, ii, jj, shift):
    same_big = (ii >> (shift + 1)) == (jj >> (shift + 1))
    same_small = (ii >> shift) == (jj >> shift)
    c = jnp.where((same_big & jnp.logical_not(same_small))[None], lmat, 0.0)
    w = c + _bdot(n, c)
    return n - (w + _bdot(w, n))


def _interleave(*stages):
    out = []
    pos = [0] * len(stages)
    while any(p < len(st) for p, st in zip(pos, stages)):
        k = min((i for i, st in enumerate(stages) if pos[i] < len(st)),
                key=lambda i: (pos[i] + 0.5) / len(stages[i]))
        out.append(stages[k][pos[k]])
        pos[k] += 1
    return out


def _gdn_kernel(x_ref, g1_ref, win_ref, wgate_ref, convw_ref, alog_ref, dtb_ref, og_ref,
                u_ref, y_ref, sfin_ref, convout_ref, s_s, proj_a, proj_b, set_a, set_b, *, tg, nt):
    s = pl.program_id(0)

    @pl.when(s == 0)
    def _zero_buffers():
        for ref in (*proj_b, *set_a):
            ref[...] = jnp.zeros_like(ref)

    @pl.when(lax.rem(jnp.maximum(s - 1, 0), nt) == 0)
    def _new_sequence_conv():
        for ext in (proj_a[0], proj_b[0]):
            ext[0:8, :] = jnp.zeros((8, 3 * D_GDN), F32)

    @pl.when(lax.rem(jnp.maximum(s - 2, 0), nt) == 0)
    def _new_sequence_state():
        s_s[...] = jnp.zeros_like(s_s)

    args = (x_ref, g1_ref, win_ref, wgate_ref, convw_ref, alog_ref, dtb_ref, og_ref, u_ref, y_ref, sfin_ref, convout_ref, s_s)

    @pl.when(lax.rem(s, 2) == 0)
    def _even():
        _gdn_step(*args, proj_wr=proj_a, proj_rd=proj_b, wr=set_b, rd=set_a, tg=tg)

    @pl.when(lax.rem(s, 2) == 1)
    def _odd():
        _gdn_step(*args, proj_wr=proj_b, proj_rd=proj_a, wr=set_a, rd=set_b, tg=tg)


def _gdn_step(x_ref, g1_ref, win_ref, wgate_ref, convw_ref, alog_ref, dtb_ref, og_ref, u_ref, y_ref, sfin_ref, convout_ref,
              s_s, *, proj_wr, proj_rd, wr, rd, tg):
    nc = tg // CHUNK
    nsys = nc * GDN_HEADS
    hcols = lambda h: slice(h * GDN_HEAD_DIM, (h + 1) * GDN_HEAD_DIM)
    og = og_ref[...]
    v = {}
    stage_a, stage_b, stage_c = [], [], []

    ext_w, z_w, ba_w = proj_wr
    o0 = D_S5
    o1 = o0 + 3 * D_GDN
    o2 = o1 + D_GDN

    def a_norm():
        v["n1"] = _rms(x_ref[...], g1_ref[...]).astype(BF16)
    stage_a.append(a_norm)

    def a_block(dst, rows, dst_c0, src_c0, width):
        def item():
            dst[rows, dst_c0:dst_c0 + width] = _dot(v["n1"], win_ref[:, src_c0:src_c0 + width])
        return item
    blk = 256
    for c0 in range(0, o0, blk):
        stage_a.append(a_block(u_ref, slice(None), c0, c0, blk))
    for c0 in range(0, o1 - o0, blk):
        stage_a.append(a_block(ext_w, slice(8, None), c0, o0 + c0, blk))
    for c0 in range(0, o2 - o1, blk):
        stage_a.append(a_block(z_w, slice(None), c0, o1 + c0, blk))

    def a_gates():
        ba_w[...] = _dot(v["n1"], wgate_ref[...])
    stage_a.append(a_gates)

    aq_r, b_r, o_r, gl_r, zs_r = rd
    states = [None] * GDN_HEADS

    def c_chunk(c):
        def item():
            rows = slice(c * CHUNK, (c + 1) * CHUNK)
            for h in range(GDN_HEADS):
                sys = c * GDN_HEADS + h
                st = s_s[h] if c == 0 else states[h]
                r = _dot(aq_r[sys], st.astype(BF16))
                o = r[GDN_HEAD_DIM:] + o_r[sys]
                states[h] = st * gl_r[sys, 0:1, :] + r[:GDN_HEAD_DIM] + b_r[sys]
                y_ref[rows, hcols(h)] = _rms(o, og) * zs_r[rows, hcols(h)]
                if c == nc - 1:
                    s_s[h] = states[h]
                    sfin_ref[0, h] = states[h]
        return item
    stage_c.extend(c_chunk(c) for c in range(nc))

    aq_s, b_s, o_s, gl_s, zs_s = wr
    ext_s, z_r, ba_r = proj_rd
    ii = lax.broadcasted_iota(jnp.int32, (CHUNK, CHUNK), 0)
    jj = lax.broadcasted_iota(jnp.int32, (CHUNK, CHUNK), 1)
    causal = jj <= ii
    strict = jj < ii
    scale = GDN_HEAD_DIM ** -0.5
    lmats = [None] * nsys
    rhss = [None] * nsys
    kgt_attn = [None] * nsys
    qgs = [None] * nsys

    def b_gates():
        raw = ba_r[...].T[:2 * GDN_HEADS, :]
        lanes = lambda p: jnp.concatenate([p] * (tg // GATE_PAD), axis=1)
        beta = jax.nn.sigmoid(raw)
        g = -jnp.exp(lanes(alog_ref[...])) * _softplus(raw + lanes(dtb_ref[...]))
        g1 = g.astype(BF16).astype(F32)
        g2 = (g - g1).astype(BF16).astype(F32)
        g3 = (g - g1 - g2).astype(BF16).astype(F32)
        ri = lax.broadcasted_iota(jnp.int32, (tg, tg), 0)
        ci = lax.broadcasted_iota(jnp.int32, (tg, tg), 1)
        ubd = jnp.where(((ri >> 6) == (ci >> 6)) & (ri <= ci), 1.0, 0.0).astype(BF16)
        parts = _dot(jnp.concatenate([g1, g2, g3], axis=0).astype(BF16), ubd)
        nh2 = 2 * GDN_HEADS
        gc = parts[:nh2] + parts[nh2:2 * nh2] + parts[2 * nh2:]
        row_id = lax.broadcasted_iota(jnp.int32, (nh2, tg), 0)
        rows = jnp.where(row_id < GDN_HEADS, beta, gc)
        cols = jnp.concatenate([rows, jnp.zeros((GATE_PAD - nh2, tg), F32)], axis=0).T
        v["beta_all"] = cols
        v["gc_col"] = cols
        v["gc_row"] = rows
    stage_b.append(b_gates)

    def b_silu_z(h):
        def item():
            zs_s[:, hcols(h)] = _silu(z_r[:, hcols(h)])
        return item

    def b_conv(h, part, name, norm):
        def item():
            c0 = part * D_GDN + h * GDN_HEAD_DIM
            cs = slice(c0, c0 + GDN_HEAD_DIM)
            ext = ext_s[:, cs]
            acc = ext[8:] * convw_ref[GDN_CONV - 1:GDN_CONV, cs]
            for j in range(GDN_CONV - 1):
                acc = acc + _rows_back(ext, GDN_CONV - 1 - j, tg) * convw_ref[j:j + 1, cs]
            acc = _silu(acc)
            v[name] = norm(acc)
        return item

    def b_prep(h, c):
        def item():
            beta_all, gc_col, gc_row = v["beta_all"], v["gc_col"], v["gc_row"]
            sys = c * GDN_HEADS + h
            rows = slice(c * CHUNK, (c + 1) * CHUNK)
            q = v["q"][rows]
            k = v["k"][rows]
            vv = v["v"][rows]
            beta = beta_all[rows, h:h + 1]
            gcc = gc_col[rows, GDN_HEADS + h:GDN_HEADS + h + 1]
            gcr = gc_row[GDN_HEADS + h:GDN_HEADS + h + 1, rows]
            decay = jnp.exp(jnp.where(causal, gcc - gcr, -jnp.inf))
            kb = k * beta
            egc = jnp.exp(gcc)
            g_last = gcc[CHUNK - 1:CHUNK, :]
            kq = _dot_nt(jnp.concatenate([kb, q], axis=0).astype(BF16), k.astype(BF16))
            lmats[sys] = jnp.where(strict, kq[:CHUNK] * decay, 0.0)
            attn = jnp.where(causal, kq[CHUNK:] * decay, 0.0)
            rhss[sys] = jnp.concatenate([vv * beta, kb * egc], axis=1)
            kgt_attn[sys] = jnp.concatenate([(k * jnp.exp(g_last - gcc)).T, attn], axis=0)
            qgs[sys] = q * egc
            gl_s[sys] = jnp.broadcast_to(jnp.exp(g_last), (8, GDN_HEAD_DIM))
        return item

    for h in range(GDN_HEADS):
        stage_b.append(b_silu_z(h))
        stage_b.append(b_conv(h, 0, "q", lambda a: _l2norm(a) * scale))
        stage_b.append(b_conv(h, 1, "k", _l2norm))
        stage_b.append(b_conv(h, 2, "v", lambda a: a))
        stage_b.extend(b_prep(h, c) for c in range(nc))

    def b_inverse_start():
        lmat = jnp.stack(lmats)
        v["lmat"] = lmat
        v["noff"] = -jnp.where(((ii >> 1) == (jj >> 1))[None], lmat, 0.0)
    stage_b.append(b_inverse_start)

    def b_inverse_level(shift):
        def item():
            v["noff"] = _tri_merge_level(v["noff"], v["lmat"], ii, jj, shift)
        return item
    shift = 1
    while (1 << shift) < CHUNK:
        stage_b.append(b_inverse_level(shift))
        shift += 1

    def b_solve():
        rhs = jnp.stack(rhss)
        sol = rhs + _bdot(v["noff"], rhs)
        v["prod"] = _bdot(jnp.stack(kgt_attn), sol)
    stage_b.append(b_solve)

    def b_store():
        prod = v["prod"]
        dk = GDN_HEAD_DIM
        for sys in range(nsys):
            aq_s[sys, :dk, :] = (-prod[sys, :dk, dk:]).astype(BF16)
            aq_s[sys, dk:, :] = (qgs[sys] - prod[sys, dk:, dk:]).astype(BF16)
            b_s[sys] = prod[sys, :dk, :dk]
            o_s[sys] = prod[sys, dk:, :dk]
    stage_b.append(b_store)

    for item in _interleave(stage_a, stage_b, stage_c):
        item()

    ext_w[0:8, :] = ext_s[pl.ds(tg, 8), :]
    convout_ref[0] = ext_s[pl.ds(tg + 8 - (GDN_CONV - 1), GDN_CONV - 1), :]


def _gdn_prompt(x, g1, w_in_bf, w_gate_bf, gw, bsz, seq, tg):
    nt = seq // tg
    ntiles = bsz * nt
    nsys = (tg // CHUNK) * GDN_HEADS
    tile_a = lambda s: jnp.minimum(s, ntiles - 1)
    tile_b = lambda s: jnp.clip(s - 1, 0, ntiles - 1)
    tile_c = lambda s: jnp.maximum(s - 2, 0)
    const = lambda s: (0, 0)
    proj_bufs = lambda: [pltpu.VMEM((tg + 8, 3 * D_GDN), F32),
                         pltpu.VMEM((tg, D_GDN), F32),
                         pltpu.VMEM((tg, GATE_PAD), F32)]
    resident = lambda shape: pl.BlockSpec(shape, const, pipeline_mode=pl.Buffered(1))
    handover = lambda: [pltpu.VMEM((nsys, GDN_HEAD_DIM + CHUNK, GDN_HEAD_DIM), BF16),
                        pltpu.VMEM((nsys, GDN_HEAD_DIM, GDN_HEAD_DIM), F32),
                        pltpu.VMEM((nsys, CHUNK, GDN_HEAD_DIM), F32),
                        pltpu.VMEM((nsys, 8, GDN_HEAD_DIM), F32),
                        pltpu.VMEM((tg, D_GDN), F32)]
    return pl.pallas_call(
        functools.partial(_gdn_kernel, tg=tg, nt=nt),
        grid=(ntiles + 2,),
        in_specs=[pl.BlockSpec((tg, D_MODEL), lambda s: (tile_a(s), 0)),
                  resident((1, D_MODEL)),
                  resident((D_MODEL, D_IN_MAIN)),
                  resident((D_MODEL, GATE_PAD)),
                  resident((GDN_CONV, 3 * D_GDN)),
                  resident((2 * GDN_HEADS, GATE_PAD)),
                  resident((2 * GDN_HEADS, GATE_PAD)),
                  resident((1, GDN_HEAD_DIM))],
        out_specs=[pl.BlockSpec((tg, D_S5), lambda s: (tile_a(s), 0)),
                   pl.BlockSpec((tg, D_GDN), lambda s: (tile_c(s), 0)),
                   pl.BlockSpec((1, GDN_HEADS, GDN_HEAD_DIM, GDN_HEAD_DIM), lambda s: (tile_c(s) // nt, 0, 0, 0)),
                   pl.BlockSpec((1, GDN_CONV - 1, 3 * D_GDN), lambda s: (tile_b(s) // nt, 0, 0))],
        out_shape=[jax.ShapeDtypeStruct((bsz * seq, D_S5), F32),
                   jax.ShapeDtypeStruct((bsz * seq, D_GDN), F32),
                   jax.ShapeDtypeStruct((bsz, GDN_HEADS, GDN_HEAD_DIM, GDN_HEAD_DIM), F32),
                   jax.ShapeDtypeStruct((bsz, GDN_CONV - 1, 3 * D_GDN), F32)],
        scratch_shapes=[pltpu.VMEM((GDN_HEADS, GDN_HEAD_DIM, GDN_HEAD_DIM), F32),
                        proj_bufs(), proj_bufs(), handover(), handover()],
        compiler_params=pltpu.CompilerParams(dimension_semantics=("arbitrary",),
                                             vmem_limit_bytes=VMEM_LIMIT),
        name="proj_gdn_prompt",
    )(x, g1, w_in_bf, w_gate_bf, gw["conv_w"], gw["a_log_col"], gw["dt_bias_col"], gw["onorm_g"])


def _gdn_step_kernel(qkv_ref, st_ref, z_ref, ba_ref, s0_ref, convw_ref, alog_ref, dtb_ref, og_ref,
                     y_ref, s1_ref, *, bb):
    beta_all, g_all = _gates(ba_ref[...], alog_ref[...], dtb_ref[...])
    alpha_all = jnp.exp(g_all)
    scale = GDN_HEAD_DIM ** -0.5
    og = og_ref[...]

    for h in range(GDN_HEADS):
        def conv_cols(part, h=h):
            c0 = part * D_GDN + h * GDN_HEAD_DIM
            cs = slice(c0, c0 + GDN_HEAD_DIM)
            acc = qkv_ref[:, cs] * convw_ref[GDN_CONV - 1:GDN_CONV, cs]
            for j in range(GDN_CONV - 1):
                acc = acc + st_ref[j, :, cs] * convw_ref[j:j + 1, cs]
            return _silu(acc)

        q = _l2norm(conv_cols(0)) * scale
        k = _l2norm(conv_cols(1))
        v = conv_cols(2)
        beta = beta_all[:, h:h + 1]
        alpha = alpha_all[:, GDN_HEADS + h:GDN_HEADS + h + 1]
        qk = jnp.sum(q * k, axis=-1, keepdims=True)
        kt = k.T
        kq = jnp.concatenate([k, q], axis=0).astype(BF16)
        o_rows = []
        for b in range(bb):
            s0 = s0_ref[b, h]
            kcol = kt[:, b:b + 1]
            r = _dot(kq, s0.astype(BF16))
            ks = r[b:b + 1, :]
            qs = r[bb + b:bb + b + 1, :]
            al = alpha[b:b + 1, :]
            v_new = beta[b:b + 1, :] * (v[b:b + 1, :] - al * ks)
            o_rows.append(al * qs + qk[b:b + 1, :] * v_new)
            s1_ref[b, h] = al * s0 + kcol * v_new
        o = jnp.concatenate(o_rows, axis=0)
        cs = slice(h * GDN_HEAD_DIM, (h + 1) * GDN_HEAD_DIM)
        y_ref[:, cs] = _out_gate(o, og, z_ref[:, cs])


def _gdn_sample(qkv, st_t, z, ba, s0, gw, bb):
    n = qkv.shape[0]
    row = lambda i: (i, 0)
    const = lambda i: (0, 0)
    state = lambda i: (i, 0, 0, 0)
    return pl.pallas_call(
        functools.partial(_gdn_step_kernel, bb=bb),
        grid=(n // bb,),
        in_specs=[pl.BlockSpec((bb, 3 * D_GDN), row),
                  pl.BlockSpec((GDN_CONV - 1, bb, 3 * D_GDN), lambda i: (0, i, 0)),
                  pl.BlockSpec((bb, D_GDN), row),
                  pl.BlockSpec((bb, GATE_PAD), row),
                  pl.BlockSpec((bb, GDN_HEADS, GDN_HEAD_DIM, GDN_HEAD_DIM), state),
                  pl.BlockSpec((GDN_CONV, 3 * D_GDN), const),
                  pl.BlockSpec((1, GATE_PAD), const),
                  pl.BlockSpec((1, GATE_PAD), const),
                  pl.BlockSpec((1, GDN_HEAD_DIM), const)],
        out_specs=[pl.BlockSpec((bb, D_GDN), row),
                   pl.BlockSpec((bb, GDN_HEADS, GDN_HEAD_DIM, GDN_HEAD_DIM), state)],
        out_shape=[jax.ShapeDtypeStruct((n, D_GDN), F32),
                   jax.ShapeDtypeStruct(s0.shape, F32)],
        compiler_params=pltpu.CompilerParams(dimension_semantics=("arbitrary",),
                                             vmem_limit_bytes=VMEM_LIMIT),
        name="gdn_sample",
    )(qkv, st_t, z, ba, s0, gw["conv_w"], gw["a_log"], gw["dt_bias"], gw["onorm_g"])


def _residual_in(x_ref, ys5_ref, ygdn_ref, wout_ref, g2_ref, x1_s, n2_s):
    x1 = (x_ref[...] + _dot(ys5_ref[...].astype(BF16), wout_ref[:D_S5, :])
          + _dot(ygdn_ref[...].astype(BF16), wout_ref[D_S5:, :]))
    x1_s[...] = x1
    n2_s[...] = _rms(x1, g2_ref[...]).astype(BF16)


def _ffn_prompt_kernel(x_ref, ys5_ref, ygdn_ref, wout_ref, g2_ref, wgate_ref, wup_ref, cw_ref, wdn_ref, gf_ref,
                       y_ref, hlast_ref, x1_s, n2_s, act_s, carry_s, *, tm):
    ti = pl.program_id(1)
    nt = pl.num_programs(1)

    @pl.when(ti == 0)
    def _reset():
        carry_s[...] = jnp.zeros_like(carry_s)

    _residual_in(x_ref, ys5_ref, ygdn_ref, wout_ref, g2_ref, x1_s, n2_s)
    n2 = n2_s[...]
    for f in range(D_FF // FFN_COLS):
        halves = []
        for part in range(2):
            c0 = part * D_FF + f * FFN_COLS
            cs = slice(c0, c0 + FFN_COLS)
            w_ref = (wgate_ref, wup_ref)[part]
            hcur = _dot(n2, w_ref[:, f * FFN_COLS:(f + 1) * FFN_COLS])
            ext = jnp.concatenate([carry_s[:, cs], hcur], axis=0)
            carry_s[:, cs] = hcur[tm - 8:, :]
            conv = (_rows_back(ext, 2, tm) * cw_ref[0:1, cs]
                    + _rows_back(ext, 1, tm) * cw_ref[1:2, cs]
                    + hcur * cw_ref[2:3, cs])
            halves.append(conv)
        act_s[:, f * FFN_COLS:(f + 1) * FFN_COLS] = (_silu(halves[0]) * halves[1]).astype(BF16)
    y_ref[...] = _rms(x1_s[...] + _dot(act_s[...], wdn_ref[...]), gf_ref[...])

    @pl.when(ti == nt - 1)
    def _fin():
        hlast_ref[0] = carry_s[8 - (FFN_CONV - 1):, :]


def _ffn_prompt(x, ys5, ygdn, fw, bsz, seq, tm):
    nt = seq // tm
    row = lambda b, i: (b * nt + i, 0)
    const = lambda b, i: (0, 0)
    resident = lambda shape: pl.BlockSpec(shape, const, pipeline_mode=pl.Buffered(1))
    return pl.pallas_call(
        functools.partial(_ffn_prompt_kernel, tm=tm),
        grid=(bsz, nt),
        in_specs=[pl.BlockSpec((tm, D_MODEL), row),
                  pl.BlockSpec((tm, D_S5), row),
                  pl.BlockSpec((tm, D_GDN), row),
                  resident((D_MODEL, D_MODEL)),
                  resident((1, D_MODEL)),
                  resident((D_MODEL, D_FF)),
                  resident((D_MODEL, D_FF)),
                  resident((FFN_CONV, 2 * D_FF)),
                  resident((D_FF, D_MODEL)),
                  resident((1, D_MODEL))],
        out_specs=[pl.BlockSpec((tm, D_MODEL), row),
                   pl.BlockSpec((1, FFN_CONV - 1, 2 * D_FF), lambda b, i: (b, 0, 0))],
        out_shape=[jax.ShapeDtypeStruct((bsz * seq, D_MODEL), F32),
                   jax.ShapeDtypeStruct((bsz, FFN_CONV - 1, 2 * D_FF), F32)],
        scratch_shapes=[pltpu.VMEM((tm, D_MODEL), F32),
                        pltpu.VMEM((tm, D_MODEL), BF16),
                        pltpu.VMEM((tm, D_FF), BF16),
                        pltpu.VMEM((8, 2 * D_FF), F32)],
        compiler_params=pltpu.CompilerParams(dimension_semantics=("arbitrary", "arbitrary"),
                                             vmem_limit_bytes=VMEM_LIMIT),
        name="ffn_prompt",
    )(x, ys5, ygdn, fw["w_out"], fw["norm2_g"], fw["w_gate"], fw["w_up"], fw["conv_w"], fw["w_down"],
      fw["normf_g"])


def _ffn_sample_kernel(x_ref, ys5_ref, ygdn_ref, stg_ref, stu_ref, wout_ref, g2_ref, wgate_ref, wup_ref,
                       cwg_ref, cwu_ref, wdn_ref, gf_ref,
                       y_ref, hg_ref, hu_ref, woutb_ref, wgateb_ref, wupb_ref, wdnb_ref, x1_s, n2_s, acc_s):
    f = pl.program_id(0)

    @pl.when(f == 0)
    def _first():
        woutb_ref[...] = wout_ref[...].astype(BF16)
        _residual_in(x_ref, ys5_ref, ygdn_ref, woutb_ref, g2_ref, x1_s, n2_s)
        acc_s[...] = jnp.zeros_like(acc_s)

    n2 = n2_s[...]
    halves = []
    for w_ref, wb_ref, st_ref, cw_ref, h_ref in ((wgate_ref, wgateb_ref, stg_ref, cwg_ref, hg_ref),
                                                 (wup_ref, wupb_ref, stu_ref, cwu_ref, hu_ref)):
        wb_ref[...] = w_ref[...].astype(BF16)
        hcur = _dot(n2, wb_ref[...])
        h_ref[...] = hcur
        halves.append(st_ref[0] * cw_ref[0:1, :] + st_ref[1] * cw_ref[1:2, :] + hcur * cw_ref[2:3, :])
    act = (_silu(halves[0]) * halves[1]).astype(BF16)
    wdnb_ref[...] = wdn_ref[...].astype(BF16)
    acc_s[...] += _dot(act, wdnb_ref[...])

    @pl.when(f == pl.num_programs(0) - 1)
    def _last():
        y_ref[...] = _rms(x1_s[...] + acc_s[...], gf_ref[...])


def _ffn_sample(x, ys5, ygdn, st_t, w_out, norm2_g, w_up, conv_w, w_down, normf_g):
    n = x.shape[0]
    nf = D_FF // FFN_COLS
    c2 = lambda f: (0, 0)
    gate_cols = lambda f: (0, f)
    up_cols = lambda f: (0, nf + f)
    return pl.pallas_call(
        _ffn_sample_kernel,
        grid=(nf,),
        in_specs=[pl.BlockSpec((n, D_MODEL), c2),
                  pl.BlockSpec((n, D_S5), c2),
                  pl.BlockSpec((n, D_GDN), c2),
                  pl.BlockSpec((FFN_CONV - 1, n, FFN_COLS), lambda f: (0, 0, f)),
                  pl.BlockSpec((FFN_CONV - 1, n, FFN_COLS), lambda f: (0, 0, nf + f)),
                  pl.BlockSpec((D_MODEL, D_MODEL), c2),
                  pl.BlockSpec((1, D_MODEL), c2),
                  pl.BlockSpec((D_MODEL, FFN_COLS), gate_cols),
                  pl.BlockSpec((D_MODEL, FFN_COLS), up_cols),
                  pl.BlockSpec((FFN_CONV, FFN_COLS), gate_cols),
                  pl.BlockSpec((FFN_CONV, FFN_COLS), up_cols),
                  pl.BlockSpec((FFN_COLS, D_MODEL), lambda f: (f, 0)),
                  pl.BlockSpec((1, D_MODEL), c2)],
        out_specs=[pl.BlockSpec((n, D_MODEL), c2),
                   pl.BlockSpec((n, FFN_COLS), gate_cols),
                   pl.BlockSpec((n, FFN_COLS), gate_cols),
                   pl.BlockSpec((D_MODEL, D_MODEL), c2),
                   pl.BlockSpec((D_MODEL, FFN_COLS), gate_cols),
                   pl.BlockSpec((D_MODEL, FFN_COLS), gate_cols),
                   pl.BlockSpec((FFN_COLS, D_MODEL), lambda f: (f, 0))],
        out_shape=[jax.ShapeDtypeStruct((n, D_MODEL), F32),
                   jax.ShapeDtypeStruct((n, D_FF), F32),
                   jax.ShapeDtypeStruct((n, D_FF), F32),
                   jax.ShapeDtypeStruct((D_MODEL, D_MODEL), BF16),
                   jax.ShapeDtypeStruct((D_MODEL, D_FF), BF16),
                   jax.ShapeDtypeStruct((D_MODEL, D_FF), BF16),
                   jax.ShapeDtypeStruct((D_FF, D_MODEL), BF16)],
        scratch_shapes=[pltpu.VMEM((n, D_MODEL), F32),
                        pltpu.VMEM((n, D_MODEL), BF16),
                        pltpu.VMEM((n, D_MODEL), F32)],
        compiler_params=pltpu.CompilerParams(dimension_semantics=("arbitrary",),
                                             vmem_limit_bytes=VMEM_LIMIT),
        name="ffn_sample",
    )(x, ys5, ygdn, st_t, st_t, w_out, norm2_g, w_up, w_up, conv_w, conv_w, w_down, normf_g)


PROJ_ROWS = 512
S5_STEPS = 128
GDN_ROWS = 512
FFN_ROWS = 512
GDN_SAMPLE_ROWS = 8


def kernel(x_prompt, x_sample, state_s5_re, state_s5_im, state_gdn, state_gdn_conv, state_ffn_conv, norm1_g, w_in, s5_a_re, s5_a_im, s5_log_dt, s5_b_re, s5_b_im, s5_c_re, s5_c_im, s5_d, s5_w_glu, gdn_conv_w, gdn_a_log, gdn_dt_bias, gdn_onorm_g, w_out, norm2_g, ffn_w_up, ffn_conv_w, ffn_w_down, normf_g):
    depth = w_in.shape[0]
    assert depth == 1, "the final rmsnorm is fused into the last layer's ffn kernel"
    bsz, seq, _ = x_prompt.shape
    nsmp = x_sample.shape[0]
    assert x_sample.shape[1] == 1
    l = 0

    xp = x_prompt.reshape(bsz * seq, D_MODEL)
    xs = x_sample.reshape(nsmp, D_MODEL)

    g1 = norm1_g[l].reshape(1, D_MODEL)
    w_in_bf = w_in[l][:, :D_IN_MAIN].astype(BF16)
    w_gate_bf = jnp.pad(w_in[l][:, D_IN_MAIN:], ((0, 0), (0, GATE_PAD - 2 * GDN_HEADS))).astype(BF16)
    sw = _s5_weights(s5_a_re[l], s5_a_im[l], s5_log_dt[l], s5_b_re[l], s5_b_im[l], s5_c_re[l], s5_c_im[l],
                     s5_d[l], s5_w_glu[l])
    gate_row = lambda v: jnp.pad(v, (GDN_HEADS, GATE_PAD - 2 * GDN_HEADS)).reshape(1, GATE_PAD)
    gate_col = lambda v: jnp.broadcast_to(jnp.pad(v, (GDN_HEADS, 0))[:, None], (2 * GDN_HEADS, GATE_PAD))
    gw = {"conv_w": gdn_conv_w[l], "a_log": gate_row(gdn_a_log[l]), "dt_bias": gate_row(gdn_dt_bias[l]),
          "a_log_col": gate_col(gdn_a_log[l]), "dt_bias_col": gate_col(gdn_dt_bias[l]),
          "onorm_g": gdn_onorm_g[l].reshape(1, GDN_HEAD_DIM)}
    norm2 = norm2_g[l].reshape(1, D_MODEL)
    normf = normf_g.reshape(1, D_MODEL)

    u_s, qkv_s, z_s, ba_s = _proj(xs, g1, w_in_bf, w_gate_bf, nsmp)
    ys5_s, s_hr, s_hi = _s5(u_s, state_s5_re[l].reshape(nsmp, S5_CH), state_s5_im[l].reshape(nsmp, S5_CH),
                            sw, nsmp, 1, False)
    gconv_t = state_gdn_conv[l].transpose(1, 0, 2)
    ygdn_s, s_gdn = _gdn_sample(qkv_s, gconv_t, z_s, ba_s, state_gdn[l], gw, GDN_SAMPLE_ROWS)
    fconv_t = state_ffn_conv[l].transpose(1, 0, 2)
    y_s, hg_s, hu_s, w_out_bf, w_gateff_bf, w_upff_bf, w_down_bf = _ffn_sample(
        xs, ys5_s, ygdn_s, fconv_t, w_out[l], norm2, ffn_w_up[l], ffn_conv_w[l], ffn_w_down[l], normf)
    h_s = jnp.concatenate([hg_s, hu_s], axis=1)
    fw = {"w_out": w_out_bf, "norm2_g": norm2, "w_gate": w_gateff_bf, "w_up": w_upff_bf,
          "conv_w": ffn_conv_w[l], "w_down": w_down_bf, "normf_g": normf}

    u_p, ygdn_p, p_gdn, p_gdn_conv = _gdn_prompt(xp, g1, w_in_bf, w_gate_bf, gw, bsz, seq, GDN_ROWS)
    zeros_h = jnp.zeros((bsz, S5_CH), F32)
    ys5_p, p_hr, p_hi = _s5(u_p.reshape(bsz, seq, D_S5), zeros_h, zeros_h, sw, bsz, S5_STEPS, True)
    ys5_p = ys5_p.reshape(bsz * seq, D_S5)
    y_p, p_ffn_conv = _ffn_prompt(xp, ys5_p, ygdn_p, fw, bsz, seq, FFN_ROWS)

    s_gdn_conv = jnp.concatenate([state_gdn_conv[l][:, 1:], qkv_s[:, None, :]], axis=1)
    s_ffn_conv = jnp.concatenate([state_ffn_conv[l][:, 1:], h_s[:, None, :]], axis=1)

    st = lambda a: a[None]
    s5_shape = lambda a, n: a.reshape(1, n, S5_N_GROUPS, S5_STATE)
    return (y_p.reshape(bsz, seq, D_MODEL), y_s.reshape(nsmp, 1, D_MODEL),
            s5_shape(p_hr, bsz), s5_shape(p_hi, bsz), st(p_gdn), st(p_gdn_conv), st(p_ffn_conv),
            s5_shape(s_hr, nsmp), s5_shape(s_hi, nsmp), st(s_gdn), st(s_gdn_conv), st(s_ffn_conv))
```

```python
import functools

import jax
import jax.numpy as jnp
from jax import lax
from jax.experimental import pallas as pl
from jax.experimental.pallas import tpu as pltpu

F32 = jnp.float32
BF16 = jnp.bfloat16
NORM_EPS = 1e-6

D_MODEL = 1024
D_S5 = 512
S5_GROUP = 16
S5_N_GROUPS = 32
S5_STATE = 64
S5_CH = S5_N_GROUPS * S5_STATE
D_GDN = 512
GDN_HEADS = 4
GDN_HEAD_DIM = 128
GDN_CONV = 4
D_FF = 2816
FFN_CONV = 3
D_IN = D_S5 + 4 * D_GDN + 2 * GDN_HEADS
GATE_PAD = 128
D_IN_MAIN = D_S5 + 4 * D_GDN

CHUNK = 64
FFN_COLS = 256
VMEM_LIMIT = 56 * 1024 * 1024


def _dot(a, b):
    return jnp.dot(a, b, preferred_element_type=F32)


def _dot_nt(a, b):
    return lax.dot_general(a, b, (((1,), (1,)), ((), ())), preferred_element_type=F32)


def _rms(x, g):
    ms = jnp.mean(x * x, axis=-1, keepdims=True)
    return x * lax.rsqrt(ms + NORM_EPS) * g


def _silu(x):
    return x * jax.nn.sigmoid(x)


def _rows_back(ext, back, n):
    return pltpu.roll(ext, back, 0)[8:8 + n]


def _softplus(x):
    return jnp.maximum(x, 0.0) + jnp.log1p(jnp.exp(-jnp.abs(x)))


def _proj_kernel(x_ref, g_ref, w_ref, wgate_ref, u_ref, qkv_ref, z_ref, ba_ref):
    n = _rms(x_ref[...], g_ref[...]).astype(BF16)
    o0 = D_S5
    o1 = o0 + 3 * D_GDN
    u_ref[...] = _dot(n, w_ref[:, :o0])
    qkv_ref[...] = _dot(n, w_ref[:, o0:o1])
    z_ref[...] = _dot(n, w_ref[:, o1:])
    ba_ref[...] = _dot(n, wgate_ref[...])


def _proj(x, g1, w_in_bf, w_gate_bf, tm):
    n = x.shape[0]
    row = lambda i: (i, 0)
    const = lambda i: (0, 0)
    return pl.pallas_call(
        _proj_kernel,
        grid=(n // tm,),
        in_specs=[pl.BlockSpec((tm, D_MODEL), row),
                  pl.BlockSpec((1, D_MODEL), const),
                  pl.BlockSpec((D_MODEL, D_IN_MAIN), const),
                  pl.BlockSpec((D_MODEL, GATE_PAD), const)],
        out_specs=[pl.BlockSpec((tm, D_S5), row),
                   pl.BlockSpec((tm, 3 * D_GDN), row),
                   pl.BlockSpec((tm, D_GDN), row),
                   pl.BlockSpec((tm, GATE_PAD), row)],
        out_shape=[jax.ShapeDtypeStruct((n, D_S5), F32),
                   jax.ShapeDtypeStruct((n, 3 * D_GDN), F32),
                   jax.ShapeDtypeStruct((n, D_GDN), F32),
                   jax.ShapeDtypeStruct((n, GATE_PAD), F32)],
        compiler_params=pltpu.CompilerParams(dimension_semantics=("arbitrary",),
                                             vmem_limit_bytes=VMEM_LIMIT),
        name="proj",
    )(x, g1, w_in_bf, w_gate_bf)


S5_HALF = S5_CH // 2
S5_COLS = 1024
S5_SCAN_GROUP = 8
S5_PARTS = 4


def _s5_kernel(u_ref, h0r_ref, h0i_ref, are_ref, aim_ref, ldt_ref, bre_ref, bim_ref, cre_ref, cim_ref,
               d_ref, wglu_ref, y_ref, hr_out, hi_out,
               abr_s, abi_s, wbr_s, wbi_s, hr_s, hi_s, xr_s, xi_s, y_s, *, nb, tt, batch_major):
    @pl.when(pl.program_id(0) == 0)
    def _init():
        ar = are_ref[...]
        ai = aim_ref[...]
        dt = jnp.exp(ldt_ref[...])
        mag = jnp.exp(ar * dt)
        abr = mag * jnp.cos(ai * dt)
        abi = mag * jnp.sin(ai * dt)
        den = ar * ar + ai * ai
        p = abr - 1.0
        fr = (p * ar + abi * ai) / den
        fi = (abi * ar - p * ai) / den
        abr_s[...] = abr
        abi_s[...] = abi
        for m in range(2):
            frm = fr[:, m * S5_HALF:(m + 1) * S5_HALF]
            fim = fi[:, m * S5_HALF:(m + 1) * S5_HALF]
            wr = bre_ref[m]
            wi = bim_ref[m]
            wbr_s[m] = (wr * frm - wi * fim).astype(BF16)
            wbi_s[m] = (wr * fim + wi * frm).astype(BF16)
        hr_s[...] = h0r_ref[...]
        hi_s[...] = h0i_ref[...]

    if batch_major:
        u = jnp.swapaxes(u_ref[...], 0, 1).reshape(tt * nb, D_S5)
    else:
        u = u_ref[...]
    ub = u.astype(BF16)
    half = D_S5 // 2

    nparts = S5_PARTS if tt % (S5_PARTS * S5_SCAN_GROUP) == 0 else 1
    tp = tt // nparts
    part_rows = lambda p: slice(p * tp * nb, (p + 1) * tp * nb)

    blk = 256

    def in_proj(p):
        def block(m, ws, xs, c0):
            def item():
                xs[part_rows(p), m * S5_HALF + c0:m * S5_HALF + c0 + blk] = _dot(
                    ub[part_rows(p), m * half:(m + 1) * half], ws[m, :, c0:c0 + blk])
            return item
        return [block(m, ws, xs, c0) for m in range(2) for ws, xs in ((wbr_s, xr_s), (wbi_s, xi_s))
                for c0 in range(0, S5_HALF, blk)]

    def scan(p):
        def group(cb, t0, steps):
            def item():
                cols = slice(cb * S5_COLS, (cb + 1) * S5_COLS)
                a_r = jnp.broadcast_to(abr_s[:, cols], (nb, S5_COLS))
                a_i = jnp.broadcast_to(abi_s[:, cols], (nb, S5_COLS))
                hr = hr_s[:, cols]
                hi = hi_s[:, cols]
                for t in range(t0, t0 + steps):
                    rows = slice(t * nb, (t + 1) * nb)
                    hr, hi = (a_r * hr - a_i * hi + xr_s[rows, cols],
                              a_r * hi + a_i * hr + xi_s[rows, cols])
                    xr_s[rows, cols] = hr
                    xi_s[rows, cols] = hi
                hr_s[:, cols] = hr
                hi_s[:, cols] = hi
            return item
        steps = min(tp, S5_SCAN_GROUP)
        return [group(cb, t0, steps) for t0 in range(p * tp, (p + 1) * tp, steps)
                for cb in range(S5_CH // S5_COLS)]

    def out_proj(p):
        ys = {}

        def re_block(m):
            def item():
                hre = xr_s[part_rows(p), m * S5_HALF:(m + 1) * S5_HALF].astype(BF16)
                ys[m] = (_dot(hre, cre_ref[m])
                         + d_ref[:, m * half:(m + 1) * half] * u[part_rows(p), m * half:(m + 1) * half])
            return item

        def im_block(m):
            def item():
                him = xi_s[part_rows(p), m * S5_HALF:(m + 1) * S5_HALF].astype(BF16)
                ys[m] = ys[m] - _dot(him, cim_ref[m])
            return item

        def act():
            ys["act"] = jax.nn.gelu(jnp.concatenate([ys[0], ys[1]], axis=-1)).astype(BF16)

        def glu(c0):
            def item():
                val = _dot(ys["act"], wglu_ref[:, c0:c0 + blk])
                gate = _dot(ys["act"], wglu_ref[:, D_S5 + c0:D_S5 + c0 + blk])
                y_s[part_rows(p), c0:c0 + blk] = val * jax.nn.sigmoid(gate)
            return item
        return ([re_block(0), im_block(0), re_block(1), im_block(1), act]
                + [glu(c0) for c0 in range(0, D_S5, blk)])

    order = in_proj(0)
    for p in range(nparts):
        nxt = in_proj(p + 1) if p + 1 < nparts else []
        prv = out_proj(p - 1) if p > 0 else []
        order += _interleave(nxt + prv, scan(p)) if (nxt or prv) else scan(p)
    order += out_proj(nparts - 1)
    for item in order:
        item()

    hr_out[...] = hr_s[...]
    hi_out[...] = hi_s[...]
    if batch_major:
        y_ref[...] = jnp.swapaxes(y_s[...].reshape(tt, nb, D_S5), 0, 1)
    else:
        y_ref[...] = y_s[...]


def _s5(u, h0r, h0i, sw, nb, tt, batch_major):
    rows = nb * tt
    const2 = lambda i: (0, 0)
    const3 = lambda i: (0, 0, 0)
    full2 = lambda shape: pl.BlockSpec(shape, const2)
    full3 = lambda shape: pl.BlockSpec(shape, const3)
    if batch_major:
        steps = u.shape[1] // tt
        io_spec = pl.BlockSpec((nb, tt, D_S5), lambda i: (0, i, 0))
    else:
        steps = u.shape[0] // rows
        io_spec = pl.BlockSpec((rows, D_S5), lambda i: (i, 0))
    return pl.pallas_call(
        functools.partial(_s5_kernel, nb=nb, tt=tt, batch_major=batch_major),
        grid=(steps,),
        in_specs=[io_spec,
                  full2((nb, S5_CH)), full2((nb, S5_CH)),
                  full2((1, S5_CH)), full2((1, S5_CH)), full2((1, S5_CH)),
                  full3((2, D_S5 // 2, S5_HALF)), full3((2, D_S5 // 2, S5_HALF)),
                  full3((2, S5_HALF, D_S5 // 2)), full3((2, S5_HALF, D_S5 // 2)),
                  full2((1, D_S5)), full2((D_S5, 2 * D_S5))],
        out_specs=[io_spec, full2((nb, S5_CH)), full2((nb, S5_CH))],
        out_shape=[jax.ShapeDtypeStruct(u.shape, F32),
                   jax.ShapeDtypeStruct((nb, S5_CH), F32),
                   jax.ShapeDtypeStruct((nb, S5_CH), F32)],
        scratch_shapes=[pltpu.VMEM((1, S5_CH), F32), pltpu.VMEM((1, S5_CH), F32),
                        pltpu.VMEM((2, D_S5 // 2, S5_HALF), BF16), pltpu.VMEM((2, D_S5 // 2, S5_HALF), BF16),
                        pltpu.VMEM((nb, S5_CH), F32), pltpu.VMEM((nb, S5_CH), F32),
                        pltpu.VMEM((rows, S5_CH), F32), pltpu.VMEM((rows, S5_CH), F32),
                        pltpu.VMEM((rows, D_S5), F32)],
        compiler_params=pltpu.CompilerParams(dimension_semantics=("arbitrary",),
                                             vmem_limit_bytes=VMEM_LIMIT),
        name="s5",
    )(u, h0r, h0i, sw["a_re"], sw["a_im"], sw["log_dt"], sw["b_re"], sw["b_im"],
      sw["c_re"], sw["c_im"], sw["d"], sw["w_glu"])


def _s5_weights(a_re, a_im, log_dt, b_re, b_im, c_re, c_im, d, w_glu):
    g, n, c = S5_N_GROUPS, S5_STATE, S5_GROUP
    gh = g // 2

    def blockdiag(w, rows_per_group, cols_per_group):
        x = jnp.transpose(w, (0, 2, 1)).reshape(2, gh * rows_per_group, cols_per_group)
        rg = jnp.arange(gh * rows_per_group)[:, None] // rows_per_group
        cg = jnp.arange(gh * cols_per_group)[None, :] // cols_per_group
        return jnp.where(rg == cg, jnp.tile(x, (1, 1, gh)), 0.0)

    b_blockdiag = lambda b: blockdiag(b, c, n)
    c_blockdiag = lambda cm: blockdiag(cm, n, c)

    return {
        "a_re": a_re.reshape(1, S5_CH), "a_im": a_im.reshape(1, S5_CH),
        "log_dt": jnp.repeat(log_dt, n).reshape(1, S5_CH),
        "b_re": b_blockdiag(b_re), "b_im": b_blockdiag(b_im),
        "c_re": c_blockdiag(c_re).astype(BF16), "c_im": c_blockdiag(c_im).astype(BF16),
        "d": d.reshape(1, D_S5), "w_glu": w_glu.astype(BF16),
    }


def _l2norm(x):
    return x * lax.rsqrt(jnp.sum(x * x, axis=-1, keepdims=True) + NORM_EPS)


def _gates(ba, alog_row, dtb_row):
    beta = jax.nn.sigmoid(ba)
    g = -jnp.exp(alog_row) * _softplus(ba + dtb_row)
    return beta, g


def _out_gate(o, og_row, z):
    return _rms(o, og_row) * _silu(z)


def _bdot(a, b):
    return jnp.einsum("bij,bjk->bik", a.astype(BF16), b.astype(BF16), preferred_element_type=F32)


def _tri_merge_level(n, lmat, ii, jj, shift):
    same_big = (ii >> (shift + 1)) == (jj >> (shift + 1))
    same_small = (ii >> shift) == (jj >> shift)
    c = jnp.where((same_big & jnp.logical_not(same_small))[None], lmat, 0.0)
    w = c + _bdot(n, c)
    return n - (w + _bdot(w, n))


def _interleave(*stages):
    out = []
    pos = [0] * len(stages)
    while any(p < len(st) for p, st in zip(pos, stages)):
        k = min((i for i, st in enumerate(stages) if pos[i] < len(st)),
                key=lambda i: (pos[i] + 0.5) / len(stages[i]))
        out.append(stages[k][pos[k]])
        pos[k] += 1
    return out


def _gdn_kernel(x_ref, g1_ref, win_ref, wgate_ref, convw_ref, alog_ref, dtb_ref, og_ref,
                u_ref, y_ref, sfin_ref, convout_ref, s_s, proj_a, proj_b, set_a, set_b, *, tg, nt):
    s = pl.program_id(0)

    @pl.when(s == 0)
    def _zero_buffers():
        for ref in (*proj_b, *set_a):
            ref[...] = jnp.zeros_like(ref)

    @pl.when(lax.rem(jnp.maximum(s - 1, 0), nt) == 0)
    def _new_sequence_conv():
        for ext in (proj_a[0], proj_b[0]):
            ext[0:8, :] = jnp.zeros((8, 3 * D_GDN), F32)

    @pl.when(lax.rem(jnp.maximum(s - 2, 0), nt) == 0)
    def _new_sequence_state():
        s_s[...] = jnp.zeros_like(s_s)

    args = (x_ref, g1_ref, win_ref, wgate_ref, convw_ref, alog_ref, dtb_ref, og_ref, u_ref, y_ref, sfin_ref, convout_ref, s_s)

    @pl.when(lax.rem(s, 2) == 0)
    def _even():
        _gdn_step(*args, proj_wr=proj_a, proj_rd=proj_b, wr=set_b, rd=set_a, tg=tg)

    @pl.when(lax.rem(s, 2) == 1)
    def _odd():
        _gdn_step(*args, proj_wr=proj_b, proj_rd=proj_a, wr=set_a, rd=set_b, tg=tg)


def _gdn_step(x_ref, g1_ref, win_ref, wgate_ref, convw_ref, alog_ref, dtb_ref, og_ref, u_ref, y_ref, sfin_ref, convout_ref,
              s_s, *, proj_wr, proj_rd, wr, rd, tg):
    nc = tg // CHUNK
    nsys = nc * GDN_HEADS
    hcols = lambda h: slice(h * GDN_HEAD_DIM, (h + 1) * GDN_HEAD_DIM)
    og = og_ref[...]
    v = {}
    stage_a, stage_b, stage_c = [], [], []

    ext_w, z_w, ba_w = proj_wr
    o0 = D_S5
    o1 = o0 + 3 * D_GDN
    o2 = o1 + D_GDN

    def a_norm():
        v["n1"] = _rms(x_ref[...], g1_ref[...]).astype(BF16)
    stage_a.append(a_norm)

    def a_block(dst, rows, dst_c0, src_c0, width):
        def item():
            dst[rows, dst_c0:dst_c0 + width] = _dot(v["n1"], win_ref[:, src_c0:src_c0 + width])
        return item
    blk = 256
    for c0 in range(0, o0, blk):
        stage_a.append(a_block(u_ref, slice(None), c0, c0, blk))
    for c0 in range(0, o1 - o0, blk):
        stage_a.append(a_block(ext_w, slice(8, None), c0, o0 + c0, blk))
    for c0 in range(0, o2 - o1, blk):
        stage_a.append(a_block(z_w, slice(None), c0, o1 + c0, blk))

    def a_gates():
        ba_w[...] = _dot(v["n1"], wgate_ref[...])
    stage_a.append(a_gates)

    aq_r, b_r, o_r, gl_r, zs_r = rd
    states = [None] * GDN_HEADS

    def c_chunk(c):
        def item():
            rows = slice(c * CHUNK, (c + 1) * CHUNK)
            for h in range(GDN_HEADS):
                sys = c * GDN_HEADS + h
                st = s_s[h] if c == 0 else states[h]
                r = _dot(aq_r[sys], st.astype(BF16))
                o = r[GDN_HEAD_DIM:] + o_r[sys]
                states[h] = st * gl_r[sys, 0:1, :] + r[:GDN_HEAD_DIM] + b_r[sys]
                y_ref[rows, hcols(h)] = _rms(o, og) * zs_r[rows, hcols(h)]
                if c == nc - 1:
                    s_s[h] = states[h]
                    sfin_ref[0, h] = states[h]
        return item
    stage_c.extend(c_chunk(c) for c in range(nc))

    aq_s, b_s, o_s, gl_s, zs_s = wr
    ext_s, z_r, ba_r = proj_rd
    ii = lax.broadcasted_iota(jnp.int32, (CHUNK, CHUNK), 0)
    jj = lax.broadcasted_iota(jnp.int32, (CHUNK, CHUNK), 1)
    causal = jj <= ii
    strict = jj < ii
    scale = GDN_HEAD_DIM ** -0.5
    lmats = [None] * nsys
    rhss = [None] * nsys
    kgt_attn = [None] * nsys
    qgs = [None] * nsys

    def b_gates():
        raw = ba_r[...].T[:2 * GDN_HEADS, :]
        lanes = lambda p: jnp.concatenate([p] * (tg // GATE_PAD), axis=1)
        beta = jax.nn.sigmoid(raw)
        g = -jnp.exp(lanes(alog_ref[...])) * _softplus(raw + lanes(dtb_ref[...]))
        g1 = g.astype(BF16).astype(F32)
        g2 = (g - g1).astype(BF16).astype(F32)
        g3 = (g - g1 - g2).astype(BF16).astype(F32)
        ri = lax.broadcasted_iota(jnp.int32, (tg, tg), 0)
        ci = lax.broadcasted_iota(jnp.int32, (tg, tg), 1)
        ubd = jnp.where(((ri >> 6) == (ci >> 6)) & (ri <= ci), 1.0, 0.0).astype(BF16)
        parts = _dot(jnp.concatenate([g1, g2, g3], axis=0).astype(BF16), ubd)
        nh2 = 2 * GDN_HEADS
        gc = parts[:nh2] + parts[nh2:2 * nh2] + parts[2 * nh2:]
        row_id = lax.broadcasted_iota(jnp.int32, (nh2, tg), 0)
        rows = jnp.where(row_id < GDN_HEADS, beta, gc)
        cols = jnp.concatenate([rows, jnp.zeros((GATE_PAD - nh2, tg), F32)], axis=0).T
        v["beta_all"] = cols
        v["gc_col"] = cols
        v["gc_row"] = rows
    stage_b.append(b_gates)

    def b_silu_z(h):
        def item():
            zs_s[:, hcols(h)] = _silu(z_r[:, hcols(h)])
        return item

    def b_conv(h, part, name, norm):
        def item():
            c0 = part * D_GDN + h * GDN_HEAD_DIM
            cs = slice(c0, c0 + GDN_HEAD_DIM)
            ext = ext_s[:, cs]
            acc = ext[8:] * convw_ref[GDN_CONV - 1:GDN_CONV, cs]
            for j in range(GDN_CONV - 1):
                acc = acc + _rows_back(ext, GDN_CONV - 1 - j, tg) * convw_ref[j:j + 1, cs]
            acc = _silu(acc)
            v[name] = norm(acc)
        return item

    def b_prep(h, c):
        def item():
            beta_all, gc_col, gc_row = v["beta_all"], v["gc_col"], v["gc_row"]
            sys = c * GDN_HEADS + h
            rows = slice(c * CHUNK, (c + 1) * CHUNK)
            q = v["q"][rows]
            k = v["k"][rows]
            vv = v["v"][rows]
            beta = beta_all[rows, h:h + 1]
            gcc = gc_col[rows, GDN_HEADS + h:GDN_HEADS + h + 1]
            gcr = gc_row[GDN_HEADS + h:GDN_HEADS + h + 1, rows]
            decay = jnp.exp(jnp.where(causal, gcc - gcr, -jnp.inf))
            kb = k * beta
            egc = jnp.exp(gcc)
            g_last = gcc[CHUNK - 1:CHUNK, :]
            kq = _dot_nt(jnp.concatenate([kb, q], axis=0).astype(BF16), k.astype(BF16))
            lmats[sys] = jnp.where(strict, kq[:CHUNK] * decay, 0.0)
            attn = jnp.where(causal, kq[CHUNK:] * decay, 0.0)
            rhss[sys] = jnp.concatenate([vv * beta, kb * egc], axis=1)
            kgt_attn[sys] = jnp.concatenate([(k * jnp.exp(g_last - gcc)).T, attn], axis=0)
            qgs[sys] = q * egc
            gl_s[sys] = jnp.broadcast_to(jnp.exp(g_last), (8, GDN_HEAD_DIM))
        return item

    for h in range(GDN_HEADS):
        stage_b.append(b_silu_z(h))
        stage_b.append(b_conv(h, 0, "q", lambda a: _l2norm(a) * scale))
        stage_b.append(b_conv(h, 1, "k", _l2norm))
        stage_b.append(b_conv(h, 2, "v", lambda a: a))
        stage_b.extend(b_prep(h, c) for c in range(nc))

    def b_inverse_start():
        lmat = jnp.stack(lmats)
        v["lmat"] = lmat
        v["noff"] = -jnp.where(((ii >> 1) == (jj >> 1))[None], lmat, 0.0)
    stage_b.append(b_inverse_start)

    def b_inverse_level(shift):
        def item():
            v["noff"] = _tri_merge_level(v["noff"], v["lmat"], ii, jj, shift)
        return item
    shift = 1
    while (1 << shift) < CHUNK:
        stage_b.append(b_inverse_level(shift))
        shift += 1

    def b_solve():
        rhs = jnp.stack(rhss)
        sol = rhs + _bdot(v["noff"], rhs)
        v["prod"] = _bdot(jnp.stack(kgt_attn), sol)
    stage_b.append(b_solve)

    def b_store():
        prod = v["prod"]
        dk = GDN_HEAD_DIM
        for sys in range(nsys):
            aq_s[sys, :dk, :] = (-prod[sys, :dk, dk:]).astype(BF16)
            aq_s[sys, dk:, :] = (qgs[sys] - prod[sys, dk:, dk:]).astype(BF16)
            b_s[sys] = prod[sys, :dk, :dk]
            o_s[sys] = prod[sys, dk:, :dk]
    stage_b.append(b_store)

    for item in _interleave(stage_a, stage_b, stage_c):
        item()

    ext_w[0:8, :] = ext_s[pl.ds(tg, 8), :]
    convout_ref[0] = ext_s[pl.ds(tg + 8 - (GDN_CONV - 1), GDN_CONV - 1), :]


def _gdn_prompt(x, g1, w_in_bf, w_gate_bf, gw, bsz, seq, tg):
    nt = seq // tg
    ntiles = bsz * nt
    nsys = (tg // CHUNK) * GDN_HEADS
    tile_a = lambda s: jnp.minimum(s, ntiles - 1)
    tile_b = lambda s: jnp.clip(s - 1, 0, ntiles - 1)
    tile_c = lambda s: jnp.maximum(s - 2, 0)
    const = lambda s: (0, 0)
    proj_bufs = lambda: [pltpu.VMEM((tg + 8, 3 * D_GDN), F32),
                         pltpu.VMEM((tg, D_GDN), F32),
                         pltpu.VMEM((tg, GATE_PAD), F32)]
    resident = lambda shape: pl.BlockSpec(shape, const, pipeline_mode=pl.Buffered(1))
    handover = lambda: [pltpu.VMEM((nsys, GDN_HEAD_DIM + CHUNK, GDN_HEAD_DIM), BF16),
                        pltpu.VMEM((nsys, GDN_HEAD_DIM, GDN_HEAD_DIM), F32),
                        pltpu.VMEM((nsys, CHUNK, GDN_HEAD_DIM), F32),
                        pltpu.VMEM((nsys, 8, GDN_HEAD_DIM), F32),
                        pltpu.VMEM((tg, D_GDN), F32)]
    return pl.pallas_call(
        functools.partial(_gdn_kernel, tg=tg, nt=nt),
        grid=(ntiles + 2,),
        in_specs=[pl.BlockSpec((tg, D_MODEL), lambda s: (tile_a(s), 0)),
                  resident((1, D_MODEL)),
                  resident((D_MODEL, D_IN_MAIN)),
                  resident((D_MODEL, GATE_PAD)),
                  resident((GDN_CONV, 3 * D_GDN)),
                  resident((2 * GDN_HEADS, GATE_PAD)),
                  resident((2 * GDN_HEADS, GATE_PAD)),
                  resident((1, GDN_HEAD_DIM))],
        out_specs=[pl.BlockSpec((tg, D_S5), lambda s: (tile_a(s), 0)),
                   pl.BlockSpec((tg, D_GDN), lambda s: (tile_c(s), 0)),
                   pl.BlockSpec((1, GDN_HEADS, GDN_HEAD_DIM, GDN_HEAD_DIM), lambda s: (tile_c(s) // nt, 0, 0, 0)),
                   pl.BlockSpec((1, GDN_CONV - 1, 3 * D_GDN), lambda s: (tile_b(s) // nt, 0, 0))],
        out_shape=[jax.ShapeDtypeStruct((bsz * seq, D_S5), F32),
                   jax.ShapeDtypeStruct((bsz * seq, D_GDN), F32),
                   jax.ShapeDtypeStruct((bsz, GDN_HEADS, GDN_HEAD_DIM, GDN_HEAD_DIM), F32),
                   jax.ShapeDtypeStruct((bsz, GDN_CONV - 1, 3 * D_GDN), F32)],
        scratch_shapes=[pltpu.VMEM((GDN_HEADS, GDN_HEAD_DIM, GDN_HEAD_DIM), F32),
                        proj_bufs(), proj_bufs(), handover(), handover()],
        compiler_params=pltpu.CompilerParams(dimension_semantics=("arbitrary",),
                                             vmem_limit_bytes=VMEM_LIMIT),
        name="proj_gdn_prompt",
    )(x, g1, w_in_bf, w_gate_bf, gw["conv_w"], gw["a_log_col"], gw["dt_bias_col"], gw["onorm_g"])


def _gdn_step_kernel(qkv_ref, st_ref, z_ref, ba_ref, s0_ref, convw_ref, alog_ref, dtb_ref, og_ref,
                     y_ref, s1_ref, *, bb):
    beta_all, g_all = _gates(ba_ref[...], alog_ref[...], dtb_ref[...])
    alpha_all = jnp.exp(g_all)
    scale = GDN_HEAD_DIM ** -0.5
    og = og_ref[...]

    for h in range(GDN_HEADS):
        def conv_cols(part, h=h):
            c0 = part * D_GDN + h * GDN_HEAD_DIM
            cs = slice(c0, c0 + GDN_HEAD_DIM)
            acc = qkv_ref[:, cs] * convw_ref[GDN_CONV - 1:GDN_CONV, cs]
            for j in range(GDN_CONV - 1):
                acc = acc + st_ref[j, :, cs] * convw_ref[j:j + 1, cs]
            return _silu(acc)

        q = _l2norm(conv_cols(0)) * scale
        k = _l2norm(conv_cols(1))
        v = conv_cols(2)
        beta = beta_all[:, h:h + 1]
        alpha = alpha_all[:, GDN_HEADS + h:GDN_HEADS + h + 1]
        qk = jnp.sum(q * k, axis=-1, keepdims=True)
        kt = k.T
        kq = jnp.concatenate([k, q], axis=0).astype(BF16)
        o_rows = []
        for b in range(bb):
            s0 = s0_ref[b, h]
            kcol = kt[:, b:b + 1]
            r = _dot(kq, s0.astype(BF16))
            ks = r[b:b + 1, :]
            qs = r[bb + b:bb + b + 1, :]
            al = alpha[b:b + 1, :]
            v_new = beta[b:b + 1, :] * (v[b:b + 1, :] - al * ks)
            o_rows.append(al * qs + qk[b:b + 1, :] * v_new)
            s1_ref[b, h] = al * s0 + kcol * v_new
        o = jnp.concatenate(o_rows, axis=0)
        cs = slice(h * GDN_HEAD_DIM, (h + 1) * GDN_HEAD_DIM)
        y_ref[:, cs] = _out_gate(o, og, z_ref[:, cs])


def _gdn_sample(qkv, st_t, z, ba, s0, gw, bb):
    n = qkv.shape[0]
    row = lambda i: (i, 0)
    const = lambda i: (0, 0)
    state = lambda i: (i, 0, 0, 0)
    return pl.pallas_call(
        functools.partial(_gdn_step_kernel, bb=bb),
        grid=(n // bb,),
        in_specs=[pl.BlockSpec((bb, 3 * D_GDN), row),
                  pl.BlockSpec((GDN_CONV - 1, bb, 3 * D_GDN), lambda i: (0, i, 0)),
                  pl.BlockSpec((bb, D_GDN), row),
                  pl.BlockSpec((bb, GATE_PAD), row),
                  pl.BlockSpec((bb, GDN_HEADS, GDN_HEAD_DIM, GDN_HEAD_DIM), state),
                  pl.BlockSpec((GDN_CONV, 3 * D_GDN), const),
                  pl.BlockSpec((1, GATE_PAD), const),
                  pl.BlockSpec((1, GATE_PAD), const),
                  pl.BlockSpec((1, GDN_HEAD_DIM), const)],
        out_specs=[pl.BlockSpec((bb, D_GDN), row),
                   pl.BlockSpec((bb, GDN_HEADS, GDN_HEAD_DIM, GDN_HEAD_DIM), state)],
        out_shape=[jax.ShapeDtypeStruct((n, D_GDN), F32),
                   jax.ShapeDtypeStruct(s0.shape, F32)],
        compiler_params=pltpu.CompilerParams(dimension_semantics=("arbitrary",),
                                             vmem_limit_bytes=VMEM_LIMIT),
        name="gdn_sample",
    )(qkv, st_t, z, ba, s0, gw["conv_w"], gw["a_log"], gw["dt_bias"], gw["onorm_g"])


def _residual_in(x_ref, ys5_ref, ygdn_ref, wout_ref, g2_ref, x1_s, n2_s):
    x1 = (x_ref[...] + _dot(ys5_ref[...].astype(BF16), wout_ref[:D_S5, :])
          + _dot(ygdn_ref[...].astype(BF16), wout_ref[D_S5:, :]))
    x1_s[...] = x1
    n2_s[...] = _rms(x1, g2_ref[...]).astype(BF16)


def _ffn_prompt_kernel(x_ref, ys5_ref, ygdn_ref, wout_ref, g2_ref, wup_ref, cw_ref, wdn_ref, gf_ref,
                       y_ref, hlast_ref, x1_s, n2_s, act_s, carry_s, *, tm):
    ti = pl.program_id(1)
    nt = pl.num_programs(1)

    @pl.when(ti == 0)
    def _reset():
        carry_s[...] = jnp.zeros_like(carry_s)

    _residual_in(x_ref, ys5_ref, ygdn_ref, wout_ref, g2_ref, x1_s, n2_s)
    n2 = n2_s[...]
    for f in range(D_FF // FFN_COLS):
        halves = []
        for part in range(2):
            c0 = part * D_FF + f * FFN_COLS
            cs = slice(c0, c0 + FFN_COLS)
            hcur = _dot(n2, wup_ref[:, cs])
            ext = jnp.concatenate([carry_s[:, cs], hcur], axis=0)
            carry_s[:, cs] = hcur[tm - 8:, :]
            conv = (_rows_back(ext, 2, tm) * cw_ref[0:1, cs]
                    + _rows_back(ext, 1, tm) * cw_ref[1:2, cs]
                    + hcur * cw_ref[2:3, cs])
            halves.append(conv)
        act_s[:, f * FFN_COLS:(f + 1) * FFN_COLS] = (_silu(halves[0]) * halves[1]).astype(BF16)
    y_ref[...] = _rms(x1_s[...] + _dot(act_s[...], wdn_ref[...]), gf_ref[...])

    @pl.when(ti == nt - 1)
    def _fin():
        hlast_ref[0] = carry_s[8 - (FFN_CONV - 1):, :]


def _ffn_prompt(x, ys5, ygdn, fw, bsz, seq, tm):
    nt = seq // tm
    row = lambda b, i: (b * nt + i, 0)
    const = lambda b, i: (0, 0)
    resident = lambda shape: pl.BlockSpec(shape, const, pipeline_mode=pl.Buffered(1))
    return pl.pallas_call(
        functools.partial(_ffn_prompt_kernel, tm=tm),
        grid=(bsz, nt),
        in_specs=[pl.BlockSpec((tm, D_MODEL), row),
                  pl.BlockSpec((tm, D_S5), row),
                  pl.BlockSpec((tm, D_GDN), row),
                  resident((D_MODEL, D_MODEL)),
                  resident((1, D_MODEL)),
                  resident((D_MODEL, 2 * D_FF)),
                  resident((FFN_CONV, 2 * D_FF)),
                  resident((D_FF, D_MODEL)),
                  resident((1, D_MODEL))],
        out_specs=[pl.BlockSpec((tm, D_MODEL), row),
                   pl.BlockSpec((1, FFN_CONV - 1, 2 * D_FF), lambda b, i: (b, 0, 0))],
        out_shape=[jax.ShapeDtypeStruct((bsz * seq, D_MODEL), F32),
                   jax.ShapeDtypeStruct((bsz, FFN_CONV - 1, 2 * D_FF), F32)],
        scratch_shapes=[pltpu.VMEM((tm, D_MODEL), F32),
                        pltpu.VMEM((tm, D_MODEL), BF16),
                        pltpu.VMEM((tm, D_FF), BF16),
                        pltpu.VMEM((8, 2 * D_FF), F32)],
        compiler_params=pltpu.CompilerParams(dimension_semantics=("arbitrary", "arbitrary"),
                                             vmem_limit_bytes=VMEM_LIMIT),
        name="ffn_prompt",
    )(x, ys5, ygdn, fw["w_out"], fw["norm2_g"], fw["w_up"], fw["conv_w"], fw["w_down"], fw["normf_g"])


def _ffn_sample_kernel(x_ref, ys5_ref, ygdn_ref, st_ref, wout_ref, g2_ref, wup_ref, cw_ref, wdn_ref, gf_ref,
                       y_ref, h_ref, x1_s, n2_s, acc_s):
    _residual_in(x_ref, ys5_ref, ygdn_ref, wout_ref, g2_ref, x1_s, n2_s)
    n2 = n2_s[...]
    for f in range(D_FF // FFN_COLS):
        halves = []
        for part in range(2):
            c0 = part * D_FF + f * FFN_COLS
            cs = slice(c0, c0 + FFN_COLS)
            hcur = _dot(n2, wup_ref[:, cs])
            h_ref[:, cs] = hcur
            halves.append(st_ref[0, :, cs] * cw_ref[0:1, cs] + st_ref[1, :, cs] * cw_ref[1:2, cs]
                          + hcur * cw_ref[2:3, cs])
        act = (_silu(halves[0]) * halves[1]).astype(BF16)
        contrib = _dot(act, wdn_ref[f * FFN_COLS:(f + 1) * FFN_COLS, :])
        if f == 0:
            acc_s[...] = contrib
        else:
            acc_s[...] += contrib
    y_ref[...] = _rms(x1_s[...] + acc_s[...], gf_ref[...])


def _ffn_sample(x, ys5, ygdn, st_t, fw):
    n = x.shape[0]
    c2 = lambda i: (0, 0)
    c3 = lambda i: (0, 0, 0)
    return pl.pallas_call(
        _ffn_sample_kernel,
        grid=(1,),
        in_specs=[pl.BlockSpec((n, D_MODEL), c2),
                  pl.BlockSpec((n, D_S5), c2),
                  pl.BlockSpec((n, D_GDN), c2),
                  pl.BlockSpec((FFN_CONV - 1, n, 2 * D_FF), c3),
                  pl.BlockSpec((D_MODEL, D_MODEL), c2),
                  pl.BlockSpec((1, D_MODEL), c2),
                  pl.BlockSpec((D_MODEL, 2 * D_FF), c2),
                  pl.BlockSpec((FFN_CONV, 2 * D_FF), c2),
                  pl.BlockSpec((D_FF, D_MODEL), c2),
                  pl.BlockSpec((1, D_MODEL), c2)],
        out_specs=[pl.BlockSpec((n, D_MODEL), c2),
                   pl.BlockSpec((n, 2 * D_FF), c2)],
        out_shape=[jax.ShapeDtypeStruct((n, D_MODEL), F32),
                   jax.ShapeDtypeStruct((n, 2 * D_FF), F32)],
        scratch_shapes=[pltpu.VMEM((n, D_MODEL), F32),
                        pltpu.VMEM((n, D_MODEL), BF16),
                        pltpu.VMEM((n, D_MODEL), F32)],
        compiler_params=pltpu.CompilerParams(dimension_semantics=("arbitrary",),
                                             vmem_limit_bytes=VMEM_LIMIT),
        name="ffn_sample",
    )(x, ys5, ygdn, st_t, fw["w_out"], fw["norm2_g"], fw["w_up"], fw["conv_w"], fw["w_down"], fw["normf_g"])


PROJ_ROWS = 512
S5_STEPS = 128
GDN_ROWS = 512
FFN_ROWS = 1024
GDN_SAMPLE_ROWS = 8


def kernel(x_prompt, x_sample, state_s5_re, state_s5_im, state_gdn, state_gdn_conv, state_ffn_conv, norm1_g, w_in, s5_a_re, s5_a_im, s5_log_dt, s5_b_re, s5_b_im, s5_c_re, s5_c_im, s5_d, s5_w_glu, gdn_conv_w, gdn_a_log, gdn_dt_bias, gdn_onorm_g, w_out, norm2_g, ffn_w_up, ffn_conv_w, ffn_w_down, normf_g):
    depth = w_in.shape[0]
    assert depth == 1, "the final rmsnorm is fused into the last layer's ffn kernel"
    bsz, seq, _ = x_prompt.shape
    nsmp = x_sample.shape[0]
    assert x_sample.shape[1] == 1
    l = 0

    xp = x_prompt.reshape(bsz * seq, D_MODEL)
    xs = x_sample.reshape(nsmp, D_MODEL)

    g1 = norm1_g[l].reshape(1, D_MODEL)
    w_in_bf = w_in[l][:, :D_IN_MAIN].astype(BF16)
    w_gate_bf = jnp.pad(w_in[l][:, D_IN_MAIN:], ((0, 0), (0, GATE_PAD - 2 * GDN_HEADS))).astype(BF16)
    sw = _s5_weights(s5_a_re[l], s5_a_im[l], s5_log_dt[l], s5_b_re[l], s5_b_im[l], s5_c_re[l], s5_c_im[l],
                     s5_d[l], s5_w_glu[l])
    gate_row = lambda v: jnp.pad(v, (GDN_HEADS, GATE_PAD - 2 * GDN_HEADS)).reshape(1, GATE_PAD)
    gate_col = lambda v: jnp.broadcast_to(jnp.pad(v, (GDN_HEADS, 0))[:, None], (2 * GDN_HEADS, GATE_PAD))
    gw = {"conv_w": gdn_conv_w[l], "a_log": gate_row(gdn_a_log[l]), "dt_bias": gate_row(gdn_dt_bias[l]),
          "a_log_col": gate_col(gdn_a_log[l]), "dt_bias_col": gate_col(gdn_dt_bias[l]),
          "onorm_g": gdn_onorm_g[l].reshape(1, GDN_HEAD_DIM)}
    fw = {"w_out": w_out[l].astype(BF16), "norm2_g": norm2_g[l].reshape(1, D_MODEL),
          "w_up": ffn_w_up[l].astype(BF16), "conv_w": ffn_conv_w[l], "w_down": ffn_w_down[l].astype(BF16),
          "normf_g": normf_g.reshape(1, D_MODEL)}

    u_p, ygdn_p, p_gdn, p_gdn_conv = _gdn_prompt(xp, g1, w_in_bf, w_gate_bf, gw, bsz, seq, GDN_ROWS)
    zeros_h = jnp.zeros((bsz, S5_CH), F32)
    ys5_p, p_hr, p_hi = _s5(u_p.reshape(bsz, seq, D_S5), zeros_h, zeros_h, sw, bsz, S5_STEPS, True)
    ys5_p = ys5_p.reshape(bsz * seq, D_S5)
    y_p, p_ffn_conv = _ffn_prompt(xp, ys5_p, ygdn_p, fw, bsz, seq, FFN_ROWS)

    u_s, qkv_s, z_s, ba_s = _proj(xs, g1, w_in_bf, w_gate_bf, nsmp)
    ys5_s, s_hr, s_hi = _s5(u_s, state_s5_re[l].reshape(nsmp, S5_CH), state_s5_im[l].reshape(nsmp, S5_CH),
                            sw, nsmp, 1, False)
    gconv_t = state_gdn_conv[l].transpose(1, 0, 2)
    ygdn_s, s_gdn = _gdn_sample(qkv_s, gconv_t, z_s, ba_s, state_gdn[l], gw, GDN_SAMPLE_ROWS)
    fconv_t = state_ffn_conv[l].transpose(1, 0, 2)
    y_s, h_s = _ffn_sample(xs, ys5_s, ygdn_s, fconv_t, fw)
    s_gdn_conv = jnp.concatenate([state_gdn_conv[l][:, 1:], qkv_s[:, None, :]], axis=1)
    s_ffn_conv = jnp.concatenate([state_ffn_conv[l][:, 1:], h_s[:, None, :]], axis=1)

    st = lambda a: a[None]
    s5_shape = lambda a, n: a.reshape(1, n, S5_N_GROUPS, S5_STATE)
    return (y_p.reshape(bsz, seq, D_MODEL), y_s.reshape(nsmp, 1, D_MODEL),
            s5_shape(p_hr, bsz), s5_shape(p_hi, bsz), st(p_gdn), st(p_gdn_conv), st(p_ffn_conv),
            s5_shape(s_hr, nsmp), s5_shape(s_hi, nsmp), st(s_gdn), st(s_gdn_conv), st(s_ffn_conv))
```

```python
import functools

import jax
import jax.numpy as jnp
from jax import lax
from jax.experimental import pallas as pl
from jax.experimental.pallas import tpu as pltpu

F32 = jnp.float32
BF16 = jnp.bfloat16
NORM_EPS = 1e-6

D_MODEL = 1024
D_S5 = 512
S5_GROUP = 16
S5_N_GROUPS = 32
S5_STATE = 64
S5_CH = S5_N_GROUPS * S5_STATE
D_GDN = 512
GDN_HEADS = 4
GDN_HEAD_DIM = 128
GDN_CONV = 4
D_FF = 2816
FFN_CONV = 3
D_IN = D_S5 + 4 * D_GDN + 2 * GDN_HEADS
GATE_PAD = 128
D_IN_MAIN = D_S5 + 4 * D_GDN

CHUNK = 64
FFN_COLS = 256
VMEM_LIMIT = 56 * 1024 * 1024


def _dot(a, b):
    return jnp.dot(a, b, preferred_element_type=F32)


def _dot_nt(a, b):
    return lax.dot_general(a, b, (((1,), (1,)), ((), ())), preferred_element_type=F32)


def _rms(x, g):
    ms = jnp.mean(x * x, axis=-1, keepdims=True)
    return x * lax.rsqrt(ms + NORM_EPS) * g


def _silu(x):
    return x * jax.nn.sigmoid(x)


def _rows_back(ext, back, n):
    return pltpu.roll(ext, back, 0)[8:8 + n]


def _softplus(x):
    return jnp.maximum(x, 0.0) + jnp.log1p(jnp.exp(-jnp.abs(x)))


def _proj_kernel(x_ref, g_ref, w_ref, wgate_ref, u_ref, qkv_ref, z_ref, ba_ref):
    n = _rms(x_ref[...], g_ref[...]).astype(BF16)
    o0 = D_S5
    o1 = o0 + 3 * D_GDN
    u_ref[...] = _dot(n, w_ref[:, :o0])
    qkv_ref[...] = _dot(n, w_ref[:, o0:o1])
    z_ref[...] = _dot(n, w_ref[:, o1:])
    ba_ref[...] = _dot(n, wgate_ref[...])


def _proj(x, g1, w_in_bf, w_gate_bf, tm):
    n = x.shape[0]
    row = lambda i: (i, 0)
    const = lambda i: (0, 0)
    return pl.pallas_call(
        _proj_kernel,
        grid=(n // tm,),
        in_specs=[pl.BlockSpec((tm, D_MODEL), row),
                  pl.BlockSpec((1, D_MODEL), const),
                  pl.BlockSpec((D_MODEL, D_IN_MAIN), const),
                  pl.BlockSpec((D_MODEL, GATE_PAD), const)],
        out_specs=[pl.BlockSpec((tm, D_S5), row),
                   pl.BlockSpec((tm, 3 * D_GDN), row),
                   pl.BlockSpec((tm, D_GDN), row),
                   pl.BlockSpec((tm, GATE_PAD), row)],
        out_shape=[jax.ShapeDtypeStruct((n, D_S5), F32),
                   jax.ShapeDtypeStruct((n, 3 * D_GDN), F32),
                   jax.ShapeDtypeStruct((n, D_GDN), F32),
                   jax.ShapeDtypeStruct((n, GATE_PAD), F32)],
        compiler_params=pltpu.CompilerParams(dimension_semantics=("arbitrary",),
                                             vmem_limit_bytes=VMEM_LIMIT),
        name="proj",
    )(x, g1, w_in_bf, w_gate_bf)


S5_HALF = S5_CH // 2
S5_COLS = 1024
S5_SCAN_GROUP = 8
S5_PARTS = 4


def _s5_kernel(u_ref, h0r_ref, h0i_ref, are_ref, aim_ref, ldt_ref, bre_ref, bim_ref, cre_ref, cim_ref,
               d_ref, wglu_ref, y_ref, hr_out, hi_out,
               abr_s, abi_s, wbr_s, wbi_s, hr_s, hi_s, xr_s, xi_s, y_s, *, nb, tt, batch_major):
    @pl.when(pl.program_id(0) == 0)
    def _init():
        ar = are_ref[...]
        ai = aim_ref[...]
        dt = jnp.exp(ldt_ref[...])
        mag = jnp.exp(ar * dt)
        abr = mag * jnp.cos(ai * dt)
        abi = mag * jnp.sin(ai * dt)
        den = ar * ar + ai * ai
        p = abr - 1.0
        fr = (p * ar + abi * ai) / den
        fi = (abi * ar - p * ai) / den
        abr_s[...] = abr
        abi_s[...] = abi
        for m in range(2):
            frm = fr[:, m * S5_HALF:(m + 1) * S5_HALF]
            fim = fi[:, m * S5_HALF:(m + 1) * S5_HALF]
            wr = bre_ref[m]
            wi = bim_ref[m]
            wbr_s[m] = (wr * frm - wi * fim).astype(BF16)
            wbi_s[m] = (wr * fim + wi * frm).astype(BF16)
        hr_s[...] = h0r_ref[...]
        hi_s[...] = h0i_ref[...]

    if batch_major:
        u = jnp.swapaxes(u_ref[...], 0, 1).reshape(tt * nb, D_S5)
    else:
        u = u_ref[...]
    ub = u.astype(BF16)
    half = D_S5 // 2

    nparts = S5_PARTS if tt % (S5_PARTS * S5_SCAN_GROUP) == 0 else 1
    tp = tt // nparts
    part_rows = lambda p: slice(p * tp * nb, (p + 1) * tp * nb)

    blk = 256

    def in_proj(p):
        def block(m, ws, xs, c0):
            def item():
                xs[part_rows(p), m * S5_HALF + c0:m * S5_HALF + c0 + blk] = _dot(
                    ub[part_rows(p), m * half:(m + 1) * half], ws[m, :, c0:c0 + blk])
            return item
        return [block(m, ws, xs, c0) for m in range(2) for ws, xs in ((wbr_s, xr_s), (wbi_s, xi_s))
                for c0 in range(0, S5_HALF, blk)]

    def scan(p):
        def group(cb, t0, steps):
            def item():
                cols = slice(cb * S5_COLS, (cb + 1) * S5_COLS)
                a_r = jnp.broadcast_to(abr_s[:, cols], (nb, S5_COLS))
                a_i = jnp.broadcast_to(abi_s[:, cols], (nb, S5_COLS))
                hr = hr_s[:, cols]
                hi = hi_s[:, cols]
                for t in range(t0, t0 + steps):
                    rows = slice(t * nb, (t + 1) * nb)
                    hr, hi = (a_r * hr - a_i * hi + xr_s[rows, cols],
                              a_r * hi + a_i * hr + xi_s[rows, cols])
                    xr_s[rows, cols] = hr
                    xi_s[rows, cols] = hi
                hr_s[:, cols] = hr
                hi_s[:, cols] = hi
            return item
        steps = min(tp, S5_SCAN_GROUP)
        return [group(cb, t0, steps) for t0 in range(p * tp, (p + 1) * tp, steps)
                for cb in range(S5_CH // S5_COLS)]

    def out_proj(p):
        ys = {}

        def re_block(m):
            def item():
                hre = xr_s[part_rows(p), m * S5_HALF:(m + 1) * S5_HALF].astype(BF16)
                ys[m] = (_dot(hre, cre_ref[m])
                         + d_ref[:, m * half:(m + 1) * half] * u[part_rows(p), m * half:(m + 1) * half])
            return item

        def im_block(m):
            def item():
                him = xi_s[part_rows(p), m * S5_HALF:(m + 1) * S5_HALF].astype(BF16)
                ys[m] = ys[m] - _dot(him, cim_ref[m])
            return item

        def act():
            ys["act"] = jax.nn.gelu(jnp.concatenate([ys[0], ys[1]], axis=-1)).astype(BF16)

        def glu(c0):
            def item():
                val = _dot(ys["act"], wglu_ref[:, c0:c0 + blk])
                gate = _dot(ys["act"], wglu_ref[:, D_S5 + c0:D_S5 + c0 + blk])
                y_s[part_rows(p), c0:c0 + blk] = val * jax.nn.sigmoid(gate)
            return item
        return ([re_block(0), im_block(0), re_block(1), im_block(1), act]
                + [glu(c0) for c0 in range(0, D_S5, blk)])

    order = in_proj(0)
    for p in range(nparts):
        nxt = in_proj(p + 1) if p + 1 < nparts else []
        prv = out_proj(p - 1) if p > 0 else []
        order += _interleave(nxt + prv, scan(p)) if (nxt or prv) else scan(p)
    order += out_proj(nparts - 1)
    for item in order:
        item()

    hr_out[...] = hr_s[...]
    hi_out[...] = hi_s[...]
    if batch_major:
        y_ref[...] = jnp.swapaxes(y_s[...].reshape(tt, nb, D_S5), 0, 1).astype(y_ref.dtype)
    else:
        y_ref[...] = y_s[...]


def _s5(u, h0r, h0i, sw, nb, tt, batch_major):
    rows = nb * tt
    const2 = lambda i: (0, 0)
    const3 = lambda i: (0, 0, 0)
    full2 = lambda shape: pl.BlockSpec(shape, const2)
    full3 = lambda shape: pl.BlockSpec(shape, const3)
    if batch_major:
        steps = u.shape[1] // tt
        io_spec = pl.BlockSpec((nb, tt, D_S5), lambda i: (0, i, 0))
    else:
        steps = u.shape[0] // rows
        io_spec = pl.BlockSpec((rows, D_S5), lambda i: (i, 0))
    return pl.pallas_call(
        functools.partial(_s5_kernel, nb=nb, tt=tt, batch_major=batch_major),
        grid=(steps,),
        in_specs=[io_spec,
                  full2((nb, S5_CH)), full2((nb, S5_CH)),
                  full2((1, S5_CH)), full2((1, S5_CH)), full2((1, S5_CH)),
                  full3((2, D_S5 // 2, S5_HALF)), full3((2, D_S5 // 2, S5_HALF)),
                  full3((2, S5_HALF, D_S5 // 2)), full3((2, S5_HALF, D_S5 // 2)),
                  full2((1, D_S5)), full2((D_S5, 2 * D_S5))],
        out_specs=[io_spec, full2((nb, S5_CH)), full2((nb, S5_CH))],
        out_shape=[jax.ShapeDtypeStruct(u.shape, BF16 if batch_major else F32),
                   jax.ShapeDtypeStruct((nb, S5_CH), F32),
                   jax.ShapeDtypeStruct((nb, S5_CH), F32)],
        scratch_shapes=[pltpu.VMEM((1, S5_CH), F32), pltpu.VMEM((1, S5_CH), F32),
                        pltpu.VMEM((2, D_S5 // 2, S5_HALF), BF16), pltpu.VMEM((2, D_S5 // 2, S5_HALF), BF16),
                        pltpu.VMEM((nb, S5_CH), F32), pltpu.VMEM((nb, S5_CH), F32),
                        pltpu.VMEM((rows, S5_CH), F32), pltpu.VMEM((rows, S5_CH), F32),
                        pltpu.VMEM((rows, D_S5), F32)],
        compiler_params=pltpu.CompilerParams(dimension_semantics=("arbitrary",),
                                             vmem_limit_bytes=VMEM_LIMIT),
        name="s5",
    )(u, h0r, h0i, sw["a_re"], sw["a_im"], sw["log_dt"], sw["b_re"], sw["b_im"],
      sw["c_re"], sw["c_im"], sw["d"], sw["w_glu"])


def _s5_weights(a_re, a_im, log_dt, b_re, b_im, c_re, c_im, d, w_glu):
    g, n, c = S5_N_GROUPS, S5_STATE, S5_GROUP
    gh = g // 2

    def blockdiag(w, rows_per_group, cols_per_group):
        x = jnp.transpose(w, (0, 2, 1)).reshape(2, gh * rows_per_group, cols_per_group)
        rg = jnp.arange(gh * rows_per_group)[:, None] // rows_per_group
        cg = jnp.arange(gh * cols_per_group)[None, :] // cols_per_group
        return jnp.where(rg == cg, jnp.tile(x, (1, 1, gh)), 0.0)

    b_blockdiag = lambda b: blockdiag(b, c, n)
    c_blockdiag = lambda cm: blockdiag(cm, n, c)

    return {
        "a_re": a_re.reshape(1, S5_CH), "a_im": a_im.reshape(1, S5_CH),
        "log_dt": jnp.repeat(log_dt, n).reshape(1, S5_CH),
        "b_re": b_blockdiag(b_re), "b_im": b_blockdiag(b_im),
        "c_re": c_blockdiag(c_re).astype(BF16), "c_im": c_blockdiag(c_im).astype(BF16),
        "d": d.reshape(1, D_S5), "w_glu": w_glu.astype(BF16),
    }


def _l2norm(x):
    return x * lax.rsqrt(jnp.sum(x * x, axis=-1, keepdims=True) + NORM_EPS)


def _gates(ba, alog_row, dtb_row):
    beta = jax.nn.sigmoid(ba)
    g = -jnp.exp(alog_row) * _softplus(ba + dtb_row)
    return beta, g


def _out_gate(o, og_row, z):
    return _rms(o, og_row) * _silu(z)


def _bdot(a, b):
    return jnp.einsum("bij,bjk->bik", a.astype(BF16), b.astype(BF16), preferred_element_type=F32)


def _tri_merge_level(n, lmat, ii, jj, shift):
    same_big = (ii >> (shift + 1)) == (jj >> (shift + 1))
    same_small = (ii >> shift) == (jj >> shift)
    c = jnp.where((same_big & jnp.logical_not(same_small))[None], lmat, 0.0)
    w = c + _bdot(n, c)
    return n - (w + _bdot(w, n))


def _interleave(*stages):
    out = []
    pos = [0] * len(stages)
    while any(p < len(st) for p, st in zip(pos, stages)):
        k = min((i for i, st in enumerate(stages) if pos[i] < len(st)),
                key=lambda i: (pos[i] + 0.5) / len(stages[i]))
        out.append(stages[k][pos[k]])
        pos[k] += 1
    return out


def _gdn_kernel(x_ref, g1_ref, win_ref, wgate_ref, convw_ref, alog_ref, dtb_ref, og_ref,
                u_ref, y_ref, sfin_ref, convout_ref, s_s, proj_a, proj_b, set_a, set_b, *, tg, nt):
    s = pl.program_id(0)

    @pl.when(s == 0)
    def _zero_buffers():
        for ref in (*proj_b, *set_a):
            ref[...] = jnp.zeros_like(ref)

    @pl.when(lax.rem(jnp.maximum(s - 1, 0), nt) == 0)
    def _new_sequence_conv():
        for ext in (proj_a[0], proj_b[0]):
            ext[0:8, :] = jnp.zeros((8, 3 * D_GDN), F32)

    @pl.when(lax.rem(jnp.maximum(s - 2, 0), nt) == 0)
    def _new_sequence_state():
        s_s[...] = jnp.zeros_like(s_s)

    args = (x_ref, g1_ref, win_ref, wgate_ref, convw_ref, alog_ref, dtb_ref, og_ref, u_ref, y_ref, sfin_ref, convout_ref, s_s)

    @pl.when(lax.rem(s, 2) == 0)
    def _even():
        _gdn_step(*args, proj_wr=proj_a, proj_rd=proj_b, wr=set_b, rd=set_a, tg=tg)

    @pl.when(lax.rem(s, 2) == 1)
    def _odd():
        _gdn_step(*args, proj_wr=proj_b, proj_rd=proj_a, wr=set_a, rd=set_b, tg=tg)


def _gdn_step(x_ref, g1_ref, win_ref, wgate_ref, convw_ref, alog_ref, dtb_ref, og_ref, u_ref, y_ref, sfin_ref, convout_ref,
              s_s, *, proj_wr, proj_rd, wr, rd, tg):
    nc = tg // CHUNK
    nsys = nc * GDN_HEADS
    hcols = lambda h: slice(h * GDN_HEAD_DIM, (h + 1) * GDN_HEAD_DIM)
    og = og_ref[...]
    v = {}
    stage_a, stage_b, stage_c = [], [], []

    ext_w, z_w, ba_w = proj_wr
    o0 = D_S5
    o1 = o0 + 3 * D_GDN
    o2 = o1 + D_GDN

    def a_norm():
        v["n1"] = _rms(x_ref[...], g1_ref[...]).astype(BF16)
    stage_a.append(a_norm)

    def a_block(dst, rows, dst_c0, src_c0, width):
        def item():
            dst[rows, dst_c0:dst_c0 + width] = _dot(v["n1"], win_ref[:, src_c0:src_c0 + width])
        return item
    blk = 256
    for c0 in range(0, o0, blk):
        stage_a.append(a_block(u_ref, slice(None), c0, c0, blk))
    for c0 in range(0, o1 - o0, blk):
        stage_a.append(a_block(ext_w, slice(8, None), c0, o0 + c0, blk))
    for c0 in range(0, o2 - o1, blk):
        stage_a.append(a_block(z_w, slice(None), c0, o1 + c0, blk))

    def a_gates():
        ba_w[...] = _dot(v["n1"], wgate_ref[...])
    stage_a.append(a_gates)

    aq_r, b_r, o_r, gl_r, zs_r = rd
    states = [None] * GDN_HEADS

    def c_chunk(c):
        def item():
            rows = slice(c * CHUNK, (c + 1) * CHUNK)
            for h in range(GDN_HEADS):
                sys = c * GDN_HEADS + h
                st = s_s[h] if c == 0 else states[h]
                r = _dot(aq_r[sys], st.astype(BF16))
                o = r[GDN_HEAD_DIM:] + o_r[sys]
                states[h] = st * gl_r[sys, 0:1, :] + r[:GDN_HEAD_DIM] + b_r[sys]
                y_ref[rows, hcols(h)] = (_rms(o, og) * zs_r[rows, hcols(h)]).astype(BF16)
                if c == nc - 1:
                    s_s[h] = states[h]
                    sfin_ref[0, h] = states[h]
        return item
    stage_c.extend(c_chunk(c) for c in range(nc))

    aq_s, b_s, o_s, gl_s, zs_s = wr
    ext_s, z_r, ba_r = proj_rd
    ii = lax.broadcasted_iota(jnp.int32, (CHUNK, CHUNK), 0)
    jj = lax.broadcasted_iota(jnp.int32, (CHUNK, CHUNK), 1)
    causal = jj <= ii
    strict = jj < ii
    scale = GDN_HEAD_DIM ** -0.5
    lmats = [None] * nsys
    rhss = [None] * nsys
    kgt_attn = [None] * nsys
    qgs = [None] * nsys

    def b_gates():
        raw = ba_r[...].T[:2 * GDN_HEADS, :]
        lanes = lambda p: jnp.concatenate([p] * (tg // GATE_PAD), axis=1)
        beta = jax.nn.sigmoid(raw)
        g = -jnp.exp(lanes(alog_ref[...])) * _softplus(raw + lanes(dtb_ref[...]))
        g1 = g.astype(BF16).astype(F32)
        g2 = (g - g1).astype(BF16).astype(F32)
        g3 = (g - g1 - g2).astype(BF16).astype(F32)
        ri = lax.broadcasted_iota(jnp.int32, (tg, tg), 0)
        ci = lax.broadcasted_iota(jnp.int32, (tg, tg), 1)
        ubd = jnp.where(((ri >> 6) == (ci >> 6)) & (ri <= ci), 1.0, 0.0).astype(BF16)
        parts = _dot(jnp.concatenate([g1, g2, g3], axis=0).astype(BF16), ubd)
        nh2 = 2 * GDN_HEADS
        gc = parts[:nh2] + parts[nh2:2 * nh2] + parts[2 * nh2:]
        row_id = lax.broadcasted_iota(jnp.int32, (nh2, tg), 0)
        rows = jnp.where(row_id < GDN_HEADS, beta, gc)
        cols = jnp.concatenate([rows, jnp.zeros((GATE_PAD - nh2, tg), F32)], axis=0).T
        v["beta_all"] = cols
        v["gc_col"] = cols
        v["gc_row"] = rows
    stage_b.append(b_gates)

    def b_silu_z(h):
        def item():
            zs_s[:, hcols(h)] = _silu(z_r[:, hcols(h)])
        return item

    def b_conv(h, part, name, norm):
        def item():
            c0 = part * D_GDN + h * GDN_HEAD_DIM
            cs = slice(c0, c0 + GDN_HEAD_DIM)
            ext = ext_s[:, cs]
            acc = ext[8:] * convw_ref[GDN_CONV - 1:GDN_CONV, cs]
            for j in range(GDN_CONV - 1):
                acc = acc + _rows_back(ext, GDN_CONV - 1 - j, tg) * convw_ref[j:j + 1, cs]
            acc = _silu(acc)
            v[name] = norm(acc)
        return item

    def b_prep(h, c):
        def item():
            beta_all, gc_col, gc_row = v["beta_all"], v["gc_col"], v["gc_row"]
            sys = c * GDN_HEADS + h
            rows = slice(c * CHUNK, (c + 1) * CHUNK)
            q = v["q"][rows]
            k = v["k"][rows]
            vv = v["v"][rows]
            beta = beta_all[rows, h:h + 1]
            gcc = gc_col[rows, GDN_HEADS + h:GDN_HEADS + h + 1]
            gcr = gc_row[GDN_HEADS + h:GDN_HEADS + h + 1, rows]
            decay = jnp.exp(jnp.where(causal, gcc - gcr, -jnp.inf))
            kb = k * beta
            egc = jnp.exp(gcc)
            g_last = gcc[CHUNK - 1:CHUNK, :]
            kq = _dot_nt(jnp.concatenate([kb, q], axis=0).astype(BF16), k.astype(BF16))
            lmats[sys] = jnp.where(strict, kq[:CHUNK] * decay, 0.0)
            attn = jnp.where(causal, kq[CHUNK:] * decay, 0.0)
            rhss[sys] = jnp.concatenate([vv * beta, kb * egc], axis=1)
            kgt_attn[sys] = jnp.concatenate([(k * jnp.exp(g_last - gcc)).T, attn], axis=0)
            qgs[sys] = q * egc
            gl_s[sys] = jnp.broadcast_to(jnp.exp(g_last), (8, GDN_HEAD_DIM))
        return item

    for h in range(GDN_HEADS):
        stage_b.append(b_silu_z(h))
        stage_b.append(b_conv(h, 0, "q", lambda a: _l2norm(a) * scale))
        stage_b.append(b_conv(h, 1, "k", _l2norm))
        stage_b.append(b_conv(h, 2, "v", lambda a: a))
        stage_b.extend(b_prep(h, c) for c in range(nc))

    def b_inverse_start():
        lmat = jnp.stack(lmats)
        v["lmat"] = lmat
        v["noff"] = -jnp.where(((ii >> 1) == (jj >> 1))[None], lmat, 0.0)
    stage_b.append(b_inverse_start)

    def b_inverse_level(shift):
        def item():
            v["noff"] = _tri_merge_level(v["noff"], v["lmat"], ii, jj, shift)
        return item
    shift = 1
    while (1 << shift) < CHUNK:
        stage_b.append(b_inverse_level(shift))
        shift += 1

    def b_solve():
        rhs = jnp.stack(rhss)
        sol = rhs + _bdot(v["noff"], rhs)
        v["prod"] = _bdot(jnp.stack(kgt_attn), sol)
    stage_b.append(b_solve)

    def b_store():
        prod = v["prod"]
        dk = GDN_HEAD_DIM
        for sys in range(nsys):
            aq_s[sys, :dk, :] = (-prod[sys, :dk, dk:]).astype(BF16)
            aq_s[sys, dk:, :] = (qgs[sys] - prod[sys, dk:, dk:]).astype(BF16)
            b_s[sys] = prod[sys, :dk, :dk]
            o_s[sys] = prod[sys, dk:, :dk]
    stage_b.append(b_store)

    for item in _interleave(stage_a, stage_b, stage_c):
        item()

    ext_w[0:8, :] = ext_s[pl.ds(tg, 8), :]
    convout_ref[0] = ext_s[pl.ds(tg + 8 - (GDN_CONV - 1), GDN_CONV - 1), :]


def _gdn_prompt(x, g1, w_in_bf, w_gate_bf, gw, bsz, seq, tg):
    nt = seq // tg
    ntiles = bsz * nt
    nsys = (tg // CHUNK) * GDN_HEADS
    tile_a = lambda s: jnp.minimum(s, ntiles - 1)
    tile_b = lambda s: jnp.clip(s - 1, 0, ntiles - 1)
    tile_c = lambda s: jnp.maximum(s - 2, 0)
    const = lambda s: (0, 0)
    proj_bufs = lambda: [pltpu.VMEM((tg + 8, 3 * D_GDN), F32),
                         pltpu.VMEM((tg, D_GDN), F32),
                         pltpu.VMEM((tg, GATE_PAD), F32)]
    resident = lambda shape: pl.BlockSpec(shape, const, pipeline_mode=pl.Buffered(1))
    handover = lambda: [pltpu.VMEM((nsys, GDN_HEAD_DIM + CHUNK, GDN_HEAD_DIM), BF16),
                        pltpu.VMEM((nsys, GDN_HEAD_DIM, GDN_HEAD_DIM), F32),
                        pltpu.VMEM((nsys, CHUNK, GDN_HEAD_DIM), F32),
                        pltpu.VMEM((nsys, 8, GDN_HEAD_DIM), F32),
                        pltpu.VMEM((tg, D_GDN), F32)]
    return pl.pallas_call(
        functools.partial(_gdn_kernel, tg=tg, nt=nt),
        grid=(ntiles + 2,),
        in_specs=[pl.BlockSpec((tg, D_MODEL), lambda s: (tile_a(s), 0)),
                  resident((1, D_MODEL)),
                  resident((D_MODEL, D_IN_MAIN)),
                  resident((D_MODEL, GATE_PAD)),
                  resident((GDN_CONV, 3 * D_GDN)),
                  resident((2 * GDN_HEADS, GATE_PAD)),
                  resident((2 * GDN_HEADS, GATE_PAD)),
                  resident((1, GDN_HEAD_DIM))],
        out_specs=[pl.BlockSpec((tg, D_S5), lambda s: (tile_a(s), 0)),
                   pl.BlockSpec((tg, D_GDN), lambda s: (tile_c(s), 0)),
                   pl.BlockSpec((1, GDN_HEADS, GDN_HEAD_DIM, GDN_HEAD_DIM), lambda s: (tile_c(s) // nt, 0, 0, 0)),
                   pl.BlockSpec((1, GDN_CONV - 1, 3 * D_GDN), lambda s: (tile_b(s) // nt, 0, 0))],
        out_shape=[jax.ShapeDtypeStruct((bsz * seq, D_S5), F32),
                   jax.ShapeDtypeStruct((bsz * seq, D_GDN), BF16),
                   jax.ShapeDtypeStruct((bsz, GDN_HEADS, GDN_HEAD_DIM, GDN_HEAD_DIM), F32),
                   jax.ShapeDtypeStruct((bsz, GDN_CONV - 1, 3 * D_GDN), F32)],
        scratch_shapes=[pltpu.VMEM((GDN_HEADS, GDN_HEAD_DIM, GDN_HEAD_DIM), F32),
                        proj_bufs(), proj_bufs(), handover(), handover()],
        compiler_params=pltpu.CompilerParams(dimension_semantics=("arbitrary",),
                                             vmem_limit_bytes=VMEM_LIMIT),
        name="proj_gdn_prompt",
    )(x, g1, w_in_bf, w_gate_bf, gw["conv_w"], gw["a_log_col"], gw["dt_bias_col"], gw["onorm_g"])


def _gdn_step_kernel(qkv_ref, st_ref, z_ref, ba_ref, s0_ref, convw_ref, alog_ref, dtb_ref, og_ref,
                     y_ref, s1_ref, *, bb):
    beta_all, g_all = _gates(ba_ref[...], alog_ref[...], dtb_ref[...])
    alpha_all = jnp.exp(g_all)
    scale = GDN_HEAD_DIM ** -0.5
    og = og_ref[...]

    for h in range(GDN_HEADS):
        def conv_cols(part, h=h):
            c0 = part * D_GDN + h * GDN_HEAD_DIM
            cs = slice(c0, c0 + GDN_HEAD_DIM)
            acc = qkv_ref[:, cs] * convw_ref[GDN_CONV - 1:GDN_CONV, cs]
            for j in range(GDN_CONV - 1):
                acc = acc + st_ref[j, :, cs] * convw_ref[j:j + 1, cs]
            return _silu(acc)

        q = _l2norm(conv_cols(0)) * scale
        k = _l2norm(conv_cols(1))
        v = conv_cols(2)
        beta = beta_all[:, h:h + 1]
        alpha = alpha_all[:, GDN_HEADS + h:GDN_HEADS + h + 1]
        qk = jnp.sum(q * k, axis=-1, keepdims=True)
        kt = k.T
        kq = jnp.concatenate([k, q], axis=0).astype(BF16)
        o_rows = []
        for b in range(bb):
            s0 = s0_ref[b, h]
            kcol = kt[:, b:b + 1]
            r = _dot(kq, s0.astype(BF16))
            ks = r[b:b + 1, :]
            qs = r[bb + b:bb + b + 1, :]
            al = alpha[b:b + 1, :]
            v_new = beta[b:b + 1, :] * (v[b:b + 1, :] - al * ks)
            o_rows.append(al * qs + qk[b:b + 1, :] * v_new)
            s1_ref[b, h] = al * s0 + kcol * v_new
        o = jnp.concatenate(o_rows, axis=0)
        cs = slice(h * GDN_HEAD_DIM, (h + 1) * GDN_HEAD_DIM)
        y_ref[:, cs] = _out_gate(o, og, z_ref[:, cs])


def _gdn_sample(qkv, st_t, z, ba, s0, gw, bb):
    n = qkv.shape[0]
    row = lambda i: (i, 0)
    const = lambda i: (0, 0)
    state = lambda i: (i, 0, 0, 0)
    return pl.pallas_call(
        functools.partial(_gdn_step_kernel, bb=bb),
        grid=(n // bb,),
        in_specs=[pl.BlockSpec((bb, 3 * D_GDN), row),
                  pl.BlockSpec((GDN_CONV - 1, bb, 3 * D_GDN), lambda i: (0, i, 0)),
                  pl.BlockSpec((bb, D_GDN), row),
                  pl.BlockSpec((bb, GATE_PAD), row),
                  pl.BlockSpec((bb, GDN_HEADS, GDN_HEAD_DIM, GDN_HEAD_DIM), state),
                  pl.BlockSpec((GDN_CONV, 3 * D_GDN), const),
                  pl.BlockSpec((1, GATE_PAD), const),
                  pl.BlockSpec((1, GATE_PAD), const),
                  pl.BlockSpec((1, GDN_HEAD_DIM), const)],
        out_specs=[pl.BlockSpec((bb, D_GDN), row),
                   pl.BlockSpec((bb, GDN_HEADS, GDN_HEAD_DIM, GDN_HEAD_DIM), state)],
        out_shape=[jax.ShapeDtypeStruct((n, D_GDN), F32),
                   jax.ShapeDtypeStruct(s0.shape, F32)],
        compiler_params=pltpu.CompilerParams(dimension_semantics=("arbitrary",),
                                             vmem_limit_bytes=VMEM_LIMIT),
        name="gdn_sample",
    )(qkv, st_t, z, ba, s0, gw["conv_w"], gw["a_log"], gw["dt_bias"], gw["onorm_g"])


def _residual_in(x_ref, ys5_ref, ygdn_ref, wout_ref, g2_ref, x1_s, n2_s):
    x1 = (x_ref[...] + _dot(ys5_ref[...].astype(BF16), wout_ref[:D_S5, :])
          + _dot(ygdn_ref[...].astype(BF16), wout_ref[D_S5:, :]))
    x1_s[...] = x1
    n2_s[...] = _rms(x1, g2_ref[...]).astype(BF16)


def _ffn_prompt_kernel(x_ref, ys5_ref, ygdn_ref, wout_ref, g2_ref, wup_ref, cw_ref, wdn_ref, gf_ref,
                       y_ref, hlast_ref, x1_s, n2_s, act_s, carry_s, *, tm):
    ti = pl.program_id(1)
    nt = pl.num_programs(1)

    @pl.when(ti == 0)
    def _reset():
        carry_s[...] = jnp.zeros_like(carry_s)

    _residual_in(x_ref, ys5_ref, ygdn_ref, wout_ref, g2_ref, x1_s, n2_s)
    n2 = n2_s[...]
    for f in range(D_FF // FFN_COLS):
        halves = []
        for part in range(2):
            c0 = part * D_FF + f * FFN_COLS
            cs = slice(c0, c0 + FFN_COLS)
            hcur = _dot(n2, wup_ref[:, cs])
            ext = jnp.concatenate([carry_s[:, cs], hcur], axis=0)
            carry_s[:, cs] = hcur[tm - 8:, :]
            conv = (_rows_back(ext, 2, tm) * cw_ref[0:1, cs]
                    + _rows_back(ext, 1, tm) * cw_ref[1:2, cs]
                    + hcur * cw_ref[2:3, cs])
            halves.append(conv)
        act_s[:, f * FFN_COLS:(f + 1) * FFN_COLS] = (_silu(halves[0]) * halves[1]).astype(BF16)
    y_ref[...] = _rms(x1_s[...] + _dot(act_s[...], wdn_ref[...]), gf_ref[...])

    @pl.when(ti == nt - 1)
    def _fin():
        hlast_ref[0] = carry_s[8 - (FFN_CONV - 1):, :]


def _ffn_prompt(x, ys5, ygdn, fw, bsz, seq, tm):
    nt = seq // tm
    row = lambda b, i: (b * nt + i, 0)
    const = lambda b, i: (0, 0)
    resident = lambda shape: pl.BlockSpec(shape, const, pipeline_mode=pl.Buffered(1))
    return pl.pallas_call(
        functools.partial(_ffn_prompt_kernel, tm=tm),
        grid=(bsz, nt),
        in_specs=[pl.BlockSpec((tm, D_MODEL), row),
                  pl.BlockSpec((tm, D_S5), row),
                  pl.BlockSpec((tm, D_GDN), row),
                  resident((D_MODEL, D_MODEL)),
                  resident((1, D_MODEL)),
                  resident((D_MODEL, 2 * D_FF)),
                  resident((FFN_CONV, 2 * D_FF)),
                  resident((D_FF, D_MODEL)),
                  resident((1, D_MODEL))],
        out_specs=[pl.BlockSpec((tm, D_MODEL), row),
                   pl.BlockSpec((1, FFN_CONV - 1, 2 * D_FF), lambda b, i: (b, 0, 0))],
        out_shape=[jax.ShapeDtypeStruct((bsz * seq, D_MODEL), F32),
                   jax.ShapeDtypeStruct((bsz, FFN_CONV - 1, 2 * D_FF), F32)],
        scratch_shapes=[pltpu.VMEM((tm, D_MODEL), F32),
                        pltpu.VMEM((tm, D_MODEL), BF16),
                        pltpu.VMEM((tm, D_FF), BF16),
                        pltpu.VMEM((8, 2 * D_FF), F32)],
        compiler_params=pltpu.CompilerParams(dimension_semantics=("arbitrary", "arbitrary"),
                                             vmem_limit_bytes=VMEM_LIMIT),
        name="ffn_prompt",
    )(x, ys5, ygdn, fw["w_out"], fw["norm2_g"], fw["w_up"], fw["conv_w"], fw["w_down"], fw["normf_g"])


def _ffn_sample_kernel(x_ref, ys5_ref, ygdn_ref, st_ref, wout_ref, g2_ref, wup_ref, cw_ref, wdn_ref, gf_ref,
                       y_ref, h_ref, x1_s, n2_s, acc_s):
    _residual_in(x_ref, ys5_ref, ygdn_ref, wout_ref, g2_ref, x1_s, n2_s)
    n2 = n2_s[...]
    for f in range(D_FF // FFN_COLS):
        halves = []
        for part in range(2):
            c0 = part * D_FF + f * FFN_COLS
            cs = slice(c0, c0 + FFN_COLS)
            hcur = _dot(n2, wup_ref[:, cs])
            h_ref[:, cs] = hcur
            halves.append(st_ref[0, :, cs] * cw_ref[0:1, cs] + st_ref[1, :, cs] * cw_ref[1:2, cs]
                          + hcur * cw_ref[2:3, cs])
        act = (_silu(halves[0]) * halves[1]).astype(BF16)
        contrib = _dot(act, wdn_ref[f * FFN_COLS:(f + 1) * FFN_COLS, :])
        if f == 0:
            acc_s[...] = contrib
        else:
            acc_s[...] += contrib
    y_ref[...] = _rms(x1_s[...] + acc_s[...], gf_ref[...])


def _ffn_sample(x, ys5, ygdn, st_t, fw):
    n = x.shape[0]
    c2 = lambda i: (0, 0)
    c3 = lambda i: (0, 0, 0)
    return pl.pallas_call(
        _ffn_sample_kernel,
        grid=(1,),
        in_specs=[pl.BlockSpec((n, D_MODEL), c2),
                  pl.BlockSpec((n, D_S5), c2),
                  pl.BlockSpec((n, D_GDN), c2),
                  pl.BlockSpec((FFN_CONV - 1, n, 2 * D_FF), c3),
                  pl.BlockSpec((D_MODEL, D_MODEL), c2),
                  pl.BlockSpec((1, D_MODEL), c2),
                  pl.BlockSpec((D_MODEL, 2 * D_FF), c2),
                  pl.BlockSpec((FFN_CONV, 2 * D_FF), c2),
                  pl.BlockSpec((D_FF, D_MODEL), c2),
                  pl.BlockSpec((1, D_MODEL), c2)],
        out_specs=[pl.BlockSpec((n, D_MODEL), c2),
                   pl.BlockSpec((n, 2 * D_FF), c2)],
        out_shape=[jax.ShapeDtypeStruct((n, D_MODEL), F32),
                   jax.ShapeDtypeStruct((n, 2 * D_FF), F32)],
        scratch_shapes=[pltpu.VMEM((n, D_MODEL), F32),
                        pltpu.VMEM((n, D_MODEL), BF16),
                        pltpu.VMEM((n, D_MODEL), F32)],
        compiler_params=pltpu.CompilerParams(dimension_semantics=("arbitrary",),
                                             vmem_limit_bytes=VMEM_LIMIT),
        name="ffn_sample",
    )(x, ys5, ygdn, st_t, fw["w_out"], fw["norm2_g"], fw["w_up"], fw["conv_w"], fw["w_down"], fw["normf_g"])


PROJ_ROWS = 512
S5_STEPS = 128
GDN_ROWS = 512
FFN_ROWS = 1024
GDN_SAMPLE_ROWS = 8


def kernel(x_prompt, x_sample, state_s5_re, state_s5_im, state_gdn, state_gdn_conv, state_ffn_conv, norm1_g, w_in, s5_a_re, s5_a_im, s5_log_dt, s5_b_re, s5_b_im, s5_c_re, s5_c_im, s5_d, s5_w_glu, gdn_conv_w, gdn_a_log, gdn_dt_bias, gdn_onorm_g, w_out, norm2_g, ffn_w_up, ffn_conv_w, ffn_w_down, normf_g):
    depth = w_in.shape[0]
    assert depth == 1, "the final rmsnorm is fused into the last layer's ffn kernel"
    bsz, seq, _ = x_prompt.shape
    nsmp = x_sample.shape[0]
    assert x_sample.shape[1] == 1
    l = 0

    xp = x_prompt.reshape(bsz * seq, D_MODEL)
    xs = x_sample.reshape(nsmp, D_MODEL)

    g1 = norm1_g[l].reshape(1, D_MODEL)
    w_in_bf = w_in[l][:, :D_IN_MAIN].astype(BF16)
    w_gate_bf = jnp.pad(w_in[l][:, D_IN_MAIN:], ((0, 0), (0, GATE_PAD - 2 * GDN_HEADS))).astype(BF16)
    sw = _s5_weights(s5_a_re[l], s5_a_im[l], s5_log_dt[l], s5_b_re[l], s5_b_im[l], s5_c_re[l], s5_c_im[l],
                     s5_d[l], s5_w_glu[l])
    gate_row = lambda v: jnp.pad(v, (GDN_HEADS, GATE_PAD - 2 * GDN_HEADS)).reshape(1, GATE_PAD)
    gate_col = lambda v: jnp.broadcast_to(jnp.pad(v, (GDN_HEADS, 0))[:, None], (2 * GDN_HEADS, GATE_PAD))
    gw = {"conv_w": gdn_conv_w[l], "a_log": gate_row(gdn_a_log[l]), "dt_bias": gate_row(gdn_dt_bias[l]),
          "a_log_col": gate_col(gdn_a_log[l]), "dt_bias_col": gate_col(gdn_dt_bias[l]),
          "onorm_g": gdn_onorm_g[l].reshape(1, GDN_HEAD_DIM)}
    fw = {"w_out": w_out[l].astype(BF16), "norm2_g": norm2_g[l].reshape(1, D_MODEL),
          "w_up": ffn_w_up[l].astype(BF16), "conv_w": ffn_conv_w[l], "w_down": ffn_w_down[l].astype(BF16),
          "normf_g": normf_g.reshape(1, D_MODEL)}

    u_p, ygdn_p, p_gdn, p_gdn_conv = _gdn_prompt(xp, g1, w_in_bf, w_gate_bf, gw, bsz, seq, GDN_ROWS)
    zeros_h = jnp.zeros((bsz, S5_CH), F32)
    ys5_p, p_hr, p_hi = _s5(u_p.reshape(bsz, seq, D_S5), zeros_h, zeros_h, sw, bsz, S5_STEPS, True)
    ys5_p = ys5_p.reshape(bsz * seq, D_S5)
    y_p, p_ffn_conv = _ffn_prompt(xp, ys5_p, ygdn_p, fw, bsz, seq, FFN_ROWS)

    u_s, qkv_s, z_s, ba_s = _proj(xs, g1, w_in_bf, w_gate_bf, nsmp)
    ys5_s, s_hr, s_hi = _s5(u_s, state_s5_re[l].reshape(nsmp, S5_CH), state_s5_im[l].reshape(nsmp, S5_CH),
                            sw, nsmp, 1, False)
    gconv_t = state_gdn_conv[l].transpose(1, 0, 2)
    ygdn_s, s_gdn = _gdn_sample(qkv_s, gconv_t, z_s, ba_s, state_gdn[l], gw, GDN_SAMPLE_ROWS)
    fconv_t = state_ffn_conv[l].transpose(1, 0, 2)
    y_s, h_s = _ffn_sample(xs, ys5_s, ygdn_s, fconv_t, fw)
    s_gdn_conv = jnp.concatenate([state_gdn_conv[l][:, 1:], qkv_s[:, None, :]], axis=1)
    s_ffn_conv = jnp.concatenate([state_ffn_conv[l][:, 1:], h_s[:, None, :]], axis=1)

    st = lambda a: a[None]
    s5_shape = lambda a, n: a.reshape(1, n, S5_N_GROUPS, S5_STATE)
    return (y_p.reshape(bsz, seq, D_MODEL), y_s.reshape(nsmp, 1, D_MODEL),
            s5_shape(p_hr, bsz), s5_shape(p_hi, bsz), st(p_gdn), st(p_gdn_conv), st(p_ffn_conv),
            s5_shape(s_hr, nsmp), s5_shape(s_hi, nsmp), st(s_gdn), st(s_gdn_conv), st(s_ffn_conv))
```

```python
import functools

import jax
import jax.numpy as jnp
from jax import lax
from jax.experimental import pallas as pl
from jax.experimental.pallas import tpu as pltpu

F32 = jnp.float32
BF16 = jnp.bfloat16
NORM_EPS = 1e-6

D_MODEL = 1024
D_S5 = 512
S5_GROUP = 16
S5_N_GROUPS = 32
S5_STATE = 64
S5_CH = S5_N_GROUPS * S5_STATE
D_GDN = 512
GDN_HEADS = 4
GDN_HEAD_DIM = 128
GDN_CONV = 4
D_FF = 2816
FFN_CONV = 3
D_IN = D_S5 + 4 * D_GDN + 2 * GDN_HEADS
GATE_PAD = 128
D_IN_MAIN = D_S5 + 4 * D_GDN

CHUNK = 64
FFN_COLS = 256
VMEM_LIMIT = 56 * 1024 * 1024


def _dot(a, b):
    return jnp.dot(a, b, preferred_element_type=F32)


def _dot_nt(a, b):
    return lax.dot_general(a, b, (((1,), (1,)), ((), ())), preferred_element_type=F32)


def _rms(x, g):
    ms = jnp.mean(x * x, axis=-1, keepdims=True)
    return x * lax.rsqrt(ms + NORM_EPS) * g


def _silu(x):
    return x * jax.nn.sigmoid(x)


def _rows_back(ext, back, n):
    return pltpu.roll(ext, back, 0)[8:8 + n]


def _softplus(x):
    return jnp.maximum(x, 0.0) + jnp.log1p(jnp.exp(-jnp.abs(x)))


def _proj_kernel(x_ref, g_ref, w_ref, wgate_ref, u_ref, qkv_ref, z_ref, ba_ref):
    n = _rms(x_ref[...], g_ref[...]).astype(BF16)
    o0 = D_S5
    o1 = o0 + 3 * D_GDN
    u_ref[...] = _dot(n, w_ref[:, :o0])
    qkv_ref[...] = _dot(n, w_ref[:, o0:o1])
    z_ref[...] = _dot(n, w_ref[:, o1:])
    ba_ref[...] = _dot(n, wgate_ref[...])


def _proj(x, g1, w_in_bf, w_gate_bf, tm):
    n = x.shape[0]
    row = lambda i: (i, 0)
    const = lambda i: (0, 0)
    return pl.pallas_call(
        _proj_kernel,
        grid=(n // tm,),
        in_specs=[pl.BlockSpec((tm, D_MODEL), row),
                  pl.BlockSpec((1, D_MODEL), const),
                  pl.BlockSpec((D_MODEL, D_IN_MAIN), const),
                  pl.BlockSpec((D_MODEL, GATE_PAD), const)],
        out_specs=[pl.BlockSpec((tm, D_S5), row),
                   pl.BlockSpec((tm, 3 * D_GDN), row),
                   pl.BlockSpec((tm, D_GDN), row),
                   pl.BlockSpec((tm, GATE_PAD), row)],
        out_shape=[jax.ShapeDtypeStruct((n, D_S5), F32),
                   jax.ShapeDtypeStruct((n, 3 * D_GDN), F32),
                   jax.ShapeDtypeStruct((n, D_GDN), F32),
                   jax.ShapeDtypeStruct((n, GATE_PAD), F32)],
        compiler_params=pltpu.CompilerParams(dimension_semantics=("arbitrary",),
                                             vmem_limit_bytes=VMEM_LIMIT),
        name="proj",
    )(x, g1, w_in_bf, w_gate_bf)


S5_HALF = S5_CH // 2
S5_COLS = 1024
S5_SCAN_GROUP = 8
S5_PARTS = 4


def _s5_kernel(u_ref, h0r_ref, h0i_ref, are_ref, aim_ref, ldt_ref, bre_ref, bim_ref, cre_ref, cim_ref,
               d_ref, wglu_ref, y_ref, hr_out, hi_out,
               abr_s, abi_s, wbr_s, wbi_s, hr_s, hi_s, xr_s, xi_s, y_s, *, nb, tt, batch_major):
    @pl.when(pl.program_id(0) == 0)
    def _init():
        ar = are_ref[...]
        ai = aim_ref[...]
        dt = jnp.exp(ldt_ref[...])
        mag = jnp.exp(ar * dt)
        abr = mag * jnp.cos(ai * dt)
        abi = mag * jnp.sin(ai * dt)
        den = ar * ar + ai * ai
        p = abr - 1.0
        fr = (p * ar + abi * ai) / den
        fi = (abi * ar - p * ai) / den
        abr_s[...] = abr
        abi_s[...] = abi
        for m in range(2):
            frm = fr[:, m * S5_HALF:(m + 1) * S5_HALF]
            fim = fi[:, m * S5_HALF:(m + 1) * S5_HALF]
            wr = bre_ref[m]
            wi = bim_ref[m]
            wbr_s[m] = (wr * frm - wi * fim).astype(BF16)
            wbi_s[m] = (wr * fim + wi * frm).astype(BF16)
        hr_s[...] = h0r_ref[...]
        hi_s[...] = h0i_ref[...]

    if batch_major:
        u = jnp.swapaxes(u_ref[...], 0, 1).reshape(tt * nb, D_S5)
    else:
        u = u_ref[...]
    ub = u.astype(BF16)
    half = D_S5 // 2

    nparts = S5_PARTS if tt % (S5_PARTS * S5_SCAN_GROUP) == 0 else 1
    tp = tt // nparts
    part_rows = lambda p: slice(p * tp * nb, (p + 1) * tp * nb)

    blk = 256

    def in_proj(p):
        def block(m, ws, xs, c0):
            def item():
                xs[part_rows(p), m * S5_HALF + c0:m * S5_HALF + c0 + blk] = _dot(
                    ub[part_rows(p), m * half:(m + 1) * half], ws[m, :, c0:c0 + blk])
            return item
        return [block(m, ws, xs, c0) for m in range(2) for ws, xs in ((wbr_s, xr_s), (wbi_s, xi_s))
                for c0 in range(0, S5_HALF, blk)]

    def scan(p):
        def group(cb, t0, steps):
            def item():
                cols = slice(cb * S5_COLS, (cb + 1) * S5_COLS)
                a_r = jnp.broadcast_to(abr_s[:, cols], (nb, S5_COLS))
                a_i = jnp.broadcast_to(abi_s[:, cols], (nb, S5_COLS))
                hr = hr_s[:, cols]
                hi = hi_s[:, cols]
                for t in range(t0, t0 + steps):
                    rows = slice(t * nb, (t + 1) * nb)
                    hr, hi = (a_r * hr - a_i * hi + xr_s[rows, cols],
                              a_r * hi + a_i * hr + xi_s[rows, cols])
                    xr_s[rows, cols] = hr
                    xi_s[rows, cols] = hi
                hr_s[:, cols] = hr
                hi_s[:, cols] = hi
            return item
        steps = min(tp, S5_SCAN_GROUP)
        return [group(cb, t0, steps) for t0 in range(p * tp, (p + 1) * tp, steps)
                for cb in range(S5_CH // S5_COLS)]

    def out_proj(p):
        ys = {}

        def re_block(m):
            def item():
                hre = xr_s[part_rows(p), m * S5_HALF:(m + 1) * S5_HALF].astype(BF16)
                ys[m] = (_dot(hre, cre_ref[m])
                         + d_ref[:, m * half:(m + 1) * half] * u[part_rows(p), m * half:(m + 1) * half])
            return item

        def im_block(m):
            def item():
                him = xi_s[part_rows(p), m * S5_HALF:(m + 1) * S5_HALF].astype(BF16)
                ys[m] = ys[m] - _dot(him, cim_ref[m])
            return item

        def act():
            ys["act"] = jax.nn.gelu(jnp.concatenate([ys[0], ys[1]], axis=-1)).astype(BF16)

        def glu(c0):
            def item():
                val = _dot(ys["act"], wglu_ref[:, c0:c0 + blk])
                gate = _dot(ys["act"], wglu_ref[:, D_S5 + c0:D_S5 + c0 + blk])
                y_s[part_rows(p), c0:c0 + blk] = val * jax.nn.sigmoid(gate)
            return item
        return ([re_block(0), im_block(0), re_block(1), im_block(1), act]
                + [glu(c0) for c0 in range(0, D_S5, blk)])

    order = in_proj(0)
    for p in range(nparts):
        nxt = in_proj(p + 1) if p + 1 < nparts else []
        prv = out_proj(p - 1) if p > 0 else []
        order += _interleave(nxt + prv, scan(p)) if (nxt or prv) else scan(p)
    order += out_proj(nparts - 1)
    for item in order:
        item()

    hr_out[...] = hr_s[...]
    hi_out[...] = hi_s[...]
    if batch_major:
        y_ref[...] = jnp.swapaxes(y_s[...].reshape(tt, nb, D_S5), 0, 1).astype(y_ref.dtype)
    else:
        y_ref[...] = y_s[...].astype(y_ref.dtype)


def _s5(u, h0r, h0i, sw, nb, tt, batch_major):
    rows = nb * tt
    const2 = lambda i: (0, 0)
    const3 = lambda i: (0, 0, 0)
    full2 = lambda shape: pl.BlockSpec(shape, const2)
    full3 = lambda shape: pl.BlockSpec(shape, const3)
    if batch_major:
        steps = u.shape[1] // tt
        io_spec = pl.BlockSpec((nb, tt, D_S5), lambda i: (0, i, 0))
    else:
        steps = u.shape[0] // rows
        io_spec = pl.BlockSpec((rows, D_S5), lambda i: (i, 0))
    return pl.pallas_call(
        functools.partial(_s5_kernel, nb=nb, tt=tt, batch_major=batch_major),
        grid=(steps,),
        in_specs=[io_spec,
                  full2((nb, S5_CH)), full2((nb, S5_CH)),
                  full2((1, S5_CH)), full2((1, S5_CH)), full2((1, S5_CH)),
                  full3((2, D_S5 // 2, S5_HALF)), full3((2, D_S5 // 2, S5_HALF)),
                  full3((2, S5_HALF, D_S5 // 2)), full3((2, S5_HALF, D_S5 // 2)),
                  full2((1, D_S5)), full2((D_S5, 2 * D_S5))],
        out_specs=[io_spec, full2((nb, S5_CH)), full2((nb, S5_CH))],
        out_shape=[jax.ShapeDtypeStruct(u.shape, BF16),
                   jax.ShapeDtypeStruct((nb, S5_CH), F32),
                   jax.ShapeDtypeStruct((nb, S5_CH), F32)],
        scratch_shapes=[pltpu.VMEM((1, S5_CH), F32), pltpu.VMEM((1, S5_CH), F32),
                        pltpu.VMEM((2, D_S5 // 2, S5_HALF), BF16), pltpu.VMEM((2, D_S5 // 2, S5_HALF), BF16),
                        pltpu.VMEM((nb, S5_CH), F32), pltpu.VMEM((nb, S5_CH), F32),
                        pltpu.VMEM((rows, S5_CH), F32), pltpu.VMEM((rows, S5_CH), F32),
                        pltpu.VMEM((rows, D_S5), F32)],
        compiler_params=pltpu.CompilerParams(dimension_semantics=("arbitrary",),
                                             vmem_limit_bytes=VMEM_LIMIT),
        name="s5",
    )(u, h0r, h0i, sw["a_re"], sw["a_im"], sw["log_dt"], sw["b_re"], sw["b_im"],
      sw["c_re"], sw["c_im"], sw["d"], sw["w_glu"])


def _s5_weights(a_re, a_im, log_dt, b_re, b_im, c_re, c_im, d, w_glu):
    g, n, c = S5_N_GROUPS, S5_STATE, S5_GROUP
    gh = g // 2

    def blockdiag(w, rows_per_group, cols_per_group):
        x = jnp.transpose(w, (0, 2, 1)).reshape(2, gh * rows_per_group, cols_per_group)
        rg = jnp.arange(gh * rows_per_group)[:, None] // rows_per_group
        cg = jnp.arange(gh * cols_per_group)[None, :] // cols_per_group
        return jnp.where(rg == cg, jnp.tile(x, (1, 1, gh)), 0.0)

    b_blockdiag = lambda b: blockdiag(b, c, n)
    c_blockdiag = lambda cm: blockdiag(cm, n, c)

    return {
        "a_re": a_re.reshape(1, S5_CH), "a_im": a_im.reshape(1, S5_CH),
        "log_dt": jnp.repeat(log_dt, n).reshape(1, S5_CH),
        "b_re": b_blockdiag(b_re), "b_im": b_blockdiag(b_im),
        "c_re": c_blockdiag(c_re).astype(BF16), "c_im": c_blockdiag(c_im).astype(BF16),
        "d": d.reshape(1, D_S5), "w_glu": w_glu.astype(BF16),
    }


def _l2norm(x):
    return x * lax.rsqrt(jnp.sum(x * x, axis=-1, keepdims=True) + NORM_EPS)


def _gates(ba, alog_row, dtb_row):
    beta = jax.nn.sigmoid(ba)
    g = -jnp.exp(alog_row) * _softplus(ba + dtb_row)
    return beta, g


def _out_gate(o, og_row, z):
    return _rms(o, og_row) * _silu(z)


def _bdot(a, b):
    return jnp.einsum("bij,bjk->bik", a.astype(BF16), b.astype(BF16), preferred_element_type=F32)


def _tri_merge_level(n, lmat, ii, jj, shift):
    same_big = (ii >> (shift + 1)) == (jj >> (shift + 1))
    same_small = (ii >> shift) == (jj >> shift)
    c = jnp.where((same_big & jnp.logical_not(same_small))[None], lmat, 0.0)
    w = c + _bdot(n, c)
    return n - (w + _bdot(w, n))


def _interleave(*stages):
    out = []
    pos = [0] * len(stages)
    while any(p < len(st) for p, st in zip(pos, stages)):
        k = min((i for i, st in enumerate(stages) if pos[i] < len(st)),
                key=lambda i: (pos[i] + 0.5) / len(stages[i]))
        out.append(stages[k][pos[k]])
        pos[k] += 1
    return out


def _gdn_kernel(x_ref, g1_ref, win_ref, wgate_ref, convw_ref, alog_ref, dtb_ref, og_ref,
                u_ref, y_ref, sfin_ref, convout_ref, s_s, proj_a, proj_b, set_a, set_b, *, tg, nt):
    s = pl.program_id(0)

    @pl.when(s == 0)
    def _zero_buffers():
        for ref in (*proj_b, *set_a):
            ref[...] = jnp.zeros_like(ref)

    @pl.when(lax.rem(jnp.maximum(s - 1, 0), nt) == 0)
    def _new_sequence_conv():
        for ext in (proj_a[0], proj_b[0]):
            ext[0:8, :] = jnp.zeros((8, 3 * D_GDN), F32)

    @pl.when(lax.rem(jnp.maximum(s - 2, 0), nt) == 0)
    def _new_sequence_state():
        s_s[...] = jnp.zeros_like(s_s)

    args = (x_ref, g1_ref, win_ref, wgate_ref, convw_ref, alog_ref, dtb_ref, og_ref, u_ref, y_ref, sfin_ref, convout_ref, s_s)

    @pl.when(lax.rem(s, 2) == 0)
    def _even():
        _gdn_step(*args, proj_wr=proj_a, proj_rd=proj_b, wr=set_b, rd=set_a, tg=tg)

    @pl.when(lax.rem(s, 2) == 1)
    def _odd():
        _gdn_step(*args, proj_wr=proj_b, proj_rd=proj_a, wr=set_a, rd=set_b, tg=tg)


def _gdn_step(x_ref, g1_ref, win_ref, wgate_ref, convw_ref, alog_ref, dtb_ref, og_ref, u_ref, y_ref, sfin_ref, convout_ref,
              s_s, *, proj_wr, proj_rd, wr, rd, tg):
    nc = tg // CHUNK
    nsys = nc * GDN_HEADS
    hcols = lambda h: slice(h * GDN_HEAD_DIM, (h + 1) * GDN_HEAD_DIM)
    og = og_ref[...]
    v = {}
    stage_a, stage_b, stage_c = [], [], []

    ext_w, z_w, ba_w = proj_wr
    o0 = D_S5
    o1 = o0 + 3 * D_GDN
    o2 = o1 + D_GDN

    def a_norm():
        v["n1"] = _rms(x_ref[...], g1_ref[...]).astype(BF16)
    stage_a.append(a_norm)

    def a_block(dst, rows, dst_c0, src_c0, width):
        def item():
            dst[rows, dst_c0:dst_c0 + width] = _dot(v["n1"], win_ref[:, src_c0:src_c0 + width])
        return item
    blk = 256
    for c0 in range(0, o0, blk):
        stage_a.append(a_block(u_ref, slice(None), c0, c0, blk))
    for c0 in range(0, o1 - o0, blk):
        stage_a.append(a_block(ext_w, slice(8, None), c0, o0 + c0, blk))
    for c0 in range(0, o2 - o1, blk):
        stage_a.append(a_block(z_w, slice(None), c0, o1 + c0, blk))

    def a_gates():
        ba_w[...] = _dot(v["n1"], wgate_ref[...])
    stage_a.append(a_gates)

    aq_r, b_r, o_r, gl_r, zs_r = rd
    states = [None] * GDN_HEADS

    def c_chunk(c):
        def item():
            rows = slice(c * CHUNK, (c + 1) * CHUNK)
            for h in range(GDN_HEADS):
                sys = c * GDN_HEADS + h
                st = s_s[h] if c == 0 else states[h]
                r = _dot(aq_r[sys], st.astype(BF16))
                o = r[GDN_HEAD_DIM:] + o_r[sys]
                states[h] = st * gl_r[sys, 0:1, :] + r[:GDN_HEAD_DIM] + b_r[sys]
                y_ref[rows, hcols(h)] = (_rms(o, og) * zs_r[rows, hcols(h)]).astype(BF16)
                if c == nc - 1:
                    s_s[h] = states[h]
                    sfin_ref[0, h] = states[h]
        return item
    stage_c.extend(c_chunk(c) for c in range(nc))

    aq_s, b_s, o_s, gl_s, zs_s = wr
    ext_s, z_r, ba_r = proj_rd
    ii = lax.broadcasted_iota(jnp.int32, (CHUNK, CHUNK), 0)
    jj = lax.broadcasted_iota(jnp.int32, (CHUNK, CHUNK), 1)
    causal = jj <= ii
    strict = jj < ii
    scale = GDN_HEAD_DIM ** -0.5
    lmats = [None] * nsys
    rhss = [None] * nsys
    kgt_attn = [None] * nsys
    qgs = [None] * nsys

    def b_gates():
        raw = ba_r[...].T[:2 * GDN_HEADS, :]
        lanes = lambda p: jnp.concatenate([p] * (tg // GATE_PAD), axis=1)
        beta = jax.nn.sigmoid(raw)
        g = -jnp.exp(lanes(alog_ref[...])) * _softplus(raw + lanes(dtb_ref[...]))
        g1 = g.astype(BF16).astype(F32)
        g2 = (g - g1).astype(BF16).astype(F32)
        g3 = (g - g1 - g2).astype(BF16).astype(F32)
        ri = lax.broadcasted_iota(jnp.int32, (tg, tg), 0)
        ci = lax.broadcasted_iota(jnp.int32, (tg, tg), 1)
        ubd = jnp.where(((ri >> 6) == (ci >> 6)) & (ri <= ci), 1.0, 0.0).astype(BF16)
        parts = _dot(jnp.concatenate([g1, g2, g3], axis=0).astype(BF16), ubd)
        nh2 = 2 * GDN_HEADS
        gc = parts[:nh2] + parts[nh2:2 * nh2] + parts[2 * nh2:]
        row_id = lax.broadcasted_iota(jnp.int32, (nh2, tg), 0)
        rows = jnp.where(row_id < GDN_HEADS, beta, gc)
        cols = jnp.concatenate([rows, jnp.zeros((GATE_PAD - nh2, tg), F32)], axis=0).T
        v["beta_all"] = cols
        v["gc_col"] = cols
        v["gc_row"] = rows
    stage_b.append(b_gates)

    def b_silu_z(h):
        def item():
            zs_s[:, hcols(h)] = _silu(z_r[:, hcols(h)])
        return item

    def b_conv(h, part, name, norm):
        def item():
            c0 = part * D_GDN + h * GDN_HEAD_DIM
            cs = slice(c0, c0 + GDN_HEAD_DIM)
            ext = ext_s[:, cs]
            acc = ext[8:] * convw_ref[GDN_CONV - 1:GDN_CONV, cs]
            for j in range(GDN_CONV - 1):
                acc = acc + _rows_back(ext, GDN_CONV - 1 - j, tg) * convw_ref[j:j + 1, cs]
            acc = _silu(acc)
            v[name] = norm(acc)
        return item

    def b_prep(h, c):
        def item():
            beta_all, gc_col, gc_row = v["beta_all"], v["gc_col"], v["gc_row"]
            sys = c * GDN_HEADS + h
            rows = slice(c * CHUNK, (c + 1) * CHUNK)
            q = v["q"][rows]
            k = v["k"][rows]
            vv = v["v"][rows]
            beta = beta_all[rows, h:h + 1]
            gcc = gc_col[rows, GDN_HEADS + h:GDN_HEADS + h + 1]
            gcr = gc_row[GDN_HEADS + h:GDN_HEADS + h + 1, rows]
            decay = jnp.exp(jnp.where(causal, gcc - gcr, -jnp.inf))
            kb = k * beta
            egc = jnp.exp(gcc)
            g_last = gcc[CHUNK - 1:CHUNK, :]
            kq = _dot_nt(jnp.concatenate([kb, q], axis=0).astype(BF16), k.astype(BF16))
            lmats[sys] = jnp.where(strict, kq[:CHUNK] * decay, 0.0)
            attn = jnp.where(causal, kq[CHUNK:] * decay, 0.0)
            rhss[sys] = jnp.concatenate([vv * beta, kb * egc], axis=1)
            kgt_attn[sys] = jnp.concatenate([(k * jnp.exp(g_last - gcc)).T, attn], axis=0)
            qgs[sys] = q * egc
            gl_s[sys] = jnp.broadcast_to(jnp.exp(g_last), (8, GDN_HEAD_DIM))
        return item

    for h in range(GDN_HEADS):
        stage_b.append(b_silu_z(h))
        stage_b.append(b_conv(h, 0, "q", lambda a: _l2norm(a) * scale))
        stage_b.append(b_conv(h, 1, "k", _l2norm))
        stage_b.append(b_conv(h, 2, "v", lambda a: a))
        stage_b.extend(b_prep(h, c) for c in range(nc))

    def b_inverse_start():
        lmat = jnp.stack(lmats)
        v["lmat"] = lmat
        v["noff"] = -jnp.where(((ii >> 1) == (jj >> 1))[None], lmat, 0.0)
    stage_b.append(b_inverse_start)

    def b_inverse_level(shift):
        def item():
            v["noff"] = _tri_merge_level(v["noff"], v["lmat"], ii, jj, shift)
        return item
    shift = 1
    while (1 << shift) < CHUNK:
        stage_b.append(b_inverse_level(shift))
        shift += 1

    def b_solve():
        rhs = jnp.stack(rhss)
        sol = rhs + _bdot(v["noff"], rhs)
        v["prod"] = _bdot(jnp.stack(kgt_attn), sol)
    stage_b.append(b_solve)

    def b_store():
        prod = v["prod"]
        dk = GDN_HEAD_DIM
        for sys in range(nsys):
            aq_s[sys, :dk, :] = (-prod[sys, :dk, dk:]).astype(BF16)
            aq_s[sys, dk:, :] = (qgs[sys] - prod[sys, dk:, dk:]).astype(BF16)
            b_s[sys] = prod[sys, :dk, :dk]
            o_s[sys] = prod[sys, dk:, :dk]
    stage_b.append(b_store)

    for item in _interleave(stage_a, stage_b, stage_c):
        item()

    ext_w[0:8, :] = ext_s[pl.ds(tg, 8), :]
    convout_ref[0] = ext_s[pl.ds(tg + 8 - (GDN_CONV - 1), GDN_CONV - 1), :]


def _gdn_prompt(x, g1, w_in_bf, w_gate_bf, gw, bsz, seq, tg):
    nt = seq // tg
    ntiles = bsz * nt
    nsys = (tg // CHUNK) * GDN_HEADS
    tile_a = lambda s: jnp.minimum(s, ntiles - 1)
    tile_b = lambda s: jnp.clip(s - 1, 0, ntiles - 1)
    tile_c = lambda s: jnp.maximum(s - 2, 0)
    const = lambda s: (0, 0)
    proj_bufs = lambda: [pltpu.VMEM((tg + 8, 3 * D_GDN), F32),
                         pltpu.VMEM((tg, D_GDN), F32),
                         pltpu.VMEM((tg, GATE_PAD), F32)]
    resident = lambda shape: pl.BlockSpec(shape, const, pipeline_mode=pl.Buffered(1))
    handover = lambda: [pltpu.VMEM((nsys, GDN_HEAD_DIM + CHUNK, GDN_HEAD_DIM), BF16),
                        pltpu.VMEM((nsys, GDN_HEAD_DIM, GDN_HEAD_DIM), F32),
                        pltpu.VMEM((nsys, CHUNK, GDN_HEAD_DIM), F32),
                        pltpu.VMEM((nsys, 8, GDN_HEAD_DIM), F32),
                        pltpu.VMEM((tg, D_GDN), F32)]
    return pl.pallas_call(
        functools.partial(_gdn_kernel, tg=tg, nt=nt),
        grid=(ntiles + 2,),
        in_specs=[pl.BlockSpec((tg, D_MODEL), lambda s: (tile_a(s), 0)),
                  resident((1, D_MODEL)),
                  resident((D_MODEL, D_IN_MAIN)),
                  resident((D_MODEL, GATE_PAD)),
                  resident((GDN_CONV, 3 * D_GDN)),
                  resident((2 * GDN_HEADS, GATE_PAD)),
                  resident((2 * GDN_HEADS, GATE_PAD)),
                  resident((1, GDN_HEAD_DIM))],
        out_specs=[pl.BlockSpec((tg, D_S5), lambda s: (tile_a(s), 0)),
                   pl.BlockSpec((tg, D_GDN), lambda s: (tile_c(s), 0)),
                   pl.BlockSpec((1, GDN_HEADS, GDN_HEAD_DIM, GDN_HEAD_DIM), lambda s: (tile_c(s) // nt, 0, 0, 0)),
                   pl.BlockSpec((1, GDN_CONV - 1, 3 * D_GDN), lambda s: (tile_b(s) // nt, 0, 0))],
        out_shape=[jax.ShapeDtypeStruct((bsz * seq, D_S5), F32),
                   jax.ShapeDtypeStruct((bsz * seq, D_GDN), BF16),
                   jax.ShapeDtypeStruct((bsz, GDN_HEADS, GDN_HEAD_DIM, GDN_HEAD_DIM), F32),
                   jax.ShapeDtypeStruct((bsz, GDN_CONV - 1, 3 * D_GDN), F32)],
        scratch_shapes=[pltpu.VMEM((GDN_HEADS, GDN_HEAD_DIM, GDN_HEAD_DIM), F32),
                        proj_bufs(), proj_bufs(), handover(), handover()],
        compiler_params=pltpu.CompilerParams(dimension_semantics=("arbitrary",),
                                             vmem_limit_bytes=VMEM_LIMIT),
        name="proj_gdn_prompt",
    )(x, g1, w_in_bf, w_gate_bf, gw["conv_w"], gw["a_log_col"], gw["dt_bias_col"], gw["onorm_g"])


def _gdn_step_kernel(qkv_ref, st_ref, z_ref, ba_ref, s0_ref, convw_ref, alog_ref, dtb_ref, og_ref,
                     y_ref, s1_ref, *, bb):
    beta_all, g_all = _gates(ba_ref[...], alog_ref[...], dtb_ref[...])
    alpha_all = jnp.exp(g_all)
    scale = GDN_HEAD_DIM ** -0.5
    og = og_ref[...]

    for h in range(GDN_HEADS):
        def conv_cols(part, h=h):
            c0 = part * D_GDN + h * GDN_HEAD_DIM
            cs = slice(c0, c0 + GDN_HEAD_DIM)
            acc = qkv_ref[:, cs] * convw_ref[GDN_CONV - 1:GDN_CONV, cs]
            for j in range(GDN_CONV - 1):
                acc = acc + st_ref[j, :, cs] * convw_ref[j:j + 1, cs]
            return _silu(acc)

        q = _l2norm(conv_cols(0)) * scale
        k = _l2norm(conv_cols(1))
        v = conv_cols(2)
        beta = beta_all[:, h:h + 1]
        alpha = alpha_all[:, GDN_HEADS + h:GDN_HEADS + h + 1]
        qk = jnp.sum(q * k, axis=-1, keepdims=True)
        kt = k.T
        kq = jnp.concatenate([k, q], axis=0).astype(BF16)
        o_rows = []
        for b in range(bb):
            s0 = s0_ref[b, h]
            kcol = kt[:, b:b + 1]
            r = _dot(kq, s0.astype(BF16))
            ks = r[b:b + 1, :]
            qs = r[bb + b:bb + b + 1, :]
            al = alpha[b:b + 1, :]
            v_new = beta[b:b + 1, :] * (v[b:b + 1, :] - al * ks)
            o_rows.append(al * qs + qk[b:b + 1, :] * v_new)
            s1_ref[b, h] = al * s0 + kcol * v_new
        o = jnp.concatenate(o_rows, axis=0)
        cs = slice(h * GDN_HEAD_DIM, (h + 1) * GDN_HEAD_DIM)
        y_ref[:, cs] = _out_gate(o, og, z_ref[:, cs]).astype(BF16)


def _gdn_sample(qkv, st_t, z, ba, s0, gw, bb):
    n = qkv.shape[0]
    row = lambda i: (i, 0)
    const = lambda i: (0, 0)
    state = lambda i: (i, 0, 0, 0)
    return pl.pallas_call(
        functools.partial(_gdn_step_kernel, bb=bb),
        grid=(n // bb,),
        in_specs=[pl.BlockSpec((bb, 3 * D_GDN), row),
                  pl.BlockSpec((GDN_CONV - 1, bb, 3 * D_GDN), lambda i: (0, i, 0)),
                  pl.BlockSpec((bb, D_GDN), row),
                  pl.BlockSpec((bb, GATE_PAD), row),
                  pl.BlockSpec((bb, GDN_HEADS, GDN_HEAD_DIM, GDN_HEAD_DIM), state),
                  pl.BlockSpec((GDN_CONV, 3 * D_GDN), const),
                  pl.BlockSpec((1, GATE_PAD), const),
                  pl.BlockSpec((1, GATE_PAD), const),
                  pl.BlockSpec((1, GDN_HEAD_DIM), const)],
        out_specs=[pl.BlockSpec((bb, D_GDN), row),
                   pl.BlockSpec((bb, GDN_HEADS, GDN_HEAD_DIM, GDN_HEAD_DIM), state)],
        out_shape=[jax.ShapeDtypeStruct((n, D_GDN), BF16),
                   jax.ShapeDtypeStruct(s0.shape, F32)],
        compiler_params=pltpu.CompilerParams(dimension_semantics=("arbitrary",),
                                             vmem_limit_bytes=VMEM_LIMIT),
        name="gdn_sample",
    )(qkv, st_t, z, ba, s0, gw["conv_w"], gw["a_log"], gw["dt_bias"], gw["onorm_g"])


def _residual_in(x_ref, ys5_ref, ygdn_ref, wout_ref, g2_ref, x1_s, n2_s):
    x1 = (x_ref[...] + _dot(ys5_ref[...].astype(BF16), wout_ref[:D_S5, :])
          + _dot(ygdn_ref[...].astype(BF16), wout_ref[D_S5:, :]))
    x1_s[...] = x1
    n2_s[...] = _rms(x1, g2_ref[...]).astype(BF16)


def _ffn_prompt_kernel(x_ref, ys5_ref, ygdn_ref, wout_ref, g2_ref, wup_ref, cw_ref, wdn_ref, gf_ref,
                       y_ref, hlast_ref, x1_s, n2_s, act_s, carry_s, *, tm):
    ti = pl.program_id(1)
    nt = pl.num_programs(1)

    @pl.when(ti == 0)
    def _reset():
        carry_s[...] = jnp.zeros_like(carry_s)

    _residual_in(x_ref, ys5_ref, ygdn_ref, wout_ref, g2_ref, x1_s, n2_s)
    n2 = n2_s[...]
    for f in range(D_FF // FFN_COLS):
        halves = []
        for part in range(2):
            c0 = part * D_FF + f * FFN_COLS
            cs = slice(c0, c0 + FFN_COLS)
            hcur = _dot(n2, wup_ref[:, cs])
            ext = jnp.concatenate([carry_s[:, cs], hcur], axis=0)
            carry_s[:, cs] = hcur[tm - 8:, :]
            conv = (_rows_back(ext, 2, tm) * cw_ref[0:1, cs]
                    + _rows_back(ext, 1, tm) * cw_ref[1:2, cs]
                    + hcur * cw_ref[2:3, cs])
            halves.append(conv)
        act_s[:, f * FFN_COLS:(f + 1) * FFN_COLS] = (_silu(halves[0]) * halves[1]).astype(BF16)
    y_ref[...] = _rms(x1_s[...] + _dot(act_s[...], wdn_ref[...]), gf_ref[...])

    @pl.when(ti == nt - 1)
    def _fin():
        hlast_ref[0] = carry_s[8 - (FFN_CONV - 1):, :]


def _ffn_prompt(x, ys5, ygdn, fw, bsz, seq, tm):
    nt = seq // tm
    row = lambda b, i: (b * nt + i, 0)
    const = lambda b, i: (0, 0)
    resident = lambda shape: pl.BlockSpec(shape, const, pipeline_mode=pl.Buffered(1))
    return pl.pallas_call(
        functools.partial(_ffn_prompt_kernel, tm=tm),
        grid=(bsz, nt),
        in_specs=[pl.BlockSpec((tm, D_MODEL), row),
                  pl.BlockSpec((tm, D_S5), row),
                  pl.BlockSpec((tm, D_GDN), row),
                  resident((D_MODEL, D_MODEL)),
                  resident((1, D_MODEL)),
                  resident((D_MODEL, 2 * D_FF)),
                  resident((FFN_CONV, 2 * D_FF)),
                  resident((D_FF, D_MODEL)),
                  resident((1, D_MODEL))],
        out_specs=[pl.BlockSpec((tm, D_MODEL), row),
                   pl.BlockSpec((1, FFN_CONV - 1, 2 * D_FF), lambda b, i: (b, 0, 0))],
        out_shape=[jax.ShapeDtypeStruct((bsz * seq, D_MODEL), F32),
                   jax.ShapeDtypeStruct((bsz, FFN_CONV - 1, 2 * D_FF), F32)],
        scratch_shapes=[pltpu.VMEM((tm, D_MODEL), F32),
                        pltpu.VMEM((tm, D_MODEL), BF16),
                        pltpu.VMEM((tm, D_FF), BF16),
                        pltpu.VMEM((8, 2 * D_FF), F32)],
        compiler_params=pltpu.CompilerParams(dimension_semantics=("arbitrary", "arbitrary"),
                                             vmem_limit_bytes=VMEM_LIMIT),
        name="ffn_prompt",
    )(x, ys5, ygdn, fw["w_out"], fw["norm2_g"], fw["w_up"], fw["conv_w"], fw["w_down"], fw["normf_g"])


def _ffn_sample_kernel(x_ref, ys5_ref, ygdn_ref, st_ref, wout_ref, g2_ref, wup_ref, cw_ref, wdn_ref, gf_ref,
                       y_ref, h_ref, x1_s, n2_s, acc_s):
    _residual_in(x_ref, ys5_ref, ygdn_ref, wout_ref, g2_ref, x1_s, n2_s)
    n2 = n2_s[...]
    for f in range(D_FF // FFN_COLS):
        halves = []
        for part in range(2):
            c0 = part * D_FF + f * FFN_COLS
            cs = slice(c0, c0 + FFN_COLS)
            hcur = _dot(n2, wup_ref[:, cs])
            h_ref[:, cs] = hcur
            halves.append(st_ref[0, :, cs] * cw_ref[0:1, cs] + st_ref[1, :, cs] * cw_ref[1:2, cs]
                          + hcur * cw_ref[2:3, cs])
        act = (_silu(halves[0]) * halves[1]).astype(BF16)
        contrib = _dot(act, wdn_ref[f * FFN_COLS:(f + 1) * FFN_COLS, :])
        if f == 0:
            acc_s[...] = contrib
        else:
            acc_s[...] += contrib
    y_ref[...] = _rms(x1_s[...] + acc_s[...], gf_ref[...])


def _ffn_sample(x, ys5, ygdn, st_t, fw):
    n = x.shape[0]
    c2 = lambda i: (0, 0)
    c3 = lambda i: (0, 0, 0)
    return pl.pallas_call(
        _ffn_sample_kernel,
        grid=(1,),
        in_specs=[pl.BlockSpec((n, D_MODEL), c2),
                  pl.BlockSpec((n, D_S5), c2),
                  pl.BlockSpec((n, D_GDN), c2),
                  pl.BlockSpec((FFN_CONV - 1, n, 2 * D_FF), c3),
                  pl.BlockSpec((D_MODEL, D_MODEL), c2),
                  pl.BlockSpec((1, D_MODEL), c2),
                  pl.BlockSpec((D_MODEL, 2 * D_FF), c2),
                  pl.BlockSpec((FFN_CONV, 2 * D_FF), c2),
                  pl.BlockSpec((D_FF, D_MODEL), c2),
                  pl.BlockSpec((1, D_MODEL), c2)],
        out_specs=[pl.BlockSpec((n, D_MODEL), c2),
                   pl.BlockSpec((n, 2 * D_FF), c2)],
        out_shape=[jax.ShapeDtypeStruct((n, D_MODEL), F32),
                   jax.ShapeDtypeStruct((n, 2 * D_FF), F32)],
        scratch_shapes=[pltpu.VMEM((n, D_MODEL), F32),
                        pltpu.VMEM((n, D_MODEL), BF16),
                        pltpu.VMEM((n, D_MODEL), F32)],
        compiler_params=pltpu.CompilerParams(dimension_semantics=("arbitrary",),
                                             vmem_limit_bytes=VMEM_LIMIT),
        name="ffn_sample",
    )(x, ys5, ygdn, st_t, fw["w_out"], fw["norm2_g"], fw["w_up"], fw["conv_w"], fw["w_down"], fw["normf_g"])


PROJ_ROWS = 512
S5_STEPS = 128
GDN_ROWS = 512
FFN_ROWS = 1024
GDN_SAMPLE_ROWS = 16


def kernel(x_prompt, x_sample, state_s5_re, state_s5_im, state_gdn, state_gdn_conv, state_ffn_conv, norm1_g, w_in, s5_a_re, s5_a_im, s5_log_dt, s5_b_re, s5_b_im, s5_c_re, s5_c_im, s5_d, s5_w_glu, gdn_conv_w, gdn_a_log, gdn_dt_bias, gdn_onorm_g, w_out, norm2_g, ffn_w_up, ffn_conv_w, ffn_w_down, normf_g):
    depth = w_in.shape[0]
    assert depth == 1, "the final rmsnorm is fused into the last layer's ffn kernel"
    bsz, seq, _ = x_prompt.shape
    nsmp = x_sample.shape[0]
    assert x_sample.shape[1] == 1
    l = 0

    xp = x_prompt.reshape(bsz * seq, D_MODEL)
    xs = x_sample.reshape(nsmp, D_MODEL)

    g1 = norm1_g[l].reshape(1, D_MODEL)
    w_in_bf = w_in[l][:, :D_IN_MAIN].astype(BF16)
    w_gate_bf = jnp.pad(w_in[l][:, D_IN_MAIN:], ((0, 0), (0, GATE_PAD - 2 * GDN_HEADS))).astype(BF16)
    sw = _s5_weights(s5_a_re[l], s5_a_im[l], s5_log_dt[l], s5_b_re[l], s5_b_im[l], s5_c_re[l], s5_c_im[l],
                     s5_d[l], s5_w_glu[l])
    gate_row = lambda v: jnp.pad(v, (GDN_HEADS, GATE_PAD - 2 * GDN_HEADS)).reshape(1, GATE_PAD)
    gate_col = lambda v: jnp.broadcast_to(jnp.pad(v, (GDN_HEADS, 0))[:, None], (2 * GDN_HEADS, GATE_PAD))
    gw = {"conv_w": gdn_conv_w[l], "a_log": gate_row(gdn_a_log[l]), "dt_bias": gate_row(gdn_dt_bias[l]),
          "a_log_col": gate_col(gdn_a_log[l]), "dt_bias_col": gate_col(gdn_dt_bias[l]),
          "onorm_g": gdn_onorm_g[l].reshape(1, GDN_HEAD_DIM)}
    fw = {"w_out": w_out[l].astype(BF16), "norm2_g": norm2_g[l].reshape(1, D_MODEL),
          "w_up": ffn_w_up[l].astype(BF16), "conv_w": ffn_conv_w[l], "w_down": ffn_w_down[l].astype(BF16),
          "normf_g": normf_g.reshape(1, D_MODEL)}

    u_p, ygdn_p, p_gdn, p_gdn_conv = _gdn_prompt(xp, g1, w_in_bf, w_gate_bf, gw, bsz, seq, GDN_ROWS)
    zeros_h = jnp.zeros((bsz, S5_CH), F32)
    ys5_p, p_hr, p_hi = _s5(u_p.reshape(bsz, seq, D_S5), zeros_h, zeros_h, sw, bsz, S5_STEPS, True)
    ys5_p = ys5_p.reshape(bsz * seq, D_S5)
    y_p, p_ffn_conv = _ffn_prompt(xp, ys5_p, ygdn_p, fw, bsz, seq, FFN_ROWS)

    u_s, qkv_s, z_s, ba_s = _proj(xs, g1, w_in_bf, w_gate_bf, nsmp)
    ys5_s, s_hr, s_hi = _s5(u_s, state_s5_re[l].reshape(nsmp, S5_CH), state_s5_im[l].reshape(nsmp, S5_CH),
                            sw, nsmp, 1, False)
    gconv_t = state_gdn_conv[l].transpose(1, 0, 2)
    ygdn_s, s_gdn = _gdn_sample(qkv_s, gconv_t, z_s, ba_s, state_gdn[l], gw, GDN_SAMPLE_ROWS)
    fconv_t = state_ffn_conv[l].transpose(1, 0, 2)
    y_s, h_s = _ffn_sample(xs, ys5_s, ygdn_s, fconv_t, fw)
    s_gdn_conv = jnp.concatenate([state_gdn_conv[l][:, 1:], qkv_s[:, None, :]], axis=1)
    s_ffn_conv = jnp.concatenate([state_ffn_conv[l][:, 1:], h_s[:, None, :]], axis=1)

    st = lambda a: a[None]
    s5_shape = lambda a, n: a.reshape(1, n, S5_N_GROUPS, S5_STATE)
    return (y_p.reshape(bsz, seq, D_MODEL), y_s.reshape(nsmp, 1, D_MODEL),
            s5_shape(p_hr, bsz), s5_shape(p_hi, bsz), st(p_gdn), st(p_gdn_conv), st(p_ffn_conv),
            s5_shape(s_hr, nsmp), s5_shape(s_hi, nsmp), st(s_gdn), st(s_gdn_conv), st(s_ffn_conv))
```
